```python
import math
import jax, jax.numpy as jnp
from jax import lax
import numpy as np

D_MODEL = 1024
BATCH = 16
SEQ = 256
DEPTH = 2
DEC_BATCH = 2
DEC_SEQ = 1024
PAST_LEN = 512

GRID_W = 64
N_EVEN = (DEPTH + 1) // 2
N_ODD = DEPTH // 2

A_WIDTH = D_MODEL // 2
A_HEAD = 64
A_HEADS = A_WIDTH // A_HEAD
DECAY_LORA = 64
ICLR_LORA = 64
GATE_LORA = 128
GN_EPS = 64e-5
B_WIDTH = D_MODEL - A_WIDTH
CONV_W = 31
A_COLS = 3 * A_WIDTH + 2 * DECAY_LORA + 2 * ICLR_LORA + GATE_LORA
A_SPLITS = (A_WIDTH, 2 * A_WIDTH, 3 * A_WIDTH, 3 * A_WIDTH + 2 * DECAY_LORA,
            3 * A_WIDTH + 2 * DECAY_LORA + 2 * ICLR_LORA)
EVEN_IN = A_COLS + 2 * B_WIDTH
C_HEAD = 64
C_HEADS = D_MODEL // (2 * C_HEAD)
C_INNER = C_HEADS * 2 * C_HEAD
ROPE_BASE = 10000.0
RMS_EPS = 1e-5
N_GROUPS = 4
EXPERTS_PER_GROUP = 8
N_EXPERTS = N_GROUPS * EXPERTS_PER_GROUP
TOP_K_INNER = 2
EXPERT_FF = 256
LN_EPS = 1e-5
DEEPNORM_ALPHA = (2 * DEPTH) ** 0.25
DEEPNORM_BETA = (8 * DEPTH) ** -0.25

F32 = jnp.float32

kernel_name = 'hybrid_rwkv7_conformer_diffattn_hmoe_dit_step'


def layer_norm(x, g, b, eps=LN_EPS):
    xf = x.astype(F32)
    mu = jnp.mean(xf, -1, keepdims=True)
    var = jnp.mean(jnp.square(xf - mu), -1, keepdims=True)
    y = (xf - mu) * lax.rsqrt(var + eps) * g.astype(F32) + b.astype(F32)
    return y.astype(x.dtype)


def ada_params(cond, w, b):
    m = jnp.einsum('nd,de->ne', jax.nn.silu(cond), w) + b
    return [t[:, None, :] for t in jnp.split(m, 6, axis=-1)]


def modulate(x, shift, scale):
    return x * (1 + scale) + shift


def post_norm(x, y, g, b):
    return layer_norm(DEEPNORM_ALPHA * x + y, g, b)


def token_shift(p):
    pad = jnp.pad(p, ((0, 0), (1, 1), (0, 0)))
    return 0.5 * (pad[:, :-2] + pad[:, 2:])


def wkv_scan(r, w, k, v, kk, a, s0, reverse):
    def step(s, inp):
        r_t, w_t, k_t, v_t, kk_t, a_t = inp
        sa = jnp.einsum('bhvk,bhk->bhv', s, -kk_t)
        s = (s * w_t[:, :, None, :] + sa[..., None] * (kk_t * a_t)[:, :, None, :]
             + v_t[..., None] * k_t[:, :, None, :])
        return s, jnp.einsum('bhvk,bhk->bhv', s, r_t)
    xs = tuple(jnp.swapaxes(t, 0, 1) for t in (r, w, k, v, kk, a))
    s_fin, y = lax.scan(step, s0, xs, reverse=reverse)
    return jnp.swapaxes(y, 0, 1), s_fin


def even_mixer(h, in_w, pw_b, ts_mu, w0, w_up, a0, a_up, g_up, k_k, k_a, r_k,
               gn_g, gn_b, dw_w, dw_b, cln_g, cln_b, out_w, s0):
    nb, nt, _ = h.shape
    proj = jnp.einsum('btd,de->bte', h, in_w)
    pa = proj[..., :A_COLS]
    pa = pa + ts_mu * (token_shift(pa) - pa)
    pb = proj[..., A_COLS:] + pw_b
    r, k, v, wd, ad, gd = jnp.split(pa, A_SPLITS, axis=-1)
    wd = wd.reshape(nb, nt, 2, DECAY_LORA)
    ad = ad.reshape(nb, nt, 2, ICLR_LORA)
    w_pre = (w0 + jnp.einsum('btzl,zlc->btzc', jnp.tanh(wd), w_up)).astype(F32)
    decay = jnp.exp(-jnp.exp(-jax.nn.softplus(-w_pre) - 0.5))
    a = jax.nn.sigmoid((a0 + jnp.einsum('btzl,zlc->btzc', ad, a_up)).astype(F32))
    g = jnp.einsum('btl,lc->btc', jax.nn.sigmoid(gd), g_up).astype(F32)
    heads = lambda t: t.reshape(nb, nt, A_HEADS, A_HEAD)
    rf, kf, vf = heads(r.astype(F32)), heads(k.astype(F32)), heads(v.astype(F32))
    kk = heads((k * k_k).astype(F32))
    kk = kk * lax.rsqrt(jnp.maximum(jnp.sum(kk * kk, -1, keepdims=True), 1e-24))
    ka = k_a.astype(F32).reshape(A_HEADS, A_HEAD)
    rk = r_k.astype(F32)
    ys, bonuses, finals = [], [], []
    for z in range(2):
        a_z = heads(a[:, :, z])
        k_z = kf * (1 + (a_z - 1) * ka)
        y_z, s_z = wkv_scan(rf, heads(decay[:, :, z]), k_z, vf, kk, a_z,
                            s0[:, z].astype(F32), reverse=(z == 1))
        ys.append(y_z)
        bonuses.append(jnp.sum(rf * k_z * rk, -1, keepdims=True) * vf)
        finals.append(s_z)
    y = ys[0] + ys[1]
    mu = jnp.mean(y, -1, keepdims=True)
    var = jnp.mean(jnp.square(y - mu), -1, keepdims=True)
    yn = ((y - mu) * lax.rsqrt(var + GN_EPS) * gn_g.astype(F32).reshape(A_HEADS, A_HEAD)
          + gn_b.astype(F32).reshape(A_HEADS, A_HEAD))
    out_a = ((yn + bonuses[0] + bonuses[1]).reshape(nb, nt, A_WIDTH) * g).astype(h.dtype)
    glu = pb[..., :B_WIDTH] * jax.nn.sigmoid(pb[..., B_WIDTH:])
    conv = lax.conv_general_dilated(glu, dw_w[:, None, :].astype(glu.dtype), (1,),
                                    [(CONV_W // 2, CONV_W // 2)],
                                    dimension_numbers=('NWC', 'WIO', 'NWC'),
                                    feature_group_count=B_WIDTH) + dw_b
    out_b = jax.nn.silu(layer_norm(conv, cln_g, cln_b)).astype(h.dtype)
    out = jnp.einsum('bte,ed->btd', jnp.concatenate([out_a, out_b], axis=-1), out_w)
    return out, jnp.stack(finals, axis=1)


def rope_2d(x, row, col):
    half = C_HEAD // 2
    nf = half // 2
    inv = ROPE_BASE ** (-jnp.arange(nf, dtype=F32) / nf)
    def rot(xp, pos):
        ang = pos.astype(F32)[:, None] * inv
        cos = jnp.cos(ang)[None, :, None, None, :]
        sin = jnp.sin(ang)[None, :, None, None, :]
        x1 = xp[..., :nf].astype(F32)
        x2 = xp[..., nf:].astype(F32)
        return jnp.concatenate([x1 * cos - x2 * sin, x1 * sin + x2 * cos], axis=-1)
    return jnp.concatenate([rot(x[..., :half], row), rot(x[..., half:], col)], axis=-1).astype(x.dtype)


def diff_qkv(h, in_w):
    nb, nt, _ = h.shape
    q, k, v = jnp.split(jnp.einsum('btd,de->bte', h, in_w), 3, axis=-1)
    return (q.reshape(nb, nt, C_HEADS, 2, C_HEAD), k.reshape(nb, nt, C_HEADS, 2, C_HEAD),
            v.reshape(nb, nt, C_HEADS, 2 * C_HEAD))


def diff_lambda(lq1, lk1, lq2, lk2, lam_init):
    return (jnp.exp(jnp.sum(lq1.astype(F32) * lk1.astype(F32)))
            - jnp.exp(jnp.sum(lq2.astype(F32) * lk2.astype(F32))) + lam_init)


def diff_attend(q, k, v, lam, lam_init, subln_g):
    nb, nq = q.shape[0], q.shape[1]
    s = jnp.einsum('bqhzd,bkhzd->bhzqk', q, k).astype(F32) * (C_HEAD ** -0.5)
    p = jax.nn.softmax(s, axis=-1)
    attn = p[:, :, 0] - lam * p[:, :, 1]
    o = jnp.einsum('bhqk,bkhe->bqhe', attn, v.astype(F32))
    o = o * lax.rsqrt(jnp.mean(o * o, -1, keepdims=True) + RMS_EPS) * subln_g.astype(F32) * (1.0 - lam_init)
    return o.reshape(nb, nq, C_INNER).astype(q.dtype)


def hier_moe(h, rg_w, rg_b, re_w, re_b, w_gate, w_up, w_down):
    nb, nt, d = h.shape
    hf = h.reshape(nb * nt, d)
    g_logits = (hf @ rg_w + rg_b).astype(F32)
    g_idx = jnp.argmax(g_logits, axis=-1)
    g_p = jnp.take_along_axis(jax.nn.softmax(g_logits, axis=-1), g_idx[:, None], axis=-1)
    e_logits = (hf @ re_w + re_b).astype(F32).reshape(-1, N_GROUPS, EXPERTS_PER_GROUP)
    e_in = jnp.take_along_axis(e_logits, g_idx[:, None, None], axis=1)[:, 0]
    top_v, top_i = lax.top_k(e_in, TOP_K_INNER)
    w_sel = g_p * jax.nn.softmax(top_v, axis=-1)
    eid = g_idx[:, None] * EXPERTS_PER_GROUP + top_i
    gates = jnp.sum(jax.nn.one_hot(eid, N_EXPERTS, dtype=F32) * w_sel[..., None], axis=1)
    hid = jax.nn.silu(jnp.einsum('nd,edf->nef', hf, w_gate)) * jnp.einsum('nd,edf->nef', hf, w_up)
    out = jnp.einsum('nef,efd->nd', hid * gates[..., None].astype(hid.dtype), w_down)
    return out.reshape(nb, nt, d).astype(h.dtype)


def setup_inputs(seed: int = 0) -> dict:
    key = jax.random.key(seed)
    ks = iter(jax.random.split(key, 64))
    D = D_MODEL
    def nrm(shape, scale=1.0):
        return scale * jax.random.normal(next(ks), shape, F32)
    def uni(shape, lo, hi):
        return jax.random.uniform(next(ks), shape, F32, lo, hi)
    inp = {}
    inp['x_prompt'] = nrm((BATCH, SEQ, D))
    inp['x_sample'] = nrm((DEC_BATCH, DEC_SEQ, D))
    inp['state_wkv'] = nrm((DEC_BATCH, N_EVEN, 2, A_HEADS, A_HEAD, A_HEAD))
    inp['cache_k'] = nrm((DEC_BATCH, N_ODD, PAST_LEN, C_HEADS, 2 * C_HEAD))
    inp['cache_v'] = nrm((DEC_BATCH, N_ODD, PAST_LEN, C_HEADS, 2 * C_HEAD))
    inp['c'] = nrm((DEC_BATCH, D))
    inp['c_ctx'] = nrm((D,))
    inp['mod_w'] = nrm((DEPTH, D, 6 * D), 0.5 * D ** -0.5)
    inp['mod_b'] = nrm((DEPTH, 6 * D), 0.02)
    inp['ln1_g'] = 1.0 + nrm((DEPTH, D), 0.05)
    inp['ln1_b'] = nrm((DEPTH, D), 0.02)
    inp['ln2_g'] = 1.0 + nrm((DEPTH, D), 0.05)
    inp['ln2_b'] = nrm((DEPTH, D), 0.02)
    inp['ev_in_w'] = nrm((N_EVEN, D, EVEN_IN), D ** -0.5)
    inp['ev_pw_b'] = nrm((N_EVEN, 2 * B_WIDTH), 0.02)
    inp['ev_ts_mu'] = uni((N_EVEN, A_COLS), 0.2, 0.8)
    inp['ev_w0'] = -2.0 + nrm((N_EVEN, 2, A_WIDTH), 1.0)
    inp['ev_w_up'] = nrm((N_EVEN, 2, DECAY_LORA, A_WIDTH), 0.1)
    inp['ev_a0'] = nrm((N_EVEN, 2, A_WIDTH), 0.5)
    inp['ev_a_up'] = nrm((N_EVEN, 2, ICLR_LORA, A_WIDTH), 0.1)
    inp['ev_g_up'] = nrm((N_EVEN, GATE_LORA, A_WIDTH), GATE_LORA ** -0.5)
    inp['ev_k_k'] = 0.85 + nrm((N_EVEN, A_WIDTH), 0.05)
    inp['ev_k_a'] = 1.0 + nrm((N_EVEN, A_WIDTH), 0.05)
    inp['ev_r_k'] = nrm((N_EVEN, A_HEADS, A_HEAD), 0.1)
    inp['ev_gn_g'] = 1.0 + nrm((N_EVEN, A_WIDTH), 0.05)
    inp['ev_gn_b'] = nrm((N_EVEN, A_WIDTH), 0.02)
    inp['ev_dw_w'] = nrm((N_EVEN, CONV_W, B_WIDTH), CONV_W ** -0.5)
    inp['ev_dw_b'] = nrm((N_EVEN, B_WIDTH), 0.02)
    inp['ev_cln_g'] = 1.0 + nrm((N_EVEN, B_WIDTH), 0.05)
    inp['ev_cln_b'] = nrm((N_EVEN, B_WIDTH), 0.02)
    inp['ev_out_w'] = nrm((N_EVEN, A_WIDTH + B_WIDTH, D), DEEPNORM_BETA * (A_WIDTH + B_WIDTH) ** -0.5)
    inp['od_in_w'] = nrm((N_ODD, D, 3 * C_INNER), D ** -0.5)
    inp['od_lq1'] = nrm((N_ODD, C_HEAD), 0.1)
    inp['od_lk1'] = nrm((N_ODD, C_HEAD), 0.1)
    inp['od_lq2'] = nrm((N_ODD, C_HEAD), 0.1)
    inp['od_lk2'] = nrm((N_ODD, C_HEAD), 0.1)
    inp['od_subln_g'] = 1.0 + nrm((N_ODD, 2 * C_HEAD), 0.05)
    inp['od_out_w'] = nrm((N_ODD, C_INNER, D), DEEPNORM_BETA * C_INNER ** -0.5)
    inp['rg_w'] = nrm((DEPTH, D, N_GROUPS), D ** -0.5)
    inp['rg_b'] = nrm((DEPTH, N_GROUPS), 0.01)
    inp['re_w'] = nrm((DEPTH, D, N_EXPERTS), D ** -0.5)
    inp['re_b'] = nrm((DEPTH, N_EXPERTS), 0.01)
    inp['ffn_w_gate'] = nrm((DEPTH, N_EXPERTS, D, EXPERT_FF), D ** -0.5)
    inp['ffn_w_up'] = nrm((DEPTH, N_EXPERTS, D, EXPERT_FF), D ** -0.5)
    inp['ffn_w_down'] = nrm((DEPTH, N_EXPERTS, EXPERT_FF, D), DEEPNORM_BETA * EXPERT_FF ** -0.5)
    return inp


def reference(x_prompt, x_sample, state_wkv, cache_k, cache_v, c, c_ctx,
              mod_w, mod_b, ln1_g, ln1_b, ln2_g, ln2_b,
              ev_in_w, ev_pw_b, ev_ts_mu, ev_w0, ev_w_up, ev_a0, ev_a_up, ev_g_up,
              ev_k_k, ev_k_a, ev_r_k, ev_gn_g, ev_gn_b, ev_dw_w, ev_dw_b, ev_cln_g, ev_cln_b, ev_out_w,
              od_in_w, od_lq1, od_lk1, od_lq2, od_lk2, od_subln_g, od_out_w,
              rg_w, rg_b, re_w, re_b, ffn_w_gate, ffn_w_up, ffn_w_down):
    nbp = x_prompt.shape[0]
    n_lat = x_sample.shape[1]
    rows = n_lat // GRID_W
    row = jnp.repeat(jnp.arange(rows), GRID_W)
    col = jnp.tile(jnp.arange(GRID_W), rows)
    cond_ctx = c_ctx[None, :]
    xp, xs = x_prompt, x_sample
    new_wkv, new_k, new_v = [], [], []
    for l in range(DEPTH):
        mp = ada_params(cond_ctx, mod_w[l], mod_b[l])
        ms = ada_params(c, mod_w[l], mod_b[l])
        hp = modulate(xp, mp[0], mp[1])
        hs = modulate(xs, ms[0], ms[1])
        if l % 2 == 0:
            i = l // 2
            ev = (ev_in_w[i], ev_pw_b[i], ev_ts_mu[i], ev_w0[i], ev_w_up[i], ev_a0[i], ev_a_up[i],
                  ev_g_up[i], ev_k_k[i], ev_k_a[i], ev_r_k[i], ev_gn_g[i], ev_gn_b[i], ev_dw_w[i],
                  ev_dw_b[i], ev_cln_g[i], ev_cln_b[i], ev_out_w[i])
            s0 = jnp.zeros((nbp, 2, A_HEADS, A_HEAD, A_HEAD), F32)
            mix_p, st = even_mixer(hp, *ev, s0)
            mix_s, _ = even_mixer(hs, *ev, state_wkv[:, i])
            new_wkv.append(st)
        else:
            j = l // 2
            lam_init = 0.8 - 0.6 * math.exp(-0.3 * l)
            lam = diff_lambda(od_lq1[j], od_lk1[j], od_lq2[j], od_lk2[j], lam_init)
            qp, kp, vp = diff_qkv(hp, od_in_w[j])
            mix_p = jnp.einsum('bte,ed->btd', diff_attend(qp, kp, vp, lam, lam_init, od_subln_g[j]), od_out_w[j])
            new_k.append(kp.reshape(nbp, -1, C_HEADS, 2 * C_HEAD))
            new_v.append(vp)
            qs, kls, vls = diff_qkv(hs, od_in_w[j])
            qs = rope_2d(qs, row, col)
            kls = rope_2d(kls, row, col)
            ck = cache_k[:, j]
            keys = jnp.concatenate([ck.reshape(ck.shape[0], ck.shape[1], C_HEADS, 2, C_HEAD).astype(kls.dtype), kls], axis=1)
            vals = jnp.concatenate([cache_v[:, j].astype(vls.dtype), vls], axis=1)
            mix_s = jnp.einsum('bte,ed->btd', diff_attend(qs, keys, vals, lam, lam_init, od_subln_g[j]), od_out_w[j])
        xp = post_norm(xp, mp[2] * mix_p, ln1_g[l], ln1_b[l])
        xs = post_norm(xs, ms[2] * mix_s, ln1_g[l], ln1_b[l])
        moe = (rg_w[l], rg_b[l], re_w[l], re_b[l], ffn_w_gate[l], ffn_w_up[l], ffn_w_down[l])
        xp = post_norm(xp, mp[5] * hier_moe(modulate(xp, mp[3], mp[4]), *moe), ln2_g[l], ln2_b[l])
        xs = post_norm(xs, ms[5] * hier_moe(modulate(xs, ms[3], ms[4]), *moe), ln2_g[l], ln2_b[l])
    y_prompt = xp
    y_sample = xs
    new_state_wkv = jnp.stack(new_wkv, axis=1)
    new_cache_k = jnp.stack(new_k, axis=1)
    new_cache_v = jnp.stack(new_v, axis=1)
    return (y_prompt, y_sample, new_state_wkv, new_cache_k, new_cache_v)
```

```python
import functools
import math

import jax
import jax.numpy as jnp
from jax import lax
from jax.experimental import pallas as pl
from jax.experimental.pallas import tpu as pltpu

F32 = jnp.float32
BF16 = jnp.bfloat16

LANES = 128
HEAD = 64
PAIR = 2 * HEAD
CHUNK = 64
ROW_TILE = 256
MOE_TILE = 1024
CONV_W = 31
N_GROUPS = 4
EXPERTS_PER_GROUP = 8
N_EXPERTS = N_GROUPS * EXPERTS_PER_GROUP
ROUTER_LANES = 128
LN_EPS = 1e-5
GN_EPS = 64e-5
RMS_EPS = 1e-5
ROPE_BASE = 10000.0
EXP_M05 = math.exp(-0.5)
VMEM_LIMIT = 56 * 1024 * 1024


def _bdot(a, b):
    return jnp.dot(a.astype(BF16), b.astype(BF16), preferred_element_type=F32)


def _bdot_nt(a, b):
    return lax.dot_general(a.astype(BF16), b.astype(BF16), (((1,), (1,)), ((), ())),
                           preferred_element_type=F32)


def _split2(a):
    hi = a.astype(BF16)
    lo = (a - hi.astype(F32)).astype(BF16)
    return hi, lo


def _split3(a):
    hi = a.astype(BF16)
    r1 = a - hi.astype(F32)
    mid = r1.astype(BF16)
    lo = (r1 - mid.astype(F32)).astype(BF16)
    return hi, mid, lo


def _dot_f32ish(a, b):
    ah, al = _split2(a)
    bh, bl = _split2(b)
    d = lambda x, y: jnp.dot(x, y, preferred_element_type=F32)
    return d(ah, bh) + d(ah, bl) + d(al, bh)


def _dot_exact_lhs(m_bf16, a):
    h, m, l = _split3(a)
    d = lambda y: jnp.dot(m_bf16, y, preferred_element_type=F32)
    return d(h) + d(m) + d(l)


def _dot_exact_rhs(a, m_bf16):
    h, m, l = _split3(a)
    d = lambda x: jnp.dot(x, m_bf16, preferred_element_type=F32)
    return d(h) + d(m) + d(l)


def _layer_norm(z, g, b, eps):
    mu = jnp.mean(z, axis=-1, keepdims=True)
    zc = z - mu
    var = jnp.mean(zc * zc, axis=-1, keepdims=True)
    return zc * lax.rsqrt(var + eps) * g + b


def _cond_row(tile, tile_rows, n_prompt_tok, dec_seq):
    tok = tile * tile_rows
    return jnp.where(tok < n_prompt_tok, 0, 1 + (tok - n_prompt_tok) // dec_seq)


def _cast_rows(src_ref, dst_ref, rows_per_step=128):
    n = src_ref.shape[0]
    for s in range(0, n, rows_per_step):
        dst_ref[s:s + rows_per_step, :] = src_ref[s:s + rows_per_step, :].astype(dst_ref.dtype)


def _cparams(sem):
    return pltpu.CompilerParams(dimension_semantics=sem, vmem_limit_bytes=VMEM_LIMIT)


def _ada_kernel(cond_ref, w_ref, b_ref, o_ref):
    c = cond_ref[...]
    s = c * jax.nn.sigmoid(c)
    o_ref[0, 0] = _dot_f32ish(s, w_ref[0]) + b_ref[0, 0]


def _ada_table(cond8, mod_w, mod_b):
    depth, d, _ = mod_w.shape
    return pl.pallas_call(
        _ada_kernel,
        out_shape=jax.ShapeDtypeStruct((depth, 6, 8, d), F32),
        grid=(depth, 6),
        in_specs=[pl.BlockSpec((8, d), lambda l, j: (0, 0)),
                  pl.BlockSpec((1, d, d), lambda l, j: (l, 0, j)),
                  pl.BlockSpec((1, 1, 1, d), lambda l, j: (l, j, 0, 0))],
        out_specs=pl.BlockSpec((1, 1, 8, d), lambda l, j: (l, j, 0, 0)),
        compiler_params=_cparams(("arbitrary", "arbitrary")),
        name="ada_table",
    )(cond8, mod_w, mod_b.reshape(depth, 6, 1, d))


def _mod_spec(layer, which, d):
    return pl.BlockSpec((None, None, 8, d), lambda *_: (layer, which, 0, 0))


def _modmm_kernel(x_ref, sh_ref, sc_ref, w_ref, *rest, splits, geom):
    outs, wb = rest[:len(splits)], rest[len(splits)]
    i = pl.program_id(0)

    @pl.when(i == 0)
    def _():
        _cast_rows(w_ref, wb)

    r = _cond_row(i, ROW_TILE, *geom)
    h = x_ref[...] * (1.0 + sc_ref[pl.ds(r, 1), :]) + sh_ref[pl.ds(r, 1), :]
    hb = h.astype(BF16)
    off = 0
    for o_ref, n in zip(outs, splits):
        o_ref[...] = jnp.dot(hb, wb[:, off:off + n], preferred_element_type=F32)
        off += n


def _mod_matmul(x, mod, layer, w, splits, geom):
    n_tok, d = x.shape
    n_out = w.shape[1]
    return pl.pallas_call(
        functools.partial(_modmm_kernel, splits=splits, geom=geom),
        out_shape=[jax.ShapeDtypeStruct((n_tok, n), F32) for n in splits],
        grid=(n_tok // ROW_TILE,),
        in_specs=[pl.BlockSpec((ROW_TILE, d), lambda i: (i, 0)),
                  _mod_spec(layer, 0, d), _mod_spec(layer, 1, d),
                  pl.BlockSpec((d, n_out), lambda i: (0, 0), pipeline_mode=pl.Buffered(1))],
        out_specs=[pl.BlockSpec((ROW_TILE, n), lambda i: (i, 0)) for n in splits],
        scratch_shapes=[pltpu.VMEM((d, n_out), BF16)],
        compiler_params=_cparams(("arbitrary",)),
        name=f"mod_matmul_l{layer}",
    )(x, mod, mod, w)


def _outproj_kernel(*refs, n_in, alpha, geom):
    h_refs = refs[:n_in]
    x_ref, gate_ref, w_ref, g_ref, b_ref, o_ref, wb = refs[n_in:]
    i = pl.program_id(0)

    @pl.when(i == 0)
    def _():
        _cast_rows(w_ref, wb)

    r = _cond_row(i, ROW_TILE, *geom)
    y = None
    off = 0
    for h_ref in h_refs:
        n = h_ref.shape[1]
        part = jnp.dot(h_ref[...].astype(BF16), wb[off:off + n, :], preferred_element_type=F32)
        y = part if y is None else y + part
        off += n
    z = alpha * x_ref[...] + gate_ref[pl.ds(r, 1), :] * y
    o_ref[...] = _layer_norm(z, g_ref[...], b_ref[...], LN_EPS)


def _out_proj_norm(hs, x, mod, layer, w, ln_g, ln_b, alpha, geom):
    n_tok, d = x.shape
    return pl.pallas_call(
        functools.partial(_outproj_kernel, n_in=len(hs), alpha=alpha, geom=geom),
        out_shape=jax.ShapeDtypeStruct((n_tok, d), F32),
        grid=(n_tok // ROW_TILE,),
        in_specs=[pl.BlockSpec((ROW_TILE, h.shape[1]), lambda i: (i, 0)) for h in hs] + [
            pl.BlockSpec((ROW_TILE, d), lambda i: (i, 0)),
            _mod_spec(layer, 2, d),
            pl.BlockSpec(w.shape, lambda i: (0, 0), pipeline_mode=pl.Buffered(1)),
            pl.BlockSpec((1, d), lambda i: (0, 0)),
            pl.BlockSpec((1, d), lambda i: (0, 0))],
        out_specs=pl.BlockSpec((ROW_TILE, d), lambda i: (i, 0)),
        scratch_shapes=[pltpu.VMEM(w.shape, BF16)],
        compiler_params=_cparams(("arbitrary",)),
        name=f"out_proj_norm_l{layer}",
    )(*hs, x, mod, w, ln_g.reshape(1, d), ln_b.reshape(1, d))


def _route(logits):
    neg = jnp.float32(-3e38)
    big = jnp.int32(1 << 20)
    lane = lax.broadcasted_iota(jnp.int32, logits.shape, 1)
    is_g = lane < N_GROUPS
    gl = jnp.where(is_g, logits, neg)
    gmax = jnp.max(gl, axis=-1, keepdims=True)
    gidx = jnp.min(jnp.where(gl == gmax, lane, big), axis=-1, keepdims=True)
    gsum = jnp.sum(jnp.where(is_g, jnp.exp(gl - gmax), 0.0), axis=-1, keepdims=True)
    g_p = 1.0 / gsum
    lo = N_GROUPS + gidx * EXPERTS_PER_GROUP
    in_grp = (lane >= lo) & (lane < lo + EXPERTS_PER_GROUP)
    el = jnp.where(in_grp, logits, neg)
    v1 = jnp.max(el, axis=-1, keepdims=True)
    i1 = jnp.min(jnp.where(el == v1, lane, big), axis=-1, keepdims=True)
    el2 = jnp.where(lane == i1, neg, el)
    v2 = jnp.max(el2, axis=-1, keepdims=True)
    i2 = jnp.min(jnp.where(el2 == v2, lane, big), axis=-1, keepdims=True)
    e21 = jnp.exp(v2 - v1)
    den = 1.0 + e21
    w1 = g_p / den
    w2 = g_p * e21 / den
    return jnp.where(lane == i1, w1, 0.0) + jnp.where(lane == i2, w2, 0.0)


def _moe_kernel(x_ref, sh_ref, sc_ref, gate_ref, rw_ref, rb_ref, wg_ref, wu_ref, wd_ref, g_ref, b_ref,
                o_ref, h_scr, gates_scr, acc_scr, wgb, wub, wdb, *, alpha, geom):
    i = pl.program_id(0)
    e = pl.program_id(1)
    r = _cond_row(i, MOE_TILE, *geom)
    sub = ROW_TILE
    n_sub = MOE_TILE // sub

    @pl.when(e == 0)
    def _():
        for s in range(n_sub):
            rows = slice(s * sub, (s + 1) * sub)
            h = x_ref[rows, :] * (1.0 + sc_ref[pl.ds(r, 1), :]) + sh_ref[pl.ds(r, 1), :]
            h_scr[rows, :] = h.astype(BF16)
            logits = _dot_f32ish(h, rw_ref[...]) + rb_ref[...]
            gates_scr[rows, :] = _route(logits)
            acc_scr[rows, :] = jnp.zeros((sub, acc_scr.shape[1]), F32)

    wgb[...] = wg_ref[0].astype(BF16)
    wub[...] = wu_ref[0].astype(BF16)
    wdb[...] = wd_ref[0].astype(BF16)
    lane = lax.broadcasted_iota(jnp.int32, (sub, ROUTER_LANES), 1)
    for s in range(n_sub):
        rows = slice(s * sub, (s + 1) * sub)
        hb = h_scr[rows, :]
        ge = jnp.sum(jnp.where(lane == e + N_GROUPS, gates_scr[rows, :], 0.0), axis=-1, keepdims=True)
        a = jnp.dot(hb, wgb[...], preferred_element_type=F32)
        u = jnp.dot(hb, wub[...], preferred_element_type=F32)
        hid = (a * jax.nn.sigmoid(a)) * u * ge
        acc_scr[rows, :] += jnp.dot(hid.astype(BF16), wdb[...], preferred_element_type=F32)

    @pl.when(e == N_EXPERTS - 1)
    def _():
        for s in range(n_sub):
            rows = slice(s * sub, (s + 1) * sub)
            z = alpha * x_ref[rows, :] + gate_ref[pl.ds(r, 1), :] * acc_scr[rows, :]
            o_ref[rows, :] = _layer_norm(z, g_ref[...], b_ref[...], LN_EPS)


def _moe_norm(x, mod, layer, rw, rb, w_gate, w_up, w_down, ln_g, ln_b, alpha, geom):
    n_tok, d = x.shape
    ff = w_gate.shape[-1]
    const = lambda i, e: (0, 0)
    return pl.pallas_call(
        functools.partial(_moe_kernel, alpha=alpha, geom=geom),
        out_shape=jax.ShapeDtypeStruct((n_tok, d), F32),
        grid=(n_tok // MOE_TILE, N_EXPERTS),
        in_specs=[pl.BlockSpec((MOE_TILE, d), lambda i, e: (i, 0)),
                  _mod_spec(layer, 3, d), _mod_spec(layer, 4, d), _mod_spec(layer, 5, d),
                  pl.BlockSpec((d, ROUTER_LANES), const),
                  pl.BlockSpec((1, ROUTER_LANES), const),
                  pl.BlockSpec((None, 1, d, ff), lambda i, e: (layer, e, 0, 0)),
                  pl.BlockSpec((None, 1, d, ff), lambda i, e: (layer, e, 0, 0)),
                  pl.BlockSpec((None, 1, ff, d), lambda i, e: (layer, e, 0, 0)),
                  pl.BlockSpec((1, d), const),
                  pl.BlockSpec((1, d), const)],
        out_specs=pl.BlockSpec((MOE_TILE, d), lambda i, e: (i, 0)),
        scratch_shapes=[pltpu.VMEM((MOE_TILE, d), BF16),
                        pltpu.VMEM((MOE_TILE, ROUTER_LANES), F32),
                        pltpu.VMEM((MOE_TILE, d), F32),
                        pltpu.VMEM((d, ff), BF16), pltpu.VMEM((d, ff), BF16), pltpu.VMEM((ff, d), BF16)],
        compiler_params=_cparams(("arbitrary", "arbitrary")),
        name=f"moe_norm_l{layer}",
    )(x, mod, mod, mod, rw, rb, w_gate, w_up, w_down, ln_g.reshape(1, d), ln_b.reshape(1, d))


def _rwkv_kernel(r_ref, k_ref, v_ref, lora_ref, mur_ref, muk_ref, muv_ref, mul_ref,
                 w0_ref, wup_ref, a0_ref, aup_ref, gup_ref, kk_ref, ka_ref, rk_ref, gng_ref, gnb_ref, s0_ref,
                 out_ref, sfin_ref,
                 lw_s, kz_s, bz_s, rr_s, vv_s, kn_s, g_s, h_s, q_s, y0_s, y_s, st_s, *, seq_len):
    t_len = seq_len
    n_chunks = t_len // CHUNK
    c2 = 2 * CHUNK

    row = lax.broadcasted_iota(jnp.int32, (t_len, 1), 0)

    def tshift(x, mu):
        prev = jnp.where(row == 0, 0.0, pltpu.roll(x, 1, 0))
        nxt = jnp.where(row == t_len - 1, 0.0, pltpu.roll(x, t_len - 1, 0))
        return x + mu * (0.5 * (prev + nxt) - x)

    r = tshift(r_ref[...], mur_ref[...])
    k = tshift(k_ref[...], muk_ref[...])
    v = tshift(v_ref[...], muv_ref[...])
    lo = tshift(lora_ref[...], mul_ref[...])
    wd = jnp.tanh(lo[:, 0:PAIR])
    ad = lo[:, PAIR:2 * PAIR]
    gate = _bdot(jax.nn.sigmoid(lo[:, 2 * PAIR:3 * PAIR]), gup_ref[...])

    ri = lax.broadcasted_iota(jnp.int32, (PAIR, PAIR), 0)
    ci = lax.broadcasted_iota(jnp.int32, (PAIR, PAIR), 1)
    same_head = (ri // HEAD) == (ci // HEAD)
    ones_bd = jnp.where(same_head, 1.0, 0.0).astype(BF16)
    eye = ri == ci

    kk = k * kk_ref[...]
    ssq = _dot_exact_rhs(kk * kk, ones_bd)
    kk = kk * lax.rsqrt(jnp.maximum(ssq, 1e-24))
    wup2 = wup_ref[...].reshape(PAIR, PAIR)
    aup2 = aup_ref[...].reshape(PAIR, PAIR)
    bonus = jnp.zeros((t_len, PAIR), F32)
    for z in range(2):
        zrows = (ri // HEAD) == z
        w_pre = w0_ref[z:z + 1, :] + _bdot(wd, jnp.where(zrows, wup2, 0.0))
        lw_s[z] = -EXP_M05 * jax.nn.sigmoid(w_pre)
        a = jax.nn.sigmoid(a0_ref[z:z + 1, :] + _bdot(ad, jnp.where(zrows, aup2, 0.0)))
        kz = k * (1.0 + (a - 1.0) * ka_ref[...])
        kz_s[z] = kz
        bz_s[z] = kk * a
        bonus = bonus + _dot_exact_rhs(r * kz * rk_ref[...], ones_bd) * v
    rr_s[...] = r
    vv_s[...] = v
    kn_s[...] = kk

    rt_i = ri % CHUNK
    cs_i = ci % CHUNK
    strict = (cs_i < rt_i, cs_i > rt_i)
    incl = (cs_i <= rt_i, cs_i >= rt_i)
    t64 = lax.broadcasted_iota(jnp.int32, (CHUNK, CHUNK), 0)
    s64 = lax.broadcasted_iota(jnp.int32, (CHUNK, CHUNK), 1)
    tri = (jnp.where(s64 <= t64, 1.0, 0.0).astype(BF16), jnp.where(s64 >= t64, 1.0, 0.0).astype(BF16))
    lane_head0 = lax.broadcasted_iota(jnp.int32, (CHUNK, PAIR), 1) < HEAD
    eye_f = jnp.where(eye, 1.0, 0.0)

    def stack(x):
        return jnp.concatenate([jnp.where(lane_head0, x, 0.0), jnp.where(lane_head0, 0.0, x)], axis=0)

    def fold(x):
        return x[:CHUNK] + x[CHUNK:]

    def solve_chunk(c, carry):
        rows = pl.ds(pl.multiple_of(c * CHUNK, CHUNK), CHUNK)
        r_c, v_c, kn_c = rr_s[rows, :], vv_s[rows, :], kn_s[rows, :]
        v_st = stack(v_c)
        for z in range(2):
            lw_c, kz_c, bz_c = lw_s[z, rows, :], kz_s[z, rows, :], bz_s[z, rows, :]
            lg = _dot_exact_lhs(tri[z], lw_c)
            total = lg[CHUNK - 1:CHUNK, :] if z == 0 else lg[0:1, :]
            e_neg = jnp.exp(-lg)
            e_rem = jnp.exp(total - lg)
            abar = stack(kn_c * jnp.exp(lg - lw_c))
            rt = stack(r_c * jnp.exp(lg))
            big = _bdot_nt(jnp.concatenate([abar, rt], axis=0),
                           jnp.concatenate([stack(bz_c * e_neg), stack(kz_c * e_neg)], axis=0))
            mb = jnp.where(strict[z], big[:c2, :c2], 0.0)
            mk = jnp.where(strict[z], big[:c2, c2:], 0.0)
            nb = jnp.where(incl[z], big[c2:, :c2], 0.0)
            nk = jnp.where(incl[z], big[c2:, c2:], 0.0)
            tm = eye_f - mb
            p = _dot_f32ish(mb, mb)
            n_round = CHUNK.bit_length() - 2
            for j in range(n_round):
                tm = tm + _dot_f32ish(tm, p)
                if j < n_round - 1:
                    p = _dot_f32ish(p, p)
            tx = _bdot(tm, jnp.concatenate([abar, _bdot(mk, v_st)], axis=1))
            ta, x1 = tx[:, :PAIR], tx[:, PAIR:]
            z2 = jnp.concatenate([v_st, x1], axis=0)
            wt = jnp.concatenate([stack(kz_c * e_rem), -stack(bz_c * e_rem)], axis=0).T
            h_s[z, c] = _bdot(wt, z2)
            g_s[z, c] = jnp.where(eye, jnp.exp(total), 0.0) + _bdot(wt[:, c2:], ta)
            q_s[z, c] = fold(rt - _bdot(nb, ta))
            y0_s[z, c] = fold(_bdot(jnp.concatenate([nk, -nb], axis=1), z2))
        return carry

    lax.fori_loop(0, n_chunks, solve_chunk, 0)

    zero64 = jnp.zeros((HEAD, HEAD), F32)

    for z in range(2):
        a = jnp.concatenate([jnp.concatenate([s0_ref[z, 0], zero64], axis=1),
                             jnp.concatenate([zero64, s0_ref[z, 1]], axis=1)], axis=0)
        st_s[z] = a.T

    def propagate(i, carry):
        for z in range(2):
            c = i if z == 0 else n_chunks - 1 - i
            s = st_s[z].astype(BF16)
            rows = pl.ds(pl.multiple_of(c * CHUNK, CHUNK), CHUNK)
            y_s[z, rows, :] = _bdot(q_s[z, c], s) + y0_s[z, c]
            st_s[z] = _bdot(g_s[z, c], s) + h_s[z, c]
        return carry

    lax.fori_loop(0, n_chunks, propagate, 0)
    for z in range(2):
        a = st_s[z].T
        sfin_ref[z, 0] = a[:HEAD, :HEAD]
        sfin_ref[z, 1] = a[HEAD:, HEAD:]

    y = y_s[0] + y_s[1]
    mu = _dot_exact_rhs(y, ones_bd) * (1.0 / HEAD)
    yc = y - mu
    var = _dot_exact_rhs(yc * yc, ones_bd) * (1.0 / HEAD)
    yn = yc * lax.rsqrt(var + GN_EPS) * gng_ref[...] + gnb_ref[...]
    out_ref[...] = (yn + bonus) * gate


def _rwkv_mix(pa, first_tok, n_seq, seq_len, s0, p):
    a_width = p["w0"].shape[-1]
    n_pairs = a_width // PAIR
    heads = a_width // HEAD
    t0 = first_tok // seq_len
    lora_w = 3 * PAIR
    lora_blk = 3 * a_width // lora_w
    n_chunks = seq_len // CHUNK
    col = lambda off: (lambda s, q: (t0 + s, off + q))
    vec = lambda off: (lambda s, q: (0, off + q))
    in_specs = [
        pl.BlockSpec((seq_len, PAIR), col(0)),
        pl.BlockSpec((seq_len, PAIR), col(n_pairs)),
        pl.BlockSpec((seq_len, PAIR), col(2 * n_pairs)),
        pl.BlockSpec((seq_len, lora_w), lambda s, q: (t0 + s, lora_blk)),
        pl.BlockSpec((1, PAIR), vec(0)),
        pl.BlockSpec((1, PAIR), vec(n_pairs)),
        pl.BlockSpec((1, PAIR), vec(2 * n_pairs)),
        pl.BlockSpec((1, lora_w), lambda s, q: (0, lora_blk)),
        pl.BlockSpec((2, PAIR), vec(0)),
        pl.BlockSpec((2, HEAD, PAIR), lambda s, q: (0, 0, q)),
        pl.BlockSpec((2, PAIR), vec(0)),
        pl.BlockSpec((2, HEAD, PAIR), lambda s, q: (0, 0, q)),
        pl.BlockSpec((PAIR, PAIR), vec(0)),
        pl.BlockSpec((1, PAIR), vec(0)),
        pl.BlockSpec((1, PAIR), vec(0)),
        pl.BlockSpec((1, PAIR), vec(0)),
        pl.BlockSpec((1, PAIR), vec(0)),
        pl.BlockSpec((1, PAIR), vec(0)),
        pl.BlockSpec((None, 2, 2, HEAD, HEAD), lambda s, q: (s, 0, q, 0, 0)),
    ]
    big = lambda: pltpu.VMEM((2, seq_len, PAIR), F32)
    one = lambda: pltpu.VMEM((seq_len, PAIR), F32)
    out, s_fin = pl.pallas_call(
        functools.partial(_rwkv_kernel, seq_len=seq_len),
        out_shape=[jax.ShapeDtypeStruct((n_seq * seq_len, a_width), F32),
                   jax.ShapeDtypeStruct((n_seq, 2, heads, HEAD, HEAD), F32)],
        grid=(n_seq, n_pairs),
        in_specs=in_specs,
        out_specs=[pl.BlockSpec((seq_len, PAIR), lambda s, q: (s, q)),
                   pl.BlockSpec((None, 2, 2, HEAD, HEAD), lambda s, q: (s, 0, q, 0, 0))],
        scratch_shapes=[big(), big(), big(), one(), one(), one(),
                        pltpu.VMEM((2, n_chunks, PAIR, PAIR), F32), pltpu.VMEM((2, n_chunks, PAIR, PAIR), F32),
                        pltpu.VMEM((2, n_chunks, CHUNK, PAIR), F32), pltpu.VMEM((2, n_chunks, CHUNK, PAIR), F32),
                        big(), pltpu.VMEM((2, PAIR, PAIR), F32)],
        compiler_params=_cparams(("arbitrary", "arbitrary")),
        name=f"rwkv_mix_t{seq_len}",
    )(pa, pa, pa, pa, p["ts_mu"], p["ts_mu"], p["ts_mu"], p["ts_mu"],
      p["w0"], p["w_up"], p["a0"], p["a_up"], p["g_up"], p["k_k"], p["k_a"], p["r_k"], p["gn_g"], p["gn_b"], s0)
    return out, s_fin


def _conv_kernel(pb_ref, pwb_ref, dw_ref, dwb_ref, g_ref, b_ref, o_ref, pad_s, *, seq_len):
    width = o_ref.shape[1]
    halo = 16
    sub = 64
    pbv = pb_ref[...] + pwb_ref[...]
    glu = pbv[:, :width] * jax.nn.sigmoid(pbv[:, width:])
    pad_s[0:halo, :] = jnp.zeros((halo, width), F32)
    pad_s[halo:halo + seq_len, :] = glu
    pad_s[halo + seq_len:, :] = jnp.zeros((halo, width), F32)
    first = halo - CONV_W // 2
    for c in range(seq_len // sub):
        acc = jnp.zeros((sub, width), F32)
        for d in range(CONV_W):
            start = c * sub + first + d
            acc = acc + pad_s[start:start + sub, :] * dw_ref[d:d + 1, :]
        y = _layer_norm(acc + dwb_ref[...], g_ref[...], b_ref[...], LN_EPS)
        o_ref[c * sub:(c + 1) * sub, :] = y * jax.nn.sigmoid(y)


def _conv_module(pb, first_tok, n_seq, seq_len, pw_b, dw_w, dw_b, cln_g, cln_b):
    width = dw_w.shape[1]
    t0 = first_tok // seq_len
    const = lambda s: (0, 0)
    return pl.pallas_call(
        functools.partial(_conv_kernel, seq_len=seq_len),
        out_shape=jax.ShapeDtypeStruct((n_seq * seq_len, width), F32),
        grid=(n_seq,),
        in_specs=[pl.BlockSpec((seq_len, 2 * width), lambda s: (t0 + s, 0)),
                  pl.BlockSpec((1, 2 * width), const),
                  pl.BlockSpec((CONV_W, width), const),
                  pl.BlockSpec((1, width), const),
                  pl.BlockSpec((1, width), const),
                  pl.BlockSpec((1, width), const)],
        out_specs=pl.BlockSpec((seq_len, width), lambda s: (s, 0)),
        scratch_shapes=[pltpu.VMEM((seq_len + 32, width), F32)],
        compiler_params=_cparams(("arbitrary",)),
        name=f"conv_module_t{seq_len}",
    )(pb, pw_b.reshape(1, -1), dw_w, dw_b.reshape(1, -1), cln_g.reshape(1, -1), cln_b.reshape(1, -1))


def _diff_lambda(lq1, lk1, lq2, lk2, lam_init):
    dot = lambda a, b: jnp.exp(jnp.sum(a[...] * b[...], axis=-1, keepdims=True))
    return dot(lq1, lk1) - dot(lq2, lk2) + lam_init


def _diff_attend(q, keys, vals, lam, lam_init, subln):
    nq = q.shape[0]
    lane_first = lax.broadcasted_iota(jnp.int32, q.shape, 1) < HEAD
    q_st = jnp.concatenate([jnp.where(lane_first, q, 0.0), jnp.where(lane_first, 0.0, q)], axis=0)
    s = _bdot_nt(q_st, keys) * (HEAD ** -0.5)
    s = s - jnp.max(s, axis=-1, keepdims=True)
    pexp = jnp.exp(s)
    prob = pexp / jnp.sum(pexp, axis=-1, keepdims=True)
    attn = prob[:nq] - lam * prob[nq:]
    o = _bdot(attn, vals)
    o = o * lax.rsqrt(jnp.mean(o * o, axis=-1, keepdims=True) + RMS_EPS)
    return o * subln * (1.0 - lam_init)


def _attn_prompt_kernel(q_ref, k_ref, v_ref, lq1, lk1, lq2, lk2, sub_ref, o_ref, *, lam_init):
    lam = _diff_lambda(lq1, lk1, lq2, lk2, lam_init)
    o_ref[...] = _diff_attend(q_ref[...], k_ref[...], v_ref[...], lam, lam_init, sub_ref[...])


def _rope(x, cos, sin_signed):
    lane = lax.broadcasted_iota(jnp.int32, x.shape, 1)
    quarter = HEAD // 4
    partner = jnp.where((lane // quarter) % 2 == 0,
                        pltpu.roll(x, LANES - quarter, 1), pltpu.roll(x, quarter, 1))
    return x * cos + partner * sin_signed


def _attn_sample_kernel(q_ref, k_ref, v_ref, ck_ref, cv_ref, cq_ref, sq_ref, cosk_ref, sink_ref,
                        lq1, lk1, lq2, lk2, sub_ref, o_ref, *, lam_init):
    lam = _diff_lambda(lq1, lk1, lq2, lk2, lam_init)
    q = _rope(q_ref[...], cq_ref[...], sq_ref[...])
    kl = _rope(k_ref[...], cosk_ref[...], sink_ref[...])
    keys = jnp.concatenate([ck_ref[...], kl], axis=0)
    vals = jnp.concatenate([cv_ref[...], v_ref[...]], axis=0)
    o_ref[...] = _diff_attend(q, keys, vals, lam, lam_init, sub_ref[...])


def _rope_tables(n_lat, grid_w):
    quarter = HEAD // 4
    inv = ROPE_BASE ** (-jnp.arange(quarter, dtype=F32) / quarter)
    t = jnp.arange(n_lat)
    row_ang = (t // grid_w).astype(F32)[:, None] * inv
    col_ang = (t % grid_w).astype(F32)[:, None] * inv
    ang = jnp.concatenate([row_ang, row_ang, col_ang, col_ang] * 2, axis=-1)
    sign = jnp.tile(jnp.concatenate([-jnp.ones(quarter, F32), jnp.ones(quarter, F32)]), 2 * HEAD // (2 * quarter))
    return jnp.cos(ang), jnp.sin(ang) * sign


def _attention(q, k, v, cache_k, cache_v, p, lam_init, n_prompt, seq, n_lat_req, n_lat, grid_w):
    n_tok, inner = q.shape
    heads = inner // PAIR
    vec = lambda n: pl.BlockSpec((1, n), lambda *_: (0, 0))
    lam_args = [p[n].reshape(1, -1) for n in ("lq1", "lk1", "lq2", "lk2")] + [p["subln_g"].reshape(1, -1)]
    lam_specs = [vec(HEAD)] * 4 + [vec(PAIR)]
    blk = pl.BlockSpec((seq, PAIR), lambda b, h: (b, h))
    o_prompt = pl.pallas_call(
        functools.partial(_attn_prompt_kernel, lam_init=lam_init),
        out_shape=jax.ShapeDtypeStruct((n_prompt * seq, inner), F32),
        grid=(n_prompt, heads),
        in_specs=[blk, blk, blk] + lam_specs,
        out_specs=blk,
        compiler_params=_cparams(("arbitrary", "arbitrary")),
        name="diff_attn_prompt",
    )(q, k, v, *lam_args)

    past = cache_k.shape[1]
    qt = ROW_TILE
    n_qt = n_lat // qt
    q0 = n_prompt * seq // qt
    l0 = n_prompt * seq // n_lat
    cos, sin = _rope_tables(n_lat, grid_w)
    o_sample = pl.pallas_call(
        functools.partial(_attn_sample_kernel, lam_init=lam_init),
        out_shape=jax.ShapeDtypeStruct((n_lat_req * n_lat, inner), F32),
        grid=(n_lat_req, heads, n_qt),
        in_specs=[pl.BlockSpec((qt, PAIR), lambda b, h, t: (q0 + b * n_qt + t, h)),
                  pl.BlockSpec((n_lat, PAIR), lambda b, h, t: (l0 + b, h)),
                  pl.BlockSpec((n_lat, PAIR), lambda b, h, t: (l0 + b, h)),
                  pl.BlockSpec((None, past, PAIR), lambda b, h, t: (b, 0, h)),
                  pl.BlockSpec((None, past, PAIR), lambda b, h, t: (b, 0, h)),
                  pl.BlockSpec((qt, PAIR), lambda b, h, t: (t, 0)),
                  pl.BlockSpec((qt, PAIR), lambda b, h, t: (t, 0)),
                  pl.BlockSpec((n_lat, PAIR), lambda b, h, t: (0, 0)),
                  pl.BlockSpec((n_lat, PAIR), lambda b, h, t: (0, 0))] + lam_specs,
        out_specs=pl.BlockSpec((qt, PAIR), lambda b, h, t: (b * n_qt + t, h)),
        compiler_params=_cparams(("arbitrary", "arbitrary", "arbitrary")),
        name="diff_attn_sample",
    )(q, k, v, cache_k, cache_v, cos, sin, cos, sin, *lam_args)
    return jnp.concatenate([o_prompt, o_sample], axis=0)


def kernel(x_prompt, x_sample, state_wkv, cache_k, cache_v, c, c_ctx, mod_w, mod_b, ln1_g, ln1_b, ln2_g, ln2_b, ev_in_w, ev_pw_b, ev_ts_mu, ev_w0, ev_w_up, ev_a0, ev_a_up, ev_g_up, ev_k_k, ev_k_a, ev_r_k, ev_gn_g, ev_gn_b, ev_dw_w, ev_dw_b, ev_cln_g, ev_cln_b, ev_out_w, od_in_w, od_lq1, od_lk1, od_lq2, od_lk2, od_subln_g, od_out_w, rg_w, rg_b, re_w, re_b, ffn_w_gate, ffn_w_up, ffn_w_down):
    nbp, seq, d = x_prompt.shape
    nbs, n_lat, _ = x_sample.shape
    depth = mod_w.shape[0]
    n_prompt_tok = nbp * seq
    geom = (n_prompt_tok, n_lat)
    alpha = (2 * depth) ** 0.25
    a_width = ev_w0.shape[-1]
    a_cols = ev_ts_mu.shape[-1]
    heads = a_width // HEAD
    grid_w = 64
    assert n_prompt_tok % MOE_TILE == 0 and n_lat % MOE_TILE == 0 and seq % ROW_TILE == 0
    assert 1 + nbs <= 8 and n_lat % ROW_TILE == 0

    x = jnp.concatenate([x_prompt.reshape(n_prompt_tok, d), x_sample.reshape(nbs * n_lat, d)], axis=0)
    cond = jnp.concatenate([c_ctx[None, :], c, jnp.zeros((8 - 1 - nbs, d), F32)], axis=0)
    mod = _ada_table(cond, mod_w, mod_b)

    new_wkv, new_k, new_v = [], [], []
    for l in range(depth):
        if l % 2 == 0:
            i = l // 2
            pa, pb = _mod_matmul(x, mod, l, ev_in_w[i], (a_cols, ev_in_w.shape[-1] - a_cols), geom)
            prm = dict(ts_mu=ev_ts_mu[i][None, :], w0=ev_w0[i], w_up=ev_w_up[i], a0=ev_a0[i], a_up=ev_a_up[i],
                       g_up=ev_g_up[i], k_k=ev_k_k[i][None, :], k_a=ev_k_a[i][None, :],
                       r_k=ev_r_k[i].reshape(1, a_width), gn_g=ev_gn_g[i][None, :], gn_b=ev_gn_b[i][None, :])
            zero_state = jnp.zeros((nbp, 2, heads, HEAD, HEAD), F32)
            oa_p, st = _rwkv_mix(pa, 0, nbp, seq, zero_state, prm)
            oa_s, _ = _rwkv_mix(pa, n_prompt_tok, nbs, n_lat, state_wkv[:, i], prm)
            conv = (ev_pw_b[i], ev_dw_w[i], ev_dw_b[i], ev_cln_g[i], ev_cln_b[i])
            ob_p = _conv_module(pb, 0, nbp, seq, *conv)
            ob_s = _conv_module(pb, n_prompt_tok, nbs, n_lat, *conv)
            hs = [jnp.concatenate([oa_p, oa_s], axis=0), jnp.concatenate([ob_p, ob_s], axis=0)]
            out_w = ev_out_w[i]
            new_wkv.append(st)
        else:
            j = l // 2
            lam_init = 0.8 - 0.6 * math.exp(-0.3 * l)
            inner = od_in_w.shape[-1] // 3
            q, k, v = _mod_matmul(x, mod, l, od_in_w[j], (inner, inner, inner), geom)
            prm = dict(lq1=od_lq1[j], lk1=od_lk1[j], lq2=od_lq2[j], lk2=od_lk2[j], subln_g=od_subln_g[j])
            ck = cache_k[:, j].reshape(nbs, -1, inner)
            cv = cache_v[:, j].reshape(nbs, -1, inner)
            hs = [_attention(q, k, v, ck, cv, prm, lam_init, nbp, seq, nbs, n_lat, grid_w)]
            out_w = od_out_w[j]
            new_k.append(k[:n_prompt_tok].reshape(nbp, seq, inner // PAIR, PAIR))
            new_v.append(v[:n_prompt_tok].reshape(nbp, seq, inner // PAIR, PAIR))
        x = _out_proj_norm(hs, x, mod, l, out_w, ln1_g[l], ln1_b[l], alpha, geom)
        rw = jnp.concatenate([rg_w[l], re_w[l],
                              jnp.zeros((d, ROUTER_LANES - N_GROUPS - N_EXPERTS), F32)], axis=1)
        rb = jnp.concatenate([rg_b[l], re_b[l], jnp.zeros((ROUTER_LANES - N_GROUPS - N_EXPERTS,), F32)])[None, :]
        x = _moe_norm(x, mod, l, rw, rb, ffn_w_gate, ffn_w_up, ffn_w_down, ln2_g[l], ln2_b[l], alpha, geom)

    y_prompt = x[:n_prompt_tok].reshape(nbp, seq, d)
    y_sample = x[n_prompt_tok:].reshape(nbs, n_lat, d)
    return (y_prompt, y_sample, jnp.stack(new_wkv, axis=1), jnp.stack(new_k, axis=1), jnp.stack(new_v, axis=1))
```

```python
import functools
import math

import jax
import jax.numpy as jnp
from jax import lax
from jax.experimental import pallas as pl
from jax.experimental.pallas import tpu as pltpu

F32 = jnp.float32
BF16 = jnp.bfloat16

LANES = 128
HEAD = 64
PAIR = 2 * HEAD
CHUNK = 64
ROW_TILE = 256
MOE_TILE = 1024
CONV_W = 31
N_GROUPS = 4
EXPERTS_PER_GROUP = 8
N_EXPERTS = N_GROUPS * EXPERTS_PER_GROUP
ROUTER_LANES = 128
LN_EPS = 1e-5
GN_EPS = 64e-5
RMS_EPS = 1e-5
ROPE_BASE = 10000.0
EXP_M05 = math.exp(-0.5)
VMEM_LIMIT = 56 * 1024 * 1024


def _bdot(a, b):
    return jnp.dot(a.astype(BF16), b.astype(BF16), preferred_element_type=F32)


def _bdot_nt(a, b):
    return lax.dot_general(a.astype(BF16), b.astype(BF16), (((1,), (1,)), ((), ())),
                           preferred_element_type=F32)


def _split2(a):
    hi = a.astype(BF16)
    lo = (a - hi.astype(F32)).astype(BF16)
    return hi, lo


def _split3(a):
    hi = a.astype(BF16)
    r1 = a - hi.astype(F32)
    mid = r1.astype(BF16)
    lo = (r1 - mid.astype(F32)).astype(BF16)
    return hi, mid, lo


def _dot_f32ish(a, b):
    ah, al = _split2(a)
    bh, bl = _split2(b)
    d = lambda x, y: jnp.dot(x, y, preferred_element_type=F32)
    return d(ah, bh) + d(ah, bl) + d(al, bh)


def _dot_exact_lhs(m_bf16, a):
    h, m, l = _split3(a)
    d = lambda y: jnp.dot(m_bf16, y, preferred_element_type=F32)
    return d(h) + d(m) + d(l)


def _dot_exact_rhs(a, m_bf16):
    h, m, l = _split3(a)
    d = lambda x: jnp.dot(x, m_bf16, preferred_element_type=F32)
    return d(h) + d(m) + d(l)


def _layer_norm(z, g, b, eps):
    mu = jnp.mean(z, axis=-1, keepdims=True)
    zc = z - mu
    var = jnp.mean(zc * zc, axis=-1, keepdims=True)
    return zc * lax.rsqrt(var + eps) * g + b


def _cond_row(tile, tile_rows, n_prompt_tok, dec_seq):
    tok = tile * tile_rows
    return jnp.where(tok < n_prompt_tok, 0, 1 + (tok - n_prompt_tok) // dec_seq)


def _cast_rows(src_ref, dst_ref, rows_per_step=128):
    n = src_ref.shape[0]
    for s in range(0, n, rows_per_step):
        dst_ref[s:s + rows_per_step, :] = src_ref[s:s + rows_per_step, :].astype(dst_ref.dtype)


def _cparams(sem):
    return pltpu.CompilerParams(dimension_semantics=sem, vmem_limit_bytes=VMEM_LIMIT)


def _ada_kernel(cond_ref, w_ref, b_ref, o_ref):
    c = cond_ref[...]
    s = c * jax.nn.sigmoid(c)
    o_ref[0, 0] = _dot_f32ish(s, w_ref[0]) + b_ref[0, 0]


def _ada_table(cond8, mod_w, mod_b):
    depth, d, _ = mod_w.shape
    return pl.pallas_call(
        _ada_kernel,
        out_shape=jax.ShapeDtypeStruct((depth, 6, 8, d), F32),
        grid=(depth, 6),
        in_specs=[pl.BlockSpec((8, d), lambda l, j: (0, 0)),
                  pl.BlockSpec((1, d, d), lambda l, j: (l, 0, j)),
                  pl.BlockSpec((1, 1, 1, d), lambda l, j: (l, j, 0, 0))],
        out_specs=pl.BlockSpec((1, 1, 8, d), lambda l, j: (l, j, 0, 0)),
        compiler_params=_cparams(("arbitrary", "arbitrary")),
        name="ada_table",
    )(cond8, mod_w, mod_b.reshape(depth, 6, 1, d))


def _mod_spec(layer, which, d):
    return pl.BlockSpec((None, None, 8, d), lambda *_: (layer, which, 0, 0))


def _modmm_kernel(x_ref, sh_ref, sc_ref, w_ref, *rest, splits, geom):
    outs, wb = rest[:len(splits)], rest[len(splits)]
    i = pl.program_id(0)

    @pl.when(i == 0)
    def _():
        _cast_rows(w_ref, wb)

    r = _cond_row(i, ROW_TILE, *geom)
    h = x_ref[...] * (1.0 + sc_ref[pl.ds(r, 1), :]) + sh_ref[pl.ds(r, 1), :]
    hb = h.astype(BF16)
    off = 0
    for o_ref, n in zip(outs, splits):
        o_ref[...] = jnp.dot(hb, wb[:, off:off + n], preferred_element_type=F32)
        off += n


def _mod_matmul(x, mod, layer, w, splits, geom):
    n_tok, d = x.shape
    n_out = w.shape[1]
    return pl.pallas_call(
        functools.partial(_modmm_kernel, splits=splits, geom=geom),
        out_shape=[jax.ShapeDtypeStruct((n_tok, n), F32) for n in splits],
        grid=(n_tok // ROW_TILE,),
        in_specs=[pl.BlockSpec((ROW_TILE, d), lambda i: (i, 0)),
                  _mod_spec(layer, 0, d), _mod_spec(layer, 1, d),
                  pl.BlockSpec((d, n_out), lambda i: (0, 0), pipeline_mode=pl.Buffered(1))],
        out_specs=[pl.BlockSpec((ROW_TILE, n), lambda i: (i, 0)) for n in splits],
        scratch_shapes=[pltpu.VMEM((d, n_out), BF16)],
        compiler_params=_cparams(("arbitrary",)),
        name=f"mod_matmul_l{layer}",
    )(x, mod, mod, w)


def _outproj_kernel(*refs, n_in, alpha, geom):
    h_refs = refs[:n_in]
    x_ref, gate_ref, w_ref, g_ref, b_ref, o_ref, wb = refs[n_in:]
    i = pl.program_id(0)

    @pl.when(i == 0)
    def _():
        _cast_rows(w_ref, wb)

    r = _cond_row(i, ROW_TILE, *geom)
    y = None
    off = 0
    for h_ref in h_refs:
        n = h_ref.shape[1]
        part = jnp.dot(h_ref[...].astype(BF16), wb[off:off + n, :], preferred_element_type=F32)
        y = part if y is None else y + part
        off += n
    z = alpha * x_ref[...] + gate_ref[pl.ds(r, 1), :] * y
    o_ref[...] = _layer_norm(z, g_ref[...], b_ref[...], LN_EPS)


def _out_proj_norm(hs, x, mod, layer, w, ln_g, ln_b, alpha, geom):
    n_tok, d = x.shape
    return pl.pallas_call(
        functools.partial(_outproj_kernel, n_in=len(hs), alpha=alpha, geom=geom),
        out_shape=jax.ShapeDtypeStruct((n_tok, d), F32),
        grid=(n_tok // ROW_TILE,),
        in_specs=[pl.BlockSpec((ROW_TILE, h.shape[1]), lambda i: (i, 0)) for h in hs] + [
            pl.BlockSpec((ROW_TILE, d), lambda i: (i, 0)),
            _mod_spec(layer, 2, d),
            pl.BlockSpec(w.shape, lambda i: (0, 0), pipeline_mode=pl.Buffered(1)),
            pl.BlockSpec((1, d), lambda i: (0, 0)),
            pl.BlockSpec((1, d), lambda i: (0, 0))],
        out_specs=pl.BlockSpec((ROW_TILE, d), lambda i: (i, 0)),
        scratch_shapes=[pltpu.VMEM(w.shape, BF16)],
        compiler_params=_cparams(("arbitrary",)),
        name=f"out_proj_norm_l{layer}",
    )(*hs, x, mod, w, ln_g.reshape(1, d), ln_b.reshape(1, d))


def _route(logits):
    neg = jnp.float32(-3e38)
    big = jnp.int32(1 << 20)
    lane = lax.broadcasted_iota(jnp.int32, logits.shape, 1)
    is_g = lane < N_GROUPS
    gl = jnp.where(is_g, logits, neg)
    gmax = jnp.max(gl, axis=-1, keepdims=True)
    gidx = jnp.min(jnp.where(gl == gmax, lane, big), axis=-1, keepdims=True)
    gsum = jnp.sum(jnp.where(is_g, jnp.exp(gl - gmax), 0.0), axis=-1, keepdims=True)
    g_p = 1.0 / gsum
    lo = N_GROUPS + gidx * EXPERTS_PER_GROUP
    in_grp = (lane >= lo) & (lane < lo + EXPERTS_PER_GROUP)
    el = jnp.where(in_grp, logits, neg)
    v1 = jnp.max(el, axis=-1, keepdims=True)
    i1 = jnp.min(jnp.where(el == v1, lane, big), axis=-1, keepdims=True)
    el2 = jnp.where(lane == i1, neg, el)
    v2 = jnp.max(el2, axis=-1, keepdims=True)
    i2 = jnp.min(jnp.where(el2 == v2, lane, big), axis=-1, keepdims=True)
    e21 = jnp.exp(v2 - v1)
    den = 1.0 + e21
    w1 = g_p / den
    w2 = g_p * e21 / den
    return jnp.where(lane == i1, w1, 0.0) + jnp.where(lane == i2, w2, 0.0)


def _moe_kernel(x_ref, sh_ref, sc_ref, gate_ref, rw_ref, rb_ref, wg_ref, wu_ref, wd_ref, g_ref, b_ref,
                o_ref, h_scr, gates_scr, acc_scr, wgb, wub, wdb, *, alpha, geom):
    i = pl.program_id(0)
    e = pl.program_id(1)
    r = _cond_row(i, MOE_TILE, *geom)
    sub = ROW_TILE
    n_sub = MOE_TILE // sub

    @pl.when(e == 0)
    def _():
        for s in range(n_sub):
            rows = slice(s * sub, (s + 1) * sub)
            h = x_ref[rows, :] * (1.0 + sc_ref[pl.ds(r, 1), :]) + sh_ref[pl.ds(r, 1), :]
            h_scr[rows, :] = h.astype(BF16)
            logits = _dot_f32ish(h, rw_ref[...]) + rb_ref[...]
            gates_scr[rows, :] = _route(logits)
            acc_scr[rows, :] = jnp.zeros((sub, acc_scr.shape[1]), F32)

    wgb[...] = wg_ref[0].astype(BF16)
    wub[...] = wu_ref[0].astype(BF16)
    wdb[...] = wd_ref[0].astype(BF16)
    lane = lax.broadcasted_iota(jnp.int32, (sub, ROUTER_LANES), 1)
    for s in range(n_sub):
        rows = slice(s * sub, (s + 1) * sub)
        hb = h_scr[rows, :]
        ge = jnp.sum(jnp.where(lane == e + N_GROUPS, gates_scr[rows, :], 0.0), axis=-1, keepdims=True)
        a = jnp.dot(hb, wgb[...], preferred_element_type=F32)
        u = jnp.dot(hb, wub[...], preferred_element_type=F32)
        hid = (a * jax.nn.sigmoid(a)) * u * ge
        acc_scr[rows, :] += jnp.dot(hid.astype(BF16), wdb[...], preferred_element_type=F32)

    @pl.when(e == N_EXPERTS - 1)
    def _():
        for s in range(n_sub):
            rows = slice(s * sub, (s + 1) * sub)
            z = alpha * x_ref[rows, :] + gate_ref[pl.ds(r, 1), :] * acc_scr[rows, :]
            o_ref[rows, :] = _layer_norm(z, g_ref[...], b_ref[...], LN_EPS)


def _moe_norm(x, mod, layer, rw, rb, w_gate, w_up, w_down, ln_g, ln_b, alpha, geom):
    n_tok, d = x.shape
    ff = w_gate.shape[-1]
    const = lambda i, e: (0, 0)
    return pl.pallas_call(
        functools.partial(_moe_kernel, alpha=alpha, geom=geom),
        out_shape=jax.ShapeDtypeStruct((n_tok, d), F32),
        grid=(n_tok // MOE_TILE, N_EXPERTS),
        in_specs=[pl.BlockSpec((MOE_TILE, d), lambda i, e: (i, 0)),
                  _mod_spec(layer, 3, d), _mod_spec(layer, 4, d), _mod_spec(layer, 5, d),
                  pl.BlockSpec((d, ROUTER_LANES), const),
                  pl.BlockSpec((1, ROUTER_LANES), const),
                  pl.BlockSpec((None, 1, d, ff), lambda i, e: (layer, e, 0, 0)),
                  pl.BlockSpec((None, 1, d, ff), lambda i, e: (layer, e, 0, 0)),
                  pl.BlockSpec((None, 1, ff, d), lambda i, e: (layer, e, 0, 0)),
                  pl.BlockSpec((1, d), const),
                  pl.BlockSpec((1, d), const)],
        out_specs=pl.BlockSpec((MOE_TILE, d), lambda i, e: (i, 0)),
        scratch_shapes=[pltpu.VMEM((MOE_TILE, d), BF16),
                        pltpu.VMEM((MOE_TILE, ROUTER_LANES), F32),
                        pltpu.VMEM((MOE_TILE, d), F32),
                        pltpu.VMEM((d, ff), BF16), pltpu.VMEM((d, ff), BF16), pltpu.VMEM((ff, d), BF16)],
        compiler_params=_cparams(("arbitrary", "arbitrary")),
        name=f"moe_norm_l{layer}",
    )(x, mod, mod, mod, rw, rb, w_gate, w_up, w_down, ln_g.reshape(1, d), ln_b.reshape(1, d))


def _rwkv_kernel(r_ref, k_ref, v_ref, lora_ref, mur_ref, muk_ref, muv_ref, mul_ref,
                 w0_ref, wup_ref, a0_ref, aup_ref, gup_ref, kk_ref, ka_ref, rk_ref, gng_ref, gnb_ref, s0_ref,
                 out_ref, sfin_ref,
                 lw_s, kz_s, bz_s, rr_s, vv_s, kn_s,
                 abar_s, rt_s, wt_s, vst_s, p_s, tm_s, mk_s, nkb_s, txr_s, tx_s, gl_s,
                 g_s, h_s, q_s, y0_s, y_s, st_s, *, seq_len, unroll):
    t_len = seq_len
    n_chunks = t_len // CHUNK
    c2 = 2 * CHUNK

    row = lax.broadcasted_iota(jnp.int32, (t_len, 1), 0)

    def tshift(x, mu):
        prev = jnp.where(row == 0, 0.0, pltpu.roll(x, 1, 0))
        nxt = jnp.where(row == t_len - 1, 0.0, pltpu.roll(x, t_len - 1, 0))
        return x + mu * (0.5 * (prev + nxt) - x)

    r = tshift(r_ref[...], mur_ref[...])
    k = tshift(k_ref[...], muk_ref[...])
    v = tshift(v_ref[...], muv_ref[...])
    lo = tshift(lora_ref[...], mul_ref[...])
    wd = jnp.tanh(lo[:, 0:PAIR])
    ad = lo[:, PAIR:2 * PAIR]
    gate = _bdot(jax.nn.sigmoid(lo[:, 2 * PAIR:3 * PAIR]), gup_ref[...])

    ri = lax.broadcasted_iota(jnp.int32, (PAIR, PAIR), 0)
    ci = lax.broadcasted_iota(jnp.int32, (PAIR, PAIR), 1)
    same_head = (ri // HEAD) == (ci // HEAD)
    ones_bd = jnp.where(same_head, 1.0, 0.0).astype(BF16)
    eye = ri == ci

    kk = k * kk_ref[...]
    ssq = _dot_exact_rhs(kk * kk, ones_bd)
    kk = kk * lax.rsqrt(jnp.maximum(ssq, 1e-24))
    wup2 = wup_ref[...].reshape(PAIR, PAIR)
    aup2 = aup_ref[...].reshape(PAIR, PAIR)
    bonus = jnp.zeros((t_len, PAIR), F32)
    for z in range(2):
        zrows = (ri // HEAD) == z
        w_pre = w0_ref[z:z + 1, :] + _bdot(wd, jnp.where(zrows, wup2, 0.0))
        lw_s[z] = -EXP_M05 * jax.nn.sigmoid(w_pre)
        a = jax.nn.sigmoid(a0_ref[z:z + 1, :] + _bdot(ad, jnp.where(zrows, aup2, 0.0)))
        kz = k * (1.0 + (a - 1.0) * ka_ref[...])
        kz_s[z] = kz
        bz_s[z] = kk * a
        bonus = bonus + _dot_exact_rhs(r * kz * rk_ref[...], ones_bd) * v
    rr_s[...] = r
    vv_s[...] = v
    kn_s[...] = kk

    rt_i = ri % CHUNK
    cs_i = ci % CHUNK
    strict = (cs_i < rt_i, cs_i > rt_i)
    incl = (cs_i <= rt_i, cs_i >= rt_i)
    t64 = lax.broadcasted_iota(jnp.int32, (CHUNK, CHUNK), 0)
    s64 = lax.broadcasted_iota(jnp.int32, (CHUNK, CHUNK), 1)
    tri = (jnp.where(s64 <= t64, 1.0, 0.0).astype(BF16), jnp.where(s64 >= t64, 1.0, 0.0).astype(BF16))
    lane_head0 = lax.broadcasted_iota(jnp.int32, (CHUNK, PAIR), 1) < HEAD
    eye_f = jnp.where(eye, 1.0, 0.0)

    def stack(x):
        return jnp.concatenate([jnp.where(lane_head0, x, 0.0), jnp.where(lane_head0, 0.0, x)], axis=0)

    def fold(x):
        return x[:CHUNK] + x[CHUNK:]

    def chunk_rows(c):
        return pl.ds(pl.multiple_of(c * CHUNK, CHUNK), CHUNK)

    def each_item(body):
        for z in range(2):
            lax.fori_loop(0, n_chunks, lambda c, carry, z=z: (body(z, c), carry)[1], 0, unroll=unroll)

    def stage_operands(z, c):
        rows = chunk_rows(c)
        r_c, kn_c = rr_s[rows, :], kn_s[rows, :]
        lw_c, kz_c, bz_c = lw_s[z, rows, :], kz_s[z, rows, :], bz_s[z, rows, :]
        lg = _dot_exact_lhs(tri[z], lw_c)
        total = lg[CHUNK - 1:CHUNK, :] if z == 0 else lg[0:1, :]
        e_neg = jnp.exp(-lg)
        e_rem = jnp.exp(total - lg)
        abar = stack(kn_c * jnp.exp(lg - lw_c))
        rt = stack(r_c * jnp.exp(lg))
        big = _bdot_nt(jnp.concatenate([abar, rt], axis=0),
                       jnp.concatenate([stack(bz_c * e_neg), stack(kz_c * e_neg)], axis=0))
        mb = jnp.where(strict[z], big[:c2, :c2], 0.0)
        abar_s[z, c] = abar.astype(BF16)
        rt_s[z, c] = rt
        wt_s[z, c] = jnp.concatenate([stack(kz_c * e_rem), -stack(bz_c * e_rem)], axis=0).T.astype(BF16)
        gl_s[z, c] = jnp.exp(total)
        tm_s[z, c] = eye_f - mb
        p_s[z, c] = mb
        mk_s[z, c] = jnp.where(strict[z], big[:c2, c2:], 0.0).astype(BF16)
        nkb_s[z, c] = jnp.concatenate([jnp.where(incl[z], big[c2:, c2:], 0.0),
                                       -jnp.where(incl[z], big[c2:, :c2], 0.0)], axis=1).astype(BF16)

    def stage_values(c, carry):
        vst_s[c] = stack(vv_s[chunk_rows(c), :]).astype(BF16)
        return carry

    lax.fori_loop(0, n_chunks, stage_values, 0, unroll=unroll)
    each_item(stage_operands)

    def stage_square(z, c):
        mb = p_s[z, c]
        p_s[z, c] = _dot_f32ish(mb, mb)
        mkv = jnp.dot(mk_s[z, c], vst_s[c], preferred_element_type=F32)
        txr_s[z, c] = jnp.concatenate([abar_s[z, c], mkv.astype(BF16)], axis=1)

    each_item(stage_square)

    n_round = CHUNK.bit_length() - 2
    for j in range(n_round):
        def stage_double(z, c, last=(j == n_round - 1)):
            tm, p = tm_s[z, c], p_s[z, c]
            tm_s[z, c] = tm + _dot_f32ish(tm, p)
            if not last:
                p_s[z, c] = _dot_f32ish(p, p)

        each_item(stage_double)

    def stage_apply(z, c):
        tx_s[z, c] = jnp.dot(tm_s[z, c].astype(BF16), txr_s[z, c], preferred_element_type=F32)

    each_item(stage_apply)

    def stage_maps(z, c):
        tx = tx_s[z, c]
        ta = tx[:, :PAIR].astype(BF16)
        x1 = tx[:, PAIR:].astype(BF16)
        z2 = jnp.concatenate([vst_s[c], x1], axis=0)
        wt = wt_s[z, c]
        h_s[z, c] = jnp.dot(wt, z2, preferred_element_type=F32)
        g_s[z, c] = jnp.where(eye, gl_s[z, c], 0.0) + jnp.dot(wt[:, c2:], ta, preferred_element_type=F32)
        nkb = nkb_s[z, c]
        q_s[z, c] = fold(rt_s[z, c] + jnp.dot(nkb[:, c2:], ta, preferred_element_type=F32))
        y0_s[z, c] = fold(jnp.dot(nkb, z2, preferred_element_type=F32))

    each_item(stage_maps)

    zero64 = jnp.zeros((HEAD, HEAD), F32)
    for z in range(2):
        a = jnp.concatenate([jnp.concatenate([s0_ref[z, 0], zero64], axis=1),
                             jnp.concatenate([zero64, s0_ref[z, 1]], axis=1)], axis=0)
        st_s[z] = a.T

    def propagate(i, carry):
        for z in range(2):
            c = i if z == 0 else n_chunks - 1 - i
            s = st_s[z].astype(BF16)
            y_s[z, chunk_rows(c), :] = _bdot(q_s[z, c], s) + y0_s[z, c]
            st_s[z] = _bdot(g_s[z, c], s) + h_s[z, c]
        return carry

    lax.fori_loop(0, n_chunks, propagate, 0)
    for z in range(2):
        a = st_s[z].T
        sfin_ref[z, 0] = a[:HEAD, :HEAD]
        sfin_ref[z, 1] = a[HEAD:, HEAD:]

    y = y_s[0] + y_s[1]
    mu = _dot_exact_rhs(y, ones_bd) * (1.0 / HEAD)
    yc = y - mu
    var = _dot_exact_rhs(yc * yc, ones_bd) * (1.0 / HEAD)
    yn = yc * lax.rsqrt(var + GN_EPS) * gng_ref[...] + gnb_ref[...]
    out_ref[...] = (yn + bonus) * gate


def _rwkv_mix(pa, first_tok, n_seq, seq_len, s0, p):
    a_width = p["w0"].shape[-1]
    n_pairs = a_width // PAIR
    heads = a_width // HEAD
    t0 = first_tok // seq_len
    lora_w = 3 * PAIR
    lora_blk = 3 * a_width // lora_w
    n_chunks = seq_len // CHUNK
    col = lambda off: (lambda s, q: (t0 + s, off + q))
    vec = lambda off: (lambda s, q: (0, off + q))
    in_specs = [
        pl.BlockSpec((seq_len, PAIR), col(0)),
        pl.BlockSpec((seq_len, PAIR), col(n_pairs)),
        pl.BlockSpec((seq_len, PAIR), col(2 * n_pairs)),
        pl.BlockSpec((seq_len, lora_w), lambda s, q: (t0 + s, lora_blk)),
        pl.BlockSpec((1, PAIR), vec(0)),
        pl.BlockSpec((1, PAIR), vec(n_pairs)),
        pl.BlockSpec((1, PAIR), vec(2 * n_pairs)),
        pl.BlockSpec((1, lora_w), lambda s, q: (0, lora_blk)),
        pl.BlockSpec((2, PAIR), vec(0)),
        pl.BlockSpec((2, HEAD, PAIR), lambda s, q: (0, 0, q)),
        pl.BlockSpec((2, PAIR), vec(0)),
        pl.BlockSpec((2, HEAD, PAIR), lambda s, q: (0, 0, q)),
        pl.BlockSpec((PAIR, PAIR), vec(0)),
        pl.BlockSpec((1, PAIR), vec(0)),
        pl.BlockSpec((1, PAIR), vec(0)),
        pl.BlockSpec((1, PAIR), vec(0)),
        pl.BlockSpec((1, PAIR), vec(0)),
        pl.BlockSpec((1, PAIR), vec(0)),
        pl.BlockSpec((None, 2, 2, HEAD, HEAD), lambda s, q: (s, 0, q, 0, 0)),
    ]
    seq2 = lambda: pltpu.VMEM((2, seq_len, PAIR), F32)
    seq1 = lambda: pltpu.VMEM((seq_len, PAIR), F32)
    item = lambda rows, cols, dt: pltpu.VMEM((2, n_chunks, rows, cols), dt)
    out, s_fin = pl.pallas_call(
        functools.partial(_rwkv_kernel, seq_len=seq_len, unroll=min(n_chunks, 4)),
        out_shape=[jax.ShapeDtypeStruct((n_seq * seq_len, a_width), F32),
                   jax.ShapeDtypeStruct((n_seq, 2, heads, HEAD, HEAD), F32)],
        grid=(n_seq, n_pairs),
        in_specs=in_specs,
        out_specs=[pl.BlockSpec((seq_len, PAIR), lambda s, q: (s, q)),
                   pl.BlockSpec((None, 2, 2, HEAD, HEAD), lambda s, q: (s, 0, q, 0, 0))],
        scratch_shapes=[seq2(), seq2(), seq2(), seq1(), seq1(), seq1(),
                        item(PAIR, PAIR, BF16),
                        item(PAIR, PAIR, F32),
                        item(PAIR, 2 * PAIR, BF16),
                        pltpu.VMEM((n_chunks, PAIR, PAIR), BF16),
                        item(PAIR, PAIR, F32),
                        item(PAIR, PAIR, F32),
                        item(PAIR, PAIR, BF16),
                        item(PAIR, 2 * PAIR, BF16),
                        item(PAIR, 2 * PAIR, BF16),
                        item(PAIR, 2 * PAIR, F32),
                        item(1, PAIR, F32),
                        item(PAIR, PAIR, F32), item(PAIR, PAIR, F32),
                        item(CHUNK, PAIR, F32), item(CHUNK, PAIR, F32),
                        seq2(), pltpu.VMEM((2, PAIR, PAIR), F32)],
        compiler_params=_cparams(("arbitrary", "arbitrary")),
        name=f"rwkv_mix_t{seq_len}",
    )(pa, pa, pa, pa, p["ts_mu"], p["ts_mu"], p["ts_mu"], p["ts_mu"],
      p["w0"], p["w_up"], p["a0"], p["a_up"], p["g_up"], p["k_k"], p["k_a"], p["r_k"], p["gn_g"], p["gn_b"], s0)
    return out, s_fin


def _conv_kernel(pb_ref, pwb_ref, dw_ref, dwb_ref, g_ref, b_ref, o_ref, pad_s, *, seq_len):
    width = o_ref.shape[1]
    halo = 16
    sub = 64
    pbv = pb_ref[...] + pwb_ref[...]
    glu = pbv[:, :width] * jax.nn.sigmoid(pbv[:, width:])
    pad_s[0:halo, :] = jnp.zeros((halo, width), F32)
    pad_s[halo:halo + seq_len, :] = glu
    pad_s[halo + seq_len:, :] = jnp.zeros((halo, width), F32)
    first = halo - CONV_W // 2
    for c in range(seq_len // sub):
        acc = jnp.zeros((sub, width), F32)
        for d in range(CONV_W):
            start = c * sub + first + d
            acc = acc + pad_s[start:start + sub, :] * dw_ref[d:d + 1, :]
        y = _layer_norm(acc + dwb_ref[...], g_ref[...], b_ref[...], LN_EPS)
        o_ref[c * sub:(c + 1) * sub, :] = y * jax.nn.sigmoid(y)


def _conv_module(pb, first_tok, n_seq, seq_len, pw_b, dw_w, dw_b, cln_g, cln_b):
    width = dw_w.shape[1]
    t0 = first_tok // seq_len
    const = lambda s: (0, 0)
    return pl.pallas_call(
        functools.partial(_conv_kernel, seq_len=seq_len),
        out_shape=jax.ShapeDtypeStruct((n_seq * seq_len, width), F32),
        grid=(n_seq,),
        in_specs=[pl.BlockSpec((seq_len, 2 * width), lambda s: (t0 + s, 0)),
                  pl.BlockSpec((1, 2 * width), const),
                  pl.BlockSpec((CONV_W, width), const),
                  pl.BlockSpec((1, width), const),
                  pl.BlockSpec((1, width), const),
                  pl.BlockSpec((1, width), const)],
        out_specs=pl.BlockSpec((seq_len, width), lambda s: (s, 0)),
        scratch_shapes=[pltpu.VMEM((seq_len + 32, width), F32)],
        compiler_params=_cparams(("arbitrary",)),
        name=f"conv_module_t{seq_len}",
    )(pb, pw_b.reshape(1, -1), dw_w, dw_b.reshape(1, -1), cln_g.reshape(1, -1), cln_b.reshape(1, -1))


def _diff_lambda(lq1, lk1, lq2, lk2, lam_init):
    dot = lambda a, b: jnp.exp(jnp.sum(a[...] * b[...], axis=-1, keepdims=True))
    return dot(lq1, lk1) - dot(lq2, lk2) + lam_init


def _diff_attend(q, keys, vals, lam, lam_init, subln):
    nq = q.shape[0]
    lane_first = lax.broadcasted_iota(jnp.int32, q.shape, 1) < HEAD
    q_st = jnp.concatenate([jnp.where(lane_first, q, 0.0), jnp.where(lane_first, 0.0, q)], axis=0)
    s = _bdot_nt(q_st, keys) * (HEAD ** -0.5)
    s = s - jnp.max(s, axis=-1, keepdims=True)
    pexp = jnp.exp(s)
    prob = pexp / jnp.sum(pexp, axis=-1, keepdims=True)
    attn = prob[:nq] - lam * prob[nq:]
    o = _bdot(attn, vals)
    o = o * lax.rsqrt(jnp.mean(o * o, axis=-1, keepdims=True) + RMS_EPS)
    return o * subln * (1.0 - lam_init)


def _attn_prompt_kernel(q_ref, k_ref, v_ref, lq1, lk1, lq2, lk2, sub_ref, o_ref, *, lam_init):
    lam = _diff_lambda(lq1, lk1, lq2, lk2, lam_init)
    o_ref[...] = _diff_attend(q_ref[...], k_ref[...], v_ref[...], lam, lam_init, sub_ref[...])


def _rope(x, cos, sin_signed):
    lane = lax.broadcasted_iota(jnp.int32, x.shape, 1)
    quarter = HEAD // 4
    partner = jnp.where((lane // quarter) % 2 == 0,
                        pltpu.roll(x, LANES - quarter, 1), pltpu.roll(x, quarter, 1))
    return x * cos + partner * sin_signed


def _attn_sample_kernel(q_ref, k_ref, v_ref, ck_ref, cv_ref, cq_ref, sq_ref, cosk_ref, sink_ref,
                        lq1, lk1, lq2, lk2, sub_ref, o_ref, *, lam_init):
    lam = _diff_lambda(lq1, lk1, lq2, lk2, lam_init)
    q = _rope(q_ref[...], cq_ref[...], sq_ref[...])
    kl = _rope(k_ref[...], cosk_ref[...], sink_ref[...])
    keys = jnp.concatenate([ck_ref[...], kl], axis=0)
    vals = jnp.concatenate([cv_ref[...], v_ref[...]], axis=0)
    o_ref[...] = _diff_attend(q, keys, vals, lam, lam_init, sub_ref[...])


def _rope_tables(n_lat, grid_w):
    quarter = HEAD // 4
    inv = ROPE_BASE ** (-jnp.arange(quarter, dtype=F32) / quarter)
    t = jnp.arange(n_lat)
    row_ang = (t // grid_w).astype(F32)[:, None] * inv
    col_ang = (t % grid_w).astype(F32)[:, None] * inv
    ang = jnp.concatenate([row_ang, row_ang, col_ang, col_ang] * 2, axis=-1)
    sign = jnp.tile(jnp.concatenate([-jnp.ones(quarter, F32), jnp.ones(quarter, F32)]), 2 * HEAD // (2 * quarter))
    return jnp.cos(ang), jnp.sin(ang) * sign


def _attention(q, k, v, cache_k, cache_v, p, lam_init, n_prompt, seq, n_lat_req, n_lat, grid_w):
    n_tok, inner = q.shape
    heads = inner // PAIR
    vec = lambda n: pl.BlockSpec((1, n), lambda *_: (0, 0))
    lam_args = [p[n].reshape(1, -1) for n in ("lq1", "lk1", "lq2", "lk2")] + [p["subln_g"].reshape(1, -1)]
    lam_specs = [vec(HEAD)] * 4 + [vec(PAIR)]
    blk = pl.BlockSpec((seq, PAIR), lambda b, h: (b, h))
    o_prompt = pl.pallas_call(
        functools.partial(_attn_prompt_kernel, lam_init=lam_init),
        out_shape=jax.ShapeDtypeStruct((n_prompt * seq, inner), F32),
        grid=(n_prompt, heads),
        in_specs=[blk, blk, blk] + lam_specs,
        out_specs=blk,
        compiler_params=_cparams(("arbitrary", "arbitrary")),
        name="diff_attn_prompt",
    )(q, k, v, *lam_args)

    past = cache_k.shape[1]
    qt = ROW_TILE
    n_qt = n_lat // qt
    q0 = n_prompt * seq // qt
    l0 = n_prompt * seq // n_lat
    cos, sin = _rope_tables(n_lat, grid_w)
    o_sample = pl.pallas_call(
        functools.partial(_attn_sample_kernel, lam_init=lam_init),
        out_shape=jax.ShapeDtypeStruct((n_lat_req * n_lat, inner), F32),
        grid=(n_lat_req, heads, n_qt),
        in_specs=[pl.BlockSpec((qt, PAIR), lambda b, h, t: (q0 + b * n_qt + t, h)),
                  pl.BlockSpec((n_lat, PAIR), lambda b, h, t: (l0 + b, h)),
                  pl.BlockSpec((n_lat, PAIR), lambda b, h, t: (l0 + b, h)),
                  pl.BlockSpec((None, past, PAIR), lambda b, h, t: (b, 0, h)),
                  pl.BlockSpec((None, past, PAIR), lambda b, h, t: (b, 0, h)),
                  pl.BlockSpec((qt, PAIR), lambda b, h, t: (t, 0)),
                  pl.BlockSpec((qt, PAIR), lambda b, h, t: (t, 0)),
                  pl.BlockSpec((n_lat, PAIR), lambda b, h, t: (0, 0)),
                  pl.BlockSpec((n_lat, PAIR), lambda b, h, t: (0, 0))] + lam_specs,
        out_specs=pl.BlockSpec((qt, PAIR), lambda b, h, t: (b * n_qt + t, h)),
        compiler_params=_cparams(("arbitrary", "arbitrary", "arbitrary")),
        name="diff_attn_sample",
    )(q, k, v, cache_k, cache_v, cos, sin, cos, sin, *lam_args)
    return jnp.concatenate([o_prompt, o_sample], axis=0)


def kernel(x_prompt, x_sample, state_wkv, cache_k, cache_v, c, c_ctx, mod_w, mod_b, ln1_g, ln1_b, ln2_g, ln2_b, ev_in_w, ev_pw_b, ev_ts_mu, ev_w0, ev_w_up, ev_a0, ev_a_up, ev_g_up, ev_k_k, ev_k_a, ev_r_k, ev_gn_g, ev_gn_b, ev_dw_w, ev_dw_b, ev_cln_g, ev_cln_b, ev_out_w, od_in_w, od_lq1, od_lk1, od_lq2, od_lk2, od_subln_g, od_out_w, rg_w, rg_b, re_w, re_b, ffn_w_gate, ffn_w_up, ffn_w_down):
    nbp, seq, d = x_prompt.shape
    nbs, n_lat, _ = x_sample.shape
    depth = mod_w.shape[0]
    n_prompt_tok = nbp * seq
    geom = (n_prompt_tok, n_lat)
    alpha = (2 * depth) ** 0.25
    a_width = ev_w0.shape[-1]
    a_cols = ev_ts_mu.shape[-1]
    heads = a_width // HEAD
    grid_w = 64
    assert n_prompt_tok % MOE_TILE == 0 and n_lat % MOE_TILE == 0 and seq % ROW_TILE == 0
    assert 1 + nbs <= 8 and n_lat % ROW_TILE == 0

    x = jnp.concatenate([x_prompt.reshape(n_prompt_tok, d), x_sample.reshape(nbs * n_lat, d)], axis=0)
    cond = jnp.concatenate([c_ctx[None, :], c, jnp.zeros((8 - 1 - nbs, d), F32)], axis=0)
    mod = _ada_table(cond, mod_w, mod_b)

    new_wkv, new_k, new_v = [], [], []
    for l in range(depth):
        if l % 2 == 0:
            i = l // 2
            pa, pb = _mod_matmul(x, mod, l, ev_in_w[i], (a_cols, ev_in_w.shape[-1] - a_cols), geom)
            prm = dict(ts_mu=ev_ts_mu[i][None, :], w0=ev_w0[i], w_up=ev_w_up[i], a0=ev_a0[i], a_up=ev_a_up[i],
                       g_up=ev_g_up[i], k_k=ev_k_k[i][None, :], k_a=ev_k_a[i][None, :],
                       r_k=ev_r_k[i].reshape(1, a_width), gn_g=ev_gn_g[i][None, :], gn_b=ev_gn_b[i][None, :])
            zero_state = jnp.zeros((nbp, 2, heads, HEAD, HEAD), F32)
            oa_p, st = _rwkv_mix(pa, 0, nbp, seq, zero_state, prm)
            oa_s, _ = _rwkv_mix(pa, n_prompt_tok, nbs, n_lat, state_wkv[:, i], prm)
            conv = (ev_pw_b[i], ev_dw_w[i], ev_dw_b[i], ev_cln_g[i], ev_cln_b[i])
            ob_p = _conv_module(pb, 0, nbp, seq, *conv)
            ob_s = _conv_module(pb, n_prompt_tok, nbs, n_lat, *conv)
            hs = [jnp.concatenate([oa_p, oa_s], axis=0), jnp.concatenate([ob_p, ob_s], axis=0)]
            out_w = ev_out_w[i]
            new_wkv.append(st)
        else:
            j = l // 2
            lam_init = 0.8 - 0.6 * math.exp(-0.3 * l)
            inner = od_in_w.shape[-1] // 3
            q, k, v = _mod_matmul(x, mod, l, od_in_w[j], (inner, inner, inner), geom)
            prm = dict(lq1=od_lq1[j], lk1=od_lk1[j], lq2=od_lq2[j], lk2=od_lk2[j], subln_g=od_subln_g[j])
            ck = cache_k[:, j].reshape(nbs, -1, inner)
            cv = cache_v[:, j].reshape(nbs, -1, inner)
            hs = [_attention(q, k, v, ck, cv, prm, lam_init, nbp, seq, nbs, n_lat, grid_w)]
            out_w = od_out_w[j]
            new_k.append(k[:n_prompt_tok].reshape(nbp, seq, inner // PAIR, PAIR))
            new_v.append(v[:n_prompt_tok].reshape(nbp, seq, inner // PAIR, PAIR))
        x = _out_proj_norm(hs, x, mod, l, out_w, ln1_g[l], ln1_b[l], alpha, geom)
        rw = jnp.concatenate([rg_w[l], re_w[l],
                              jnp.zeros((d, ROUTER_LANES - N_GROUPS - N_EXPERTS), F32)], axis=1)
        rb = jnp.concatenate([rg_b[l], re_b[l], jnp.zeros((ROUTER_LANES - N_GROUPS - N_EXPERTS,), F32)])[None, :]
        x = _moe_norm(x, mod, l, rw, rb, ffn_w_gate, ffn_w_up, ffn_w_down, ln2_g[l], ln2_b[l], alpha, geom)

    y_prompt = x[:n_prompt_tok].reshape(nbp, seq, d)
    y_sample = x[n_prompt_tok:].reshape(nbs, n_lat, d)
    return (y_prompt, y_sample, jnp.stack(new_wkv, axis=1), jnp.stack(new_k, axis=1), jnp.stack(new_v, axis=1))
```

```python
import functools
import math

import jax
import jax.numpy as jnp
from jax import lax
from jax.experimental import pallas as pl
from jax.experimental.pallas import tpu as pltpu

F32 = jnp.float32
BF16 = jnp.bfloat16

LANES = 128
HEAD = 64
PAIR = 2 * HEAD
CHUNK = 64
ROW_TILE = 256
MOE_TILE = 1024
CONV_W = 31
N_GROUPS = 4
EXPERTS_PER_GROUP = 8
N_EXPERTS = N_GROUPS * EXPERTS_PER_GROUP
ROUTER_LANES = 128
LN_EPS = 1e-5
GN_EPS = 64e-5
RMS_EPS = 1e-5
ROPE_BASE = 10000.0
EXP_M05 = math.exp(-0.5)
VMEM_LIMIT = 56 * 1024 * 1024


def _bdot(a, b):
    return jnp.dot(a.astype(BF16), b.astype(BF16), preferred_element_type=F32)


def _bdot_nt(a, b):
    return lax.dot_general(a.astype(BF16), b.astype(BF16), (((1,), (1,)), ((), ())),
                           preferred_element_type=F32)


def _split2(a):
    hi = a.astype(BF16)
    lo = (a - hi.astype(F32)).astype(BF16)
    return hi, lo


def _split3(a):
    hi = a.astype(BF16)
    r1 = a - hi.astype(F32)
    mid = r1.astype(BF16)
    lo = (r1 - mid.astype(F32)).astype(BF16)
    return hi, mid, lo


def _dot_f32ish(a, b):
    ah, al = _split2(a)
    bh, bl = _split2(b)
    d = lambda x, y: jnp.dot(x, y, preferred_element_type=F32)
    return d(ah, bh) + d(ah, bl) + d(al, bh)


def _dot_exact_lhs(m_bf16, a):
    h, m, l = _split3(a)
    d = lambda y: jnp.dot(m_bf16, y, preferred_element_type=F32)
    return d(h) + d(m) + d(l)


def _dot_exact_rhs(a, m_bf16):
    h, m, l = _split3(a)
    d = lambda x: jnp.dot(x, m_bf16, preferred_element_type=F32)
    return d(h) + d(m) + d(l)


def _layer_norm(z, g, b, eps):
    mu = jnp.mean(z, axis=-1, keepdims=True)
    zc = z - mu
    var = jnp.mean(zc * zc, axis=-1, keepdims=True)
    return zc * lax.rsqrt(var + eps) * g + b


def _cond_row(tile, tile_rows, n_prompt_tok, dec_seq):
    tok = tile * tile_rows
    return jnp.where(tok < n_prompt_tok, 0, 1 + (tok - n_prompt_tok) // dec_seq)


def _cast_rows(src_ref, dst_ref, rows_per_step=128):
    n = src_ref.shape[0]
    for s in range(0, n, rows_per_step):
        dst_ref[s:s + rows_per_step, :] = src_ref[s:s + rows_per_step, :].astype(dst_ref.dtype)


def _cparams(sem):
    return pltpu.CompilerParams(dimension_semantics=sem, vmem_limit_bytes=VMEM_LIMIT)


def _ada_kernel(cond_ref, w_ref, b_ref, o_ref):
    c = cond_ref[...]
    s = c * jax.nn.sigmoid(c)
    o_ref[0, 0] = _dot_f32ish(s, w_ref[0]) + b_ref[0, 0]


def _ada_table(cond8, mod_w, mod_b):
    depth, d, _ = mod_w.shape
    return pl.pallas_call(
        _ada_kernel,
        out_shape=jax.ShapeDtypeStruct((depth, 6, 8, d), F32),
        grid=(depth, 6),
        in_specs=[pl.BlockSpec((8, d), lambda l, j: (0, 0)),
                  pl.BlockSpec((1, d, d), lambda l, j: (l, 0, j)),
                  pl.BlockSpec((1, 1, 1, d), lambda l, j: (l, j, 0, 0))],
        out_specs=pl.BlockSpec((1, 1, 8, d), lambda l, j: (l, j, 0, 0)),
        compiler_params=_cparams(("arbitrary", "arbitrary")),
        name="ada_table",
    )(cond8, mod_w, mod_b.reshape(depth, 6, 1, d))


def _mod_spec(layer, which, d):
    return pl.BlockSpec((None, None, 8, d), lambda *_: (layer, which, 0, 0))


def _group_specs(tile, width, n_prompt_tiles):
    return [pl.BlockSpec((tile, width), lambda i, *_: (jnp.minimum(i, n_prompt_tiles - 1), 0)),
            pl.BlockSpec((tile, width), lambda i, *_: (jnp.maximum(i - n_prompt_tiles, 0), 0))]


def _load_group(i, n_prompt_tiles, refs, rows=slice(None)):
    if len(refs) == 1:
        return refs[0][rows, :]
    return jnp.where(i < n_prompt_tiles, refs[0][rows, :], refs[1][rows, :])


def _store_group(i, n_prompt_tiles, refs, value, rows=slice(None)):
    if len(refs) == 1:
        refs[0][rows, :] = value
        return

    @pl.when(i < n_prompt_tiles)
    def _():
        refs[0][rows, :] = value

    @pl.when(i >= n_prompt_tiles)
    def _():
        refs[1][rows, :] = value


def _modmm_kernel(*refs, n_x, outs, geom):
    x_refs = refs[:n_x]
    sh_ref, sc_ref, w_ref = refs[n_x:n_x + 3]
    o_refs, wb = refs[n_x + 3:-1], refs[-1]
    i = pl.program_id(0)
    n_pt = geom[0] // ROW_TILE

    @pl.when(i == 0)
    def _():
        _cast_rows(w_ref, wb)

    r = _cond_row(i, ROW_TILE, *geom)
    h = _load_group(i, n_pt, x_refs) * (1.0 + sc_ref[pl.ds(r, 1), :]) + sh_ref[pl.ds(r, 1), :]
    hb = h.astype(BF16)
    off = 0
    k = 0
    for n, split in outs:
        y = jnp.dot(hb, wb[:, off:off + n], preferred_element_type=F32)
        _store_group(i, n_pt, o_refs[k:k + 1 + split], y)
        off += n
        k += 1 + split


def _mod_matmul(xs, mod, layer, w, outs, geom):
    d, n_out = w.shape
    n_prompt_tok = geom[0]
    n_tok = sum(x.shape[0] for x in xs)
    n_pt = n_prompt_tok // ROW_TILE
    whole = lambda n: [pl.BlockSpec((ROW_TILE, n), lambda i: (i, 0))]
    x_specs = whole(d) if len(xs) == 1 else _group_specs(ROW_TILE, d, n_pt)
    out_shape, out_specs = [], []
    for n, split in outs:
        if split:
            out_shape += [jax.ShapeDtypeStruct((n_prompt_tok, n), F32),
                          jax.ShapeDtypeStruct((n_tok - n_prompt_tok, n), F32)]
            out_specs += _group_specs(ROW_TILE, n, n_pt)
        else:
            out_shape += [jax.ShapeDtypeStruct((n_tok, n), F32)]
            out_specs += whole(n)
    return pl.pallas_call(
        functools.partial(_modmm_kernel, n_x=len(xs), outs=outs, geom=geom),
        out_shape=out_shape,
        grid=(n_tok // ROW_TILE,),
        in_specs=x_specs + [_mod_spec(layer, 0, d), _mod_spec(layer, 1, d),
                            pl.BlockSpec((d, n_out), lambda i: (0, 0), pipeline_mode=pl.Buffered(1))],
        out_specs=out_specs,
        scratch_shapes=[pltpu.VMEM((d, n_out), BF16)],
        compiler_params=_cparams(("arbitrary",)),
        name=f"mod_matmul_l{layer}",
    )(*xs, mod, mod, w)


def _outproj_kernel(*refs, arity, alpha, geom):
    groups, k = [], 0
    for n in arity:
        groups.append(refs[k:k + n])
        k += n
    gate_ref, w_ref, g_ref, b_ref, o_ref, wb = refs[k:]
    h_groups, x_group = groups[:-1], groups[-1]
    i = pl.program_id(0)
    n_pt = geom[0] // ROW_TILE

    @pl.when(i == 0)
    def _():
        _cast_rows(w_ref, wb)

    r = _cond_row(i, ROW_TILE, *geom)
    y = None
    off = 0
    for h_refs in h_groups:
        n = h_refs[0].shape[1]
        h = _load_group(i, n_pt, h_refs)
        part = jnp.dot(h.astype(BF16), wb[off:off + n, :], preferred_element_type=F32)
        y = part if y is None else y + part
        off += n
    z = alpha * _load_group(i, n_pt, x_group) + gate_ref[pl.ds(r, 1), :] * y
    o_ref[...] = _layer_norm(z, g_ref[...], b_ref[...], LN_EPS)


def _out_proj_norm(hs, xs, mod, layer, w, ln_g, ln_b, alpha, geom):
    d = w.shape[1]
    n_tok = sum(x.shape[0] for x in xs)
    n_pt = geom[0] // ROW_TILE
    specs = []
    for grp in list(hs) + [xs]:
        n = grp[0].shape[1]
        specs += [pl.BlockSpec((ROW_TILE, n), lambda i: (i, 0))] if len(grp) == 1 else _group_specs(ROW_TILE, n, n_pt)
    return pl.pallas_call(
        functools.partial(_outproj_kernel, arity=tuple(len(g) for g in list(hs) + [xs]), alpha=alpha, geom=geom),
        out_shape=jax.ShapeDtypeStruct((n_tok, d), F32),
        grid=(n_tok // ROW_TILE,),
        in_specs=specs + [
            _mod_spec(layer, 2, d),
            pl.BlockSpec(w.shape, lambda i: (0, 0), pipeline_mode=pl.Buffered(1)),
            pl.BlockSpec((1, d), lambda i: (0, 0)),
            pl.BlockSpec((1, d), lambda i: (0, 0))],
        out_specs=pl.BlockSpec((ROW_TILE, d), lambda i: (i, 0)),
        scratch_shapes=[pltpu.VMEM(w.shape, BF16)],
        compiler_params=_cparams(("arbitrary",)),
        name=f"out_proj_norm_l{layer}",
    )(*[a for g in hs for a in g], *xs, mod, w, ln_g.reshape(1, d), ln_b.reshape(1, d))


def _route(logits):
    neg = jnp.float32(-3e38)
    big = jnp.int32(1 << 20)
    lane = lax.broadcasted_iota(jnp.int32, logits.shape, 1)
    is_g = lane < N_GROUPS
    gl = jnp.where(is_g, logits, neg)
    gmax = jnp.max(gl, axis=-1, keepdims=True)
    gidx = jnp.min(jnp.where(gl == gmax, lane, big), axis=-1, keepdims=True)
    gsum = jnp.sum(jnp.where(is_g, jnp.exp(gl - gmax), 0.0), axis=-1, keepdims=True)
    g_p = 1.0 / gsum
    lo = N_GROUPS + gidx * EXPERTS_PER_GROUP
    in_grp = (lane >= lo) & (lane < lo + EXPERTS_PER_GROUP)
    el = jnp.where(in_grp, logits, neg)
    v1 = jnp.max(el, axis=-1, keepdims=True)
    i1 = jnp.min(jnp.where(el == v1, lane, big), axis=-1, keepdims=True)
    el2 = jnp.where(lane == i1, neg, el)
    v2 = jnp.max(el2, axis=-1, keepdims=True)
    i2 = jnp.min(jnp.where(el2 == v2, lane, big), axis=-1, keepdims=True)
    e21 = jnp.exp(v2 - v1)
    den = 1.0 + e21
    w1 = g_p / den
    w2 = g_p * e21 / den
    return jnp.where(lane == i1, w1, 0.0) + jnp.where(lane == i2, w2, 0.0)


def _moe_kernel(x_ref, sh_ref, sc_ref, gate_ref, rw_ref, rb_ref, wg_ref, wu_ref, wd_ref, g_ref, b_ref,
                *rest, alpha, geom):
    o_refs = rest[:-6]
    h_scr, gates_scr, acc_scr, wgb, wub, wdb = rest[-6:]
    i = pl.program_id(0)
    e = pl.program_id(1)
    r = _cond_row(i, MOE_TILE, *geom)
    n_pt = geom[0] // MOE_TILE
    sub = ROW_TILE
    n_sub = MOE_TILE // sub

    @pl.when(e == 0)
    def _():
        for s in range(n_sub):
            rows = slice(s * sub, (s + 1) * sub)
            h = x_ref[rows, :] * (1.0 + sc_ref[pl.ds(r, 1), :]) + sh_ref[pl.ds(r, 1), :]
            h_scr[rows, :] = h.astype(BF16)
            logits = _dot_f32ish(h, rw_ref[...]) + rb_ref[...]
            gates_scr[rows, :] = _route(logits)
            acc_scr[rows, :] = jnp.zeros((sub, acc_scr.shape[1]), F32)

    wgb[...] = wg_ref[0].astype(BF16)
    wub[...] = wu_ref[0].astype(BF16)
    wdb[...] = wd_ref[0].astype(BF16)
    lane = lax.broadcasted_iota(jnp.int32, (sub, ROUTER_LANES), 1)
    for s in range(n_sub):
        rows = slice(s * sub, (s + 1) * sub)
        hb = h_scr[rows, :]
        ge = jnp.sum(jnp.where(lane == e + N_GROUPS, gates_scr[rows, :], 0.0), axis=-1, keepdims=True)
        a = jnp.dot(hb, wgb[...], preferred_element_type=F32)
        u = jnp.dot(hb, wub[...], preferred_element_type=F32)
        hid = (a * jax.nn.sigmoid(a)) * u * ge
        acc_scr[rows, :] += jnp.dot(hid.astype(BF16), wdb[...], preferred_element_type=F32)

    @pl.when(e == N_EXPERTS - 1)
    def _():
        for s in range(n_sub):
            rows = slice(s * sub, (s + 1) * sub)
            z = alpha * x_ref[rows, :] + gate_ref[pl.ds(r, 1), :] * acc_scr[rows, :]
            _store_group(i, n_pt, o_refs, _layer_norm(z, g_ref[...], b_ref[...], LN_EPS), rows)


def _moe_norm(x, mod, layer, rw, rb, w_gate, w_up, w_down, ln_g, ln_b, alpha, geom, split_out):
    n_tok, d = x.shape
    ff = w_gate.shape[-1]
    const = lambda i, e: (0, 0)
    if split_out:
        out_shape = [jax.ShapeDtypeStruct((geom[0], d), F32), jax.ShapeDtypeStruct((n_tok - geom[0], d), F32)]
        out_specs = _group_specs(MOE_TILE, d, geom[0] // MOE_TILE)
    else:
        out_shape = [jax.ShapeDtypeStruct((n_tok, d), F32)]
        out_specs = [pl.BlockSpec((MOE_TILE, d), lambda i, e: (i, 0))]
    return pl.pallas_call(
        functools.partial(_moe_kernel, alpha=alpha, geom=geom),
        out_shape=out_shape,
        grid=(n_tok // MOE_TILE, N_EXPERTS),
        in_specs=[pl.BlockSpec((MOE_TILE, d), lambda i, e: (i, 0)),
                  _mod_spec(layer, 3, d), _mod_spec(layer, 4, d), _mod_spec(layer, 5, d),
                  pl.BlockSpec((d, ROUTER_LANES), const),
                  pl.BlockSpec((1, ROUTER_LANES), const),
                  pl.BlockSpec((None, 1, d, ff), lambda i, e: (layer, e, 0, 0)),
                  pl.BlockSpec((None, 1, d, ff), lambda i, e: (layer, e, 0, 0)),
                  pl.BlockSpec((None, 1, ff, d), lambda i, e: (layer, e, 0, 0)),
                  pl.BlockSpec((1, d), const),
                  pl.BlockSpec((1, d), const)],
        out_specs=out_specs,
        scratch_shapes=[pltpu.VMEM((MOE_TILE, d), BF16),
                        pltpu.VMEM((MOE_TILE, ROUTER_LANES), F32),
                        pltpu.VMEM((MOE_TILE, d), F32),
                        pltpu.VMEM((d, ff), BF16), pltpu.VMEM((d, ff), BF16), pltpu.VMEM((ff, d), BF16)],
        compiler_params=_cparams(("arbitrary", "arbitrary")),
        name=f"moe_norm_l{layer}",
    )(x, mod, mod, mod, rw, rb, w_gate, w_up, w_down, ln_g.reshape(1, d), ln_b.reshape(1, d))


def _rwkv_kernel(r_ref, k_ref, v_ref, lora_ref, mur_ref, muk_ref, muv_ref, mul_ref,
                 w0_ref, wup_ref, a0_ref, aup_ref, gup_ref, kk_ref, ka_ref, rk_ref, gng_ref, gnb_ref, s0_ref,
                 out_ref, sfin_ref,
                 lw_s, kz_s, bz_s, rr_s, vv_s, kn_s,
                 abar_s, rt_s, wt_s, vst_s, p_s, tm_s, mk_s, nkb_s, txr_s, tx_s, gl_s,
                 g_s, h_s, q_s, y0_s, y_s, st_s, *, seq_len, unroll):
    t_len = seq_len
    n_chunks = t_len // CHUNK
    c2 = 2 * CHUNK

    row = lax.broadcasted_iota(jnp.int32, (t_len, 1), 0)

    def tshift(x, mu):
        prev = jnp.where(row == 0, 0.0, pltpu.roll(x, 1, 0))
        nxt = jnp.where(row == t_len - 1, 0.0, pltpu.roll(x, t_len - 1, 0))
        return x + mu * (0.5 * (prev + nxt) - x)

    r = tshift(r_ref[...], mur_ref[...])
    k = tshift(k_ref[...], muk_ref[...])
    v = tshift(v_ref[...], muv_ref[...])
    lo = tshift(lora_ref[...], mul_ref[...])
    wd = jnp.tanh(lo[:, 0:PAIR])
    ad = lo[:, PAIR:2 * PAIR]
    gate = _bdot(jax.nn.sigmoid(lo[:, 2 * PAIR:3 * PAIR]), gup_ref[...])

    ri = lax.broadcasted_iota(jnp.int32, (PAIR, PAIR), 0)
    ci = lax.broadcasted_iota(jnp.int32, (PAIR, PAIR), 1)
    same_head = (ri // HEAD) == (ci // HEAD)
    ones_bd = jnp.where(same_head, 1.0, 0.0).astype(BF16)
    eye = ri == ci

    kk = k * kk_ref[...]
    ssq = _dot_exact_rhs(kk * kk, ones_bd)
    kk = kk * lax.rsqrt(jnp.maximum(ssq, 1e-24))
    wup2 = wup_ref[...].reshape(PAIR, PAIR)
    aup2 = aup_ref[...].reshape(PAIR, PAIR)
    bonus = jnp.zeros((t_len, PAIR), F32)
    for z in range(2):
        zrows = (ri // HEAD) == z
        w_pre = w0_ref[z:z + 1, :] + _bdot(wd, jnp.where(zrows, wup2, 0.0))
        lw_s[z] = -EXP_M05 * jax.nn.sigmoid(w_pre)
        a = jax.nn.sigmoid(a0_ref[z:z + 1, :] + _bdot(ad, jnp.where(zrows, aup2, 0.0)))
        kz = k * (1.0 + (a - 1.0) * ka_ref[...])
        kz_s[z] = kz
        bz_s[z] = kk * a
        bonus = bonus + _dot_exact_rhs(r * kz * rk_ref[...], ones_bd) * v
    rr_s[...] = r
    vv_s[...] = v
    kn_s[...] = kk

    rt_i = ri % CHUNK
    cs_i = ci % CHUNK
    strict = (cs_i < rt_i, cs_i > rt_i)
    incl = (cs_i <= rt_i, cs_i >= rt_i)
    t64 = lax.broadcasted_iota(jnp.int32, (CHUNK, CHUNK), 0)
    s64 = lax.broadcasted_iota(jnp.int32, (CHUNK, CHUNK), 1)
    tri = (jnp.where(s64 <= t64, 1.0, 0.0).astype(BF16), jnp.where(s64 >= t64, 1.0, 0.0).astype(BF16))
    lane_head0 = lax.broadcasted_iota(jnp.int32, (CHUNK, PAIR), 1) < HEAD
    eye_f = jnp.where(eye, 1.0, 0.0)

    def stack(x):
        return jnp.concatenate([jnp.where(lane_head0, x, 0.0), jnp.where(lane_head0, 0.0, x)], axis=0)

    def fold(x):
        return x[:CHUNK] + x[CHUNK:]

    def chunk_rows(c):
        return pl.ds(pl.multiple_of(c * CHUNK, CHUNK), CHUNK)

    def each_item(body):
        for z in range(2):
            lax.fori_loop(0, n_chunks, lambda c, carry, z=z: (body(z, c), carry)[1], 0, unroll=unroll)

    def stage_operands(z, c):
        rows = chunk_rows(c)
        r_c, kn_c = rr_s[rows, :], kn_s[rows, :]
        lw_c, kz_c, bz_c = lw_s[z, rows, :], kz_s[z, rows, :], bz_s[z, rows, :]
        lg = _dot_exact_lhs(tri[z], lw_c)
        total = lg[CHUNK - 1:CHUNK, :] if z == 0 else lg[0:1, :]
        e_neg = jnp.exp(-lg)
        e_rem = jnp.exp(total - lg)
        abar = stack(kn_c * jnp.exp(lg - lw_c))
        rt = stack(r_c * jnp.exp(lg))
        big = _bdot_nt(jnp.concatenate([abar, rt], axis=0),
                       jnp.concatenate([stack(bz_c * e_neg), stack(kz_c * e_neg)], axis=0))
        mb = jnp.where(strict[z], big[:c2, :c2], 0.0)
        abar_s[z, c] = abar.astype(BF16)
        rt_s[z, c] = rt
        wt_s[z, c] = jnp.concatenate([stack(kz_c * e_rem), -stack(bz_c * e_rem)], axis=0).T.astype(BF16)
        gl_s[z, c] = jnp.exp(total)
        tm_s[z, c] = eye_f - mb
        p_s[z, c] = mb
        mk_s[z, c] = jnp.where(strict[z], big[:c2, c2:], 0.0).astype(BF16)
        nkb_s[z, c] = jnp.concatenate([jnp.where(incl[z], big[c2:, c2:], 0.0),
                                       -jnp.where(incl[z], big[c2:, :c2], 0.0)], axis=1).astype(BF16)

    def stage_values(c, carry):
        vst_s[c] = stack(vv_s[chunk_rows(c), :]).astype(BF16)
        return carry

    lax.fori_loop(0, n_chunks, stage_values, 0, unroll=unroll)
    each_item(stage_operands)

    def stage_square(z, c):
        mb = p_s[z, c]
        p_s[z, c] = _dot_f32ish(mb, mb)
        mkv = jnp.dot(mk_s[z, c], vst_s[c], preferred_element_type=F32)
        txr_s[z, c] = jnp.concatenate([abar_s[z, c], mkv.astype(BF16)], axis=1)

    each_item(stage_square)

    n_round = CHUNK.bit_length() - 2
    for j in range(n_round):
        def stage_double(z, c, last=(j == n_round - 1)):
            tm, p = tm_s[z, c], p_s[z, c]
            tm_s[z, c] = tm + _dot_f32ish(tm, p)
            if not last:
                p_s[z, c] = _dot_f32ish(p, p)

        each_item(stage_double)

    def stage_apply(z, c):
        tx_s[z, c] = jnp.dot(tm_s[z, c].astype(BF16), txr_s[z, c], preferred_element_type=F32)

    each_item(stage_apply)

    def stage_maps(z, c):
        tx = tx_s[z, c]
        ta = tx[:, :PAIR].astype(BF16)
        x1 = tx[:, PAIR:].astype(BF16)
        z2 = jnp.concatenate([vst_s[c], x1], axis=0)
        wt = wt_s[z, c]
        h_s[z, c] = jnp.dot(wt, z2, preferred_element_type=F32)
        g_s[z, c] = jnp.where(eye, gl_s[z, c], 0.0) + jnp.dot(wt[:, c2:], ta, preferred_element_type=F32)
        nkb = nkb_s[z, c]
        q_s[z, c] = fold(rt_s[z, c] + jnp.dot(nkb[:, c2:], ta, preferred_element_type=F32))
        y0_s[z, c] = fold(jnp.dot(nkb, z2, preferred_element_type=F32))

    each_item(stage_maps)

    zero64 = jnp.zeros((HEAD, HEAD), F32)
    for z in range(2):
        a = jnp.concatenate([jnp.concatenate([s0_ref[z, 0], zero64], axis=1),
                             jnp.concatenate([zero64, s0_ref[z, 1]], axis=1)], axis=0)
        st_s[z] = a.T

    def propagate(i, carry):
        for z in range(2):
            c = i if z == 0 else n_chunks - 1 - i
            s = st_s[z].astype(BF16)
            y_s[z, chunk_rows(c), :] = _bdot(q_s[z, c], s) + y0_s[z, c]
            st_s[z] = _bdot(g_s[z, c], s) + h_s[z, c]
        return carry

    lax.fori_loop(0, n_chunks, propagate, 0)
    for z in range(2):
        a = st_s[z].T
        sfin_ref[z, 0] = a[:HEAD, :HEAD]
        sfin_ref[z, 1] = a[HEAD:, HEAD:]

    y = y_s[0] + y_s[1]
    mu = _dot_exact_rhs(y, ones_bd) * (1.0 / HEAD)
    yc = y - mu
    var = _dot_exact_rhs(yc * yc, ones_bd) * (1.0 / HEAD)
    yn = yc * lax.rsqrt(var + GN_EPS) * gng_ref[...] + gnb_ref[...]
    out_ref[...] = (yn + bonus) * gate


def _rwkv_mix(pa, first_tok, n_seq, seq_len, s0, p):
    a_width = p["w0"].shape[-1]
    n_pairs = a_width // PAIR
    heads = a_width // HEAD
    t0 = first_tok // seq_len
    lora_w = 3 * PAIR
    lora_blk = 3 * a_width // lora_w
    n_chunks = seq_len // CHUNK
    col = lambda off: (lambda s, q: (t0 + s, off + q))
    vec = lambda off: (lambda s, q: (0, off + q))
    in_specs = [
        pl.BlockSpec((seq_len, PAIR), col(0)),
        pl.BlockSpec((seq_len, PAIR), col(n_pairs)),
        pl.BlockSpec((seq_len, PAIR), col(2 * n_pairs)),
        pl.BlockSpec((seq_len, lora_w), lambda s, q: (t0 + s, lora_blk)),
        pl.BlockSpec((1, PAIR), vec(0)),
        pl.BlockSpec((1, PAIR), vec(n_pairs)),
        pl.BlockSpec((1, PAIR), vec(2 * n_pairs)),
        pl.BlockSpec((1, lora_w), lambda s, q: (0, lora_blk)),
        pl.BlockSpec((2, PAIR), vec(0)),
        pl.BlockSpec((2, HEAD, PAIR), lambda s, q: (0, 0, q)),
        pl.BlockSpec((2, PAIR), vec(0)),
        pl.BlockSpec((2, HEAD, PAIR), lambda s, q: (0, 0, q)),
        pl.BlockSpec((PAIR, PAIR), vec(0)),
        pl.BlockSpec((1, PAIR), vec(0)),
        pl.BlockSpec((1, PAIR), vec(0)),
        pl.BlockSpec((1, PAIR), vec(0)),
        pl.BlockSpec((1, PAIR), vec(0)),
        pl.BlockSpec((1, PAIR), vec(0)),
        pl.BlockSpec((None, 2, 2, HEAD, HEAD), lambda s, q: (s, 0, q, 0, 0)),
    ]
    seq2 = lambda: pltpu.VMEM((2, seq_len, PAIR), F32)
    seq1 = lambda: pltpu.VMEM((seq_len, PAIR), F32)
    item = lambda rows, cols, dt: pltpu.VMEM((2, n_chunks, rows, cols), dt)
    out, s_fin = pl.pallas_call(
        functools.partial(_rwkv_kernel, seq_len=seq_len, unroll=min(n_chunks, 4)),
        out_shape=[jax.ShapeDtypeStruct((n_seq * seq_len, a_width), F32),
                   jax.ShapeDtypeStruct((n_seq, 2, heads, HEAD, HEAD), F32)],
        grid=(n_seq, n_pairs),
        in_specs=in_specs,
        out_specs=[pl.BlockSpec((seq_len, PAIR), lambda s, q: (s, q)),
                   pl.BlockSpec((None, 2, 2, HEAD, HEAD), lambda s, q: (s, 0, q, 0, 0))],
        scratch_shapes=[seq2(), seq2(), seq2(), seq1(), seq1(), seq1(),
                        item(PAIR, PAIR, BF16),
                        item(PAIR, PAIR, F32),
                        item(PAIR, 2 * PAIR, BF16),
                        pltpu.VMEM((n_chunks, PAIR, PAIR), BF16),
                        item(PAIR, PAIR, F32),
                        item(PAIR, PAIR, F32),
                        item(PAIR, PAIR, BF16),
                        item(PAIR, 2 * PAIR, BF16),
                        item(PAIR, 2 * PAIR, BF16),
                        item(PAIR, 2 * PAIR, F32),
                        item(1, PAIR, F32),
                        item(PAIR, PAIR, F32), item(PAIR, PAIR, F32),
                        item(CHUNK, PAIR, F32), item(CHUNK, PAIR, F32),
                        seq2(), pltpu.VMEM((2, PAIR, PAIR), F32)],
        compiler_params=_cparams(("arbitrary", "arbitrary")),
        name=f"rwkv_mix_t{seq_len}",
    )(pa, pa, pa, pa, p["ts_mu"], p["ts_mu"], p["ts_mu"], p["ts_mu"],
      p["w0"], p["w_up"], p["a0"], p["a_up"], p["g_up"], p["k_k"], p["k_a"], p["r_k"], p["gn_g"], p["gn_b"], s0)
    return out, s_fin


def _conv_kernel(pb_ref, pwb_ref, dw_ref, dwb_ref, g_ref, b_ref, o_ref, pad_s, *, seq_len):
    width = o_ref.shape[1]
    halo = 16
    sub = 64
    pbv = pb_ref[...] + pwb_ref[...]
    glu = pbv[:, :width] * jax.nn.sigmoid(pbv[:, width:])
    pad_s[0:halo, :] = jnp.zeros((halo, width), F32)
    pad_s[halo:halo + seq_len, :] = glu
    pad_s[halo + seq_len:, :] = jnp.zeros((halo, width), F32)
    first = halo - CONV_W // 2
    for c in range(seq_len // sub):
        acc = jnp.zeros((sub, width), F32)
        for d in range(CONV_W):
            start = c * sub + first + d
            acc = acc + pad_s[start:start + sub, :] * dw_ref[d:d + 1, :]
        y = _layer_norm(acc + dwb_ref[...], g_ref[...], b_ref[...], LN_EPS)
        o_ref[c * sub:(c + 1) * sub, :] = y * jax.nn.sigmoid(y)


def _conv_module(pb, first_tok, n_seq, seq_len, pw_b, dw_w, dw_b, cln_g, cln_b):
    width = dw_w.shape[1]
    t0 = first_tok // seq_len
    const = lambda s: (0, 0)
    return pl.pallas_call(
        functools.partial(_conv_kernel, seq_len=seq_len),
        out_shape=jax.ShapeDtypeStruct((n_seq * seq_len, width), F32),
        grid=(n_seq,),
        in_specs=[pl.BlockSpec((seq_len, 2 * width), lambda s: (t0 + s, 0)),
                  pl.BlockSpec((1, 2 * width), const),
                  pl.BlockSpec((CONV_W, width), const),
                  pl.BlockSpec((1, width), const),
                  pl.BlockSpec((1, width), const),
                  pl.BlockSpec((1, width), const)],
        out_specs=pl.BlockSpec((seq_len, width), lambda s: (s, 0)),
        scratch_shapes=[pltpu.VMEM((seq_len + 32, width), F32)],
        compiler_params=_cparams(("arbitrary",)),
        name=f"conv_module_t{seq_len}",
    )(pb, pw_b.reshape(1, -1), dw_w, dw_b.reshape(1, -1), cln_g.reshape(1, -1), cln_b.reshape(1, -1))


def _diff_lambda(lq1, lk1, lq2, lk2, lam_init):
    dot = lambda a, b: jnp.exp(jnp.sum(a[...] * b[...], axis=-1, keepdims=True))
    return dot(lq1, lk1) - dot(lq2, lk2) + lam_init


def _diff_attend(q, keys, vals, lam, lam_init, subln):
    nq = q.shape[0]
    lane_first = lax.broadcasted_iota(jnp.int32, q.shape, 1) < HEAD
    q_st = jnp.concatenate([jnp.where(lane_first, q, 0.0), jnp.where(lane_first, 0.0, q)], axis=0)
    s = _bdot_nt(q_st, keys) * (HEAD ** -0.5)
    s = s - jnp.max(s, axis=-1, keepdims=True)
    pexp = jnp.exp(s)
    ov = _bdot(pexp, vals) / jnp.sum(pexp, axis=-1, keepdims=True)
    o = ov[:nq] - lam * ov[nq:]
    o = o * lax.rsqrt(jnp.mean(o * o, axis=-1, keepdims=True) + RMS_EPS)
    return o * subln * (1.0 - lam_init)


def _head_cols(h):
    return slice(h * PAIR, (h + 1) * PAIR)


def _attn_prompt_kernel(q_ref, k_ref, v_ref, lq1, lk1, lq2, lk2, sub_ref, o_ref, *, lam_init):
    lam = _diff_lambda(lq1, lk1, lq2, lk2, lam_init)
    for h in range(q_ref.shape[1] // PAIR):
        c = _head_cols(h)
        o_ref[:, c] = _diff_attend(q_ref[:, c], k_ref[:, c], v_ref[:, c], lam, lam_init, sub_ref[...])


def _rope(x, cos, sin_signed):
    lane = lax.broadcasted_iota(jnp.int32, x.shape, 1)
    quarter = HEAD // 4
    partner = jnp.where((lane // quarter) % 2 == 0,
                        pltpu.roll(x, LANES - quarter, 1), pltpu.roll(x, quarter, 1))
    return x * cos + partner * sin_signed


def _attn_sample_kernel(q_ref, k_ref, v_ref, ck_ref, cv_ref, cq_ref, sq_ref, cosk_ref, sink_ref,
                        lq1, lk1, lq2, lk2, sub_ref, o_ref, *, lam_init):
    lam = _diff_lambda(lq1, lk1, lq2, lk2, lam_init)
    for h in range(q_ref.shape[1] // PAIR):
        c = _head_cols(h)
        q = _rope(q_ref[:, c], cq_ref[...], sq_ref[...])
        kl = _rope(k_ref[:, c], cosk_ref[...], sink_ref[...])
        keys = jnp.concatenate([ck_ref[:, c], kl], axis=0)
        vals = jnp.concatenate([cv_ref[:, c], v_ref[:, c]], axis=0)
        o_ref[:, c] = _diff_attend(q, keys, vals, lam, lam_init, sub_ref[...])


def _rope_tables(n_lat, grid_w):
    quarter = HEAD // 4
    inv = ROPE_BASE ** (-jnp.arange(quarter, dtype=F32) / quarter)
    t = jnp.arange(n_lat)
    row_ang = (t // grid_w).astype(F32)[:, None] * inv
    col_ang = (t % grid_w).astype(F32)[:, None] * inv
    ang = jnp.concatenate([row_ang, row_ang, col_ang, col_ang] * 2, axis=-1)
    sign = jnp.tile(jnp.concatenate([-jnp.ones(quarter, F32), jnp.ones(quarter, F32)]), 2 * HEAD // (2 * quarter))
    return jnp.cos(ang), jnp.sin(ang) * sign


def _attention(q, k_p, v_p, k_s, v_s, cache_k, cache_v, p, lam_init, n_prompt, seq, n_lat_req, n_lat, grid_w):
    inner = q.shape[1]
    vec = lambda n: pl.BlockSpec((1, n), lambda *_: (0, 0))
    lam_args = [p[n].reshape(1, -1) for n in ("lq1", "lk1", "lq2", "lk2")] + [p["subln_g"].reshape(1, -1)]
    lam_specs = [vec(HEAD)] * 4 + [vec(PAIR)]
    blk = pl.BlockSpec((seq, inner), lambda b: (b, 0))
    o_prompt = pl.pallas_call(
        functools.partial(_attn_prompt_kernel, lam_init=lam_init),
        out_shape=jax.ShapeDtypeStruct((n_prompt * seq, inner), F32),
        grid=(n_prompt,),
        in_specs=[blk, blk, blk] + lam_specs,
        out_specs=blk,
        compiler_params=_cparams(("arbitrary",)),
        name="diff_attn_prompt",
    )(q, k_p, v_p, *lam_args)

    past = cache_k.shape[1]
    qt = ROW_TILE
    n_qt = n_lat // qt
    q0 = n_prompt * seq // qt
    cos, sin = _rope_tables(n_lat, grid_w)
    lat = pl.BlockSpec((n_lat, inner), lambda b, t: (b, 0))
    ctx = pl.BlockSpec((None, past, inner), lambda b, t: (b, 0, 0))
    o_sample = pl.pallas_call(
        functools.partial(_attn_sample_kernel, lam_init=lam_init),
        out_shape=jax.ShapeDtypeStruct((n_lat_req * n_lat, inner), F32),
        grid=(n_lat_req, n_qt),
        in_specs=[pl.BlockSpec((qt, inner), lambda b, t: (q0 + b * n_qt + t, 0)),
                  lat, lat, ctx, ctx,
                  pl.BlockSpec((qt, PAIR), lambda b, t: (t, 0)),
                  pl.BlockSpec((qt, PAIR), lambda b, t: (t, 0)),
                  pl.BlockSpec((n_lat, PAIR), lambda b, t: (0, 0)),
                  pl.BlockSpec((n_lat, PAIR), lambda b, t: (0, 0))] + lam_specs,
        out_specs=pl.BlockSpec((qt, inner), lambda b, t: (b * n_qt + t, 0)),
        compiler_params=_cparams(("arbitrary", "arbitrary")),
        name="diff_attn_sample",
    )(q, k_s, v_s, cache_k, cache_v, cos, sin, cos, sin, *lam_args)
    return o_prompt, o_sample


def kernel(x_prompt, x_sample, state_wkv, cache_k, cache_v, c, c_ctx, mod_w, mod_b, ln1_g, ln1_b, ln2_g, ln2_b, ev_in_w, ev_pw_b, ev_ts_mu, ev_w0, ev_w_up, ev_a0, ev_a_up, ev_g_up, ev_k_k, ev_k_a, ev_r_k, ev_gn_g, ev_gn_b, ev_dw_w, ev_dw_b, ev_cln_g, ev_cln_b, ev_out_w, od_in_w, od_lq1, od_lk1, od_lq2, od_lk2, od_subln_g, od_out_w, rg_w, rg_b, re_w, re_b, ffn_w_gate, ffn_w_up, ffn_w_down):
    nbp, seq, d = x_prompt.shape
    nbs, n_lat, _ = x_sample.shape
    depth = mod_w.shape[0]
    n_prompt_tok = nbp * seq
    geom = (n_prompt_tok, n_lat)
    alpha = (2 * depth) ** 0.25
    a_width = ev_w0.shape[-1]
    a_cols = ev_ts_mu.shape[-1]
    heads = a_width // HEAD
    grid_w = 64
    assert n_prompt_tok % MOE_TILE == 0 and n_lat % MOE_TILE == 0 and seq % ROW_TILE == 0
    assert 1 + nbs <= 8 and n_lat % ROW_TILE == 0

    xs = (x_prompt.reshape(n_prompt_tok, d), x_sample.reshape(nbs * n_lat, d))
    cond = jnp.concatenate([c_ctx[None, :], c, jnp.zeros((8 - 1 - nbs, d), F32)], axis=0)
    mod = _ada_table(cond, mod_w, mod_b)

    new_wkv, new_k, new_v = [], [], []
    for l in range(depth):
        last = l == depth - 1
        if l % 2 == 0:
            i = l // 2
            pa, pb = _mod_matmul(xs, mod, l, ev_in_w[i], ((a_cols, False), (ev_in_w.shape[-1] - a_cols, False)), geom)
            prm = dict(ts_mu=ev_ts_mu[i][None, :], w0=ev_w0[i], w_up=ev_w_up[i], a0=ev_a0[i], a_up=ev_a_up[i],
                       g_up=ev_g_up[i], k_k=ev_k_k[i][None, :], k_a=ev_k_a[i][None, :],
                       r_k=ev_r_k[i].reshape(1, a_width), gn_g=ev_gn_g[i][None, :], gn_b=ev_gn_b[i][None, :])
            zero_state = jnp.zeros((nbp, 2, heads, HEAD, HEAD), F32)
            oa_p, st = _rwkv_mix(pa, 0, nbp, seq, zero_state, prm)
            oa_s, _ = _rwkv_mix(pa, n_prompt_tok, nbs, n_lat, state_wkv[:, i], prm)
            conv = (ev_pw_b[i], ev_dw_w[i], ev_dw_b[i], ev_cln_g[i], ev_cln_b[i])
            ob_p = _conv_module(pb, 0, nbp, seq, *conv)
            ob_s = _conv_module(pb, n_prompt_tok, nbs, n_lat, *conv)
            hs = [(oa_p, oa_s), (ob_p, ob_s)]
            out_w = ev_out_w[i]
            new_wkv.append(st)
        else:
            j = l // 2
            lam_init = 0.8 - 0.6 * math.exp(-0.3 * l)
            inner = od_in_w.shape[-1] // 3
            q, k_p, k_s, v_p, v_s = _mod_matmul(xs, mod, l, od_in_w[j],
                                                 ((inner, False), (inner, True), (inner, True)), geom)
            prm = dict(lq1=od_lq1[j], lk1=od_lk1[j], lq2=od_lq2[j], lk2=od_lk2[j], subln_g=od_subln_g[j])
            ck = cache_k[:, j].reshape(nbs, -1, inner)
            cv = cache_v[:, j].reshape(nbs, -1, inner)
            hs = [_attention(q, k_p, v_p, k_s, v_s, ck, cv, prm, lam_init, nbp, seq, nbs, n_lat, grid_w)]
            out_w = od_out_w[j]
            new_k.append(k_p.reshape(nbp, seq, inner // PAIR, PAIR))
            new_v.append(v_p.reshape(nbp, seq, inner // PAIR, PAIR))
        x = _out_proj_norm(hs, xs, mod, l, out_w, ln1_g[l], ln1_b[l], alpha, geom)
        rw = jnp.concatenate([rg_w[l], re_w[l],
                              jnp.zeros((d, ROUTER_LANES - N_GROUPS - N_EXPERTS), F32)], axis=1)
        rb = jnp.concatenate([rg_b[l], re_b[l], jnp.zeros((ROUTER_LANES - N_GROUPS - N_EXPERTS,), F32)])[None, :]
        xs = tuple(_moe_norm(x, mod, l, rw, rb, ffn_w_gate, ffn_w_up, ffn_w_down, ln2_g[l], ln2_b[l], alpha, geom,
                             split_out=last))

    y_prompt = xs[0].reshape(nbp, seq, d)
    y_sample = xs[1].reshape(nbs, n_lat, d)
    return (y_prompt, y_sample, jnp.stack(new_wkv, axis=1), jnp.stack(new_k, axis=1), jnp.stack(new_v, axis=1))
```

```python
import functools
import math

import jax
import jax.numpy as jnp
from jax import lax
from jax.experimental import pallas as pl
from jax.experimental.pallas import tpu as pltpu

F32 = jnp.float32
BF16 = jnp.bfloat16

LANES = 128
HEAD = 64
PAIR = 2 * HEAD
CHUNK = 64
INV_BLOCK = 16
ROW_TILE = 256
MOE_TILE = 1024
CONV_W = 31
N_GROUPS = 4
EXPERTS_PER_GROUP = 8
N_EXPERTS = N_GROUPS * EXPERTS_PER_GROUP
ROUTER_LANES = 128
LN_EPS = 1e-5
GN_EPS = 64e-5
RMS_EPS = 1e-5
ROPE_BASE = 10000.0
EXP_M05 = math.exp(-0.5)
VMEM_LIMIT = 56 * 1024 * 1024


def _bdot(a, b):
    return jnp.dot(a.astype(BF16), b.astype(BF16), preferred_element_type=F32)


def _bdot_nt(a, b):
    return lax.dot_general(a.astype(BF16), b.astype(BF16), (((1,), (1,)), ((), ())),
                           preferred_element_type=F32)


def _split2(a):
    hi = a.astype(BF16)
    lo = (a - hi.astype(F32)).astype(BF16)
    return hi, lo


def _split3(a):
    hi = a.astype(BF16)
    r1 = a - hi.astype(F32)
    mid = r1.astype(BF16)
    lo = (r1 - mid.astype(F32)).astype(BF16)
    return hi, mid, lo


def _dot_f32ish(a, b):
    ah, al = _split2(a)
    bh, bl = _split2(b)
    d = lambda x, y: jnp.dot(x, y, preferred_element_type=F32)
    return d(ah, bh) + d(ah, bl) + d(al, bh)


def _dot_exact_lhs(m_bf16, a):
    h, m, l = _split3(a)
    d = lambda y: jnp.dot(m_bf16, y, preferred_element_type=F32)
    return d(h) + d(m) + d(l)


def _dot_exact_rhs(a, m_bf16):
    h, m, l = _split3(a)
    d = lambda x: jnp.dot(x, m_bf16, preferred_element_type=F32)
    return d(h) + d(m) + d(l)


def _layer_norm(z, g, b, eps):
    mu = jnp.mean(z, axis=-1, keepdims=True)
    zc = z - mu
    var = jnp.mean(zc * zc, axis=-1, keepdims=True)
    return zc * lax.rsqrt(var + eps) * g + b


def _cond_row(tile, tile_rows, n_prompt_tok, dec_seq):
    tok = tile * tile_rows
    return jnp.where(tok < n_prompt_tok, 0, 1 + (tok - n_prompt_tok) // dec_seq)


def _cast_rows(src_ref, dst_ref, rows_per_step=128):
    n = src_ref.shape[0]
    for s in range(0, n, rows_per_step):
        dst_ref[s:s + rows_per_step, :] = src_ref[s:s + rows_per_step, :].astype(dst_ref.dtype)


def _cparams(sem):
    return pltpu.CompilerParams(dimension_semantics=sem, vmem_limit_bytes=VMEM_LIMIT)


def _ada_kernel(cond_ref, w_ref, b_ref, o_ref):
    c = cond_ref[...]
    s = c * jax.nn.sigmoid(c)
    o_ref[0, 0] = _dot_f32ish(s, w_ref[0]) + b_ref[0, 0]


def _ada_table(cond8, mod_w, mod_b):
    depth, d, _ = mod_w.shape
    return pl.pallas_call(
        _ada_kernel,
        out_shape=jax.ShapeDtypeStruct((depth, 6, 8, d), F32),
        grid=(depth, 6),
        in_specs=[pl.BlockSpec((8, d), lambda l, j: (0, 0)),
                  pl.BlockSpec((1, d, d), lambda l, j: (l, 0, j)),
                  pl.BlockSpec((1, 1, 1, d), lambda l, j: (l, j, 0, 0))],
        out_specs=pl.BlockSpec((1, 1, 8, d), lambda l, j: (l, j, 0, 0)),
        compiler_params=_cparams(("arbitrary", "arbitrary")),
        name="ada_table",
    )(cond8, mod_w, mod_b.reshape(depth, 6, 1, d))


def _mod_spec(layer, which, d):
    return pl.BlockSpec((None, None, 8, d), lambda *_: (layer, which, 0, 0))


def _group_specs(tile, width, n_prompt_tiles):
    return [pl.BlockSpec((tile, width), lambda i, *_: (jnp.minimum(i, n_prompt_tiles - 1), 0)),
            pl.BlockSpec((tile, width), lambda i, *_: (jnp.maximum(i - n_prompt_tiles, 0), 0))]


def _load_group(i, n_prompt_tiles, refs, rows=slice(None)):
    if len(refs) == 1:
        return refs[0][rows, :]
    return jnp.where(i < n_prompt_tiles, refs[0][rows, :], refs[1][rows, :])


def _store_group(i, n_prompt_tiles, refs, value, rows=slice(None)):
    if len(refs) == 1:
        refs[0][rows, :] = value
        return

    @pl.when(i < n_prompt_tiles)
    def _():
        refs[0][rows, :] = value

    @pl.when(i >= n_prompt_tiles)
    def _():
        refs[1][rows, :] = value


def _modmm_kernel(*refs, n_x, outs, geom):
    x_refs = refs[:n_x]
    sh_ref, sc_ref, w_ref = refs[n_x:n_x + 3]
    o_refs, wb = refs[n_x + 3:-1], refs[-1]
    i = pl.program_id(0)
    n_pt = geom[0] // ROW_TILE

    @pl.when(i == 0)
    def _():
        _cast_rows(w_ref, wb)

    r = _cond_row(i, ROW_TILE, *geom)
    h = _load_group(i, n_pt, x_refs) * (1.0 + sc_ref[pl.ds(r, 1), :]) + sh_ref[pl.ds(r, 1), :]
    hb = h.astype(BF16)
    off = 0
    k = 0
    for n, split in outs:
        y = jnp.dot(hb, wb[:, off:off + n], preferred_element_type=F32)
        _store_group(i, n_pt, o_refs[k:k + 1 + split], y)
        off += n
        k += 1 + split


def _mod_matmul(xs, mod, layer, w, outs, geom):
    d, n_out = w.shape
    n_prompt_tok = geom[0]
    n_tok = sum(x.shape[0] for x in xs)
    n_pt = n_prompt_tok // ROW_TILE
    whole = lambda n: [pl.BlockSpec((ROW_TILE, n), lambda i: (i, 0))]
    x_specs = whole(d) if len(xs) == 1 else _group_specs(ROW_TILE, d, n_pt)
    out_shape, out_specs = [], []
    for n, split in outs:
        if split:
            out_shape += [jax.ShapeDtypeStruct((n_prompt_tok, n), F32),
                          jax.ShapeDtypeStruct((n_tok - n_prompt_tok, n), F32)]
            out_specs += _group_specs(ROW_TILE, n, n_pt)
        else:
            out_shape += [jax.ShapeDtypeStruct((n_tok, n), F32)]
            out_specs += whole(n)
    return pl.pallas_call(
        functools.partial(_modmm_kernel, n_x=len(xs), outs=outs, geom=geom),
        out_shape=out_shape,
        grid=(n_tok // ROW_TILE,),
        in_specs=x_specs + [_mod_spec(layer, 0, d), _mod_spec(layer, 1, d),
                            pl.BlockSpec((d, n_out), lambda i: (0, 0), pipeline_mode=pl.Buffered(1))],
        out_specs=out_specs,
        scratch_shapes=[pltpu.VMEM((d, n_out), BF16)],
        compiler_params=_cparams(("arbitrary",)),
        name=f"mod_matmul_l{layer}",
    )(*xs, mod, mod, w)


def _outproj_kernel(*refs, arity, alpha, geom):
    groups, k = [], 0
    for n in arity:
        groups.append(refs[k:k + n])
        k += n
    gate_ref, w_ref, g_ref, b_ref, o_ref, wb = refs[k:]
    h_groups, x_group = groups[:-1], groups[-1]
    i = pl.program_id(0)
    n_pt = geom[0] // ROW_TILE

    @pl.when(i == 0)
    def _():
        _cast_rows(w_ref, wb)

    r = _cond_row(i, ROW_TILE, *geom)
    y = None
    off = 0
    for h_refs in h_groups:
        n = h_refs[0].shape[1]
        h = _load_group(i, n_pt, h_refs)
        part = jnp.dot(h.astype(BF16), wb[off:off + n, :], preferred_element_type=F32)
        y = part if y is None else y + part
        off += n
    z = alpha * _load_group(i, n_pt, x_group) + gate_ref[pl.ds(r, 1), :] * y
    o_ref[...] = _layer_norm(z, g_ref[...], b_ref[...], LN_EPS)


def _out_proj_norm(hs, xs, mod, layer, w, ln_g, ln_b, alpha, geom):
    d = w.shape[1]
    n_tok = sum(x.shape[0] for x in xs)
    n_pt = geom[0] // ROW_TILE
    specs = []
    for grp in list(hs) + [xs]:
        n = grp[0].shape[1]
        specs += [pl.BlockSpec((ROW_TILE, n), lambda i: (i, 0))] if len(grp) == 1 else _group_specs(ROW_TILE, n, n_pt)
    return pl.pallas_call(
        functools.partial(_outproj_kernel, arity=tuple(len(g) for g in list(hs) + [xs]), alpha=alpha, geom=geom),
        out_shape=jax.ShapeDtypeStruct((n_tok, d), F32),
        grid=(n_tok // ROW_TILE,),
        in_specs=specs + [
            _mod_spec(layer, 2, d),
            pl.BlockSpec(w.shape, lambda i: (0, 0), pipeline_mode=pl.Buffered(1)),
            pl.BlockSpec((1, d), lambda i: (0, 0)),
            pl.BlockSpec((1, d), lambda i: (0, 0))],
        out_specs=pl.BlockSpec((ROW_TILE, d), lambda i: (i, 0)),
        scratch_shapes=[pltpu.VMEM(w.shape, BF16)],
        compiler_params=_cparams(("arbitrary",)),
        name=f"out_proj_norm_l{layer}",
    )(*[a for g in hs for a in g], *xs, mod, w, ln_g.reshape(1, d), ln_b.reshape(1, d))


def _route(logits):
    neg = jnp.float32(-3e38)
    big = jnp.int32(1 << 20)
    lane = lax.broadcasted_iota(jnp.int32, logits.shape, 1)
    is_g = lane < N_GROUPS
    gl = jnp.where(is_g, logits, neg)
    gmax = jnp.max(gl, axis=-1, keepdims=True)
    gidx = jnp.min(jnp.where(gl == gmax, lane, big), axis=-1, keepdims=True)
    gsum = jnp.sum(jnp.where(is_g, jnp.exp(gl - gmax), 0.0), axis=-1, keepdims=True)
    g_p = 1.0 / gsum
    lo = N_GROUPS + gidx * EXPERTS_PER_GROUP
    in_grp = (lane >= lo) & (lane < lo + EXPERTS_PER_GROUP)
    el = jnp.where(in_grp, logits, neg)
    v1 = jnp.max(el, axis=-1, keepdims=True)
    i1 = jnp.min(jnp.where(el == v1, lane, big), axis=-1, keepdims=True)
    el2 = jnp.where(lane == i1, neg, el)
    v2 = jnp.max(el2, axis=-1, keepdims=True)
    i2 = jnp.min(jnp.where(el2 == v2, lane, big), axis=-1, keepdims=True)
    e21 = jnp.exp(v2 - v1)
    den = 1.0 + e21
    w1 = g_p / den
    w2 = g_p * e21 / den
    return jnp.where(lane == i1, w1, 0.0) + jnp.where(lane == i2, w2, 0.0), gidx


def _moe_kernel(x_ref, sh_ref, sc_ref, gate_ref, rw_ref, rb_ref, wg_ref, wu_ref, wd_ref, g_ref, b_ref,
                *rest, alpha, geom):
    o_refs = rest[:-9]
    h_scr, gates_scr, acc_scr, wgb, wub, wdb, perm_scr, permt_scr, bounds = rest[-9:]
    i = pl.program_id(0)
    e = pl.program_id(1)
    r = _cond_row(i, MOE_TILE, *geom)
    n_pt = geom[0] // MOE_TILE
    sub = ROW_TILE
    n_sub = MOE_TILE // sub

    @pl.when(e == 0)
    def _():
        lane = lax.broadcasted_iota(jnp.int32, (sub, ROUTER_LANES), 1)
        ri = lax.broadcasted_iota(jnp.int32, (sub, sub), 0)
        ci = lax.broadcasted_iota(jnp.int32, (sub, sub), 1)
        tril = jnp.where(ci <= ri, 1.0, 0.0).astype(BF16)
        running = jnp.zeros((1, ROUTER_LANES), F32)
        onehots, cums = [], []
        for s in range(n_sub):
            rows = slice(s * sub, (s + 1) * sub)
            h = x_ref[rows, :] * (1.0 + sc_ref[pl.ds(r, 1), :]) + sh_ref[pl.ds(r, 1), :]
            acc_scr[rows, :] = h
            logits = _dot_f32ish(h, rw_ref[...]) + rb_ref[...]
            gates, gidx = _route(logits)
            gates_scr[rows, :] = gates
            onehot = jnp.where(lane == gidx, 1.0, 0.0)
            cum = jnp.dot(tril, onehot.astype(BF16), preferred_element_type=F32) + running
            running = cum[sub - 1:sub, :]
            onehots.append(onehot)
            cums.append(cum)
        gi = lax.broadcasted_iota(jnp.int32, (ROUTER_LANES, ROUTER_LANES), 0)
        gj = lax.broadcasted_iota(jnp.int32, (ROUTER_LANES, ROUTER_LANES), 1)
        before = jnp.where((gi < gj) & (gi < N_GROUPS), 1.0, 0.0).astype(BF16)
        starts = _dot_exact_rhs(jnp.broadcast_to(running, (8, ROUTER_LANES)), before)[0:1, :]
        for g in range(N_GROUPS + 1):
            bounds[g] = starts[0, g].astype(jnp.int32)
        dest_cols = [jnp.sum(onehots[s] * (starts + cums[s] - 1.0), axis=-1, keepdims=True) for s in range(n_sub)]
        dest_row = jnp.concatenate(
            [jnp.broadcast_to(dc, (sub, ROUTER_LANES)).T[0:1, :] for dc in dest_cols], axis=1)
        col_id = lax.broadcasted_iota(jnp.int32, (sub, MOE_TILE), 1).astype(F32)
        row_id = lax.broadcasted_iota(jnp.int32, (sub, MOE_TILE), 0).astype(F32)
        for s in range(n_sub):
            rows = slice(s * sub, (s + 1) * sub)
            perm_scr[rows, :] = jnp.where(row_id + float(s * sub) == dest_row, 1.0, 0.0).astype(BF16)
            permt_scr[rows, :] = jnp.where(dest_cols[s] == col_id, 1.0, 0.0).astype(BF16)
        hb = acc_scr[...].astype(BF16)
        g_hi, g_mid, g_lo = _split3(gates_scr[...])
        for s in range(n_sub):
            rows = slice(s * sub, (s + 1) * sub)
            pm = perm_scr[rows, :]
            h_scr[rows, :] = jnp.dot(pm, hb, preferred_element_type=F32).astype(BF16)
        for s in range(n_sub):
            rows = slice(s * sub, (s + 1) * sub)
            pm = perm_scr[rows, :]
            d = lambda y: jnp.dot(pm, y, preferred_element_type=F32)
            gates_scr[rows, :] = d(g_hi) + d(g_mid) + d(g_lo)
            acc_scr[rows, :] = jnp.zeros((sub, acc_scr.shape[1]), F32)

    wgb[...] = wg_ref[0].astype(BF16)
    wub[...] = wu_ref[0].astype(BF16)
    wdb[...] = wd_ref[0].astype(BF16)
    grp = e // EXPERTS_PER_GROUP
    first, end = bounds[grp], bounds[grp + 1]
    lane = lax.broadcasted_iota(jnp.int32, (sub, ROUTER_LANES), 1)
    for s in range(n_sub):
        rows = slice(s * sub, (s + 1) * sub)

        @pl.when((first < (s + 1) * sub) & (end > s * sub))
        def _():
            hb = h_scr[rows, :]
            ge = jnp.sum(jnp.where(lane == e + N_GROUPS, gates_scr[rows, :], 0.0), axis=-1, keepdims=True)
            a = jnp.dot(hb, wgb[...], preferred_element_type=F32)
            u = jnp.dot(hb, wub[...], preferred_element_type=F32)
            hid = (a * jax.nn.sigmoid(a)) * u * ge
            acc_scr[rows, :] += jnp.dot(hid.astype(BF16), wdb[...], preferred_element_type=F32)

    @pl.when(e == N_EXPERTS - 1)
    def _():
        a_hi, a_lo = _split2(acc_scr[...])
        for s in range(n_sub):
            rows = slice(s * sub, (s + 1) * sub)
            pt = permt_scr[rows, :]
            moe = jnp.dot(pt, a_hi, preferred_element_type=F32) + jnp.dot(pt, a_lo, preferred_element_type=F32)
            z = alpha * x_ref[rows, :] + gate_ref[pl.ds(r, 1), :] * moe
            _store_group(i, n_pt, o_refs, _layer_norm(z, g_ref[...], b_ref[...], LN_EPS), rows)


def _moe_norm(x, mod, layer, rw, rb, w_gate, w_up, w_down, ln_g, ln_b, alpha, geom, split_out):
    n_tok, d = x.shape
    ff = w_gate.shape[-1]
    const = lambda i, e: (0, 0)
    if split_out:
        out_shape = [jax.ShapeDtypeStruct((geom[0], d), F32), jax.ShapeDtypeStruct((n_tok - geom[0], d), F32)]
        out_specs = _group_specs(MOE_TILE, d, geom[0] // MOE_TILE)
    else:
        out_shape = [jax.ShapeDtypeStruct((n_tok, d), F32)]
        out_specs = [pl.BlockSpec((MOE_TILE, d), lambda i, e: (i, 0))]
    return pl.pallas_call(
        functools.partial(_moe_kernel, alpha=alpha, geom=geom),
        out_shape=out_shape,
        grid=(n_tok // MOE_TILE, N_EXPERTS),
        in_specs=[pl.BlockSpec((MOE_TILE, d), lambda i, e: (i, 0)),
                  _mod_spec(layer, 3, d), _mod_spec(layer, 4, d), _mod_spec(layer, 5, d),
                  pl.BlockSpec((d, ROUTER_LANES), const),
                  pl.BlockSpec((1, ROUTER_LANES), const),
                  pl.BlockSpec((None, 1, d, ff), lambda i, e: (layer, e, 0, 0)),
                  pl.BlockSpec((None, 1, d, ff), lambda i, e: (layer, e, 0, 0)),
                  pl.BlockSpec((None, 1, ff, d), lambda i, e: (layer, e, 0, 0)),
                  pl.BlockSpec((1, d), const),
                  pl.BlockSpec((1, d), const)],
        out_specs=out_specs,
        scratch_shapes=[pltpu.VMEM((MOE_TILE, d), BF16),
                        pltpu.VMEM((MOE_TILE, ROUTER_LANES), F32),
                        pltpu.VMEM((MOE_TILE, d), F32),
                        pltpu.VMEM((d, ff), BF16), pltpu.VMEM((d, ff), BF16), pltpu.VMEM((ff, d), BF16),
                        pltpu.VMEM((MOE_TILE, MOE_TILE), BF16), pltpu.VMEM((MOE_TILE, MOE_TILE), BF16),
                        pltpu.SMEM((8,), jnp.int32)],
        compiler_params=_cparams(("arbitrary", "arbitrary")),
        name=f"moe_norm_l{layer}",
    )(x, mod, mod, mod, rw, rb, w_gate, w_up, w_down, ln_g.reshape(1, d), ln_b.reshape(1, d))


def _rwkv_kernel(r_ref, k_ref, v_ref, lora_ref, mur_ref, muk_ref, muv_ref, mul_ref,
                 w0_ref, wup_ref, a0_ref, aup_ref, gup_ref, kk_ref, ka_ref, rk_ref, gng_ref, gnb_ref, s0_ref,
                 out_ref, sfin_ref,
                 lw_s, kz_s, bz_s, rr_s, vv_s, kn_s,
                 abar_s, rt_s, wt_s, vst_s, p_s, tm_s, mb_s, mk_s, nkb_s, txr_s, tx_s, gl_s,
                 g_s, h_s, q_s, y0_s, y_s, st_s, *, seq_len, unroll):
    t_len = seq_len
    n_chunks = t_len // CHUNK
    c2 = 2 * CHUNK

    row = lax.broadcasted_iota(jnp.int32, (t_len, 1), 0)

    def tshift(x, mu):
        prev = jnp.where(row == 0, 0.0, pltpu.roll(x, 1, 0))
        nxt = jnp.where(row == t_len - 1, 0.0, pltpu.roll(x, t_len - 1, 0))
        return x + mu * (0.5 * (prev + nxt) - x)

    r = tshift(r_ref[...], mur_ref[...])
    k = tshift(k_ref[...], muk_ref[...])
    v = tshift(v_ref[...], muv_ref[...])
    lo = tshift(lora_ref[...], mul_ref[...])
    wd = jnp.tanh(lo[:, 0:PAIR])
    ad = lo[:, PAIR:2 * PAIR]
    gate = _bdot(jax.nn.sigmoid(lo[:, 2 * PAIR:3 * PAIR]), gup_ref[...])

    ri = lax.broadcasted_iota(jnp.int32, (PAIR, PAIR), 0)
    ci = lax.broadcasted_iota(jnp.int32, (PAIR, PAIR), 1)
    same_head = (ri // HEAD) == (ci // HEAD)
    ones_bd = jnp.where(same_head, 1.0, 0.0).astype(BF16)
    eye = ri == ci

    kk = k * kk_ref[...]
    ssq = _dot_exact_rhs(kk * kk, ones_bd)
    kk = kk * lax.rsqrt(jnp.maximum(ssq, 1e-24))
    wup2 = wup_ref[...].reshape(PAIR, PAIR)
    aup2 = aup_ref[...].reshape(PAIR, PAIR)
    bonus = jnp.zeros((t_len, PAIR), F32)
    for z in range(2):
        zrows = (ri // HEAD) == z
        w_pre = w0_ref[z:z + 1, :] + _bdot(wd, jnp.where(zrows, wup2, 0.0))
        lw_s[z] = -EXP_M05 * jax.nn.sigmoid(w_pre)
        a = jax.nn.sigmoid(a0_ref[z:z + 1, :] + _bdot(ad, jnp.where(zrows, aup2, 0.0)))
        kz = k * (1.0 + (a - 1.0) * ka_ref[...])
        kz_s[z] = kz
        bz_s[z] = kk * a
        bonus = bonus + _dot_exact_rhs(r * kz * rk_ref[...], ones_bd) * v
    rr_s[...] = r
    vv_s[...] = v
    kn_s[...] = kk

    rt_i = ri % CHUNK
    cs_i = ci % CHUNK
    strict = (cs_i < rt_i, cs_i > rt_i)
    incl = (cs_i <= rt_i, cs_i >= rt_i)
    t64 = lax.broadcasted_iota(jnp.int32, (CHUNK, CHUNK), 0)
    s64 = lax.broadcasted_iota(jnp.int32, (CHUNK, CHUNK), 1)
    tri = (jnp.where(s64 <= t64, 1.0, 0.0).astype(BF16), jnp.where(s64 >= t64, 1.0, 0.0).astype(BF16))
    lane_head0 = lax.broadcasted_iota(jnp.int32, (CHUNK, PAIR), 1) < HEAD
    eye_f = jnp.where(eye, 1.0, 0.0)
    same16 = (ri // INV_BLOCK) == (ci // INV_BLOCK)

    def stack(x):
        return jnp.concatenate([jnp.where(lane_head0, x, 0.0), jnp.where(lane_head0, 0.0, x)], axis=0)

    def fold(x):
        return x[:CHUNK] + x[CHUNK:]

    def chunk_rows(c):
        return pl.ds(pl.multiple_of(c * CHUNK, CHUNK), CHUNK)

    def each_item(body):
        for z in range(2):
            lax.fori_loop(0, n_chunks, lambda c, carry, z=z: (body(z, c), carry)[1], 0, unroll=unroll)

    def stage_operands(z, c):
        rows = chunk_rows(c)
        r_c, kn_c = rr_s[rows, :], kn_s[rows, :]
        lw_c, kz_c, bz_c = lw_s[z, rows, :], kz_s[z, rows, :], bz_s[z, rows, :]
        lg = _dot_exact_lhs(tri[z], lw_c)
        total = lg[CHUNK - 1:CHUNK, :] if z == 0 else lg[0:1, :]
        e_neg = jnp.exp(-lg)
        e_rem = jnp.exp(total - lg)
        abar = stack(kn_c * jnp.exp(lg - lw_c))
        rt = stack(r_c * jnp.exp(lg))
        big = _bdot_nt(jnp.concatenate([abar, rt], axis=0),
                       jnp.concatenate([stack(bz_c * e_neg), stack(kz_c * e_neg)], axis=0))
        mb = jnp.where(strict[z], big[:c2, :c2], 0.0)
        abar_s[z, c] = abar.astype(BF16)
        rt_s[z, c] = rt
        wt_s[z, c] = jnp.concatenate([stack(kz_c * e_rem), -stack(bz_c * e_rem)], axis=0).T.astype(BF16)
        gl_s[z, c] = jnp.exp(total)
        m0 = jnp.where(same16, mb, 0.0)
        tm_s[z, c] = eye_f - m0
        p_s[z, c] = m0.astype(BF16)
        mb_s[z, c] = mb.astype(BF16)
        mk_s[z, c] = jnp.where(strict[z], big[:c2, c2:], 0.0).astype(BF16)
        nkb_s[z, c] = jnp.concatenate([jnp.where(incl[z], big[c2:, c2:], 0.0),
                                       -jnp.where(incl[z], big[c2:, :c2], 0.0)], axis=1).astype(BF16)

    def stage_values(c, carry):
        vst_s[c] = stack(vv_s[chunk_rows(c), :]).astype(BF16)
        return carry

    lax.fori_loop(0, n_chunks, stage_values, 0, unroll=unroll)
    each_item(stage_operands)

    def stage_square(z, c):
        m0 = p_s[z, c]
        p_s[z, c] = jnp.dot(m0, m0, preferred_element_type=F32).astype(BF16)
        mkv = jnp.dot(mk_s[z, c], vst_s[c], preferred_element_type=F32)
        txr_s[z, c] = jnp.concatenate([abar_s[z, c], mkv.astype(BF16)], axis=1)

    each_item(stage_square)

    n_round = INV_BLOCK.bit_length() - 2
    for j in range(n_round):
        def stage_double(z, c, last=(j == n_round - 1)):
            tm, p = tm_s[z, c], p_s[z, c]
            tm_s[z, c] = tm + jnp.dot(tm.astype(BF16), p, preferred_element_type=F32)
            if not last:
                p_s[z, c] = jnp.dot(p, p, preferred_element_type=F32).astype(BF16)

        each_item(stage_double)

    size = INV_BLOCK
    while size < CHUNK:
        off_diag = ((ri // (2 * size)) == (ci // (2 * size))) & ((ri // size) != (ci // size))

        def stage_merge_rhs(z, c, off_diag=off_diag):
            cross = jnp.where(off_diag, mb_s[z, c], jnp.zeros((), BF16))
            p_s[z, c] = jnp.dot(cross, tm_s[z, c].astype(BF16), preferred_element_type=F32).astype(BF16)

        def stage_merge(z, c):
            tm = tm_s[z, c]
            tm_s[z, c] = tm - jnp.dot(tm.astype(BF16), p_s[z, c], preferred_element_type=F32)

        each_item(stage_merge_rhs)
        each_item(stage_merge)
        size *= 2

    def stage_apply(z, c):
        tx_s[z, c] = jnp.dot(tm_s[z, c].astype(BF16), txr_s[z, c], preferred_element_type=F32)

    each_item(stage_apply)

    def stage_maps(z, c):
        tx = tx_s[z, c]
        ta = tx[:, :PAIR].astype(BF16)
        x1 = tx[:, PAIR:].astype(BF16)
        z2 = jnp.concatenate([vst_s[c], x1], axis=0)
        wt = wt_s[z, c]
        h_s[z, c] = jnp.dot(wt, z2, preferred_element_type=F32)
        g_s[z, c] = jnp.where(eye, gl_s[z, c], 0.0) + jnp.dot(wt[:, c2:], ta, preferred_element_type=F32)
        nkb = nkb_s[z, c]
        q_s[z, c] = fold(rt_s[z, c] + jnp.dot(nkb[:, c2:], ta, preferred_element_type=F32))
        y0_s[z, c] = fold(jnp.dot(nkb, z2, preferred_element_type=F32))

    each_item(stage_maps)

    zero64 = jnp.zeros((HEAD, HEAD), F32)
    for z in range(2):
        a = jnp.concatenate([jnp.concatenate([s0_ref[z, 0], zero64], axis=1),
                             jnp.concatenate([zero64, s0_ref[z, 1]], axis=1)], axis=0)
        st_s[z] = a.T

    def propagate(i, carry):
        for z in range(2):
            c = i if z == 0 else n_chunks - 1 - i
            s = st_s[z].astype(BF16)
            y_s[z, chunk_rows(c), :] = _bdot(q_s[z, c], s) + y0_s[z, c]
            st_s[z] = _bdot(g_s[z, c], s) + h_s[z, c]
        return carry

    lax.fori_loop(0, n_chunks, propagate, 0)
    for z in range(2):
        a = st_s[z].T
        sfin_ref[z, 0] = a[:HEAD, :HEAD]
        sfin_ref[z, 1] = a[HEAD:, HEAD:]

    y = y_s[0] + y_s[1]
    mu = _dot_exact_rhs(y, ones_bd) * (1.0 / HEAD)
    yc = y - mu
    var = _dot_exact_rhs(yc * yc, ones_bd) * (1.0 / HEAD)
    yn = yc * lax.rsqrt(var + GN_EPS) * gng_ref[...] + gnb_ref[...]
    out_ref[...] = (yn + bonus) * gate


def _rwkv_mix(pa, first_tok, n_seq, seq_len, s0, p):
    a_width = p["w0"].shape[-1]
    n_pairs = a_width // PAIR
    heads = a_width // HEAD
    t0 = first_tok // seq_len
    lora_w = 3 * PAIR
    lora_blk = 3 * a_width // lora_w
    n_chunks = seq_len // CHUNK
    col = lambda off: (lambda s, q: (t0 + s, off + q))
    vec = lambda off: (lambda s, q: (0, off + q))
    in_specs = [
        pl.BlockSpec((seq_len, PAIR), col(0)),
        pl.BlockSpec((seq_len, PAIR), col(n_pairs)),
        pl.BlockSpec((seq_len, PAIR), col(2 * n_pairs)),
        pl.BlockSpec((seq_len, lora_w), lambda s, q: (t0 + s, lora_blk)),
        pl.BlockSpec((1, PAIR), vec(0)),
        pl.BlockSpec((1, PAIR), vec(n_pairs)),
        pl.BlockSpec((1, PAIR), vec(2 * n_pairs)),
        pl.BlockSpec((1, lora_w), lambda s, q: (0, lora_blk)),
        pl.BlockSpec((2, PAIR), vec(0)),
        pl.BlockSpec((2, HEAD, PAIR), lambda s, q: (0, 0, q)),
        pl.BlockSpec((2, PAIR), vec(0)),
        pl.BlockSpec((2, HEAD, PAIR), lambda s, q: (0, 0, q)),
        pl.BlockSpec((PAIR, PAIR), vec(0)),
        pl.BlockSpec((1, PAIR), vec(0)),
        pl.BlockSpec((1, PAIR), vec(0)),
        pl.BlockSpec((1, PAIR), vec(0)),
        pl.BlockSpec((1, PAIR), vec(0)),
        pl.BlockSpec((1, PAIR), vec(0)),
        pl.BlockSpec((None, 2, 2, HEAD, HEAD), lambda s, q: (s, 0, q, 0, 0)),
    ]
    seq2 = lambda: pltpu.VMEM((2, seq_len, PAIR), F32)
    seq1 = lambda: pltpu.VMEM((seq_len, PAIR), F32)
    item = lambda rows, cols, dt: pltpu.VMEM((2, n_chunks, rows, cols), dt)
    out, s_fin = pl.pallas_call(
        functools.partial(_rwkv_kernel, seq_len=seq_len, unroll=min(n_chunks, 4)),
        out_shape=[jax.ShapeDtypeStruct((n_seq * seq_len, a_width), F32),
                   jax.ShapeDtypeStruct((n_seq, 2, heads, HEAD, HEAD), F32)],
        grid=(n_seq, n_pairs),
        in_specs=in_specs,
        out_specs=[pl.BlockSpec((seq_len, PAIR), lambda s, q: (s, q)),
                   pl.BlockSpec((None, 2, 2, HEAD, HEAD), lambda s, q: (s, 0, q, 0, 0))],
        scratch_shapes=[seq2(), seq2(), seq2(), seq1(), seq1(), seq1(),
                        item(PAIR, PAIR, BF16),
                        item(PAIR, PAIR, F32),
                        item(PAIR, 2 * PAIR, BF16),
                        pltpu.VMEM((n_chunks, PAIR, PAIR), BF16),
                        item(PAIR, PAIR, BF16),
                        item(PAIR, PAIR, F32),
                        item(PAIR, PAIR, BF16),
                        item(PAIR, PAIR, BF16),
                        item(PAIR, 2 * PAIR, BF16),
                        item(PAIR, 2 * PAIR, BF16),
                        item(PAIR, 2 * PAIR, F32),
                        item(1, PAIR, F32),
                        item(PAIR, PAIR, F32), item(PAIR, PAIR, F32),
                        item(CHUNK, PAIR, F32), item(CHUNK, PAIR, F32),
                        seq2(), pltpu.VMEM((2, PAIR, PAIR), F32)],
        compiler_params=_cparams(("arbitrary", "arbitrary")),
        name=f"rwkv_mix_t{seq_len}",
    )(pa, pa, pa, pa, p["ts_mu"], p["ts_mu"], p["ts_mu"], p["ts_mu"],
      p["w0"], p["w_up"], p["a0"], p["a_up"], p["g_up"], p["k_k"], p["k_a"], p["r_k"], p["gn_g"], p["gn_b"], s0)
    return out, s_fin


def _conv_kernel(pb_ref, pwb_ref, dw_ref, dwb_ref, g_ref, b_ref, o_ref, pad_s, *, seq_len):
    width = o_ref.shape[1]
    halo = 16
    sub = 64
    pbv = pb_ref[...] + pwb_ref[...]
    glu = pbv[:, :width] * jax.nn.sigmoid(pbv[:, width:])
    pad_s[0:halo, :] = jnp.zeros((halo, width), F32)
    pad_s[halo:halo + seq_len, :] = glu
    pad_s[halo + seq_len:, :] = jnp.zeros((halo, width), F32)
    first = halo - CONV_W // 2
    for c in range(seq_len // sub):
        acc = jnp.zeros((sub, width), F32)
        for d in range(CONV_W):
            start = c * sub + first + d
            acc = acc + pad_s[start:start + sub, :] * dw_ref[d:d + 1, :]
        y = _layer_norm(acc + dwb_ref[...], g_ref[...], b_ref[...], LN_EPS)
        o_ref[c * sub:(c + 1) * sub, :] = y * jax.nn.sigmoid(y)


def _conv_module(pb, first_tok, n_seq, seq_len, pw_b, dw_w, dw_b, cln_g, cln_b):
    width = dw_w.shape[1]
    t0 = first_tok // seq_len
    const = lambda s: (0, 0)
    return pl.pallas_call(
        functools.partial(_conv_kernel, seq_len=seq_len),
        out_shape=jax.ShapeDtypeStruct((n_seq * seq_len, width), F32),
        grid=(n_seq,),
        in_specs=[pl.BlockSpec((seq_len, 2 * width), lambda s: (t0 + s, 0)),
                  pl.BlockSpec((1, 2 * width), const),
                  pl.BlockSpec((CONV_W, width), const),
                  pl.BlockSpec((1, width), const),
                  pl.BlockSpec((1, width), const),
                  pl.BlockSpec((1, width), const)],
        out_specs=pl.BlockSpec((seq_len, width), lambda s: (s, 0)),
        scratch_shapes=[pltpu.VMEM((seq_len + 32, width), F32)],
        compiler_params=_cparams(("arbitrary",)),
        name=f"conv_module_t{seq_len}",
    )(pb, pw_b.reshape(1, -1), dw_w, dw_b.reshape(1, -1), cln_g.reshape(1, -1), cln_b.reshape(1, -1))


def _diff_lambda(lq1, lk1, lq2, lk2, lam_init):
    dot = lambda a, b: jnp.exp(jnp.sum(a[...] * b[...], axis=-1, keepdims=True))
    return dot(lq1, lk1) - dot(lq2, lk2) + lam_init


def _diff_attend(q, keys, vals, lam, lam_init, subln):
    nq = q.shape[0]
    lane_first = lax.broadcasted_iota(jnp.int32, q.shape, 1) < HEAD
    q_st = jnp.concatenate([jnp.where(lane_first, q, 0.0), jnp.where(lane_first, 0.0, q)], axis=0)
    s = _bdot_nt(q_st, keys) * (HEAD ** -0.5)
    s = s - jnp.max(s, axis=-1, keepdims=True)
    pexp = jnp.exp(s)
    ov = _bdot(pexp, vals) / jnp.sum(pexp, axis=-1, keepdims=True)
    o = ov[:nq] - lam * ov[nq:]
    o = o * lax.rsqrt(jnp.mean(o * o, axis=-1, keepdims=True) + RMS_EPS)
    return o * subln * (1.0 - lam_init)


def _head_cols(h):
    return slice(h * PAIR, (h + 1) * PAIR)


def _attn_prompt_kernel(q_ref, k_ref, v_ref, lq1, lk1, lq2, lk2, sub_ref, o_ref, *, lam_init):
    lam = _diff_lambda(lq1, lk1, lq2, lk2, lam_init)
    for h in range(q_ref.shape[1] // PAIR):
        c = _head_cols(h)
        o_ref[:, c] = _diff_attend(q_ref[:, c], k_ref[:, c], v_ref[:, c], lam, lam_init, sub_ref[...])


def _rope(x, cos, sin_signed):
    lane = lax.broadcasted_iota(jnp.int32, x.shape, 1)
    quarter = HEAD // 4
    partner = jnp.where((lane // quarter) % 2 == 0,
                        pltpu.roll(x, LANES - quarter, 1), pltpu.roll(x, quarter, 1))
    return x * cos + partner * sin_signed


def _attn_sample_kernel(q_ref, k_ref, v_ref, ck_ref, cv_ref, cq_ref, sq_ref, cosk_ref, sink_ref,
                        lq1, lk1, lq2, lk2, sub_ref, o_ref, *, lam_init):
    lam = _diff_lambda(lq1, lk1, lq2, lk2, lam_init)
    for h in range(q_ref.shape[1] // PAIR):
        c = _head_cols(h)
        q = _rope(q_ref[:, c], cq_ref[...], sq_ref[...])
        kl = _rope(k_ref[:, c], cosk_ref[...], sink_ref[...])
        keys = jnp.concatenate([ck_ref[:, c], kl], axis=0)
        vals = jnp.concatenate([cv_ref[:, c], v_ref[:, c]], axis=0)
        o_ref[:, c] = _diff_attend(q, keys, vals, lam, lam_init, sub_ref[...])


def _rope_tables(n_lat, grid_w):
    quarter = HEAD // 4
    inv = ROPE_BASE ** (-jnp.arange(quarter, dtype=F32) / quarter)
    t = jnp.arange(n_lat)
    row_ang = (t // grid_w).astype(F32)[:, None] * inv
    col_ang = (t % grid_w).astype(F32)[:, None] * inv
    ang = jnp.concatenate([row_ang, row_ang, col_ang, col_ang] * 2, axis=-1)
    sign = jnp.tile(jnp.concatenate([-jnp.ones(quarter, F32), jnp.ones(quarter, F32)]), 2 * HEAD // (2 * quarter))
    return jnp.cos(ang), jnp.sin(ang) * sign


def _attention(q, k_p, v_p, k_s, v_s, cache_k, cache_v, p, lam_init, n_prompt, seq, n_lat_req, n_lat, grid_w):
    inner = q.shape[1]
    vec = lambda n: pl.BlockSpec((1, n), lambda *_: (0, 0))
    lam_args = [p[n].reshape(1, -1) for n in ("lq1", "lk1", "lq2", "lk2")] + [p["subln_g"].reshape(1, -1)]
    lam_specs = [vec(HEAD)] * 4 + [vec(PAIR)]
    blk = pl.BlockSpec((seq, inner), lambda b: (b, 0))
    o_prompt = pl.pallas_call(
        functools.partial(_attn_prompt_kernel, lam_init=lam_init),
        out_shape=jax.ShapeDtypeStruct((n_prompt * seq, inner), F32),
        grid=(n_prompt,),
        in_specs=[blk, blk, blk] + lam_specs,
        out_specs=blk,
        compiler_params=_cparams(("arbitrary",)),
        name="diff_attn_prompt",
    )(q, k_p, v_p, *lam_args)

    past = cache_k.shape[1]
    qt = ROW_TILE
    n_qt = n_lat // qt
    q0 = n_prompt * seq // qt
    cos, sin = _rope_tables(n_lat, grid_w)
    lat = pl.BlockSpec((n_lat, inner), lambda b, t: (b, 0))
    ctx = pl.BlockSpec((None, past, inner), lambda b, t: (b, 0, 0))
    o_sample = pl.pallas_call(
        functools.partial(_attn_sample_kernel, lam_init=lam_init),
        out_shape=jax.ShapeDtypeStruct((n_lat_req * n_lat, inner), F32),
        grid=(n_lat_req, n_qt),
        in_specs=[pl.BlockSpec((qt, inner), lambda b, t: (q0 + b * n_qt + t, 0)),
                  lat, lat, ctx, ctx,
                  pl.BlockSpec((qt, PAIR), lambda b, t: (t, 0)),
                  pl.BlockSpec((qt, PAIR), lambda b, t: (t, 0)),
                  pl.BlockSpec((n_lat, PAIR), lambda b, t: (0, 0)),
                  pl.BlockSpec((n_lat, PAIR), lambda b, t: (0, 0))] + lam_specs,
        out_specs=pl.BlockSpec((qt, inner), lambda b, t: (b * n_qt + t, 0)),
        compiler_params=_cparams(("arbitrary", "arbitrary")),
        name="diff_attn_sample",
    )(q, k_s, v_s, cache_k, cache_v, cos, sin, cos, sin, *lam_args)
    return o_prompt, o_sample


def kernel(x_prompt, x_sample, state_wkv, cache_k, cache_v, c, c_ctx, mod_w, mod_b, ln1_g, ln1_b, ln2_g, ln2_b, ev_in_w, ev_pw_b, ev_ts_mu, ev_w0, ev_w_up, ev_a0, ev_a_up, ev_g_up, ev_k_k, ev_k_a, ev_r_k, ev_gn_g, ev_gn_b, ev_dw_w, ev_dw_b, ev_cln_g, ev_cln_b, ev_out_w, od_in_w, od_lq1, od_lk1, od_lq2, od_lk2, od_subln_g, od_out_w, rg_w, rg_b, re_w, re_b, ffn_w_gate, ffn_w_up, ffn_w_down):
    nbp, seq, d = x_prompt.shape
    nbs, n_lat, _ = x_sample.shape
    depth = mod_w.shape[0]
    n_prompt_tok = nbp * seq
    geom = (n_prompt_tok, n_lat)
    alpha = (2 * depth) ** 0.25
    a_width = ev_w0.shape[-1]
    a_cols = ev_ts_mu.shape[-1]
    heads = a_width // HEAD
    grid_w = 64
    assert n_prompt_tok % MOE_TILE == 0 and n_lat % MOE_TILE == 0 and seq % ROW_TILE == 0
    assert 1 + nbs <= 8 and n_lat % ROW_TILE == 0

    xs = (x_prompt.reshape(n_prompt_tok, d), x_sample.reshape(nbs * n_lat, d))
    cond = jnp.concatenate([c_ctx[None, :], c, jnp.zeros((8 - 1 - nbs, d), F32)], axis=0)
    mod = _ada_table(cond, mod_w, mod_b)

    new_wkv, new_k, new_v = [], [], []
    for l in range(depth):
        last = l == depth - 1
        if l % 2 == 0:
            i = l // 2
            pa, pb = _mod_matmul(xs, mod, l, ev_in_w[i], ((a_cols, False), (ev_in_w.shape[-1] - a_cols, False)), geom)
            prm = dict(ts_mu=ev_ts_mu[i][None, :], w0=ev_w0[i], w_up=ev_w_up[i], a0=ev_a0[i], a_up=ev_a_up[i],
                       g_up=ev_g_up[i], k_k=ev_k_k[i][None, :], k_a=ev_k_a[i][None, :],
                       r_k=ev_r_k[i].reshape(1, a_width), gn_g=ev_gn_g[i][None, :], gn_b=ev_gn_b[i][None, :])
            zero_state = jnp.zeros((nbp, 2, heads, HEAD, HEAD), F32)
            oa_p, st = _rwkv_mix(pa, 0, nbp, seq, zero_state, prm)
            oa_s, _ = _rwkv_mix(pa, n_prompt_tok, nbs, n_lat, state_wkv[:, i], prm)
            conv = (ev_pw_b[i], ev_dw_w[i], ev_dw_b[i], ev_cln_g[i], ev_cln_b[i])
            ob_p = _conv_module(pb, 0, nbp, seq, *conv)
            ob_s = _conv_module(pb, n_prompt_tok, nbs, n_lat, *conv)
            hs = [(oa_p, oa_s), (ob_p, ob_s)]
            out_w = ev_out_w[i]
            new_wkv.append(st)
        else:
            j = l // 2
            lam_init = 0.8 - 0.6 * math.exp(-0.3 * l)
            inner = od_in_w.shape[-1] // 3
            q, k_p, k_s, v_p, v_s = _mod_matmul(xs, mod, l, od_in_w[j],
                                                 ((inner, False), (inner, True), (inner, True)), geom)
            prm = dict(lq1=od_lq1[j], lk1=od_lk1[j], lq2=od_lq2[j], lk2=od_lk2[j], subln_g=od_subln_g[j])
            ck = cache_k[:, j].reshape(nbs, -1, inner)
            cv = cache_v[:, j].reshape(nbs, -1, inner)
            hs = [_attention(q, k_p, v_p, k_s, v_s, ck, cv, prm, lam_init, nbp, seq, nbs, n_lat, grid_w)]
            out_w = od_out_w[j]
            new_k.append(k_p.reshape(nbp, seq, inner // PAIR, PAIR))
            new_v.append(v_p.reshape(nbp, seq, inner // PAIR, PAIR))
        x = _out_proj_norm(hs, xs, mod, l, out_w, ln1_g[l], ln1_b[l], alpha, geom)
        rw = jnp.concatenate([rg_w[l], re_w[l],
                              jnp.zeros((d, ROUTER_LANES - N_GROUPS - N_EXPERTS), F32)], axis=1)
        rb = jnp.concatenate([rg_b[l], re_b[l], jnp.zeros((ROUTER_LANES - N_GROUPS - N_EXPERTS,), F32)])[None, :]
        xs = tuple(_moe_norm(x, mod, l, rw, rb, ffn_w_gate, ffn_w_up, ffn_w_down, ln2_g[l], ln2_b[l], alpha, geom,
                             split_out=last))

    y_prompt = xs[0].reshape(nbp, seq, d)
    y_sample = xs[1].reshape(nbs, n_lat, d)
    return (y_prompt, y_sample, jnp.stack(new_wkv, axis=1), jnp.stack(new_k, axis=1), jnp.stack(new_v, axis=1))
```

```python
import functools
import math

import jax
import jax.numpy as jnp
from jax import lax
from jax.experimental import pallas as pl
from jax.experimental.pallas import tpu as pltpu

F32 = jnp.float32
BF16 = jnp.bfloat16

LANES = 128
HEAD = 64
PAIR = 2 * HEAD
CHUNK = 64
INV_BLOCK = 16
ROW_TILE = 256
MOE_TILE = 1024
CONV_W = 31
N_GROUPS = 4
EXPERTS_PER_GROUP = 8
N_EXPERTS = N_GROUPS * EXPERTS_PER_GROUP
ROUTER_LANES = 128
LN_EPS = 1e-5
GN_EPS = 64e-5
RMS_EPS = 1e-5
ROPE_BASE = 10000.0
EXP_M05 = math.exp(-0.5)
VMEM_LIMIT = 56 * 1024 * 1024


def _bdot(a, b):
    return jnp.dot(a.astype(BF16), b.astype(BF16), preferred_element_type=F32)


def _bdot_nt(a, b):
    return lax.dot_general(a.astype(BF16), b.astype(BF16), (((1,), (1,)), ((), ())),
                           preferred_element_type=F32)


def _split2(a):
    hi = a.astype(BF16)
    lo = (a - hi.astype(F32)).astype(BF16)
    return hi, lo


def _split3(a):
    hi = a.astype(BF16)
    r1 = a - hi.astype(F32)
    mid = r1.astype(BF16)
    lo = (r1 - mid.astype(F32)).astype(BF16)
    return hi, mid, lo


def _dot_f32ish(a, b):
    ah, al = _split2(a)
    bh, bl = _split2(b)
    d = lambda x, y: jnp.dot(x, y, preferred_element_type=F32)
    return d(ah, bh) + d(ah, bl) + d(al, bh)


def _dot_exact_lhs(m_bf16, a):
    h, m, l = _split3(a)
    d = lambda y: jnp.dot(m_bf16, y, preferred_element_type=F32)
    return d(h) + d(m) + d(l)


def _dot_exact_rhs(a, m_bf16):
    h, m, l = _split3(a)
    d = lambda x: jnp.dot(x, m_bf16, preferred_element_type=F32)
    return d(h) + d(m) + d(l)


def _layer_norm(z, g, b, eps):
    mu = jnp.mean(z, axis=-1, keepdims=True)
    zc = z - mu
    var = jnp.mean(zc * zc, axis=-1, keepdims=True)
    return zc * lax.rsqrt(var + eps) * g + b


def _cond_row(tile, tile_rows, n_prompt_tok, dec_seq):
    tok = tile * tile_rows
    return jnp.where(tok < n_prompt_tok, 0, 1 + (tok - n_prompt_tok) // dec_seq)


def _cast_rows(src_ref, dst_ref, rows_per_step=128):
    n = src_ref.shape[0]
    for s in range(0, n, rows_per_step):
        dst_ref[s:s + rows_per_step, :] = src_ref[s:s + rows_per_step, :].astype(dst_ref.dtype)


def _cparams(sem):
    return pltpu.CompilerParams(dimension_semantics=sem, vmem_limit_bytes=VMEM_LIMIT)


def _ada_kernel(cond_ref, w_ref, b_ref, o_ref):
    c = cond_ref[...]
    s = c * jax.nn.sigmoid(c)
    o_ref[0, 0] = _dot_f32ish(s, w_ref[0]) + b_ref[0, 0]


def _ada_table(cond8, mod_w, mod_b):
    depth, d, _ = mod_w.shape
    return pl.pallas_call(
        _ada_kernel,
        out_shape=jax.ShapeDtypeStruct((depth, 6, 8, d), F32),
        grid=(depth, 6),
        in_specs=[pl.BlockSpec((8, d), lambda l, j: (0, 0)),
                  pl.BlockSpec((1, d, d), lambda l, j: (l, 0, j)),
                  pl.BlockSpec((1, 1, 1, d), lambda l, j: (l, j, 0, 0))],
        out_specs=pl.BlockSpec((1, 1, 8, d), lambda l, j: (l, j, 0, 0)),
        compiler_params=_cparams(("arbitrary", "arbitrary")),
        name="ada_table",
    )(cond8, mod_w, mod_b.reshape(depth, 6, 1, d))


def _mod_spec(layer, which, d):
    return pl.BlockSpec((None, None, 8, d), lambda *_: (layer, which, 0, 0))


def _group_specs(tile, width, n_prompt_tiles):
    return [pl.BlockSpec((tile, width), lambda i, *_: (jnp.minimum(i, n_prompt_tiles - 1), 0)),
            pl.BlockSpec((tile, width), lambda i, *_: (jnp.maximum(i - n_prompt_tiles, 0), 0))]


def _load_group(i, n_prompt_tiles, refs, rows=slice(None)):
    if len(refs) == 1:
        return refs[0][rows, :]
    return jnp.where(i < n_prompt_tiles, refs[0][rows, :], refs[1][rows, :])


def _store_group(i, n_prompt_tiles, refs, value, rows=slice(None)):
    if len(refs) == 1:
        refs[0][rows, :] = value
        return

    @pl.when(i < n_prompt_tiles)
    def _():
        refs[0][rows, :] = value

    @pl.when(i >= n_prompt_tiles)
    def _():
        refs[1][rows, :] = value


def _modmm_kernel(*refs, n_x, outs, geom):
    x_refs = refs[:n_x]
    sh_ref, sc_ref, w_ref = refs[n_x:n_x + 3]
    o_refs, wb = refs[n_x + 3:-1], refs[-1]
    i = pl.program_id(0)
    n_pt = geom[0] // ROW_TILE

    @pl.when(i == 0)
    def _():
        _cast_rows(w_ref, wb)

    r = _cond_row(i, ROW_TILE, *geom)
    h = _load_group(i, n_pt, x_refs) * (1.0 + sc_ref[pl.ds(r, 1), :]) + sh_ref[pl.ds(r, 1), :]
    hb = h.astype(BF16)
    off = 0
    k = 0
    for n, split in outs:
        y = jnp.dot(hb, wb[:, off:off + n], preferred_element_type=F32)
        _store_group(i, n_pt, o_refs[k:k + 1 + split], y)
        off += n
        k += 1 + split


def _mod_matmul(xs, mod, layer, w, outs, geom):
    d, n_out = w.shape
    n_prompt_tok = geom[0]
    n_tok = sum(x.shape[0] for x in xs)
    n_pt = n_prompt_tok // ROW_TILE
    whole = lambda n: [pl.BlockSpec((ROW_TILE, n), lambda i: (i, 0))]
    x_specs = whole(d) if len(xs) == 1 else _group_specs(ROW_TILE, d, n_pt)
    out_shape, out_specs = [], []
    for n, split in outs:
        if split:
            out_shape += [jax.ShapeDtypeStruct((n_prompt_tok, n), F32),
                          jax.ShapeDtypeStruct((n_tok - n_prompt_tok, n), F32)]
            out_specs += _group_specs(ROW_TILE, n, n_pt)
        else:
            out_shape += [jax.ShapeDtypeStruct((n_tok, n), F32)]
            out_specs += whole(n)
    return pl.pallas_call(
        functools.partial(_modmm_kernel, n_x=len(xs), outs=outs, geom=geom),
        out_shape=out_shape,
        grid=(n_tok // ROW_TILE,),
        in_specs=x_specs + [_mod_spec(layer, 0, d), _mod_spec(layer, 1, d),
                            pl.BlockSpec((d, n_out), lambda i: (0, 0), pipeline_mode=pl.Buffered(1))],
        out_specs=out_specs,
        scratch_shapes=[pltpu.VMEM((d, n_out), BF16)],
        compiler_params=_cparams(("arbitrary",)),
        name=f"mod_matmul_l{layer}",
    )(*xs, mod, mod, w)


def _outproj_kernel(*refs, arity, alpha, geom):
    groups, k = [], 0
    for n in arity:
        groups.append(refs[k:k + n])
        k += n
    gate_ref, w_ref, g_ref, b_ref, o_ref, wb = refs[k:]
    h_groups, x_group = groups[:-1], groups[-1]
    i = pl.program_id(0)
    n_pt = geom[0] // ROW_TILE

    @pl.when(i == 0)
    def _():
        _cast_rows(w_ref, wb)

    r = _cond_row(i, ROW_TILE, *geom)
    y = None
    off = 0
    for h_refs in h_groups:
        n = h_refs[0].shape[1]
        h = _load_group(i, n_pt, h_refs)
        part = jnp.dot(h.astype(BF16), wb[off:off + n, :], preferred_element_type=F32)
        y = part if y is None else y + part
        off += n
    z = alpha * _load_group(i, n_pt, x_group) + gate_ref[pl.ds(r, 1), :] * y
    o_ref[...] = _layer_norm(z, g_ref[...], b_ref[...], LN_EPS)


def _out_proj_norm(hs, xs, mod, layer, w, ln_g, ln_b, alpha, geom):
    d = w.shape[1]
    n_tok = sum(x.shape[0] for x in xs)
    n_pt = geom[0] // ROW_TILE
    specs = []
    for grp in list(hs) + [xs]:
        n = grp[0].shape[1]
        specs += [pl.BlockSpec((ROW_TILE, n), lambda i: (i, 0))] if len(grp) == 1 else _group_specs(ROW_TILE, n, n_pt)
    return pl.pallas_call(
        functools.partial(_outproj_kernel, arity=tuple(len(g) for g in list(hs) + [xs]), alpha=alpha, geom=geom),
        out_shape=jax.ShapeDtypeStruct((n_tok, d), F32),
        grid=(n_tok // ROW_TILE,),
        in_specs=specs + [
            _mod_spec(layer, 2, d),
            pl.BlockSpec(w.shape, lambda i: (0, 0), pipeline_mode=pl.Buffered(1)),
            pl.BlockSpec((1, d), lambda i: (0, 0)),
            pl.BlockSpec((1, d), lambda i: (0, 0))],
        out_specs=pl.BlockSpec((ROW_TILE, d), lambda i: (i, 0)),
        scratch_shapes=[pltpu.VMEM(w.shape, BF16)],
        compiler_params=_cparams(("arbitrary",)),
        name=f"out_proj_norm_l{layer}",
    )(*[a for g in hs for a in g], *xs, mod, w, ln_g.reshape(1, d), ln_b.reshape(1, d))


def _route(logits):
    neg = jnp.float32(-3e38)
    big = jnp.int32(1 << 20)
    lane = lax.broadcasted_iota(jnp.int32, logits.shape, 1)
    is_g = lane < N_GROUPS
    gl = jnp.where(is_g, logits, neg)
    gmax = jnp.max(gl, axis=-1, keepdims=True)
    gidx = jnp.min(jnp.where(gl == gmax, lane, big), axis=-1, keepdims=True)
    gsum = jnp.sum(jnp.where(is_g, jnp.exp(gl - gmax), 0.0), axis=-1, keepdims=True)
    g_p = 1.0 / gsum
    lo = N_GROUPS + gidx * EXPERTS_PER_GROUP
    in_grp = (lane >= lo) & (lane < lo + EXPERTS_PER_GROUP)
    el = jnp.where(in_grp, logits, neg)
    v1 = jnp.max(el, axis=-1, keepdims=True)
    i1 = jnp.min(jnp.where(el == v1, lane, big), axis=-1, keepdims=True)
    el2 = jnp.where(lane == i1, neg, el)
    v2 = jnp.max(el2, axis=-1, keepdims=True)
    i2 = jnp.min(jnp.where(el2 == v2, lane, big), axis=-1, keepdims=True)
    e21 = jnp.exp(v2 - v1)
    den = 1.0 + e21
    w1 = g_p / den
    w2 = g_p * e21 / den
    return jnp.where(lane == i1, w1, 0.0) + jnp.where(lane == i2, w2, 0.0), gidx


def _moe_kernel(x_ref, sh_ref, sc_ref, gate_ref, rw_ref, rb_ref, wg_ref, wu_ref, wd_ref, g_ref, b_ref,
                *rest, alpha, geom):
    o_refs = rest[:-6]
    h_scr, gates_scr, acc_scr, perm_scr, permt_scr, bounds = rest[-6:]
    i = pl.program_id(0)
    e = pl.program_id(1)
    r = _cond_row(i, MOE_TILE, *geom)
    n_pt = geom[0] // MOE_TILE
    sub = ROW_TILE
    n_sub = MOE_TILE // sub

    @pl.when(e == 0)
    def _():
        lane = lax.broadcasted_iota(jnp.int32, (sub, ROUTER_LANES), 1)
        ri = lax.broadcasted_iota(jnp.int32, (sub, sub), 0)
        ci = lax.broadcasted_iota(jnp.int32, (sub, sub), 1)
        tril = jnp.where(ci <= ri, 1.0, 0.0).astype(BF16)
        running = jnp.zeros((1, ROUTER_LANES), F32)
        onehots, cums = [], []
        for s in range(n_sub):
            rows = slice(s * sub, (s + 1) * sub)
            h = x_ref[rows, :] * (1.0 + sc_ref[pl.ds(r, 1), :]) + sh_ref[pl.ds(r, 1), :]
            acc_scr[rows, :] = h
            logits = _dot_f32ish(h, rw_ref[...]) + rb_ref[...]
            gates, gidx = _route(logits)
            gates_scr[rows, :] = gates
            onehot = jnp.where(lane == gidx, 1.0, 0.0)
            cum = jnp.dot(tril, onehot.astype(BF16), preferred_element_type=F32) + running
            running = cum[sub - 1:sub, :]
            onehots.append(onehot)
            cums.append(cum)
        gi = lax.broadcasted_iota(jnp.int32, (ROUTER_LANES, ROUTER_LANES), 0)
        gj = lax.broadcasted_iota(jnp.int32, (ROUTER_LANES, ROUTER_LANES), 1)
        before = jnp.where((gi < gj) & (gi < N_GROUPS), 1.0, 0.0).astype(BF16)
        starts = _dot_exact_rhs(jnp.broadcast_to(running, (8, ROUTER_LANES)), before)[0:1, :]
        for g in range(N_GROUPS + 1):
            bounds[g] = starts[0, g].astype(jnp.int32)
        dest_cols = [jnp.sum(onehots[s] * (starts + cums[s] - 1.0), axis=-1, keepdims=True) for s in range(n_sub)]
        dest_row = jnp.concatenate(
            [jnp.broadcast_to(dc, (sub, ROUTER_LANES)).T[0:1, :] for dc in dest_cols], axis=1)
        col_id = lax.broadcasted_iota(jnp.int32, (sub, MOE_TILE), 1).astype(F32)
        row_id = lax.broadcasted_iota(jnp.int32, (sub, MOE_TILE), 0).astype(F32)
        for s in range(n_sub):
            rows = slice(s * sub, (s + 1) * sub)
            perm_scr[rows, :] = jnp.where(row_id + float(s * sub) == dest_row, 1.0, 0.0).astype(BF16)
            permt_scr[rows, :] = jnp.where(dest_cols[s] == col_id, 1.0, 0.0).astype(BF16)
        hb = acc_scr[...].astype(BF16)
        g_hi, g_mid, g_lo = _split3(gates_scr[...])
        for s in range(n_sub):
            rows = slice(s * sub, (s + 1) * sub)
            pm = perm_scr[rows, :]
            h_scr[rows, :] = jnp.dot(pm, hb, preferred_element_type=F32).astype(BF16)
        for s in range(n_sub):
            rows = slice(s * sub, (s + 1) * sub)
            pm = perm_scr[rows, :]
            d = lambda y: jnp.dot(pm, y, preferred_element_type=F32)
            gates_scr[rows, :] = d(g_hi) + d(g_mid) + d(g_lo)
            acc_scr[rows, :] = jnp.zeros((sub, acc_scr.shape[1]), F32)

    grp = e // EXPERTS_PER_GROUP
    first, end = bounds[grp], bounds[grp + 1]
    lane = lax.broadcasted_iota(jnp.int32, (sub, ROUTER_LANES), 1)
    for s in range(n_sub):
        rows = slice(s * sub, (s + 1) * sub)

        @pl.when((first < (s + 1) * sub) & (end > s * sub))
        def _():
            hb = h_scr[rows, :]
            ge = jnp.sum(jnp.where(lane == e + N_GROUPS, gates_scr[rows, :], 0.0), axis=-1, keepdims=True)
            a = jnp.dot(hb, wg_ref[0], preferred_element_type=F32)
            u = jnp.dot(hb, wu_ref[0], preferred_element_type=F32)
            hid = (a * jax.nn.sigmoid(a)) * u * ge
            acc_scr[rows, :] += jnp.dot(hid.astype(BF16), wd_ref[0], preferred_element_type=F32)

    @pl.when(e == N_EXPERTS - 1)
    def _():
        a_hi, a_lo = _split2(acc_scr[...])
        for s in range(n_sub):
            rows = slice(s * sub, (s + 1) * sub)
            pt = permt_scr[rows, :]
            moe = jnp.dot(pt, a_hi, preferred_element_type=F32) + jnp.dot(pt, a_lo, preferred_element_type=F32)
            z = alpha * x_ref[rows, :] + gate_ref[pl.ds(r, 1), :] * moe
            _store_group(i, n_pt, o_refs, _layer_norm(z, g_ref[...], b_ref[...], LN_EPS), rows)


def _moe_norm(x, mod, layer, rw, rb, w_gate, w_up, w_down, ln_g, ln_b, alpha, geom, split_out):
    n_tok, d = x.shape
    ff = w_gate.shape[-1]
    assert w_gate.dtype == BF16 and w_up.dtype == BF16 and w_down.dtype == BF16
    const = lambda i, e: (0, 0)
    if split_out:
        out_shape = [jax.ShapeDtypeStruct((geom[0], d), F32), jax.ShapeDtypeStruct((n_tok - geom[0], d), F32)]
        out_specs = _group_specs(MOE_TILE, d, geom[0] // MOE_TILE)
    else:
        out_shape = [jax.ShapeDtypeStruct((n_tok, d), F32)]
        out_specs = [pl.BlockSpec((MOE_TILE, d), lambda i, e: (i, 0))]
    return pl.pallas_call(
        functools.partial(_moe_kernel, alpha=alpha, geom=geom),
        out_shape=out_shape,
        grid=(n_tok // MOE_TILE, N_EXPERTS),
        in_specs=[pl.BlockSpec((MOE_TILE, d), lambda i, e: (i, 0)),
                  _mod_spec(layer, 3, d), _mod_spec(layer, 4, d), _mod_spec(layer, 5, d),
                  pl.BlockSpec((d, ROUTER_LANES), const),
                  pl.BlockSpec((1, ROUTER_LANES), const),
                  pl.BlockSpec((None, 1, d, ff), lambda i, e: (layer, e, 0, 0)),
                  pl.BlockSpec((None, 1, d, ff), lambda i, e: (layer, e, 0, 0)),
                  pl.BlockSpec((None, 1, ff, d), lambda i, e: (layer, e, 0, 0)),
                  pl.BlockSpec((1, d), const),
                  pl.BlockSpec((1, d), const)],
        out_specs=out_specs,
        scratch_shapes=[pltpu.VMEM((MOE_TILE, d), BF16),
                        pltpu.VMEM((MOE_TILE, ROUTER_LANES), F32),
                        pltpu.VMEM((MOE_TILE, d), F32),
                        pltpu.VMEM((MOE_TILE, MOE_TILE), BF16), pltpu.VMEM((MOE_TILE, MOE_TILE), BF16),
                        pltpu.SMEM((8,), jnp.int32)],
        compiler_params=_cparams(("arbitrary", "arbitrary")),
        name=f"moe_norm_l{layer}",
    )(x, mod, mod, mod, rw, rb, w_gate, w_up, w_down, ln_g.reshape(1, d), ln_b.reshape(1, d))


def _rwkv_kernel(r_ref, k_ref, v_ref, lora_ref, mur_ref, muk_ref, muv_ref, mul_ref,
                 w0_ref, wup_ref, a0_ref, aup_ref, gup_ref, kk_ref, ka_ref, rk_ref, gng_ref, gnb_ref, s0_ref,
                 out_ref, sfin_ref,
                 lw_s, kz_s, bz_s, rr_s, vv_s, kn_s,
                 abar_s, rt_s, wt_s, vst_s, p_s, tm_s, mb_s, mk_s, nkb_s, txr_s, tx_s, gl_s,
                 g_s, h_s, q_s, y0_s, y_s, st_s, *, seq_len, unroll):
    t_len = seq_len
    n_chunks = t_len // CHUNK
    c2 = 2 * CHUNK

    row = lax.broadcasted_iota(jnp.int32, (t_len, 1), 0)

    def tshift(x, mu):
        prev = jnp.where(row == 0, 0.0, pltpu.roll(x, 1, 0))
        nxt = jnp.where(row == t_len - 1, 0.0, pltpu.roll(x, t_len - 1, 0))
        return x + mu * (0.5 * (prev + nxt) - x)

    r = tshift(r_ref[...], mur_ref[...])
    k = tshift(k_ref[...], muk_ref[...])
    v = tshift(v_ref[...], muv_ref[...])
    lo = tshift(lora_ref[...], mul_ref[...])
    wd = jnp.tanh(lo[:, 0:PAIR])
    ad = lo[:, PAIR:2 * PAIR]
    gate = _bdot(jax.nn.sigmoid(lo[:, 2 * PAIR:3 * PAIR]), gup_ref[...])

    ri = lax.broadcasted_iota(jnp.int32, (PAIR, PAIR), 0)
    ci = lax.broadcasted_iota(jnp.int32, (PAIR, PAIR), 1)
    same_head = (ri // HEAD) == (ci // HEAD)
    ones_bd = jnp.where(same_head, 1.0, 0.0).astype(BF16)
    eye = ri == ci

    kk = k * kk_ref[...]
    ssq = _dot_exact_rhs(kk * kk, ones_bd)
    kk = kk * lax.rsqrt(jnp.maximum(ssq, 1e-24))
    wup2 = wup_ref[...].reshape(PAIR, PAIR)
    aup2 = aup_ref[...].reshape(PAIR, PAIR)
    bonus = jnp.zeros((t_len, PAIR), F32)
    for z in range(2):
        zrows = (ri // HEAD) == z
        w_pre = w0_ref[z:z + 1, :] + _bdot(wd, jnp.where(zrows, wup2, 0.0))
        lw_s[z] = -EXP_M05 * jax.nn.sigmoid(w_pre)
        a = jax.nn.sigmoid(a0_ref[z:z + 1, :] + _bdot(ad, jnp.where(zrows, aup2, 0.0)))
        kz = k * (1.0 + (a - 1.0) * ka_ref[...])
        kz_s[z] = kz
        bz_s[z] = kk * a
        bonus = bonus + _dot_exact_rhs(r * kz * rk_ref[...], ones_bd) * v
    rr_s[...] = r
    vv_s[...] = v
    kn_s[...] = kk

    rt_i = ri % CHUNK
    cs_i = ci % CHUNK
    strict = (cs_i < rt_i, cs_i > rt_i)
    incl = (cs_i <= rt_i, cs_i >= rt_i)
    t64 = lax.broadcasted_iota(jnp.int32, (CHUNK, CHUNK), 0)
    s64 = lax.broadcasted_iota(jnp.int32, (CHUNK, CHUNK), 1)
    tri = (jnp.where(s64 <= t64, 1.0, 0.0).astype(BF16), jnp.where(s64 >= t64, 1.0, 0.0).astype(BF16))
    lane_head0 = lax.broadcasted_iota(jnp.int32, (CHUNK, PAIR), 1) < HEAD
    eye_f = jnp.where(eye, 1.0, 0.0)
    same16 = (ri // INV_BLOCK) == (ci // INV_BLOCK)

    def stack(x):
        return jnp.concatenate([jnp.where(lane_head0, x, 0.0), jnp.where(lane_head0, 0.0, x)], axis=0)

    def fold(x):
        return x[:CHUNK] + x[CHUNK:]

    def chunk_rows(c):
        return pl.ds(pl.multiple_of(c * CHUNK, CHUNK), CHUNK)

    def each_item(body):
        def both_directions(c, carry):
            body(0, c)
            body(1, c)
            return carry

        lax.fori_loop(0, n_chunks, both_directions, 0, unroll=unroll)

    def stage_operands(z, c):
        rows = chunk_rows(c)
        r_c, kn_c = rr_s[rows, :], kn_s[rows, :]
        lw_c, kz_c, bz_c = lw_s[z, rows, :], kz_s[z, rows, :], bz_s[z, rows, :]
        lg = _dot_exact_lhs(tri[z], lw_c)
        total = lg[CHUNK - 1:CHUNK, :] if z == 0 else lg[0:1, :]
        e_neg = jnp.exp(-lg)
        e_rem = jnp.exp(total - lg)
        abar = stack(kn_c * jnp.exp(lg - lw_c))
        rt = stack(r_c * jnp.exp(lg))
        big = _bdot_nt(jnp.concatenate([abar, rt], axis=0),
                       jnp.concatenate([stack(bz_c * e_neg), stack(kz_c * e_neg)], axis=0))
        mb = jnp.where(strict[z], big[:c2, :c2], 0.0)
        abar_s[z, c] = abar.astype(BF16)
        rt_s[z, c] = rt
        wt_s[z, c] = jnp.concatenate([stack(kz_c * e_rem), -stack(bz_c * e_rem)], axis=0).T.astype(BF16)
        gl_s[z, c] = jnp.exp(total)
        m0 = jnp.where(same16, mb, 0.0)
        tm_s[z, c] = eye_f - m0
        p_s[z, c] = m0.astype(BF16)
        mb_s[z, c] = mb.astype(BF16)
        mk_s[z, c] = jnp.where(strict[z], big[:c2, c2:], 0.0).astype(BF16)
        nkb_s[z, c] = jnp.concatenate([jnp.where(incl[z], big[c2:, c2:], 0.0),
                                       -jnp.where(incl[z], big[c2:, :c2], 0.0)], axis=1).astype(BF16)

    def stage_values(c, carry):
        vst_s[c] = stack(vv_s[chunk_rows(c), :]).astype(BF16)
        return carry

    lax.fori_loop(0, n_chunks, stage_values, 0, unroll=unroll)
    each_item(stage_operands)

    def stage_square(z, c):
        m0 = p_s[z, c]
        p_s[z, c] = jnp.dot(m0, m0, preferred_element_type=F32).astype(BF16)
        mkv = jnp.dot(mk_s[z, c], vst_s[c], preferred_element_type=F32)
        txr_s[z, c] = jnp.concatenate([abar_s[z, c], mkv.astype(BF16)], axis=1)

    each_item(stage_square)

    n_round = INV_BLOCK.bit_length() - 2
    for j in range(n_round):
        def stage_double(z, c, last=(j == n_round - 1)):
            tm, p = tm_s[z, c], p_s[z, c]
            tm_s[z, c] = tm + jnp.dot(tm.astype(BF16), p, preferred_element_type=F32)
            if not last:
                p_s[z, c] = jnp.dot(p, p, preferred_element_type=F32).astype(BF16)

        each_item(stage_double)

    size = INV_BLOCK
    while size < CHUNK:
        off_diag = ((ri // (2 * size)) == (ci // (2 * size))) & ((ri // size) != (ci // size))

        def stage_merge_rhs(z, c, off_diag=off_diag):
            cross = jnp.where(off_diag, mb_s[z, c], jnp.zeros((), BF16))
            p_s[z, c] = jnp.dot(cross, tm_s[z, c].astype(BF16), preferred_element_type=F32).astype(BF16)

        def stage_merge(z, c):
            tm = tm_s[z, c]
            tm_s[z, c] = tm - jnp.dot(tm.astype(BF16), p_s[z, c], preferred_element_type=F32)

        each_item(stage_merge_rhs)
        each_item(stage_merge)
        size *= 2

    def stage_apply(z, c):
        tx_s[z, c] = jnp.dot(tm_s[z, c].astype(BF16), txr_s[z, c], preferred_element_type=F32)

    each_item(stage_apply)

    def stage_maps(z, c):
        tx = tx_s[z, c]
        ta = tx[:, :PAIR].astype(BF16)
        x1 = tx[:, PAIR:].astype(BF16)
        z2 = jnp.concatenate([vst_s[c], x1], axis=0)
        wt = wt_s[z, c]
        h_s[z, c] = jnp.dot(wt, z2, preferred_element_type=F32)
        g_s[z, c] = jnp.where(eye, gl_s[z, c], 0.0) + jnp.dot(wt[:, c2:], ta, preferred_element_type=F32)
        nkb = nkb_s[z, c]
        q_s[z, c] = fold(rt_s[z, c] + jnp.dot(nkb[:, c2:], ta, preferred_element_type=F32))
        y0_s[z, c] = fold(jnp.dot(nkb, z2, preferred_element_type=F32))

    each_item(stage_maps)

    zero64 = jnp.zeros((HEAD, HEAD), F32)
    for z in range(2):
        a = jnp.concatenate([jnp.concatenate([s0_ref[z, 0], zero64], axis=1),
                             jnp.concatenate([zero64, s0_ref[z, 1]], axis=1)], axis=0)
        st_s[z] = a.T

    def propagate(i, carry):
        for z in range(2):
            c = i if z == 0 else n_chunks - 1 - i
            s = st_s[z].astype(BF16)
            y_s[z, chunk_rows(c), :] = _bdot(q_s[z, c], s) + y0_s[z, c]
            st_s[z] = _bdot(g_s[z, c], s) + h_s[z, c]
        return carry

    lax.fori_loop(0, n_chunks, propagate, 0)
    for z in range(2):
        a = st_s[z].T
        sfin_ref[z, 0] = a[:HEAD, :HEAD]
        sfin_ref[z, 1] = a[HEAD:, HEAD:]

    y = y_s[0] + y_s[1]
    mu = _dot_exact_rhs(y, ones_bd) * (1.0 / HEAD)
    yc = y - mu
    var = _dot_exact_rhs(yc * yc, ones_bd) * (1.0 / HEAD)
    yn = yc * lax.rsqrt(var + GN_EPS) * gng_ref[...] + gnb_ref[...]
    out_ref[...] = (yn + bonus) * gate


def _rwkv_mix(pa, first_tok, n_seq, seq_len, s0, p):
    a_width = p["w0"].shape[-1]
    n_pairs = a_width // PAIR
    heads = a_width // HEAD
    t0 = first_tok // seq_len
    lora_w = 3 * PAIR
    lora_blk = 3 * a_width // lora_w
    n_chunks = seq_len // CHUNK
    col = lambda off: (lambda s, q: (t0 + s, off + q))
    vec = lambda off: (lambda s, q: (0, off + q))
    in_specs = [
        pl.BlockSpec((seq_len, PAIR), col(0)),
        pl.BlockSpec((seq_len, PAIR), col(n_pairs)),
        pl.BlockSpec((seq_len, PAIR), col(2 * n_pairs)),
        pl.BlockSpec((seq_len, lora_w), lambda s, q: (t0 + s, lora_blk)),
        pl.BlockSpec((1, PAIR), vec(0)),
        pl.BlockSpec((1, PAIR), vec(n_pairs)),
        pl.BlockSpec((1, PAIR), vec(2 * n_pairs)),
        pl.BlockSpec((1, lora_w), lambda s, q: (0, lora_blk)),
        pl.BlockSpec((2, PAIR), vec(0)),
        pl.BlockSpec((2, HEAD, PAIR), lambda s, q: (0, 0, q)),
        pl.BlockSpec((2, PAIR), vec(0)),
        pl.BlockSpec((2, HEAD, PAIR), lambda s, q: (0, 0, q)),
        pl.BlockSpec((PAIR, PAIR), vec(0)),
        pl.BlockSpec((1, PAIR), vec(0)),
        pl.BlockSpec((1, PAIR), vec(0)),
        pl.BlockSpec((1, PAIR), vec(0)),
        pl.BlockSpec((1, PAIR), vec(0)),
        pl.BlockSpec((1, PAIR), vec(0)),
        pl.BlockSpec((None, 2, 2, HEAD, HEAD), lambda s, q: (s, 0, q, 0, 0)),
    ]
    seq2 = lambda: pltpu.VMEM((2, seq_len, PAIR), F32)
    seq1 = lambda: pltpu.VMEM((seq_len, PAIR), F32)
    item = lambda rows, cols, dt: pltpu.VMEM((2, n_chunks, rows, cols), dt)
    out, s_fin = pl.pallas_call(
        functools.partial(_rwkv_kernel, seq_len=seq_len, unroll=min(n_chunks, 4)),
        out_shape=[jax.ShapeDtypeStruct((n_seq * seq_len, a_width), F32),
                   jax.ShapeDtypeStruct((n_seq, 2, heads, HEAD, HEAD), F32)],
        grid=(n_seq, n_pairs),
        in_specs=in_specs,
        out_specs=[pl.BlockSpec((seq_len, PAIR), lambda s, q: (s, q)),
                   pl.BlockSpec((None, 2, 2, HEAD, HEAD), lambda s, q: (s, 0, q, 0, 0))],
        scratch_shapes=[seq2(), seq2(), seq2(), seq1(), seq1(), seq1(),
                        item(PAIR, PAIR, BF16),
                        item(PAIR, PAIR, F32),
                        item(PAIR, 2 * PAIR, BF16),
                        pltpu.VMEM((n_chunks, PAIR, PAIR), BF16),
                        item(PAIR, PAIR, BF16),
                        item(PAIR, PAIR, F32),
                        item(PAIR, PAIR, BF16),
                        item(PAIR, PAIR, BF16),
                        item(PAIR, 2 * PAIR, BF16),
                        item(PAIR, 2 * PAIR, BF16),
                        item(PAIR, 2 * PAIR, F32),
                        item(1, PAIR, F32),
                        item(PAIR, PAIR, F32), item(PAIR, PAIR, F32),
                        item(CHUNK, PAIR, F32), item(CHUNK, PAIR, F32),
                        seq2(), pltpu.VMEM((2, PAIR, PAIR), F32)],
        compiler_params=_cparams(("arbitrary", "arbitrary")),
        name=f"rwkv_mix_t{seq_len}",
    )(pa, pa, pa, pa, p["ts_mu"], p["ts_mu"], p["ts_mu"], p["ts_mu"],
      p["w0"], p["w_up"], p["a0"], p["a_up"], p["g_up"], p["k_k"], p["k_a"], p["r_k"], p["gn_g"], p["gn_b"], s0)
    return out, s_fin


def _conv_kernel(pb_ref, pwb_ref, dw_ref, dwb_ref, g_ref, b_ref, o_ref, pad_s, *, seq_len):
    width = o_ref.shape[1]
    halo = 16
    sub = 64
    pbv = pb_ref[...] + pwb_ref[...]
    glu = pbv[:, :width] * jax.nn.sigmoid(pbv[:, width:])
    pad_s[0:halo, :] = jnp.zeros((halo, width), F32)
    pad_s[halo:halo + seq_len, :] = glu
    pad_s[halo + seq_len:, :] = jnp.zeros((halo, width), F32)
    first = halo - CONV_W // 2
    for c in range(seq_len // sub):
        acc = jnp.zeros((sub, width), F32)
        for d in range(CONV_W):
            start = c * sub + first + d
            acc = acc + pad_s[start:start + sub, :] * dw_ref[d:d + 1, :]
        y = _layer_norm(acc + dwb_ref[...], g_ref[...], b_ref[...], LN_EPS)
        o_ref[c * sub:(c + 1) * sub, :] = y * jax.nn.sigmoid(y)


def _conv_module(pb, first_tok, n_seq, seq_len, pw_b, dw_w, dw_b, cln_g, cln_b):
    width = dw_w.shape[1]
    t0 = first_tok // seq_len
    const = lambda s: (0, 0)
    return pl.pallas_call(
        functools.partial(_conv_kernel, seq_len=seq_len),
        out_shape=jax.ShapeDtypeStruct((n_seq * seq_len, width), F32),
        grid=(n_seq,),
        in_specs=[pl.BlockSpec((seq_len, 2 * width), lambda s: (t0 + s, 0)),
                  pl.BlockSpec((1, 2 * width), const),
                  pl.BlockSpec((CONV_W, width), const),
                  pl.BlockSpec((1, width), const),
                  pl.BlockSpec((1, width), const),
                  pl.BlockSpec((1, width), const)],
        out_specs=pl.BlockSpec((seq_len, width), lambda s: (s, 0)),
        scratch_shapes=[pltpu.VMEM((seq_len + 32, width), F32)],
        compiler_params=_cparams(("arbitrary",)),
        name=f"conv_module_t{seq_len}",
    )(pb, pw_b.reshape(1, -1), dw_w, dw_b.reshape(1, -1), cln_g.reshape(1, -1), cln_b.reshape(1, -1))


def _diff_lambda(lq1, lk1, lq2, lk2, lam_init):
    dot = lambda a, b: jnp.exp(jnp.sum(a[...] * b[...], axis=-1, keepdims=True))
    return dot(lq1, lk1) - dot(lq2, lk2) + lam_init


def _diff_attend(q, keys, vals, lam, lam_init, subln):
    nq = q.shape[0]
    lane_first = lax.broadcasted_iota(jnp.int32, q.shape, 1) < HEAD
    q_st = jnp.concatenate([jnp.where(lane_first, q, 0.0), jnp.where(lane_first, 0.0, q)], axis=0)
    s = _bdot_nt(q_st, keys) * (HEAD ** -0.5)
    s = s - jnp.max(s, axis=-1, keepdims=True)
    pexp = jnp.exp(s)
    ov = _bdot(pexp, vals) / jnp.sum(pexp, axis=-1, keepdims=True)
    o = ov[:nq] - lam * ov[nq:]
    o = o * lax.rsqrt(jnp.mean(o * o, axis=-1, keepdims=True) + RMS_EPS)
    return o * subln * (1.0 - lam_init)


def _head_cols(h):
    return slice(h * PAIR, (h + 1) * PAIR)


def _attn_prompt_kernel(q_ref, k_ref, v_ref, lq1, lk1, lq2, lk2, sub_ref, o_ref, *, lam_init):
    lam = _diff_lambda(lq1, lk1, lq2, lk2, lam_init)
    for h in range(q_ref.shape[1] // PAIR):
        c = _head_cols(h)
        o_ref[:, c] = _diff_attend(q_ref[:, c], k_ref[:, c], v_ref[:, c], lam, lam_init, sub_ref[...])


def _rope(x, cos, sin_signed):
    lane = lax.broadcasted_iota(jnp.int32, x.shape, 1)
    quarter = HEAD // 4
    partner = jnp.where((lane // quarter) % 2 == 0,
                        pltpu.roll(x, LANES - quarter, 1), pltpu.roll(x, quarter, 1))
    return x * cos + partner * sin_signed


def _attn_sample_kernel(q_ref, k_ref, v_ref, ck_ref, cv_ref, cq_ref, sq_ref, cosk_ref, sink_ref,
                        lq1, lk1, lq2, lk2, sub_ref, o_ref, *, lam_init):
    lam = _diff_lambda(lq1, lk1, lq2, lk2, lam_init)
    for h in range(q_ref.shape[1] // PAIR):
        c = _head_cols(h)
        q = _rope(q_ref[:, c], cq_ref[...], sq_ref[...])
        kl = _rope(k_ref[:, c], cosk_ref[...], sink_ref[...])
        keys = jnp.concatenate([ck_ref[:, c], kl], axis=0)
        vals = jnp.concatenate([cv_ref[:, c], v_ref[:, c]], axis=0)
        o_ref[:, c] = _diff_attend(q, keys, vals, lam, lam_init, sub_ref[...])


def _rope_tables(n_lat, grid_w):
    quarter = HEAD // 4
    inv = ROPE_BASE ** (-jnp.arange(quarter, dtype=F32) / quarter)
    t = jnp.arange(n_lat)
    row_ang = (t // grid_w).astype(F32)[:, None] * inv
    col_ang = (t % grid_w).astype(F32)[:, None] * inv
    ang = jnp.concatenate([row_ang, row_ang, col_ang, col_ang] * 2, axis=-1)
    sign = jnp.tile(jnp.concatenate([-jnp.ones(quarter, F32), jnp.ones(quarter, F32)]), 2 * HEAD // (2 * quarter))
    return jnp.cos(ang), jnp.sin(ang) * sign


def _attention(q, k_p, v_p, k_s, v_s, cache_k, cache_v, p, lam_init, n_prompt, seq, n_lat_req, n_lat, grid_w):
    inner = q.shape[1]
    vec = lambda n: pl.BlockSpec((1, n), lambda *_: (0, 0))
    lam_args = [p[n].reshape(1, -1) for n in ("lq1", "lk1", "lq2", "lk2")] + [p["subln_g"].reshape(1, -1)]
    lam_specs = [vec(HEAD)] * 4 + [vec(PAIR)]
    blk = pl.BlockSpec((seq, inner), lambda b: (b, 0))
    o_prompt = pl.pallas_call(
        functools.partial(_attn_prompt_kernel, lam_init=lam_init),
        out_shape=jax.ShapeDtypeStruct((n_prompt * seq, inner), F32),
        grid=(n_prompt,),
        in_specs=[blk, blk, blk] + lam_specs,
        out_specs=blk,
        compiler_params=_cparams(("arbitrary",)),
        name="diff_attn_prompt",
    )(q, k_p, v_p, *lam_args)

    past = cache_k.shape[1]
    qt = ROW_TILE
    n_qt = n_lat // qt
    q0 = n_prompt * seq // qt
    cos, sin = _rope_tables(n_lat, grid_w)
    lat = pl.BlockSpec((n_lat, inner), lambda b, t: (b, 0))
    ctx = pl.BlockSpec((None, past, inner), lambda b, t: (b, 0, 0))
    o_sample = pl.pallas_call(
        functools.partial(_attn_sample_kernel, lam_init=lam_init),
        out_shape=jax.ShapeDtypeStruct((n_lat_req * n_lat, inner), F32),
        grid=(n_lat_req, n_qt),
        in_specs=[pl.BlockSpec((qt, inner), lambda b, t: (q0 + b * n_qt + t, 0)),
                  lat, lat, ctx, ctx,
                  pl.BlockSpec((qt, PAIR), lambda b, t: (t, 0)),
                  pl.BlockSpec((qt, PAIR), lambda b, t: (t, 0)),
                  pl.BlockSpec((n_lat, PAIR), lambda b, t: (0, 0)),
                  pl.BlockSpec((n_lat, PAIR), lambda b, t: (0, 0))] + lam_specs,
        out_specs=pl.BlockSpec((qt, inner), lambda b, t: (b * n_qt + t, 0)),
        compiler_params=_cparams(("arbitrary", "arbitrary")),
        name="diff_attn_sample",
    )(q, k_s, v_s, cache_k, cache_v, cos, sin, cos, sin, *lam_args)
    return o_prompt, o_sample


def kernel(x_prompt, x_sample, state_wkv, cache_k, cache_v, c, c_ctx, mod_w, mod_b, ln1_g, ln1_b, ln2_g, ln2_b, ev_in_w, ev_pw_b, ev_ts_mu, ev_w0, ev_w_up, ev_a0, ev_a_up, ev_g_up, ev_k_k, ev_k_a, ev_r_k, ev_gn_g, ev_gn_b, ev_dw_w, ev_dw_b, ev_cln_g, ev_cln_b, ev_out_w, od_in_w, od_lq1, od_lk1, od_lq2, od_lk2, od_subln_g, od_out_w, rg_w, rg_b, re_w, re_b, ffn_w_gate, ffn_w_up, ffn_w_down):
    nbp, seq, d = x_prompt.shape
    nbs, n_lat, _ = x_sample.shape
    depth = mod_w.shape[0]
    n_prompt_tok = nbp * seq
    geom = (n_prompt_tok, n_lat)
    alpha = (2 * depth) ** 0.25
    a_width = ev_w0.shape[-1]
    a_cols = ev_ts_mu.shape[-1]
    heads = a_width // HEAD
    grid_w = 64
    assert n_prompt_tok % MOE_TILE == 0 and n_lat % MOE_TILE == 0 and seq % ROW_TILE == 0
    assert 1 + nbs <= 8 and n_lat % ROW_TILE == 0

    xs = (x_prompt.reshape(n_prompt_tok, d), x_sample.reshape(nbs * n_lat, d))
    cond = jnp.concatenate([c_ctx[None, :], c, jnp.zeros((8 - 1 - nbs, d), F32)], axis=0)
    mod = _ada_table(cond, mod_w, mod_b)

    w_gate, w_up, w_down = (w.astype(BF16) for w in (ffn_w_gate, ffn_w_up, ffn_w_down))
    new_wkv, new_k, new_v = [], [], []
    for l in range(depth):
        last = l == depth - 1
        if l % 2 == 0:
            i = l // 2
            pa, pb = _mod_matmul(xs, mod, l, ev_in_w[i], ((a_cols, False), (ev_in_w.shape[-1] - a_cols, False)), geom)
            prm = dict(ts_mu=ev_ts_mu[i][None, :], w0=ev_w0[i], w_up=ev_w_up[i], a0=ev_a0[i], a_up=ev_a_up[i],
                       g_up=ev_g_up[i], k_k=ev_k_k[i][None, :], k_a=ev_k_a[i][None, :],
                       r_k=ev_r_k[i].reshape(1, a_width), gn_g=ev_gn_g[i][None, :], gn_b=ev_gn_b[i][None, :])
            zero_state = jnp.zeros((nbp, 2, heads, HEAD, HEAD), F32)
            oa_p, st = _rwkv_mix(pa, 0, nbp, seq, zero_state, prm)
            oa_s, _ = _rwkv_mix(pa, n_prompt_tok, nbs, n_lat, state_wkv[:, i], prm)
            conv = (ev_pw_b[i], ev_dw_w[i], ev_dw_b[i], ev_cln_g[i], ev_cln_b[i])
            ob_p = _conv_module(pb, 0, nbp, seq, *conv)
            ob_s = _conv_module(pb, n_prompt_tok, nbs, n_lat, *conv)
            hs = [(oa_p, oa_s), (ob_p, ob_s)]
            out_w = ev_out_w[i]
            new_wkv.append(st)
        else:
            j = l // 2
            lam_init = 0.8 - 0.6 * math.exp(-0.3 * l)
            inner = od_in_w.shape[-1] // 3
            q, k_p, k_s, v_p, v_s = _mod_matmul(xs, mod, l, od_in_w[j],
                                                 ((inner, False), (inner, True), (inner, True)), geom)
            prm = dict(lq1=od_lq1[j], lk1=od_lk1[j], lq2=od_lq2[j], lk2=od_lk2[j], subln_g=od_subln_g[j])
            ck = cache_k[:, j].reshape(nbs, -1, inner)
            cv = cache_v[:, j].reshape(nbs, -1, inner)
            hs = [_attention(q, k_p, v_p, k_s, v_s, ck, cv, prm, lam_init, nbp, seq, nbs, n_lat, grid_w)]
            out_w = od_out_w[j]
            new_k.append(k_p.reshape(nbp, seq, inner // PAIR, PAIR))
            new_v.append(v_p.reshape(nbp, seq, inner // PAIR, PAIR))
        x = _out_proj_norm(hs, xs, mod, l, out_w, ln1_g[l], ln1_b[l], alpha, geom)
        rw = jnp.concatenate([rg_w[l], re_w[l],
                              jnp.zeros((d, ROUTER_LANES - N_GROUPS - N_EXPERTS), F32)], axis=1)
        rb = jnp.concatenate([rg_b[l], re_b[l], jnp.zeros((ROUTER_LANES - N_GROUPS - N_EXPERTS,), F32)])[None, :]
        xs = tuple(_moe_norm(x, mod, l, rw, rb, w_gate, w_up, w_down, ln2_g[l], ln2_b[l], alpha, geom,
                             split_out=last))

    y_prompt = xs[0].reshape(nbp, seq, d)
    y_sample = xs[1].reshape(nbs, n_lat, d)
    return (y_prompt, y_sample, jnp.stack(new_wkv, axis=1), jnp.stack(new_k, axis=1), jnp.stack(new_v, axis=1))
```

```python
import functools
import math

import jax
import jax.numpy as jnp
from jax import lax
from jax.experimental import pallas as pl
from jax.experimental.pallas import tpu as pltpu

F32 = jnp.float32
BF16 = jnp.bfloat16

LANES = 128
HEAD = 64
PAIR = 2 * HEAD
CHUNK = 64
INV_BLOCK = 16
ROW_TILE = 256
MOE_TILE = 1024
CONV_W = 31
N_GROUPS = 4
EXPERTS_PER_GROUP = 8
N_EXPERTS = N_GROUPS * EXPERTS_PER_GROUP
ROUTER_LANES = 128
EXPERT_LANE0 = 8
EXPERTS_PER_STEP = 4
LN_EPS = 1e-5
GN_EPS = 64e-5
RMS_EPS = 1e-5
ROPE_BASE = 10000.0
EXP_M05 = math.exp(-0.5)
VMEM_LIMIT = 56 * 1024 * 1024


def _bdot(a, b):
    return jnp.dot(a.astype(BF16), b.astype(BF16), preferred_element_type=F32)


def _bdot_nt(a, b):
    return lax.dot_general(a.astype(BF16), b.astype(BF16), (((1,), (1,)), ((), ())),
                           preferred_element_type=F32)


def _split2(a):
    hi = a.astype(BF16)
    lo = (a - hi.astype(F32)).astype(BF16)
    return hi, lo


def _split3(a):
    hi = a.astype(BF16)
    r1 = a - hi.astype(F32)
    mid = r1.astype(BF16)
    lo = (r1 - mid.astype(F32)).astype(BF16)
    return hi, mid, lo


def _dot_f32ish(a, b):
    ah, al = _split2(a)
    bh, bl = _split2(b)
    d = lambda x, y: jnp.dot(x, y, preferred_element_type=F32)
    return d(ah, bh) + d(ah, bl) + d(al, bh)


def _dot_exact_lhs(m_bf16, a):
    h, m, l = _split3(a)
    d = lambda y: jnp.dot(m_bf16, y, preferred_element_type=F32)
    return d(h) + d(m) + d(l)


def _dot_exact_rhs(a, m_bf16):
    h, m, l = _split3(a)
    d = lambda x: jnp.dot(x, m_bf16, preferred_element_type=F32)
    return d(h) + d(m) + d(l)


def _layer_norm(z, g, b, eps):
    mu = jnp.mean(z, axis=-1, keepdims=True)
    zc = z - mu
    var = jnp.mean(zc * zc, axis=-1, keepdims=True)
    return zc * lax.rsqrt(var + eps) * g + b


def _cond_row(tile, tile_rows, n_prompt_tok, dec_seq):
    tok = tile * tile_rows
    return jnp.where(tok < n_prompt_tok, 0, 1 + (tok - n_prompt_tok) // dec_seq)


def _cast_rows(src_ref, dst_ref, rows_per_step=128):
    n = src_ref.shape[0]
    for s in range(0, n, rows_per_step):
        dst_ref[s:s + rows_per_step, :] = src_ref[s:s + rows_per_step, :].astype(dst_ref.dtype)


def _cparams(sem):
    return pltpu.CompilerParams(dimension_semantics=sem, vmem_limit_bytes=VMEM_LIMIT)


def _ada_kernel(cond_ref, w_ref, b_ref, o_ref):
    c = cond_ref[...]
    s = c * jax.nn.sigmoid(c)
    o_ref[0, 0] = _dot_f32ish(s, w_ref[0]) + b_ref[0, 0]


def _ada_table(cond8, mod_w, mod_b):
    depth, d, _ = mod_w.shape
    return pl.pallas_call(
        _ada_kernel,
        out_shape=jax.ShapeDtypeStruct((depth, 6, 8, d), F32),
        grid=(depth, 6),
        in_specs=[pl.BlockSpec((8, d), lambda l, j: (0, 0)),
                  pl.BlockSpec((1, d, d), lambda l, j: (l, 0, j)),
                  pl.BlockSpec((1, 1, 1, d), lambda l, j: (l, j, 0, 0))],
        out_specs=pl.BlockSpec((1, 1, 8, d), lambda l, j: (l, j, 0, 0)),
        compiler_params=_cparams(("arbitrary", "arbitrary")),
        name="ada_table",
    )(cond8, mod_w, mod_b.reshape(depth, 6, 1, d))


def _mod_spec(layer, which, d):
    return pl.BlockSpec((None, None, 8, d), lambda *_: (layer, which, 0, 0))


def _group_specs(tile, width, n_prompt_tiles):
    return [pl.BlockSpec((tile, width), lambda i, *_: (jnp.minimum(i, n_prompt_tiles - 1), 0)),
            pl.BlockSpec((tile, width), lambda i, *_: (jnp.maximum(i - n_prompt_tiles, 0), 0))]


def _load_group(i, n_prompt_tiles, refs, rows=slice(None)):
    if len(refs) == 1:
        return refs[0][rows, :]
    return jnp.where(i < n_prompt_tiles, refs[0][rows, :], refs[1][rows, :])


def _store_group(i, n_prompt_tiles, refs, value, rows=slice(None)):
    if len(refs) == 1:
        refs[0][rows, :] = value
        return

    @pl.when(i < n_prompt_tiles)
    def _():
        refs[0][rows, :] = value

    @pl.when(i >= n_prompt_tiles)
    def _():
        refs[1][rows, :] = value


def _modmm_kernel(*refs, n_x, outs, geom):
    x_refs = refs[:n_x]
    sh_ref, sc_ref, w_ref = refs[n_x:n_x + 3]
    o_refs, wb = refs[n_x + 3:-1], refs[-1]
    i = pl.program_id(0)
    n_pt = geom[0] // ROW_TILE

    @pl.when(i == 0)
    def _():
        _cast_rows(w_ref, wb)

    r = _cond_row(i, ROW_TILE, *geom)
    h = _load_group(i, n_pt, x_refs) * (1.0 + sc_ref[pl.ds(r, 1), :]) + sh_ref[pl.ds(r, 1), :]
    hb = h.astype(BF16)
    off = 0
    k = 0
    for n, split in outs:
        y = jnp.dot(hb, wb[:, off:off + n], preferred_element_type=F32)
        _store_group(i, n_pt, o_refs[k:k + 1 + split], y)
        off += n
        k += 1 + split


def _mod_matmul(xs, mod, layer, w, outs, geom):
    d, n_out = w.shape
    n_prompt_tok = geom[0]
    n_tok = sum(x.shape[0] for x in xs)
    n_pt = n_prompt_tok // ROW_TILE
    whole = lambda n: [pl.BlockSpec((ROW_TILE, n), lambda i: (i, 0))]
    x_specs = whole(d) if len(xs) == 1 else _group_specs(ROW_TILE, d, n_pt)
    out_shape, out_specs = [], []
    for n, split in outs:
        if split:
            out_shape += [jax.ShapeDtypeStruct((n_prompt_tok, n), F32),
                          jax.ShapeDtypeStruct((n_tok - n_prompt_tok, n), F32)]
            out_specs += _group_specs(ROW_TILE, n, n_pt)
        else:
            out_shape += [jax.ShapeDtypeStruct((n_tok, n), F32)]
            out_specs += whole(n)
    return pl.pallas_call(
        functools.partial(_modmm_kernel, n_x=len(xs), outs=outs, geom=geom),
        out_shape=out_shape,
        grid=(n_tok // ROW_TILE,),
        in_specs=x_specs + [_mod_spec(layer, 0, d), _mod_spec(layer, 1, d),
                            pl.BlockSpec((d, n_out), lambda i: (0, 0), pipeline_mode=pl.Buffered(1))],
        out_specs=out_specs,
        scratch_shapes=[pltpu.VMEM((d, n_out), BF16)],
        compiler_params=_cparams(("arbitrary",)),
        name=f"mod_matmul_l{layer}",
    )(*xs, mod, mod, w)


def _outproj_kernel(*refs, arity, alpha, geom):
    groups, k = [], 0
    for n in arity:
        groups.append(refs[k:k + n])
        k += n
    gate_ref, w_ref, g_ref, b_ref, o_ref, wb = refs[k:]
    h_groups, x_group = groups[:-1], groups[-1]
    i = pl.program_id(0)
    n_pt = geom[0] // ROW_TILE

    @pl.when(i == 0)
    def _():
        _cast_rows(w_ref, wb)

    r = _cond_row(i, ROW_TILE, *geom)
    y = None
    off = 0
    for h_refs in h_groups:
        n = h_refs[0].shape[1]
        h = _load_group(i, n_pt, h_refs)
        part = jnp.dot(h.astype(BF16), wb[off:off + n, :], preferred_element_type=F32)
        y = part if y is None else y + part
        off += n
    z = alpha * _load_group(i, n_pt, x_group) + gate_ref[pl.ds(r, 1), :] * y
    o_ref[...] = _layer_norm(z, g_ref[...], b_ref[...], LN_EPS)


def _out_proj_norm(hs, xs, mod, layer, w, ln_g, ln_b, alpha, geom):
    d = w.shape[1]
    n_tok = sum(x.shape[0] for x in xs)
    n_pt = geom[0] // ROW_TILE
    specs = []
    for grp in list(hs) + [xs]:
        n = grp[0].shape[1]
        specs += [pl.BlockSpec((ROW_TILE, n), lambda i: (i, 0))] if len(grp) == 1 else _group_specs(ROW_TILE, n, n_pt)
    return pl.pallas_call(
        functools.partial(_outproj_kernel, arity=tuple(len(g) for g in list(hs) + [xs]), alpha=alpha, geom=geom),
        out_shape=jax.ShapeDtypeStruct((n_tok, d), F32),
        grid=(n_tok // ROW_TILE,),
        in_specs=specs + [
            _mod_spec(layer, 2, d),
            pl.BlockSpec(w.shape, lambda i: (0, 0), pipeline_mode=pl.Buffered(1)),
            pl.BlockSpec((1, d), lambda i: (0, 0)),
            pl.BlockSpec((1, d), lambda i: (0, 0))],
        out_specs=pl.BlockSpec((ROW_TILE, d), lambda i: (i, 0)),
        scratch_shapes=[pltpu.VMEM(w.shape, BF16)],
        compiler_params=_cparams(("arbitrary",)),
        name=f"out_proj_norm_l{layer}",
    )(*[a for g in hs for a in g], *xs, mod, w, ln_g.reshape(1, d), ln_b.reshape(1, d))


def _expert_lane(e):
    return EXPERT_LANE0 + (e // EXPERTS_PER_STEP) * 8 + e % EXPERTS_PER_STEP


def _route(logits):
    neg = jnp.float32(-3e38)
    big = jnp.int32(1 << 20)
    lane = lax.broadcasted_iota(jnp.int32, logits.shape, 1)
    is_g = lane < N_GROUPS
    gl = jnp.where(is_g, logits, neg)
    gmax = jnp.max(gl, axis=-1, keepdims=True)
    gidx = jnp.min(jnp.where(gl == gmax, lane, big), axis=-1, keepdims=True)
    gsum = jnp.sum(jnp.where(is_g, jnp.exp(gl - gmax), 0.0), axis=-1, keepdims=True)
    g_p = 1.0 / gsum
    lanes_per_group = 8 * (EXPERTS_PER_GROUP // EXPERTS_PER_STEP)
    lo = EXPERT_LANE0 + gidx * lanes_per_group
    in_grp = (lane >= lo) & (lane < lo + lanes_per_group) & (lane % 8 < EXPERTS_PER_STEP)
    el = jnp.where(in_grp, logits, neg)
    v1 = jnp.max(el, axis=-1, keepdims=True)
    i1 = jnp.min(jnp.where(el == v1, lane, big), axis=-1, keepdims=True)
    el2 = jnp.where(lane == i1, neg, el)
    v2 = jnp.max(el2, axis=-1, keepdims=True)
    i2 = jnp.min(jnp.where(el2 == v2, lane, big), axis=-1, keepdims=True)
    e21 = jnp.exp(v2 - v1)
    den = 1.0 + e21
    w1 = g_p / den
    w2 = g_p * e21 / den
    return jnp.where(lane == i1, w1, 0.0) + jnp.where(lane == i2, w2, 0.0), gidx


def _dot_tn(a, b):
    return lax.dot_general(a, b, (((0,), (1,)), ((), ())), preferred_element_type=F32)


def _moe_kernel(x_ref, sh_ref, sc_ref, gate_ref, rw_ref, rb_ref, wg_ref, wu_ref, wd_ref, g_ref, b_ref,
                *rest, alpha, geom):
    hb_scr, gates_scr, hst_scr, gst_scr, acct_scr, perm_scr, permt_scr, bounds = rest[-8:]
    o_refs = rest[:-8]
    i = pl.program_id(0)
    e = pl.program_id(1)
    r = _cond_row(i, MOE_TILE, *geom)
    n_pt = geom[0] // MOE_TILE
    sub = ROW_TILE
    n_sub = MOE_TILE // sub
    ff = wg_ref.shape[1] // EXPERTS_PER_STEP

    @pl.when(e == 0)
    def _():
        lane = lax.broadcasted_iota(jnp.int32, (sub, ROUTER_LANES), 1)
        ri = lax.broadcasted_iota(jnp.int32, (sub, sub), 0)
        ci = lax.broadcasted_iota(jnp.int32, (sub, sub), 1)
        tril = jnp.where(ci <= ri, 1.0, 0.0).astype(BF16)
        running = jnp.zeros((1, ROUTER_LANES), F32)
        onehots, cums = [], []
        for s in range(n_sub):
            rows = slice(s * sub, (s + 1) * sub)
            h = x_ref[rows, :] * (1.0 + sc_ref[pl.ds(r, 1), :]) + sh_ref[pl.ds(r, 1), :]
            hb_scr[rows, :] = h.astype(BF16)
            logits = _dot_f32ish(h, rw_ref[...]) + rb_ref[...]
            gates, gidx = _route(logits)
            gates_scr[rows, :] = gates
            onehot = jnp.where(lane == gidx, 1.0, 0.0)
            cum = jnp.dot(tril, onehot.astype(BF16), preferred_element_type=F32) + running
            running = cum[sub - 1:sub, :]
            onehots.append(onehot)
            cums.append(cum)
        gi = lax.broadcasted_iota(jnp.int32, (ROUTER_LANES, ROUTER_LANES), 0)
        gj = lax.broadcasted_iota(jnp.int32, (ROUTER_LANES, ROUTER_LANES), 1)
        before = jnp.where((gi < gj) & (gi < N_GROUPS), 1.0, 0.0).astype(BF16)
        starts = _dot_exact_rhs(jnp.broadcast_to(running, (8, ROUTER_LANES)), before)[0:1, :]
        for g in range(N_GROUPS + 1):
            bounds[g] = starts[0, g].astype(jnp.int32)
        dest_cols = [jnp.sum(onehots[s] * (starts + cums[s] - 1.0), axis=-1, keepdims=True) for s in range(n_sub)]
        dest_row = jnp.concatenate(
            [jnp.broadcast_to(dc, (sub, ROUTER_LANES)).T[0:1, :] for dc in dest_cols], axis=1)
        col_id = lax.broadcasted_iota(jnp.int32, (sub, MOE_TILE), 1).astype(F32)
        row_id = lax.broadcasted_iota(jnp.int32, (sub, MOE_TILE), 0).astype(F32)
        for s in range(n_sub):
            rows = slice(s * sub, (s + 1) * sub)
            perm_scr[rows, :] = jnp.where(row_id + float(s * sub) == dest_row, 1.0, 0.0).astype(BF16)
            permt_scr[rows, :] = jnp.where(dest_cols[s] == col_id, 1.0, 0.0).astype(BF16)
        perm = perm_scr[...]
        hst_scr[...] = _dot_tn(hb_scr[...], perm).astype(BF16)
        g_hi, g_mid, g_lo = _split3(gates_scr[...])
        gst_scr[...] = _dot_tn(g_hi, perm) + _dot_tn(g_mid, perm) + _dot_tn(g_lo, perm)
        acct_scr[...] = jnp.zeros(acct_scr.shape, F32)

    grp = e // (EXPERTS_PER_GROUP // EXPERTS_PER_STEP)
    first, end = bounds[grp], bounds[grp + 1]
    gate_rows = pl.ds(pl.multiple_of(EXPERT_LANE0 + 8 * e, 8), 8)
    for s in range(n_sub):
        cols = slice(s * sub, (s + 1) * sub)

        @pl.when((first < (s + 1) * sub) & (end > s * sub))
        def _():
            hs = hst_scr[:, cols]
            at = jnp.dot(wg_ref[0], hs, preferred_element_type=F32)
            ut = jnp.dot(wu_ref[0], hs, preferred_element_type=F32)
            ge = gst_scr[gate_rows, cols]
            hid = []
            for j in range(EXPERTS_PER_STEP):
                a = at[j * ff:(j + 1) * ff, :]
                hid.append((a * jax.nn.sigmoid(a)) * ut[j * ff:(j + 1) * ff, :] * ge[j:j + 1, :])
            hid = jnp.concatenate(hid, axis=0).astype(BF16)
            acct_scr[:, cols] += jnp.dot(wd_ref[0], hid, preferred_element_type=F32)

    @pl.when(e == pl.num_programs(1) - 1)
    def _():
        a_hi, a_lo = _split2(acct_scr[...])
        for s in range(n_sub):
            rows = slice(s * sub, (s + 1) * sub)
            pt = permt_scr[rows, :]
            moe = (lax.dot_general(pt, a_hi, (((1,), (1,)), ((), ())), preferred_element_type=F32)
                   + lax.dot_general(pt, a_lo, (((1,), (1,)), ((), ())), preferred_element_type=F32))
            z = alpha * x_ref[rows, :] + gate_ref[pl.ds(r, 1), :] * moe
            _store_group(i, n_pt, o_refs, _layer_norm(z, g_ref[...], b_ref[...], LN_EPS), rows)


def _moe_weights(rg_w, rg_b, re_w, re_b, w_gate, w_up, w_down):
    depth, d, _ = rg_w.shape
    ff = w_gate.shape[-1]
    n_step = N_EXPERTS // EXPERTS_PER_STEP

    def lanes(g, e):
        lead = e.shape[:-1]
        e8 = jnp.pad(e.reshape(lead + (n_step, EXPERTS_PER_STEP)), [(0, 0)] * len(lead) + [(0, 0), (0, 8 - EXPERTS_PER_STEP)])
        parts = [g, jnp.zeros(lead + (EXPERT_LANE0 - N_GROUPS,), F32), e8.reshape(lead + (n_step * 8,)),
                 jnp.zeros(lead + (ROUTER_LANES - EXPERT_LANE0 - n_step * 8,), F32)]
        return jnp.concatenate(parts, axis=-1)

    rw = lanes(rg_w, re_w)
    rb = lanes(rg_b, re_b)[:, None, :]
    up_t = lambda w: jnp.swapaxes(w.astype(BF16), 2, 3).reshape(depth, n_step, EXPERTS_PER_STEP * ff, d)
    wd_t = jnp.transpose(w_down.astype(BF16).reshape(depth, n_step, EXPERTS_PER_STEP, ff, d), (0, 1, 4, 2, 3))
    return rw, rb, up_t(w_gate), up_t(w_up), wd_t.reshape(depth, n_step, d, EXPERTS_PER_STEP * ff)


def _moe_norm(x, mod, layer, rw, rb, wg_t, wu_t, wd_t, ln_g, ln_b, alpha, geom, split_out):
    n_tok, d = x.shape
    n_step, rows_w = wg_t.shape[1:3]
    const = lambda i, e: (0, 0)
    if split_out:
        out_shape = [jax.ShapeDtypeStruct((geom[0], d), F32), jax.ShapeDtypeStruct((n_tok - geom[0], d), F32)]
        out_specs = _group_specs(MOE_TILE, d, geom[0] // MOE_TILE)
    else:
        out_shape = [jax.ShapeDtypeStruct((n_tok, d), F32)]
        out_specs = [pl.BlockSpec((MOE_TILE, d), lambda i, e: (i, 0))]
    return pl.pallas_call(
        functools.partial(_moe_kernel, alpha=alpha, geom=geom),
        out_shape=out_shape,
        grid=(n_tok // MOE_TILE, n_step),
        in_specs=[pl.BlockSpec((MOE_TILE, d), lambda i, e: (i, 0), pipeline_mode=pl.Buffered(1)),
                  _mod_spec(layer, 3, d), _mod_spec(layer, 4, d), _mod_spec(layer, 5, d),
                  pl.BlockSpec((None, d, ROUTER_LANES), lambda i, e: (layer, 0, 0)),
                  pl.BlockSpec((None, 1, ROUTER_LANES), lambda i, e: (layer, 0, 0)),
                  pl.BlockSpec((None, 1, rows_w, d), lambda i, e: (layer, e, 0, 0)),
                  pl.BlockSpec((None, 1, rows_w, d), lambda i, e: (layer, e, 0, 0)),
                  pl.BlockSpec((None, 1, d, rows_w), lambda i, e: (layer, e, 0, 0)),
                  pl.BlockSpec((1, d), const),
                  pl.BlockSpec((1, d), const)],
        out_specs=out_specs,
        scratch_shapes=[pltpu.VMEM((MOE_TILE, d), BF16),
                        pltpu.VMEM((MOE_TILE, ROUTER_LANES), F32),
                        pltpu.VMEM((d, MOE_TILE), BF16),
                        pltpu.VMEM((ROUTER_LANES, MOE_TILE), F32),
                        pltpu.VMEM((d, MOE_TILE), F32),
                        pltpu.VMEM((MOE_TILE, MOE_TILE), BF16), pltpu.VMEM((MOE_TILE, MOE_TILE), BF16),
                        pltpu.SMEM((8,), jnp.int32)],
        compiler_params=_cparams(("arbitrary", "arbitrary")),
        name=f"moe_norm_l{layer}",
    )(x, mod, mod, mod, rw, rb, wg_t, wu_t, wd_t, ln_g.reshape(1, d), ln_b.reshape(1, d))


def _rwkv_kernel(r_ref, k_ref, v_ref, lora_ref, mur_ref, muk_ref, muv_ref, mul_ref,
                 w0_ref, wup_ref, a0_ref, aup_ref, gup_ref, kk_ref, ka_ref, rk_ref, gng_ref, gnb_ref, s0_ref,
                 out_ref, sfin_ref,
                 lw_s, kz_s, bz_s, rr_s, vv_s, kn_s,
                 abar_s, rt_s, wt_s, vst_s, p_s, tm_s, mb_s, mk_s, nkb_s, txr_s, tx_s, gl_s,
                 g_s, h_s, q_s, y0_s, y_s, st_s, *, seq_len, unroll):
    t_len = seq_len
    n_chunks = t_len // CHUNK
    c2 = 2 * CHUNK

    row = lax.broadcasted_iota(jnp.int32, (t_len, 1), 0)

    def tshift(x, mu):
        prev = jnp.where(row == 0, 0.0, pltpu.roll(x, 1, 0))
        nxt = jnp.where(row == t_len - 1, 0.0, pltpu.roll(x, t_len - 1, 0))
        return x + mu * (0.5 * (prev + nxt) - x)

    r = tshift(r_ref[...], mur_ref[...])
    k = tshift(k_ref[...], muk_ref[...])
    v = tshift(v_ref[...], muv_ref[...])
    lo = tshift(lora_ref[...], mul_ref[...])
    wd = jnp.tanh(lo[:, 0:PAIR])
    ad = lo[:, PAIR:2 * PAIR]
    gate = _bdot(jax.nn.sigmoid(lo[:, 2 * PAIR:3 * PAIR]), gup_ref[...])

    ri = lax.broadcasted_iota(jnp.int32, (PAIR, PAIR), 0)
    ci = lax.broadcasted_iota(jnp.int32, (PAIR, PAIR), 1)
    same_head = (ri // HEAD) == (ci // HEAD)
    ones_bd = jnp.where(same_head, 1.0, 0.0).astype(BF16)
    eye = ri == ci

    kk = k * kk_ref[...]
    ssq = _dot_exact_rhs(kk * kk, ones_bd)
    kk = kk * lax.rsqrt(jnp.maximum(ssq, 1e-24))
    wup2 = wup_ref[...].reshape(PAIR, PAIR)
    aup2 = aup_ref[...].reshape(PAIR, PAIR)
    bonus = jnp.zeros((t_len, PAIR), F32)
    for z in range(2):
        zrows = (ri // HEAD) == z
        w_pre = w0_ref[z:z + 1, :] + _bdot(wd, jnp.where(zrows, wup2, 0.0))
        lw_s[z] = -EXP_M05 * jax.nn.sigmoid(w_pre)
        a = jax.nn.sigmoid(a0_ref[z:z + 1, :] + _bdot(ad, jnp.where(zrows, aup2, 0.0)))
        kz = k * (1.0 + (a - 1.0) * ka_ref[...])
        kz_s[z] = kz
        bz_s[z] = kk * a
        bonus = bonus + _dot_exact_rhs(r * kz * rk_ref[...], ones_bd) * v
    rr_s[...] = r
    vv_s[...] = v
    kn_s[...] = kk

    rt_i = ri % CHUNK
    cs_i = ci % CHUNK
    strict = (cs_i < rt_i, cs_i > rt_i)
    incl = (cs_i <= rt_i, cs_i >= rt_i)
    t64 = lax.broadcasted_iota(jnp.int32, (CHUNK, CHUNK), 0)
    s64 = lax.broadcasted_iota(jnp.int32, (CHUNK, CHUNK), 1)
    tri = (jnp.where(s64 <= t64, 1.0, 0.0).astype(BF16), jnp.where(s64 >= t64, 1.0, 0.0).astype(BF16))
    lane_head0 = lax.broadcasted_iota(jnp.int32, (CHUNK, PAIR), 1) < HEAD
    eye_f = jnp.where(eye, 1.0, 0.0)
    same16 = (ri // INV_BLOCK) == (ci // INV_BLOCK)

    def stack(x):
        return jnp.concatenate([jnp.where(lane_head0, x, 0.0), jnp.where(lane_head0, 0.0, x)], axis=0)

    def fold(x):
        return x[:CHUNK] + x[CHUNK:]

    def chunk_rows(c):
        return pl.ds(pl.multiple_of(c * CHUNK, CHUNK), CHUNK)

    def each_item(body):
        def both_directions(c, carry):
            body(0, c)
            body(1, c)
            return carry

        lax.fori_loop(0, n_chunks, both_directions, 0, unroll=unroll)

    def stage_operands(z, c):
        rows = chunk_rows(c)
        r_c, kn_c = rr_s[rows, :], kn_s[rows, :]
        lw_c, kz_c, bz_c = lw_s[z, rows, :], kz_s[z, rows, :], bz_s[z, rows, :]
        lg = _dot_exact_lhs(tri[z], lw_c)
        total = lg[CHUNK - 1:CHUNK, :] if z == 0 else lg[0:1, :]
        e_neg = jnp.exp(-lg)
        e_rem = jnp.exp(total - lg)
        abar = stack(kn_c * jnp.exp(lg - lw_c))
        rt = stack(r_c * jnp.exp(lg))
        big = _bdot_nt(jnp.concatenate([abar, rt], axis=0),
                       jnp.concatenate([stack(bz_c * e_neg), stack(kz_c * e_neg)], axis=0))
        mb = jnp.where(strict[z], big[:c2, :c2], 0.0)
        abar_s[z, c] = abar.astype(BF16)
        rt_s[z, c] = rt
        wt_s[z, c] = jnp.concatenate([stack(kz_c * e_rem), -stack(bz_c * e_rem)], axis=0).T.astype(BF16)
        gl_s[z, c] = jnp.exp(total)
        m0 = jnp.where(same16, mb, 0.0)
        tm_s[z, c] = eye_f - m0
        p_s[z, c] = m0.astype(BF16)
        mb_s[z, c] = mb.astype(BF16)
        mk_s[z, c] = jnp.where(strict[z], big[:c2, c2:], 0.0).astype(BF16)
        nkb_s[z, c] = jnp.concatenate([jnp.where(incl[z], big[c2:, c2:], 0.0),
                                       -jnp.where(incl[z], big[c2:, :c2], 0.0)], axis=1).astype(BF16)

    def stage_values(c, carry):
        vst_s[c] = stack(vv_s[chunk_rows(c), :]).astype(BF16)
        return carry

    lax.fori_loop(0, n_chunks, stage_values, 0, unroll=unroll)
    each_item(stage_operands)

    def stage_square(z, c):
        m0 = p_s[z, c]
        p_s[z, c] = jnp.dot(m0, m0, preferred_element_type=F32).astype(BF16)
        mkv = jnp.dot(mk_s[z, c], vst_s[c], preferred_element_type=F32)
        txr_s[z, c] = jnp.concatenate([abar_s[z, c], mkv.astype(BF16)], axis=1)

    each_item(stage_square)

    n_round = INV_BLOCK.bit_length() - 2
    for j in range(n_round):
        def stage_double(z, c, last=(j == n_round - 1)):
            tm, p = tm_s[z, c], p_s[z, c]
            tm_s[z, c] = tm + jnp.dot(tm.astype(BF16), p, preferred_element_type=F32)
            if not last:
                p_s[z, c] = jnp.dot(p, p, preferred_element_type=F32).astype(BF16)

        each_item(stage_double)

    size = INV_BLOCK
    while size < CHUNK:
        off_diag = ((ri // (2 * size)) == (ci // (2 * size))) & ((ri // size) != (ci // size))

        def stage_merge_rhs(z, c, off_diag=off_diag):
            cross = jnp.where(off_diag, mb_s[z, c], jnp.zeros((), BF16))
            p_s[z, c] = jnp.dot(cross, tm_s[z, c].astype(BF16), preferred_element_type=F32).astype(BF16)

        def stage_merge(z, c):
            tm = tm_s[z, c]
            tm_s[z, c] = tm - jnp.dot(tm.astype(BF16), p_s[z, c], preferred_element_type=F32)

        each_item(stage_merge_rhs)
        each_item(stage_merge)
        size *= 2

    def stage_apply(z, c):
        tx_s[z, c] = jnp.dot(tm_s[z, c].astype(BF16), txr_s[z, c], preferred_element_type=F32)

    each_item(stage_apply)

    def stage_maps(z, c):
        tx = tx_s[z, c]
        ta = tx[:, :PAIR].astype(BF16)
        x1 = tx[:, PAIR:].astype(BF16)
        z2 = jnp.concatenate([vst_s[c], x1], axis=0)
        wt = wt_s[z, c]
        h_s[z, c] = jnp.dot(wt, z2, preferred_element_type=F32)
        g_s[z, c] = jnp.where(eye, gl_s[z, c], 0.0) + jnp.dot(wt[:, c2:], ta, preferred_element_type=F32)
        nkb = nkb_s[z, c]
        q_s[z, c] = fold(rt_s[z, c] + jnp.dot(nkb[:, c2:], ta, preferred_element_type=F32))
        y0_s[z, c] = fold(jnp.dot(nkb, z2, preferred_element_type=F32))

    each_item(stage_maps)

    zero64 = jnp.zeros((HEAD, HEAD), F32)
    for z in range(2):
        a = jnp.concatenate([jnp.concatenate([s0_ref[z, 0], zero64], axis=1),
                             jnp.concatenate([zero64, s0_ref[z, 1]], axis=1)], axis=0)
        st_s[z] = a.T

    def propagate(i, carry):
        for z in range(2):
            c = i if z == 0 else n_chunks - 1 - i
            s = st_s[z].astype(BF16)
            y_s[z, chunk_rows(c), :] = _bdot(q_s[z, c], s) + y0_s[z, c]
            st_s[z] = _bdot(g_s[z, c], s) + h_s[z, c]
        return carry

    lax.fori_loop(0, n_chunks, propagate, 0)
    for z in range(2):
        a = st_s[z].T
        sfin_ref[z, 0] = a[:HEAD, :HEAD]
        sfin_ref[z, 1] = a[HEAD:, HEAD:]

    y = y_s[0] + y_s[1]
    mu = _dot_exact_rhs(y, ones_bd) * (1.0 / HEAD)
    yc = y - mu
    var = _dot_exact_rhs(yc * yc, ones_bd) * (1.0 / HEAD)
    yn = yc * lax.rsqrt(var + GN_EPS) * gng_ref[...] + gnb_ref[...]
    out_ref[...] = (yn + bonus) * gate


def _rwkv_mix(pa, first_tok, n_seq, seq_len, s0, p):
    a_width = p["w0"].shape[-1]
    n_pairs = a_width // PAIR
    heads = a_width // HEAD
    t0 = first_tok // seq_len
    lora_w = 3 * PAIR
    lora_blk = 3 * a_width // lora_w
    n_chunks = seq_len // CHUNK
    col = lambda off: (lambda s, q: (t0 + s, off + q))
    vec = lambda off: (lambda s, q: (0, off + q))
    in_specs = [
        pl.BlockSpec((seq_len, PAIR), col(0)),
        pl.BlockSpec((seq_len, PAIR), col(n_pairs)),
        pl.BlockSpec((seq_len, PAIR), col(2 * n_pairs)),
        pl.BlockSpec((seq_len, lora_w), lambda s, q: (t0 + s, lora_blk)),
        pl.BlockSpec((1, PAIR), vec(0)),
        pl.BlockSpec((1, PAIR), vec(n_pairs)),
        pl.BlockSpec((1, PAIR), vec(2 * n_pairs)),
        pl.BlockSpec((1, lora_w), lambda s, q: (0, lora_blk)),
        pl.BlockSpec((2, PAIR), vec(0)),
        pl.BlockSpec((2, HEAD, PAIR), lambda s, q: (0, 0, q)),
        pl.BlockSpec((2, PAIR), vec(0)),
        pl.BlockSpec((2, HEAD, PAIR), lambda s, q: (0, 0, q)),
        pl.BlockSpec((PAIR, PAIR), vec(0)),
        pl.BlockSpec((1, PAIR), vec(0)),
        pl.BlockSpec((1, PAIR), vec(0)),
        pl.BlockSpec((1, PAIR), vec(0)),
        pl.BlockSpec((1, PAIR), vec(0)),
        pl.BlockSpec((1, PAIR), vec(0)),
        pl.BlockSpec((None, 2, 2, HEAD, HEAD), lambda s, q: (s, 0, q, 0, 0)),
    ]
    seq2 = lambda: pltpu.VMEM((2, seq_len, PAIR), F32)
    seq1 = lambda: pltpu.VMEM((seq_len, PAIR), F32)
    item = lambda rows, cols, dt: pltpu.VMEM((2, n_chunks, rows, cols), dt)
    out, s_fin = pl.pallas_call(
        functools.partial(_rwkv_kernel, seq_len=seq_len, unroll=min(n_chunks, 4)),
        out_shape=[jax.ShapeDtypeStruct((n_seq * seq_len, a_width), F32),
                   jax.ShapeDtypeStruct((n_seq, 2, heads, HEAD, HEAD), F32)],
        grid=(n_seq, n_pairs),
        in_specs=in_specs,
        out_specs=[pl.BlockSpec((seq_len, PAIR), lambda s, q: (s, q)),
                   pl.BlockSpec((None, 2, 2, HEAD, HEAD), lambda s, q: (s, 0, q, 0, 0))],
        scratch_shapes=[seq2(), seq2(), seq2(), seq1(), seq1(), seq1(),
                        item(PAIR, PAIR, BF16),
                        item(PAIR, PAIR, F32),
                        item(PAIR, 2 * PAIR, BF16),
                        pltpu.VMEM((n_chunks, PAIR, PAIR), BF16),
                        item(PAIR, PAIR, BF16),
                        item(PAIR, PAIR, F32),
                        item(PAIR, PAIR, BF16),
                        item(PAIR, PAIR, BF16),
                        item(PAIR, 2 * PAIR, BF16),
                        item(PAIR, 2 * PAIR, BF16),
                        item(PAIR, 2 * PAIR, F32),
                        item(1, PAIR, F32),
                        item(PAIR, PAIR, F32), item(PAIR, PAIR, F32),
                        item(CHUNK, PAIR, F32), item(CHUNK, PAIR, F32),
                        seq2(), pltpu.VMEM((2, PAIR, PAIR), F32)],
        compiler_params=_cparams(("arbitrary", "arbitrary")),
        name=f"rwkv_mix_t{seq_len}",
    )(pa, pa, pa, pa, p["ts_mu"], p["ts_mu"], p["ts_mu"], p["ts_mu"],
      p["w0"], p["w_up"], p["a0"], p["a_up"], p["g_up"], p["k_k"], p["k_a"], p["r_k"], p["gn_g"], p["gn_b"], s0)
    return out, s_fin


def _conv_kernel(pb_ref, pwb_ref, dw_ref, dwb_ref, g_ref, b_ref, o_ref, pad_s, *, seq_len):
    width = o_ref.shape[1]
    halo = 16
    sub = 64
    pbv = pb_ref[...] + pwb_ref[...]
    glu = pbv[:, :width] * jax.nn.sigmoid(pbv[:, width:])
    pad_s[0:halo, :] = jnp.zeros((halo, width), F32)
    pad_s[halo:halo + seq_len, :] = glu
    pad_s[halo + seq_len:, :] = jnp.zeros((halo, width), F32)
    first = halo - CONV_W // 2
    for c in range(seq_len // sub):
        acc = jnp.zeros((sub, width), F32)
        for d in range(CONV_W):
            start = c * sub + first + d
            acc = acc + pad_s[start:start + sub, :] * dw_ref[d:d + 1, :]
        y = _layer_norm(acc + dwb_ref[...], g_ref[...], b_ref[...], LN_EPS)
        o_ref[c * sub:(c + 1) * sub, :] = y * jax.nn.sigmoid(y)


def _conv_module(pb, first_tok, n_seq, seq_len, pw_b, dw_w, dw_b, cln_g, cln_b):
    width = dw_w.shape[1]
    t0 = first_tok // seq_len
    const = lambda s: (0, 0)
    return pl.pallas_call(
        functools.partial(_conv_kernel, seq_len=seq_len),
        out_shape=jax.ShapeDtypeStruct((n_seq * seq_len, width), F32),
        grid=(n_seq,),
        in_specs=[pl.BlockSpec((seq_len, 2 * width), lambda s: (t0 + s, 0)),
                  pl.BlockSpec((1, 2 * width), const),
                  pl.BlockSpec((CONV_W, width), const),
                  pl.BlockSpec((1, width), const),
                  pl.BlockSpec((1, width), const),
                  pl.BlockSpec((1, width), const)],
        out_specs=pl.BlockSpec((seq_len, width), lambda s: (s, 0)),
        scratch_shapes=[pltpu.VMEM((seq_len + 32, width), F32)],
        compiler_params=_cparams(("arbitrary",)),
        name=f"conv_module_t{seq_len}",
    )(pb, pw_b.reshape(1, -1), dw_w, dw_b.reshape(1, -1), cln_g.reshape(1, -1), cln_b.reshape(1, -1))


def _diff_lambda(lq1, lk1, lq2, lk2, lam_init):
    dot = lambda a, b: jnp.exp(jnp.sum(a[...] * b[...], axis=-1, keepdims=True))
    return dot(lq1, lk1) - dot(lq2, lk2) + lam_init


def _diff_attend(q, keys, vals, lam, lam_init, subln):
    nq = q.shape[0]
    lane_first = lax.broadcasted_iota(jnp.int32, q.shape, 1) < HEAD
    q_st = jnp.concatenate([jnp.where(lane_first, q, 0.0), jnp.where(lane_first, 0.0, q)], axis=0)
    s = _bdot_nt(q_st, keys) * (HEAD ** -0.5)
    s = s - jnp.max(s, axis=-1, keepdims=True)
    pexp = jnp.exp(s)
    ov = _bdot(pexp, vals) / jnp.sum(pexp, axis=-1, keepdims=True)
    o = ov[:nq] - lam * ov[nq:]
    o = o * lax.rsqrt(jnp.mean(o * o, axis=-1, keepdims=True) + RMS_EPS)
    return o * subln * (1.0 - lam_init)


def _head_cols(h):
    return slice(h * PAIR, (h + 1) * PAIR)


def _attn_prompt_kernel(q_ref, k_ref, v_ref, lq1, lk1, lq2, lk2, sub_ref, o_ref, *, lam_init):
    lam = _diff_lambda(lq1, lk1, lq2, lk2, lam_init)
    for h in range(q_ref.shape[1] // PAIR):
        c = _head_cols(h)
        o_ref[:, c] = _diff_attend(q_ref[:, c], k_ref[:, c], v_ref[:, c], lam, lam_init, sub_ref[...])


def _rope(x, cos, sin_signed):
    lane = lax.broadcasted_iota(jnp.int32, x.shape, 1)
    quarter = HEAD // 4
    partner = jnp.where((lane // quarter) % 2 == 0,
                        pltpu.roll(x, LANES - quarter, 1), pltpu.roll(x, quarter, 1))
    return x * cos + partner * sin_signed


def _attn_sample_kernel(q_ref, k_ref, v_ref, ck_ref, cv_ref, cq_ref, sq_ref, cosk_ref, sink_ref,
                        lq1, lk1, lq2, lk2, sub_ref, o_ref, *, lam_init):
    lam = _diff_lambda(lq1, lk1, lq2, lk2, lam_init)
    for h in range(q_ref.shape[1] // PAIR):
        c = _head_cols(h)
        q = _rope(q_ref[:, c], cq_ref[...], sq_ref[...])
        kl = _rope(k_ref[:, c], cosk_ref[...], sink_ref[...])
        keys = jnp.concatenate([ck_ref[:, c], kl], axis=0)
        vals = jnp.concatenate([cv_ref[:, c], v_ref[:, c]], axis=0)
        o_ref[:, c] = _diff_attend(q, keys, vals, lam, lam_init, sub_ref[...])


def _rope_tables(n_lat, grid_w):
    quarter = HEAD // 4
    inv = ROPE_BASE ** (-jnp.arange(quarter, dtype=F32) / quarter)
    t = jnp.arange(n_lat)
    row_ang = (t // grid_w).astype(F32)[:, None] * inv
    col_ang = (t % grid_w).astype(F32)[:, None] * inv
    ang = jnp.concatenate([row_ang, row_ang, col_ang, col_ang] * 2, axis=-1)
    sign = jnp.tile(jnp.concatenate([-jnp.ones(quarter, F32), jnp.ones(quarter, F32)]), 2 * HEAD // (2 * quarter))
    return jnp.cos(ang), jnp.sin(ang) * sign


def _attention(q, k_p, v_p, k_s, v_s, cache_k, cache_v, p, lam_init, n_prompt, seq, n_lat_req, n_lat, grid_w):
    inner = q.shape[1]
    vec = lambda n: pl.BlockSpec((1, n), lambda *_: (0, 0))
    lam_args = [p[n].reshape(1, -1) for n in ("lq1", "lk1", "lq2", "lk2")] + [p["subln_g"].reshape(1, -1)]
    lam_specs = [vec(HEAD)] * 4 + [vec(PAIR)]
    blk = pl.BlockSpec((seq, inner), lambda b: (b, 0))
    o_prompt = pl.pallas_call(
        functools.partial(_attn_prompt_kernel, lam_init=lam_init),
        out_shape=jax.ShapeDtypeStruct((n_prompt * seq, inner), F32),
        grid=(n_prompt,),
        in_specs=[blk, blk, blk] + lam_specs,
        out_specs=blk,
        compiler_params=_cparams(("arbitrary",)),
        name="diff_attn_prompt",
    )(q, k_p, v_p, *lam_args)

    past = cache_k.shape[1]
    qt = ROW_TILE
    n_qt = n_lat // qt
    q0 = n_prompt * seq // qt
    cos, sin = _rope_tables(n_lat, grid_w)
    lat = pl.BlockSpec((n_lat, inner), lambda b, t: (b, 0))
    ctx = pl.BlockSpec((None, past, inner), lambda b, t: (b, 0, 0))
    o_sample = pl.pallas_call(
        functools.partial(_attn_sample_kernel, lam_init=lam_init),
        out_shape=jax.ShapeDtypeStruct((n_lat_req * n_lat, inner), F32),
        grid=(n_lat_req, n_qt),
        in_specs=[pl.BlockSpec((qt, inner), lambda b, t: (q0 + b * n_qt + t, 0)),
                  lat, lat, ctx, ctx,
                  pl.BlockSpec((qt, PAIR), lambda b, t: (t, 0)),
                  pl.BlockSpec((qt, PAIR), lambda b, t: (t, 0)),
                  pl.BlockSpec((n_lat, PAIR), lambda b, t: (0, 0)),
                  pl.BlockSpec((n_lat, PAIR), lambda b, t: (0, 0))] + lam_specs,
        out_specs=pl.BlockSpec((qt, inner), lambda b, t: (b * n_qt + t, 0)),
        compiler_params=_cparams(("arbitrary", "arbitrary")),
        name="diff_attn_sample",
    )(q, k_s, v_s, cache_k, cache_v, cos, sin, cos, sin, *lam_args)
    return o_prompt, o_sample


def kernel(x_prompt, x_sample, state_wkv, cache_k, cache_v, c, c_ctx, mod_w, mod_b, ln1_g, ln1_b, ln2_g, ln2_b, ev_in_w, ev_pw_b, ev_ts_mu, ev_w0, ev_w_up, ev_a0, ev_a_up, ev_g_up, ev_k_k, ev_k_a, ev_r_k, ev_gn_g, ev_gn_b, ev_dw_w, ev_dw_b, ev_cln_g, ev_cln_b, ev_out_w, od_in_w, od_lq1, od_lk1, od_lq2, od_lk2, od_subln_g, od_out_w, rg_w, rg_b, re_w, re_b, ffn_w_gate, ffn_w_up, ffn_w_down):
    nbp, seq, d = x_prompt.shape
    nbs, n_lat, _ = x_sample.shape
    depth = mod_w.shape[0]
    n_prompt_tok = nbp * seq
    geom = (n_prompt_tok, n_lat)
    alpha = (2 * depth) ** 0.25
    a_width = ev_w0.shape[-1]
    a_cols = ev_ts_mu.shape[-1]
    heads = a_width // HEAD
    grid_w = 64
    assert n_prompt_tok % MOE_TILE == 0 and n_lat % MOE_TILE == 0 and seq % ROW_TILE == 0
    assert 1 + nbs <= 8 and n_lat % ROW_TILE == 0

    xs = (x_prompt.reshape(n_prompt_tok, d), x_sample.reshape(nbs * n_lat, d))
    cond = jnp.concatenate([c_ctx[None, :], c, jnp.zeros((8 - 1 - nbs, d), F32)], axis=0)
    mod = _ada_table(cond, mod_w, mod_b)

    moe_rw, moe_rb, wg_t, wu_t, wd_t = _moe_weights(rg_w, rg_b, re_w, re_b, ffn_w_gate, ffn_w_up, ffn_w_down)
    new_wkv, new_k, new_v = [], [], []
    for l in range(depth):
        last = l == depth - 1
        if l % 2 == 0:
            i = l // 2
            pa, pb = _mod_matmul(xs, mod, l, ev_in_w[i], ((a_cols, False), (ev_in_w.shape[-1] - a_cols, False)), geom)
            prm = dict(ts_mu=ev_ts_mu[i][None, :], w0=ev_w0[i], w_up=ev_w_up[i], a0=ev_a0[i], a_up=ev_a_up[i],
                       g_up=ev_g_up[i], k_k=ev_k_k[i][None, :], k_a=ev_k_a[i][None, :],
                       r_k=ev_r_k[i].reshape(1, a_width), gn_g=ev_gn_g[i][None, :], gn_b=ev_gn_b[i][None, :])
            zero_state = jnp.zeros((nbp, 2, heads, HEAD, HEAD), F32)
            oa_p, st = _rwkv_mix(pa, 0, nbp, seq, zero_state, prm)
            oa_s, _ = _rwkv_mix(pa, n_prompt_tok, nbs, n_lat, state_wkv[:, i], prm)
            conv = (ev_pw_b[i], ev_dw_w[i], ev_dw_b[i], ev_cln_g[i], ev_cln_b[i])
            ob_p = _conv_module(pb, 0, nbp, seq, *conv)
            ob_s = _conv_module(pb, n_prompt_tok, nbs, n_lat, *conv)
            hs = [(oa_p, oa_s), (ob_p, ob_s)]
            out_w = ev_out_w[i]
            new_wkv.append(st)
        else:
            j = l // 2
            lam_init = 0.8 - 0.6 * math.exp(-0.3 * l)
            inner = od_in_w.shape[-1] // 3
            q, k_p, k_s, v_p, v_s = _mod_matmul(xs, mod, l, od_in_w[j],
                                                 ((inner, False), (inner, True), (inner, True)), geom)
            prm = dict(lq1=od_lq1[j], lk1=od_lk1[j], lq2=od_lq2[j], lk2=od_lk2[j], subln_g=od_subln_g[j])
            ck = cache_k[:, j].reshape(nbs, -1, inner)
            cv = cache_v[:, j].reshape(nbs, -1, inner)
            hs = [_attention(q, k_p, v_p, k_s, v_s, ck, cv, prm, lam_init, nbp, seq, nbs, n_lat, grid_w)]
            out_w = od_out_w[j]
            new_k.append(k_p.reshape(nbp, seq, inner // PAIR, PAIR))
            new_v.append(v_p.reshape(nbp, seq, inner // PAIR, PAIR))
        x = _out_proj_norm(hs, xs, mod, l, out_w, ln1_g[l], ln1_b[l], alpha, geom)
        xs = tuple(_moe_norm(x, mod, l, moe_rw, moe_rb, wg_t, wu_t, wd_t, ln2_g[l], ln2_b[l], alpha, geom,
                             split_out=last))

    y_prompt = xs[0].reshape(nbp, seq, d)
    y_sample = xs[1].reshape(nbs, n_lat, d)
    return (y_prompt, y_sample, jnp.stack(new_wkv, axis=1), jnp.stack(new_k, axis=1), jnp.stack(new_v, axis=1))
```

```python
import functools
import math

import jax
import jax.numpy as jnp
from jax import lax
from jax.experimental import pallas as pl
from jax.experimental.pallas import tpu as pltpu

F32 = jnp.float32
BF16 = jnp.bfloat16

LANES = 128
HEAD = 64
PAIR = 2 * HEAD
CHUNK = 64
INV_BLOCK = 16
ROW_TILE = 256
MOE_TILE = 1024
CONV_W = 31
N_GROUPS = 4
EXPERTS_PER_GROUP = 8
N_EXPERTS = N_GROUPS * EXPERTS_PER_GROUP
ROUTER_LANES = 128
EXPERTS_PER_STEP = 4
LN_EPS = 1e-5
GN_EPS = 64e-5
RMS_EPS = 1e-5
ROPE_BASE = 10000.0
EXP_M05 = math.exp(-0.5)
VMEM_LIMIT = 56 * 1024 * 1024


def _bdot(a, b):
    return jnp.dot(a.astype(BF16), b.astype(BF16), preferred_element_type=F32)


def _bdot_nt(a, b):
    return lax.dot_general(a.astype(BF16), b.astype(BF16), (((1,), (1,)), ((), ())),
                           preferred_element_type=F32)


def _split2(a):
    hi = a.astype(BF16)
    lo = (a - hi.astype(F32)).astype(BF16)
    return hi, lo


def _split3(a):
    hi = a.astype(BF16)
    r1 = a - hi.astype(F32)
    mid = r1.astype(BF16)
    lo = (r1 - mid.astype(F32)).astype(BF16)
    return hi, mid, lo


def _dot_f32ish(a, b):
    ah, al = _split2(a)
    bh, bl = _split2(b)
    d = lambda x, y: jnp.dot(x, y, preferred_element_type=F32)
    return d(ah, bh) + d(ah, bl) + d(al, bh)


def _dot_exact_lhs(m_bf16, a):
    h, m, l = _split3(a)
    d = lambda y: jnp.dot(m_bf16, y, preferred_element_type=F32)
    return d(h) + d(m) + d(l)


def _dot_exact_rhs(a, m_bf16):
    h, m, l = _split3(a)
    d = lambda x: jnp.dot(x, m_bf16, preferred_element_type=F32)
    return d(h) + d(m) + d(l)


def _layer_norm(z, g, b, eps):
    mu = jnp.mean(z, axis=-1, keepdims=True)
    zc = z - mu
    var = jnp.mean(zc * zc, axis=-1, keepdims=True)
    return zc * lax.rsqrt(var + eps) * g + b


def _cond_row(tile, tile_rows, n_prompt_tok, dec_seq):
    tok = tile * tile_rows
    return jnp.where(tok < n_prompt_tok, 0, 1 + (tok - n_prompt_tok) // dec_seq)


def _cast_rows(src_ref, dst_ref, rows_per_step=128):
    n = src_ref.shape[0]
    for s in range(0, n, rows_per_step):
        dst_ref[s:s + rows_per_step, :] = src_ref[s:s + rows_per_step, :].astype(dst_ref.dtype)


def _cparams(sem):
    return pltpu.CompilerParams(dimension_semantics=sem, vmem_limit_bytes=VMEM_LIMIT)


def _ada_kernel(cond_ref, w_ref, b_ref, o_ref):
    c = cond_ref[...]
    s = c * jax.nn.sigmoid(c)
    o_ref[0, 0] = _dot_f32ish(s, w_ref[0]) + b_ref[0, 0]


def _ada_table(cond8, mod_w, mod_b):
    depth, d, _ = mod_w.shape
    return pl.pallas_call(
        _ada_kernel,
        out_shape=jax.ShapeDtypeStruct((depth, 6, 8, d), F32),
        grid=(depth, 6),
        in_specs=[pl.BlockSpec((8, d), lambda l, j: (0, 0)),
                  pl.BlockSpec((1, d, d), lambda l, j: (l, 0, j)),
                  pl.BlockSpec((1, 1, 1, d), lambda l, j: (l, j, 0, 0))],
        out_specs=pl.BlockSpec((1, 1, 8, d), lambda l, j: (l, j, 0, 0)),
        compiler_params=_cparams(("arbitrary", "arbitrary")),
        name="ada_table",
    )(cond8, mod_w, mod_b.reshape(depth, 6, 1, d))


def _mod_spec(layer, which, d):
    return pl.BlockSpec((None, None, 8, d), lambda *_: (layer, which, 0, 0))


def _group_specs(tile, width, n_prompt_tiles):
    return [pl.BlockSpec((tile, width), lambda i, *_: (jnp.minimum(i, n_prompt_tiles - 1), 0)),
            pl.BlockSpec((tile, width), lambda i, *_: (jnp.maximum(i - n_prompt_tiles, 0), 0))]


def _load_group(i, n_prompt_tiles, refs, rows=slice(None)):
    if len(refs) == 1:
        return refs[0][rows, :]
    return jnp.where(i < n_prompt_tiles, refs[0][rows, :], refs[1][rows, :])


def _store_group(i, n_prompt_tiles, refs, value, rows=slice(None)):
    if len(refs) == 1:
        refs[0][rows, :] = value
        return

    @pl.when(i < n_prompt_tiles)
    def _():
        refs[0][rows, :] = value

    @pl.when(i >= n_prompt_tiles)
    def _():
        refs[1][rows, :] = value


def _modmm_kernel(*refs, n_x, outs, geom):
    x_refs = refs[:n_x]
    sh_ref, sc_ref, w_ref = refs[n_x:n_x + 3]
    o_refs, wb = refs[n_x + 3:-1], refs[-1]
    i = pl.program_id(0)
    n_pt = geom[0] // ROW_TILE

    @pl.when(i == 0)
    def _():
        _cast_rows(w_ref, wb)

    r = _cond_row(i, ROW_TILE, *geom)
    h = _load_group(i, n_pt, x_refs) * (1.0 + sc_ref[pl.ds(r, 1), :]) + sh_ref[pl.ds(r, 1), :]
    hb = h.astype(BF16)
    off = 0
    k = 0
    for n, split in outs:
        y = jnp.dot(hb, wb[:, off:off + n], preferred_element_type=F32)
        _store_group(i, n_pt, o_refs[k:k + 1 + split], y)
        off += n
        k += 1 + split


def _mod_matmul(xs, mod, layer, w, outs, geom):
    d, n_out = w.shape
    n_prompt_tok = geom[0]
    n_tok = sum(x.shape[0] for x in xs)
    n_pt = n_prompt_tok // ROW_TILE
    whole = lambda n: [pl.BlockSpec((ROW_TILE, n), lambda i: (i, 0))]
    x_specs = whole(d) if len(xs) == 1 else _group_specs(ROW_TILE, d, n_pt)
    out_shape, out_specs = [], []
    for n, split in outs:
        if split:
            out_shape += [jax.ShapeDtypeStruct((n_prompt_tok, n), F32),
                          jax.ShapeDtypeStruct((n_tok - n_prompt_tok, n), F32)]
            out_specs += _group_specs(ROW_TILE, n, n_pt)
        else:
            out_shape += [jax.ShapeDtypeStruct((n_tok, n), F32)]
            out_specs += whole(n)
    return pl.pallas_call(
        functools.partial(_modmm_kernel, n_x=len(xs), outs=outs, geom=geom),
        out_shape=out_shape,
        grid=(n_tok // ROW_TILE,),
        in_specs=x_specs + [_mod_spec(layer, 0, d), _mod_spec(layer, 1, d),
                            pl.BlockSpec((d, n_out), lambda i: (0, 0), pipeline_mode=pl.Buffered(1))],
        out_specs=out_specs,
        scratch_shapes=[pltpu.VMEM((d, n_out), BF16)],
        compiler_params=_cparams(("arbitrary",)),
        name=f"mod_matmul_l{layer}",
    )(*xs, mod, mod, w)


def _outproj_kernel(*refs, arity, alpha, geom):
    groups, k = [], 0
    for n in arity:
        groups.append(refs[k:k + n])
        k += n
    gate_ref, w_ref, g_ref, b_ref, o_ref, wb = refs[k:]
    h_groups, x_group = groups[:-1], groups[-1]
    i = pl.program_id(0)
    n_pt = geom[0] // ROW_TILE

    @pl.when(i == 0)
    def _():
        _cast_rows(w_ref, wb)

    r = _cond_row(i, ROW_TILE, *geom)
    y = None
    off = 0
    for h_refs in h_groups:
        n = h_refs[0].shape[1]
        h = _load_group(i, n_pt, h_refs)
        part = jnp.dot(h.astype(BF16), wb[off:off + n, :], preferred_element_type=F32)
        y = part if y is None else y + part
        off += n
    z = alpha * _load_group(i, n_pt, x_group) + gate_ref[pl.ds(r, 1), :] * y
    o_ref[...] = _layer_norm(z, g_ref[...], b_ref[...], LN_EPS)


def _out_proj_norm(hs, xs, mod, layer, w, ln_g, ln_b, alpha, geom):
    d = w.shape[1]
    n_tok = sum(x.shape[0] for x in xs)
    n_pt = geom[0] // ROW_TILE
    specs = []
    for grp in list(hs) + [xs]:
        n = grp[0].shape[1]
        specs += [pl.BlockSpec((ROW_TILE, n), lambda i: (i, 0))] if len(grp) == 1 else _group_specs(ROW_TILE, n, n_pt)
    return pl.pallas_call(
        functools.partial(_outproj_kernel, arity=tuple(len(g) for g in list(hs) + [xs]), alpha=alpha, geom=geom),
        out_shape=jax.ShapeDtypeStruct((n_tok, d), F32),
        grid=(n_tok // ROW_TILE,),
        in_specs=specs + [
            _mod_spec(layer, 2, d),
            pl.BlockSpec(w.shape, lambda i: (0, 0), pipeline_mode=pl.Buffered(1)),
            pl.BlockSpec((1, d), lambda i: (0, 0)),
            pl.BlockSpec((1, d), lambda i: (0, 0))],
        out_specs=pl.BlockSpec((ROW_TILE, d), lambda i: (i, 0)),
        scratch_shapes=[pltpu.VMEM(w.shape, BF16)],
        compiler_params=_cparams(("arbitrary",)),
        name=f"out_proj_norm_l{layer}",
    )(*[a for g in hs for a in g], *xs, mod, w, ln_g.reshape(1, d), ln_b.reshape(1, d))


def _route(logits):
    neg = jnp.float32(-3e38)
    big = jnp.int32(1 << 20)
    lane = lax.broadcasted_iota(jnp.int32, logits.shape, 1)
    is_g = lane < N_GROUPS
    gl = jnp.where(is_g, logits, neg)
    gmax = jnp.max(gl, axis=-1, keepdims=True)
    gidx = jnp.min(jnp.where(gl == gmax, lane, big), axis=-1, keepdims=True)
    gsum = jnp.sum(jnp.where(is_g, jnp.exp(gl - gmax), 0.0), axis=-1, keepdims=True)
    g_p = 1.0 / gsum
    lo = N_GROUPS + gidx * EXPERTS_PER_GROUP
    in_grp = (lane >= lo) & (lane < lo + EXPERTS_PER_GROUP)
    el = jnp.where(in_grp, logits, neg)
    v1 = jnp.max(el, axis=-1, keepdims=True)
    i1 = jnp.min(jnp.where(el == v1, lane, big), axis=-1, keepdims=True)
    el2 = jnp.where(lane == i1, neg, el)
    v2 = jnp.max(el2, axis=-1, keepdims=True)
    i2 = jnp.min(jnp.where(el2 == v2, lane, big), axis=-1, keepdims=True)
    e21 = jnp.exp(v2 - v1)
    den = 1.0 + e21
    w1 = g_p / den
    w2 = g_p * e21 / den
    return jnp.where(lane == i1, w1, 0.0) + jnp.where(lane == i2, w2, 0.0), gidx


def _moe_kernel(x_ref, sh_ref, sc_ref, gate_ref, rw_ref, rb_ref, wg_ref, wu_ref, wd_ref, g_ref, b_ref,
                *rest, alpha, geom):
    o_refs = rest[:-6]
    h_scr, gates_scr, acc_scr, perm_scr, permt_scr, bounds = rest[-6:]
    i = pl.program_id(0)
    e = pl.program_id(1)
    r = _cond_row(i, MOE_TILE, *geom)
    n_pt = geom[0] // MOE_TILE
    sub = ROW_TILE
    n_sub = MOE_TILE // sub

    @pl.when(e == 0)
    def _():
        lane = lax.broadcasted_iota(jnp.int32, (sub, ROUTER_LANES), 1)
        ri = lax.broadcasted_iota(jnp.int32, (sub, sub), 0)
        ci = lax.broadcasted_iota(jnp.int32, (sub, sub), 1)
        tril = jnp.where(ci <= ri, 1.0, 0.0).astype(BF16)
        running = jnp.zeros((1, ROUTER_LANES), F32)
        onehots, cums = [], []
        for s in range(n_sub):
            rows = slice(s * sub, (s + 1) * sub)
            h = x_ref[rows, :] * (1.0 + sc_ref[pl.ds(r, 1), :]) + sh_ref[pl.ds(r, 1), :]
            acc_scr[rows, :] = h
            logits = _dot_f32ish(h, rw_ref[...]) + rb_ref[...]
            gates, gidx = _route(logits)
            gates_scr[rows, :] = gates
            onehot = jnp.where(lane == gidx, 1.0, 0.0)
            cum = jnp.dot(tril, onehot.astype(BF16), preferred_element_type=F32) + running
            running = cum[sub - 1:sub, :]
            onehots.append(onehot)
            cums.append(cum)
        gi = lax.broadcasted_iota(jnp.int32, (ROUTER_LANES, ROUTER_LANES), 0)
        gj = lax.broadcasted_iota(jnp.int32, (ROUTER_LANES, ROUTER_LANES), 1)
        before = jnp.where((gi < gj) & (gi < N_GROUPS), 1.0, 0.0).astype(BF16)
        starts = _dot_exact_rhs(jnp.broadcast_to(running, (8, ROUTER_LANES)), before)[0:1, :]
        for g in range(N_GROUPS + 1):
            bounds[g] = starts[0, g].astype(jnp.int32)
        dest_cols = [jnp.sum(onehots[s] * (starts + cums[s] - 1.0), axis=-1, keepdims=True) for s in range(n_sub)]
        dest_row = jnp.concatenate(
            [jnp.broadcast_to(dc, (sub, ROUTER_LANES)).T[0:1, :] for dc in dest_cols], axis=1)
        col_id = lax.broadcasted_iota(jnp.int32, (sub, MOE_TILE), 1).astype(F32)
        row_id = lax.broadcasted_iota(jnp.int32, (sub, MOE_TILE), 0).astype(F32)
        for s in range(n_sub):
            rows = slice(s * sub, (s + 1) * sub)
            perm_scr[rows, :] = jnp.where(row_id + float(s * sub) == dest_row, 1.0, 0.0).astype(BF16)
            permt_scr[rows, :] = jnp.where(dest_cols[s] == col_id, 1.0, 0.0).astype(BF16)
        hb = acc_scr[...].astype(BF16)
        g_hi, g_mid, g_lo = _split3(gates_scr[...])
        for s in range(n_sub):
            rows = slice(s * sub, (s + 1) * sub)
            pm = perm_scr[rows, :]
            h_scr[rows, :] = jnp.dot(pm, hb, preferred_element_type=F32).astype(BF16)
        for s in range(n_sub):
            rows = slice(s * sub, (s + 1) * sub)
            pm = perm_scr[rows, :]
            d = lambda y: jnp.dot(pm, y, preferred_element_type=F32)
            gates_scr[rows, :] = d(g_hi) + d(g_mid) + d(g_lo)
            acc_scr[rows, :] = jnp.zeros((sub, acc_scr.shape[1]), F32)

    grp = e // (EXPERTS_PER_GROUP // EXPERTS_PER_STEP)
    first, end = bounds[grp], bounds[grp + 1]
    lane = lax.broadcasted_iota(jnp.int32, (sub, ROUTER_LANES), 1)
    for s in range(n_sub):
        rows = slice(s * sub, (s + 1) * sub)

        @pl.when((first < (s + 1) * sub) & (end > s * sub))
        def _():
            hb = h_scr[rows, :]
            gates = gates_scr[rows, :]
            out = acc_scr[rows, :]
            for j in range(EXPERTS_PER_STEP):
                lane_j = N_GROUPS + e * EXPERTS_PER_STEP + j
                ge = jnp.sum(jnp.where(lane == lane_j, gates, 0.0), axis=-1, keepdims=True)
                a = jnp.dot(hb, wg_ref[j], preferred_element_type=F32)
                u = jnp.dot(hb, wu_ref[j], preferred_element_type=F32)
                hid = (a * jax.nn.sigmoid(a)) * u * ge
                out = out + jnp.dot(hid.astype(BF16), wd_ref[j], preferred_element_type=F32)
            acc_scr[rows, :] = out

    @pl.when(e == pl.num_programs(1) - 1)
    def _():
        a_hi, a_lo = _split2(acc_scr[...])
        for s in range(n_sub):
            rows = slice(s * sub, (s + 1) * sub)
            pt = permt_scr[rows, :]
            moe = jnp.dot(pt, a_hi, preferred_element_type=F32) + jnp.dot(pt, a_lo, preferred_element_type=F32)
            z = alpha * x_ref[rows, :] + gate_ref[pl.ds(r, 1), :] * moe
            _store_group(i, n_pt, o_refs, _layer_norm(z, g_ref[...], b_ref[...], LN_EPS), rows)


def _moe_norm(x, mod, layer, rw, rb, w_gate, w_up, w_down, ln_g, ln_b, alpha, geom, split_out):
    n_tok, d = x.shape
    ff = w_gate.shape[-1]
    assert w_gate.dtype == BF16 and w_up.dtype == BF16 and w_down.dtype == BF16
    const = lambda i, e: (0, 0)
    if split_out:
        out_shape = [jax.ShapeDtypeStruct((geom[0], d), F32), jax.ShapeDtypeStruct((n_tok - geom[0], d), F32)]
        out_specs = _group_specs(MOE_TILE, d, geom[0] // MOE_TILE)
    else:
        out_shape = [jax.ShapeDtypeStruct((n_tok, d), F32)]
        out_specs = [pl.BlockSpec((MOE_TILE, d), lambda i, e: (i, 0))]
    return pl.pallas_call(
        functools.partial(_moe_kernel, alpha=alpha, geom=geom),
        out_shape=out_shape,
        grid=(n_tok // MOE_TILE, N_EXPERTS // EXPERTS_PER_STEP),
        in_specs=[pl.BlockSpec((MOE_TILE, d), lambda i, e: (i, 0)),
                  _mod_spec(layer, 3, d), _mod_spec(layer, 4, d), _mod_spec(layer, 5, d),
                  pl.BlockSpec((d, ROUTER_LANES), const),
                  pl.BlockSpec((1, ROUTER_LANES), const),
                  pl.BlockSpec((None, EXPERTS_PER_STEP, d, ff), lambda i, e: (layer, e, 0, 0)),
                  pl.BlockSpec((None, EXPERTS_PER_STEP, d, ff), lambda i, e: (layer, e, 0, 0)),
                  pl.BlockSpec((None, EXPERTS_PER_STEP, ff, d), lambda i, e: (layer, e, 0, 0)),
                  pl.BlockSpec((1, d), const),
                  pl.BlockSpec((1, d), const)],
        out_specs=out_specs,
        scratch_shapes=[pltpu.VMEM((MOE_TILE, d), BF16),
                        pltpu.VMEM((MOE_TILE, ROUTER_LANES), F32),
                        pltpu.VMEM((MOE_TILE, d), F32),
                        pltpu.VMEM((MOE_TILE, MOE_TILE), BF16), pltpu.VMEM((MOE_TILE, MOE_TILE), BF16),
                        pltpu.SMEM((8,), jnp.int32)],
        compiler_params=_cparams(("arbitrary", "arbitrary")),
        name=f"moe_norm_l{layer}",
    )(x, mod, mod, mod, rw, rb, w_gate, w_up, w_down, ln_g.reshape(1, d), ln_b.reshape(1, d))


def _rwkv_kernel(r_ref, k_ref, v_ref, lora_ref, mur_ref, muk_ref, muv_ref, mul_ref,
                 w0_ref, wup_ref, a0_ref, aup_ref, gup_ref, kk_ref, ka_ref, rk_ref, gng_ref, gnb_ref, s0_ref,
                 out_ref, sfin_ref,
                 lw_s, kz_s, bz_s, rr_s, vv_s, kn_s,
                 abar_s, rt_s, wt_s, vst_s, p_s, tm_s, mb_s, mk_s, nkb_s, txr_s, tx_s, gl_s,
                 g_s, h_s, q_s, y0_s, y_s, st_s, *, seq_len, unroll):
    t_len = seq_len
    n_chunks = t_len // CHUNK
    c2 = 2 * CHUNK

    row = lax.broadcasted_iota(jnp.int32, (t_len, 1), 0)

    def tshift(x, mu):
        prev = jnp.where(row == 0, 0.0, pltpu.roll(x, 1, 0))
        nxt = jnp.where(row == t_len - 1, 0.0, pltpu.roll(x, t_len - 1, 0))
        return x + mu * (0.5 * (prev + nxt) - x)

    r = tshift(r_ref[...], mur_ref[...])
    k = tshift(k_ref[...], muk_ref[...])
    v = tshift(v_ref[...], muv_ref[...])
    lo = tshift(lora_ref[...], mul_ref[...])
    wd = jnp.tanh(lo[:, 0:PAIR])
    ad = lo[:, PAIR:2 * PAIR]
    gate = _bdot(jax.nn.sigmoid(lo[:, 2 * PAIR:3 * PAIR]), gup_ref[...])

    ri = lax.broadcasted_iota(jnp.int32, (PAIR, PAIR), 0)
    ci = lax.broadcasted_iota(jnp.int32, (PAIR, PAIR), 1)
    same_head = (ri // HEAD) == (ci // HEAD)
    ones_bd = jnp.where(same_head, 1.0, 0.0).astype(BF16)
    eye = ri == ci

    kk = k * kk_ref[...]
    ssq = _dot_exact_rhs(kk * kk, ones_bd)
    kk = kk * lax.rsqrt(jnp.maximum(ssq, 1e-24))
    wup2 = wup_ref[...].reshape(PAIR, PAIR)
    aup2 = aup_ref[...].reshape(PAIR, PAIR)
    bonus = jnp.zeros((t_len, PAIR), F32)
    for z in range(2):
        zrows = (ri // HEAD) == z
        w_pre = w0_ref[z:z + 1, :] + _bdot(wd, jnp.where(zrows, wup2, 0.0))
        lw_s[z] = -EXP_M05 * jax.nn.sigmoid(w_pre)
        a = jax.nn.sigmoid(a0_ref[z:z + 1, :] + _bdot(ad, jnp.where(zrows, aup2, 0.0)))
        kz = k * (1.0 + (a - 1.0) * ka_ref[...])
        kz_s[z] = kz
        bz_s[z] = kk * a
        bonus = bonus + _dot_exact_rhs(r * kz * rk_ref[...], ones_bd) * v
    rr_s[...] = r
    vv_s[...] = v
    kn_s[...] = kk

    rt_i = ri % CHUNK
    cs_i = ci % CHUNK
    strict = (cs_i < rt_i, cs_i > rt_i)
    incl = (cs_i <= rt_i, cs_i >= rt_i)
    t64 = lax.broadcasted_iota(jnp.int32, (CHUNK, CHUNK), 0)
    s64 = lax.broadcasted_iota(jnp.int32, (CHUNK, CHUNK), 1)
    tri = (jnp.where(s64 <= t64, 1.0, 0.0).astype(BF16), jnp.where(s64 >= t64, 1.0, 0.0).astype(BF16))
    lane_head0 = lax.broadcasted_iota(jnp.int32, (CHUNK, PAIR), 1) < HEAD
    eye_f = jnp.where(eye, 1.0, 0.0)
    same16 = (ri // INV_BLOCK) == (ci // INV_BLOCK)

    def stack(x):
        return jnp.concatenate([jnp.where(lane_head0, x, 0.0), jnp.where(lane_head0, 0.0, x)], axis=0)

    def fold(x):
        return x[:CHUNK] + x[CHUNK:]

    def chunk_rows(c):
        return pl.ds(pl.multiple_of(c * CHUNK, CHUNK), CHUNK)

    def each_item(body):
        def both_directions(c, carry):
            body(0, c)
            body(1, c)
            return carry

        lax.fori_loop(0, n_chunks, both_directions, 0, unroll=unroll)

    def stage_operands(z, c):
        rows = chunk_rows(c)
        r_c, kn_c = rr_s[rows, :], kn_s[rows, :]
        lw_c, kz_c, bz_c = lw_s[z, rows, :], kz_s[z, rows, :], bz_s[z, rows, :]
        lg = _dot_exact_lhs(tri[z], lw_c)
        total = lg[CHUNK - 1:CHUNK, :] if z == 0 else lg[0:1, :]
        e_neg = jnp.exp(-lg)
        e_rem = jnp.exp(total - lg)
        abar = stack(kn_c * jnp.exp(lg - lw_c))
        rt = stack(r_c * jnp.exp(lg))
        big = _bdot_nt(jnp.concatenate([abar, rt], axis=0),
                       jnp.concatenate([stack(bz_c * e_neg), stack(kz_c * e_neg)], axis=0))
        mb = jnp.where(strict[z], big[:c2, :c2], 0.0)
        abar_s[z, c] = abar.astype(BF16)
        rt_s[z, c] = rt
        wt_s[z, c] = jnp.concatenate([stack(kz_c * e_rem), -stack(bz_c * e_rem)], axis=0).T.astype(BF16)
        gl_s[z, c] = jnp.exp(total)
        m0 = jnp.where(same16, mb, 0.0)
        tm_s[z, c] = eye_f - m0
        p_s[z, c] = m0.astype(BF16)
        mb_s[z, c] = mb.astype(BF16)
        mk_s[z, c] = jnp.where(strict[z], big[:c2, c2:], 0.0).astype(BF16)
        nkb_s[z, c] = jnp.concatenate([jnp.where(incl[z], big[c2:, c2:], 0.0),
                                       -jnp.where(incl[z], big[c2:, :c2], 0.0)], axis=1).astype(BF16)

    def stage_values(c, carry):
        vst_s[c] = stack(vv_s[chunk_rows(c), :]).astype(BF16)
        return carry

    lax.fori_loop(0, n_chunks, stage_values, 0, unroll=unroll)
    each_item(stage_operands)

    def stage_square(z, c):
        m0 = p_s[z, c]
        p_s[z, c] = jnp.dot(m0, m0, preferred_element_type=F32).astype(BF16)
        mkv = jnp.dot(mk_s[z, c], vst_s[c], preferred_element_type=F32)
        txr_s[z, c] = jnp.concatenate([abar_s[z, c], mkv.astype(BF16)], axis=1)

    each_item(stage_square)

    n_round = INV_BLOCK.bit_length() - 2
    for j in range(n_round):
        def stage_double(z, c, last=(j == n_round - 1)):
            tm, p = tm_s[z, c], p_s[z, c]
            tm_s[z, c] = tm + jnp.dot(tm.astype(BF16), p, preferred_element_type=F32)
            if not last:
                p_s[z, c] = jnp.dot(p, p, preferred_element_type=F32).astype(BF16)

        each_item(stage_double)

    size = INV_BLOCK
    while size < CHUNK:
        off_diag = ((ri // (2 * size)) == (ci // (2 * size))) & ((ri // size) != (ci // size))

        def stage_merge_rhs(z, c, off_diag=off_diag):
            cross = jnp.where(off_diag, mb_s[z, c], jnp.zeros((), BF16))
            p_s[z, c] = jnp.dot(cross, tm_s[z, c].astype(BF16), preferred_element_type=F32).astype(BF16)

        def stage_merge(z, c):
            tm = tm_s[z, c]
            tm_s[z, c] = tm - jnp.dot(tm.astype(BF16), p_s[z, c], preferred_element_type=F32)

        each_item(stage_merge_rhs)
        each_item(stage_merge)
        size *= 2

    def stage_apply(z, c):
        tx_s[z, c] = jnp.dot(tm_s[z, c].astype(BF16), txr_s[z, c], preferred_element_type=F32)

    each_item(stage_apply)

    def stage_maps(z, c):
        tx = tx_s[z, c].astype(BF16)
        v_st = vst_s[c]
        wt = wt_s[z, c]
        from_tx = jnp.dot(wt[:, c2:], tx, preferred_element_type=F32)
        h_s[z, c] = jnp.dot(wt[:, :c2], v_st, preferred_element_type=F32) + from_tx[:, PAIR:]
        g_s[z, c] = jnp.where(eye, gl_s[z, c], 0.0) + from_tx[:, :PAIR]
        nkb = nkb_s[z, c]
        from_tx = jnp.dot(nkb[:, c2:], tx, preferred_element_type=F32)
        q_s[z, c] = fold(rt_s[z, c] + from_tx[:, :PAIR])
        y0_s[z, c] = fold(jnp.dot(nkb[:, :c2], v_st, preferred_element_type=F32) + from_tx[:, PAIR:])

    each_item(stage_maps)

    zero64 = jnp.zeros((HEAD, HEAD), F32)
    for z in range(2):
        a = jnp.concatenate([jnp.concatenate([s0_ref[z, 0], zero64], axis=1),
                             jnp.concatenate([zero64, s0_ref[z, 1]], axis=1)], axis=0)
        st_s[z] = a.T

    def propagate(i, carry):
        for z in range(2):
            c = i if z == 0 else n_chunks - 1 - i
            s = st_s[z].astype(BF16)
            y_s[z, chunk_rows(c), :] = _bdot(q_s[z, c], s) + y0_s[z, c]
            st_s[z] = _bdot(g_s[z, c], s) + h_s[z, c]
        return carry

    lax.fori_loop(0, n_chunks, propagate, 0)
    for z in range(2):
        a = st_s[z].T
        sfin_ref[z, 0] = a[:HEAD, :HEAD]
        sfin_ref[z, 1] = a[HEAD:, HEAD:]

    y = y_s[0] + y_s[1]
    mu = _dot_exact_rhs(y, ones_bd) * (1.0 / HEAD)
    yc = y - mu
    var = _dot_exact_rhs(yc * yc, ones_bd) * (1.0 / HEAD)
    yn = yc * lax.rsqrt(var + GN_EPS) * gng_ref[...] + gnb_ref[...]
    out_ref[...] = (yn + bonus) * gate


def _rwkv_mix(pa, first_tok, n_seq, seq_len, s0, p):
    a_width = p["w0"].shape[-1]
    n_pairs = a_width // PAIR
    heads = a_width // HEAD
    t0 = first_tok // seq_len
    lora_w = 3 * PAIR
    lora_blk = 3 * a_width // lora_w
    n_chunks = seq_len // CHUNK
    col = lambda off: (lambda s, q: (t0 + s, off + q))
    vec = lambda off: (lambda s, q: (0, off + q))
    in_specs = [
        pl.BlockSpec((seq_len, PAIR), col(0)),
        pl.BlockSpec((seq_len, PAIR), col(n_pairs)),
        pl.BlockSpec((seq_len, PAIR), col(2 * n_pairs)),
        pl.BlockSpec((seq_len, lora_w), lambda s, q: (t0 + s, lora_blk)),
        pl.BlockSpec((1, PAIR), vec(0)),
        pl.BlockSpec((1, PAIR), vec(n_pairs)),
        pl.BlockSpec((1, PAIR), vec(2 * n_pairs)),
        pl.BlockSpec((1, lora_w), lambda s, q: (0, lora_blk)),
        pl.BlockSpec((2, PAIR), vec(0)),
        pl.BlockSpec((2, HEAD, PAIR), lambda s, q: (0, 0, q)),
        pl.BlockSpec((2, PAIR), vec(0)),
        pl.BlockSpec((2, HEAD, PAIR), lambda s, q: (0, 0, q)),
        pl.BlockSpec((PAIR, PAIR), vec(0)),
        pl.BlockSpec((1, PAIR), vec(0)),
        pl.BlockSpec((1, PAIR), vec(0)),
        pl.BlockSpec((1, PAIR), vec(0)),
        pl.BlockSpec((1, PAIR), vec(0)),
        pl.BlockSpec((1, PAIR), vec(0)),
        pl.BlockSpec((None, 2, 2, HEAD, HEAD), lambda s, q: (s, 0, q, 0, 0)),
    ]
    seq2 = lambda: pltpu.VMEM((2, seq_len, PAIR), F32)
    seq1 = lambda: pltpu.VMEM((seq_len, PAIR), F32)
    item = lambda rows, cols, dt: pltpu.VMEM((2, n_chunks, rows, cols), dt)
    out, s_fin = pl.pallas_call(
        functools.partial(_rwkv_kernel, seq_len=seq_len, unroll=min(n_chunks, 4)),
        out_shape=[jax.ShapeDtypeStruct((n_seq * seq_len, a_width), F32),
                   jax.ShapeDtypeStruct((n_seq, 2, heads, HEAD, HEAD), F32)],
        grid=(n_seq, n_pairs),
        in_specs=in_specs,
        out_specs=[pl.BlockSpec((seq_len, PAIR), lambda s, q: (s, q)),
                   pl.BlockSpec((None, 2, 2, HEAD, HEAD), lambda s, q: (s, 0, q, 0, 0))],
        scratch_shapes=[seq2(), seq2(), seq2(), seq1(), seq1(), seq1(),
                        item(PAIR, PAIR, BF16),
                        item(PAIR, PAIR, F32),
                        item(PAIR, 2 * PAIR, BF16),
                        pltpu.VMEM((n_chunks, PAIR, PAIR), BF16),
                        item(PAIR, PAIR, BF16),
                        item(PAIR, PAIR, F32),
                        item(PAIR, PAIR, BF16),
                        item(PAIR, PAIR, BF16),
                        item(PAIR, 2 * PAIR, BF16),
                        item(PAIR, 2 * PAIR, BF16),
                        item(PAIR, 2 * PAIR, F32),
                        item(1, PAIR, F32),
                        item(PAIR, PAIR, F32), item(PAIR, PAIR, F32),
                        item(CHUNK, PAIR, F32), item(CHUNK, PAIR, F32),
                        seq2(), pltpu.VMEM((2, PAIR, PAIR), F32)],
        compiler_params=_cparams(("arbitrary", "arbitrary")),
        name=f"rwkv_mix_t{seq_len}",
    )(pa, pa, pa, pa, p["ts_mu"], p["ts_mu"], p["ts_mu"], p["ts_mu"],
      p["w0"], p["w_up"], p["a0"], p["a_up"], p["g_up"], p["k_k"], p["k_a"], p["r_k"], p["gn_g"], p["gn_b"], s0)
    return out, s_fin


def _conv_kernel(pb_ref, pwb_ref, dw_ref, dwb_ref, g_ref, b_ref, o_ref, pad_s, *, seq_len):
    width = o_ref.shape[1]
    halo = 16
    sub = 64
    pbv = pb_ref[...] + pwb_ref[...]
    glu = pbv[:, :width] * jax.nn.sigmoid(pbv[:, width:])
    pad_s[0:halo, :] = jnp.zeros((halo, width), F32)
    pad_s[halo:halo + seq_len, :] = glu
    pad_s[halo + seq_len:, :] = jnp.zeros((halo, width), F32)
    first = halo - CONV_W // 2
    for c in range(seq_len // sub):
        acc = jnp.zeros((sub, width), F32)
        for d in range(CONV_W):
            start = c * sub + first + d
            acc = acc + pad_s[start:start + sub, :] * dw_ref[d:d + 1, :]
        y = _layer_norm(acc + dwb_ref[...], g_ref[...], b_ref[...], LN_EPS)
        o_ref[c * sub:(c + 1) * sub, :] = y * jax.nn.sigmoid(y)


def _conv_module(pb, first_tok, n_seq, seq_len, pw_b, dw_w, dw_b, cln_g, cln_b):
    width = dw_w.shape[1]
    t0 = first_tok // seq_len
    const = lambda s: (0, 0)
    return pl.pallas_call(
        functools.partial(_conv_kernel, seq_len=seq_len),
        out_shape=jax.ShapeDtypeStruct((n_seq * seq_len, width), F32),
        grid=(n_seq,),
        in_specs=[pl.BlockSpec((seq_len, 2 * width), lambda s: (t0 + s, 0)),
                  pl.BlockSpec((1, 2 * width), const),
                  pl.BlockSpec((CONV_W, width), const),
                  pl.BlockSpec((1, width), const),
                  pl.BlockSpec((1, width), const),
                  pl.BlockSpec((1, width), const)],
        out_specs=pl.BlockSpec((seq_len, width), lambda s: (s, 0)),
        scratch_shapes=[pltpu.VMEM((seq_len + 32, width), F32)],
        compiler_params=_cparams(("arbitrary",)),
        name=f"conv_module_t{seq_len}",
    )(pb, pw_b.reshape(1, -1), dw_w, dw_b.reshape(1, -1), cln_g.reshape(1, -1), cln_b.reshape(1, -1))


def _diff_lambda(lq1, lk1, lq2, lk2, lam_init):
    dot = lambda a, b: jnp.exp(jnp.sum(a[...] * b[...], axis=-1, keepdims=True))
    return dot(lq1, lk1) - dot(lq2, lk2) + lam_init


def _diff_attend(q, keys, vals, lam, lam_init, subln):
    nq = q.shape[0]
    lane_first = lax.broadcasted_iota(jnp.int32, q.shape, 1) < HEAD
    q = q * (HEAD ** -0.5)
    q_st = jnp.concatenate([jnp.where(lane_first, q, 0.0), jnp.where(lane_first, 0.0, q)], axis=0)
    s = _bdot_nt(q_st, keys)
    s = s - jnp.max(s, axis=-1, keepdims=True)
    pexp = jnp.exp(s)
    ov = _bdot(pexp, vals) / jnp.sum(pexp, axis=-1, keepdims=True)
    o = ov[:nq] - lam * ov[nq:]
    o = o * lax.rsqrt(jnp.mean(o * o, axis=-1, keepdims=True) + RMS_EPS)
    return o * subln * (1.0 - lam_init)


def _head_cols(h):
    return slice(h * PAIR, (h + 1) * PAIR)


def _attn_prompt_kernel(q_ref, k_ref, v_ref, lq1, lk1, lq2, lk2, sub_ref, o_ref, *, lam_init):
    lam = _diff_lambda(lq1, lk1, lq2, lk2, lam_init)
    for h in range(q_ref.shape[1] // PAIR):
        c = _head_cols(h)
        o_ref[:, c] = _diff_attend(q_ref[:, c], k_ref[:, c], v_ref[:, c], lam, lam_init, sub_ref[...])


def _rope(x, cos, sin_signed):
    lane = lax.broadcasted_iota(jnp.int32, x.shape, 1)
    quarter = HEAD // 4
    partner = jnp.where((lane // quarter) % 2 == 0,
                        pltpu.roll(x, LANES - quarter, 1), pltpu.roll(x, quarter, 1))
    return x * cos + partner * sin_signed


def _attn_sample_kernel(q_ref, k_ref, v_ref, ck_ref, cv_ref, cq_ref, sq_ref, cosk_ref, sink_ref,
                        lq1, lk1, lq2, lk2, sub_ref, o_ref, *, lam_init):
    lam = _diff_lambda(lq1, lk1, lq2, lk2, lam_init)
    for h in range(q_ref.shape[1] // PAIR):
        c = _head_cols(h)
        q = _rope(q_ref[:, c], cq_ref[...], sq_ref[...])
        kl = _rope(k_ref[:, c], cosk_ref[...], sink_ref[...])
        keys = jnp.concatenate([ck_ref[:, c], kl], axis=0)
        vals = jnp.concatenate([cv_ref[:, c], v_ref[:, c]], axis=0)
        o_ref[:, c] = _diff_attend(q, keys, vals, lam, lam_init, sub_ref[...])


def _rope_tables(n_lat, grid_w):
    quarter = HEAD // 4
    inv = ROPE_BASE ** (-jnp.arange(quarter, dtype=F32) / quarter)
    t = jnp.arange(n_lat)
    row_ang = (t // grid_w).astype(F32)[:, None] * inv
    col_ang = (t % grid_w).astype(F32)[:, None] * inv
    ang = jnp.concatenate([row_ang, row_ang, col_ang, col_ang] * 2, axis=-1)
    sign = jnp.tile(jnp.concatenate([-jnp.ones(quarter, F32), jnp.ones(quarter, F32)]), 2 * HEAD // (2 * quarter))
    return jnp.cos(ang), jnp.sin(ang) * sign


def _attention(q, k_p, v_p, k_s, v_s, cache_k, cache_v, p, lam_init, n_prompt, seq, n_lat_req, n_lat, grid_w):
    inner = q.shape[1]
    vec = lambda n: pl.BlockSpec((1, n), lambda *_: (0, 0))
    lam_args = [p[n].reshape(1, -1) for n in ("lq1", "lk1", "lq2", "lk2")] + [p["subln_g"].reshape(1, -1)]
    lam_specs = [vec(HEAD)] * 4 + [vec(PAIR)]
    blk = pl.BlockSpec((seq, inner), lambda b: (b, 0))
    o_prompt = pl.pallas_call(
        functools.partial(_attn_prompt_kernel, lam_init=lam_init),
        out_shape=jax.ShapeDtypeStruct((n_prompt * seq, inner), F32),
        grid=(n_prompt,),
        in_specs=[blk, blk, blk] + lam_specs,
        out_specs=blk,
        compiler_params=_cparams(("arbitrary",)),
        name="diff_attn_prompt",
    )(q, k_p, v_p, *lam_args)

    past = cache_k.shape[1]
    qt = ROW_TILE
    n_qt = n_lat // qt
    q0 = n_prompt * seq // qt
    cos, sin = _rope_tables(n_lat, grid_w)
    lat = pl.BlockSpec((n_lat, inner), lambda b, t: (b, 0))
    ctx = pl.BlockSpec((None, past, inner), lambda b, t: (b, 0, 0))
    o_sample = pl.pallas_call(
        functools.partial(_attn_sample_kernel, lam_init=lam_init),
        out_shape=jax.ShapeDtypeStruct((n_lat_req * n_lat, inner), F32),
        grid=(n_lat_req, n_qt),
        in_specs=[pl.BlockSpec((qt, inner), lambda b, t: (q0 + b * n_qt + t, 0)),
                  lat, lat, ctx, ctx,
                  pl.BlockSpec((qt, PAIR), lambda b, t: (t, 0)),
                  pl.BlockSpec((qt, PAIR), lambda b, t: (t, 0)),
                  pl.BlockSpec((n_lat, PAIR), lambda b, t: (0, 0)),
                  pl.BlockSpec((n_lat, PAIR), lambda b, t: (0, 0))] + lam_specs,
        out_specs=pl.BlockSpec((qt, inner), lambda b, t: (b * n_qt + t, 0)),
        compiler_params=_cparams(("arbitrary", "arbitrary")),
        name="diff_attn_sample",
    )(q, k_s, v_s, cache_k, cache_v, cos, sin, cos, sin, *lam_args)
    return o_prompt, o_sample


def kernel(x_prompt, x_sample, state_wkv, cache_k, cache_v, c, c_ctx, mod_w, mod_b, ln1_g, ln1_b, ln2_g, ln2_b, ev_in_w, ev_pw_b, ev_ts_mu, ev_w0, ev_w_up, ev_a0, ev_a_up, ev_g_up, ev_k_k, ev_k_a, ev_r_k, ev_gn_g, ev_gn_b, ev_dw_w, ev_dw_b, ev_cln_g, ev_cln_b, ev_out_w, od_in_w, od_lq1, od_lk1, od_lq2, od_lk2, od_subln_g, od_out_w, rg_w, rg_b, re_w, re_b, ffn_w_gate, ffn_w_up, ffn_w_down):
    nbp, seq, d = x_prompt.shape
    nbs, n_lat, _ = x_sample.shape
    depth = mod_w.shape[0]
    n_prompt_tok = nbp * seq
    geom = (n_prompt_tok, n_lat)
    alpha = (2 * depth) ** 0.25
    a_width = ev_w0.shape[-1]
    a_cols = ev_ts_mu.shape[-1]
    heads = a_width // HEAD
    grid_w = 64
    assert n_prompt_tok % MOE_TILE == 0 and n_lat % MOE_TILE == 0 and seq % ROW_TILE == 0
    assert 1 + nbs <= 8 and n_lat % ROW_TILE == 0

    xs = (x_prompt.reshape(n_prompt_tok, d), x_sample.reshape(nbs * n_lat, d))
    cond = jnp.concatenate([c_ctx[None, :], c, jnp.zeros((8 - 1 - nbs, d), F32)], axis=0)
    mod = _ada_table(cond, mod_w, mod_b)

    w_gate, w_up, w_down = (w.astype(BF16) for w in (ffn_w_gate, ffn_w_up, ffn_w_down))
    new_wkv, new_k, new_v = [], [], []
    for l in range(depth):
        last = l == depth - 1
        if l % 2 == 0:
            i = l // 2
            pa, pb = _mod_matmul(xs, mod, l, ev_in_w[i], ((a_cols, False), (ev_in_w.shape[-1] - a_cols, False)), geom)
            prm = dict(ts_mu=ev_ts_mu[i][None, :], w0=ev_w0[i], w_up=ev_w_up[i], a0=ev_a0[i], a_up=ev_a_up[i],
                       g_up=ev_g_up[i], k_k=ev_k_k[i][None, :], k_a=ev_k_a[i][None, :],
                       r_k=ev_r_k[i].reshape(1, a_width), gn_g=ev_gn_g[i][None, :], gn_b=ev_gn_b[i][None, :])
            zero_state = jnp.zeros((nbp, 2, heads, HEAD, HEAD), F32)
            oa_p, st = _rwkv_mix(pa, 0, nbp, seq, zero_state, prm)
            oa_s, _ = _rwkv_mix(pa, n_prompt_tok, nbs, n_lat, state_wkv[:, i], prm)
            conv = (ev_pw_b[i], ev_dw_w[i], ev_dw_b[i], ev_cln_g[i], ev_cln_b[i])
            ob_p = _conv_module(pb, 0, nbp, seq, *conv)
            ob_s = _conv_module(pb, n_prompt_tok, nbs, n_lat, *conv)
            hs = [(oa_p, oa_s), (ob_p, ob_s)]
            out_w = ev_out_w[i]
            new_wkv.append(st)
        else:
            j = l // 2
            lam_init = 0.8 - 0.6 * math.exp(-0.3 * l)
            inner = od_in_w.shape[-1] // 3
            q, k_p, k_s, v_p, v_s = _mod_matmul(xs, mod, l, od_in_w[j],
                                                 ((inner, False), (inner, True), (inner, True)), geom)
            prm = dict(lq1=od_lq1[j], lk1=od_lk1[j], lq2=od_lq2[j], lk2=od_lk2[j], subln_g=od_subln_g[j])
            ck = cache_k[:, j].reshape(nbs, -1, inner)
            cv = cache_v[:, j].reshape(nbs, -1, inner)
            hs = [_attention(q, k_p, v_p, k_s, v_s, ck, cv, prm, lam_init, nbp, seq, nbs, n_lat, grid_w)]
            out_w = od_out_w[j]
            new_k.append(k_p.reshape(nbp, seq, inner // PAIR, PAIR))
            new_v.append(v_p.reshape(nbp, seq, inner // PAIR, PAIR))
        x = _out_proj_norm(hs, xs, mod, l, out_w, ln1_g[l], ln1_b[l], alpha, geom)
        rw = jnp.concatenate([rg_w[l], re_w[l],
                              jnp.zeros((d, ROUTER_LANES - N_GROUPS - N_EXPERTS), F32)], axis=1)
        rb = jnp.concatenate([rg_b[l], re_b[l], jnp.zeros((ROUTER_LANES - N_GROUPS - N_EXPERTS,), F32)])[None, :]
        xs = tuple(_moe_norm(x, mod, l, rw, rb, w_gate, w_up, w_down, ln2_g[l], ln2_b[l], alpha, geom,
                             split_out=last))

    y_prompt = xs[0].reshape(nbp, seq, d)
    y_sample = xs[1].reshape(nbs, n_lat, d)
    return (y_prompt, y_sample, jnp.stack(new_wkv, axis=1), jnp.stack(new_k, axis=1), jnp.stack(new_v, axis=1))
```

```python
import functools
import math

import jax
import jax.numpy as jnp
from jax import lax
from jax.experimental import pallas as pl
from jax.experimental.pallas import tpu as pltpu

F32 = jnp.float32
BF16 = jnp.bfloat16

LANES = 128
HEAD = 64
PAIR = 2 * HEAD
CHUNK = 64
INV_BLOCK = 16
ROW_TILE = 512
SUB_TILE = 256
MOE_TILE = 1024
CONV_W = 31
N_GROUPS = 4
EXPERTS_PER_GROUP = 8
N_EXPERTS = N_GROUPS * EXPERTS_PER_GROUP
ROUTER_LANES = 128
EXPERTS_PER_STEP = 4
MOE_WINDOW = 320
BF16_ROWS = 16
LN_EPS = 1e-5
GN_EPS = 64e-5
RMS_EPS = 1e-5
ROPE_BASE = 10000.0
EXP_M05 = math.exp(-0.5)
VMEM_LIMIT = 56 * 1024 * 1024


def _bdot(a, b):
    return jnp.dot(a.astype(BF16), b.astype(BF16), preferred_element_type=F32)


def _bdot_nt(a, b):
    return lax.dot_general(a.astype(BF16), b.astype(BF16), (((1,), (1,)), ((), ())),
                           preferred_element_type=F32)


def _split2(a):
    hi = a.astype(BF16)
    lo = (a - hi.astype(F32)).astype(BF16)
    return hi, lo


def _split3(a):
    hi = a.astype(BF16)
    r1 = a - hi.astype(F32)
    mid = r1.astype(BF16)
    lo = (r1 - mid.astype(F32)).astype(BF16)
    return hi, mid, lo


def _dot_f32ish(a, b):
    ah, al = _split2(a)
    bh, bl = _split2(b)
    d = lambda x, y: jnp.dot(x, y, preferred_element_type=F32)
    return d(ah, bh) + d(ah, bl) + d(al, bh)


def _dot_exact_lhs(m_bf16, a):
    h, m, l = _split3(a)
    d = lambda y: jnp.dot(m_bf16, y, preferred_element_type=F32)
    return d(h) + d(m) + d(l)


def _dot_exact_rhs(a, m_bf16):
    h, m, l = _split3(a)
    d = lambda x: jnp.dot(x, m_bf16, preferred_element_type=F32)
    return d(h) + d(m) + d(l)


def _layer_norm(z, g, b, eps):
    mu = jnp.mean(z, axis=-1, keepdims=True)
    zc = z - mu
    var = jnp.mean(zc * zc, axis=-1, keepdims=True)
    return zc * lax.rsqrt(var + eps) * g + b


def _cond_row(tile, tile_rows, n_prompt_tok, dec_seq):
    tok = tile * tile_rows
    return jnp.where(tok < n_prompt_tok, 0, 1 + (tok - n_prompt_tok) // dec_seq)


def _cast_rows(src_ref, dst_ref, rows_per_step=128):
    n = src_ref.shape[0]
    for s in range(0, n, rows_per_step):
        dst_ref[s:s + rows_per_step, :] = src_ref[s:s + rows_per_step, :].astype(dst_ref.dtype)


def _cparams(sem):
    return pltpu.CompilerParams(dimension_semantics=sem, vmem_limit_bytes=VMEM_LIMIT)


def _ada_kernel(cond_ref, w_ref, b_ref, o_ref):
    c = cond_ref[...]
    s = c * jax.nn.sigmoid(c)
    o_ref[0, 0] = _dot_f32ish(s, w_ref[0]) + b_ref[0, 0]


def _ada_table(cond8, mod_w, mod_b):
    depth, d, _ = mod_w.shape
    return pl.pallas_call(
        _ada_kernel,
        out_shape=jax.ShapeDtypeStruct((depth, 6, 8, d), F32),
        grid=(depth, 6),
        in_specs=[pl.BlockSpec((8, d), lambda l, j: (0, 0)),
                  pl.BlockSpec((1, d, d), lambda l, j: (l, 0, j)),
                  pl.BlockSpec((1, 1, 1, d), lambda l, j: (l, j, 0, 0))],
        out_specs=pl.BlockSpec((1, 1, 8, d), lambda l, j: (l, j, 0, 0)),
        compiler_params=_cparams(("arbitrary", "arbitrary")),
        name="ada_table",
    )(cond8, mod_w, mod_b.reshape(depth, 6, 1, d))


def _mod_spec(layer, which, d):
    return pl.BlockSpec((None, None, 8, d), lambda *_: (layer, which, 0, 0))


def _group_specs(tile, width, n_prompt_tiles):
    return [pl.BlockSpec((tile, width), lambda i, *_: (jnp.minimum(i, n_prompt_tiles - 1), 0)),
            pl.BlockSpec((tile, width), lambda i, *_: (jnp.maximum(i - n_prompt_tiles, 0), 0))]


def _load_group(i, n_prompt_tiles, refs, rows=slice(None)):
    if len(refs) == 1:
        return refs[0][rows, :]
    return jnp.where(i < n_prompt_tiles, refs[0][rows, :], refs[1][rows, :])


def _store_group(i, n_prompt_tiles, refs, value, rows=slice(None)):
    if len(refs) == 1:
        refs[0][rows, :] = value
        return

    @pl.when(i < n_prompt_tiles)
    def _():
        refs[0][rows, :] = value

    @pl.when(i >= n_prompt_tiles)
    def _():
        refs[1][rows, :] = value


def _modmm_kernel(*refs, n_x, outs, geom):
    x_refs = refs[:n_x]
    sh_ref, sc_ref, w_ref = refs[n_x:n_x + 3]
    o_refs, wb = refs[n_x + 3:-1], refs[-1]
    i = pl.program_id(0)
    n_pt = geom[0] // ROW_TILE

    @pl.when(i == 0)
    def _():
        _cast_rows(w_ref, wb)

    r = _cond_row(i, ROW_TILE, *geom)
    h = _load_group(i, n_pt, x_refs) * (1.0 + sc_ref[pl.ds(r, 1), :]) + sh_ref[pl.ds(r, 1), :]
    hb = h.astype(BF16)
    off = 0
    k = 0
    for n, split in outs:
        y = jnp.dot(hb, wb[:, off:off + n], preferred_element_type=F32)
        _store_group(i, n_pt, o_refs[k:k + 1 + split], y)
        off += n
        k += 1 + split


def _mod_matmul(xs, mod, layer, w, outs, geom):
    d, n_out = w.shape
    n_prompt_tok = geom[0]
    n_tok = sum(x.shape[0] for x in xs)
    n_pt = n_prompt_tok // ROW_TILE
    whole = lambda n: [pl.BlockSpec((ROW_TILE, n), lambda i: (i, 0))]
    x_specs = whole(d) if len(xs) == 1 else _group_specs(ROW_TILE, d, n_pt)
    out_shape, out_specs = [], []
    for n, split in outs:
        if split:
            out_shape += [jax.ShapeDtypeStruct((n_prompt_tok, n), F32),
                          jax.ShapeDtypeStruct((n_tok - n_prompt_tok, n), F32)]
            out_specs += _group_specs(ROW_TILE, n, n_pt)
        else:
            out_shape += [jax.ShapeDtypeStruct((n_tok, n), F32)]
            out_specs += whole(n)
    return pl.pallas_call(
        functools.partial(_modmm_kernel, n_x=len(xs), outs=outs, geom=geom),
        out_shape=out_shape,
        grid=(n_tok // ROW_TILE,),
        in_specs=x_specs + [_mod_spec(layer, 0, d), _mod_spec(layer, 1, d),
                            pl.BlockSpec((d, n_out), lambda i: (0, 0), pipeline_mode=pl.Buffered(1))],
        out_specs=out_specs,
        scratch_shapes=[pltpu.VMEM((d, n_out), BF16)],
        compiler_params=_cparams(("arbitrary",)),
        name=f"mod_matmul_l{layer}",
    )(*xs, mod, mod, w)


def _outproj_kernel(*refs, arity, alpha, geom):
    groups, k = [], 0
    for n in arity:
        groups.append(refs[k:k + n])
        k += n
    gate_ref, w_ref, g_ref, b_ref, o_ref, wb = refs[k:]
    h_groups, x_group = groups[:-1], groups[-1]
    i = pl.program_id(0)
    n_pt = geom[0] // ROW_TILE

    @pl.when(i == 0)
    def _():
        _cast_rows(w_ref, wb)

    r = _cond_row(i, ROW_TILE, *geom)
    y = None
    off = 0
    for h_refs in h_groups:
        n = h_refs[0].shape[1]
        h = _load_group(i, n_pt, h_refs)
        part = jnp.dot(h.astype(BF16), wb[off:off + n, :], preferred_element_type=F32)
        y = part if y is None else y + part
        off += n
    z = alpha * _load_group(i, n_pt, x_group) + gate_ref[pl.ds(r, 1), :] * y
    o_ref[...] = _layer_norm(z, g_ref[...], b_ref[...], LN_EPS)


def _out_proj_norm(hs, xs, mod, layer, w, ln_g, ln_b, alpha, geom):
    d = w.shape[1]
    n_tok = sum(x.shape[0] for x in xs)
    n_pt = geom[0] // ROW_TILE
    specs = []
    for grp in list(hs) + [xs]:
        n = grp[0].shape[1]
        specs += [pl.BlockSpec((ROW_TILE, n), lambda i: (i, 0))] if len(grp) == 1 else _group_specs(ROW_TILE, n, n_pt)
    return pl.pallas_call(
        functools.partial(_outproj_kernel, arity=tuple(len(g) for g in list(hs) + [xs]), alpha=alpha, geom=geom),
        out_shape=jax.ShapeDtypeStruct((n_tok, d), F32),
        grid=(n_tok // ROW_TILE,),
        in_specs=specs + [
            _mod_spec(layer, 2, d),
            pl.BlockSpec(w.shape, lambda i: (0, 0), pipeline_mode=pl.Buffered(1)),
            pl.BlockSpec((1, d), lambda i: (0, 0)),
            pl.BlockSpec((1, d), lambda i: (0, 0))],
        out_specs=pl.BlockSpec((ROW_TILE, d), lambda i: (i, 0)),
        scratch_shapes=[pltpu.VMEM(w.shape, BF16)],
        compiler_params=_cparams(("arbitrary",)),
        name=f"out_proj_norm_l{layer}",
    )(*[a for g in hs for a in g], *xs, mod, w, ln_g.reshape(1, d), ln_b.reshape(1, d))


def _route(logits):
    neg = jnp.float32(-3e38)
    big = jnp.int32(1 << 20)
    lane = lax.broadcasted_iota(jnp.int32, logits.shape, 1)
    is_g = lane < N_GROUPS
    gl = jnp.where(is_g, logits, neg)
    gmax = jnp.max(gl, axis=-1, keepdims=True)
    gidx = jnp.min(jnp.where(gl == gmax, lane, big), axis=-1, keepdims=True)
    gsum = jnp.sum(jnp.where(is_g, jnp.exp(gl - gmax), 0.0), axis=-1, keepdims=True)
    g_p = 1.0 / gsum
    lo = N_GROUPS + gidx * EXPERTS_PER_GROUP
    in_grp = (lane >= lo) & (lane < lo + EXPERTS_PER_GROUP)
    el = jnp.where(in_grp, logits, neg)
    v1 = jnp.max(el, axis=-1, keepdims=True)
    i1 = jnp.min(jnp.where(el == v1, lane, big), axis=-1, keepdims=True)
    el2 = jnp.where(lane == i1, neg, el)
    v2 = jnp.max(el2, axis=-1, keepdims=True)
    i2 = jnp.min(jnp.where(el2 == v2, lane, big), axis=-1, keepdims=True)
    e21 = jnp.exp(v2 - v1)
    den = 1.0 + e21
    w1 = g_p / den
    w2 = g_p * e21 / den
    return jnp.where(lane == i1, w1, 0.0) + jnp.where(lane == i2, w2, 0.0), gidx


def _moe_kernel(x_ref, sh_ref, sc_ref, gate_ref, rw_ref, rb_ref, wg_ref, wu_ref, wd_ref, g_ref, b_ref,
                *rest, alpha, geom):
    o_refs = rest[:-6]
    h_scr, gates_scr, acc_scr, perm_scr, permt_scr, bounds = rest[-6:]
    i = pl.program_id(0)
    e = pl.program_id(1)
    r = _cond_row(i, MOE_TILE, *geom)
    n_pt = geom[0] // MOE_TILE
    sub = SUB_TILE
    n_sub = MOE_TILE // sub

    @pl.when(e == 0)
    def _():
        lane = lax.broadcasted_iota(jnp.int32, (sub, ROUTER_LANES), 1)
        ri = lax.broadcasted_iota(jnp.int32, (sub, sub), 0)
        ci = lax.broadcasted_iota(jnp.int32, (sub, sub), 1)
        tril = jnp.where(ci <= ri, 1.0, 0.0).astype(BF16)
        running = jnp.zeros((1, ROUTER_LANES), F32)
        onehots, cums = [], []
        for s in range(n_sub):
            rows = slice(s * sub, (s + 1) * sub)
            h = x_ref[rows, :] * (1.0 + sc_ref[pl.ds(r, 1), :]) + sh_ref[pl.ds(r, 1), :]
            acc_scr[rows, :] = h
            logits = _dot_f32ish(h, rw_ref[...]) + rb_ref[...]
            gates, gidx = _route(logits)
            gates_scr[rows, :] = gates
            onehot = jnp.where(lane == gidx, 1.0, 0.0)
            cum = jnp.dot(tril, onehot.astype(BF16), preferred_element_type=F32) + running
            running = cum[sub - 1:sub, :]
            onehots.append(onehot)
            cums.append(cum)
        gi = lax.broadcasted_iota(jnp.int32, (ROUTER_LANES, ROUTER_LANES), 0)
        gj = lax.broadcasted_iota(jnp.int32, (ROUTER_LANES, ROUTER_LANES), 1)
        before = jnp.where((gi < gj) & (gi < N_GROUPS), 1.0, 0.0).astype(BF16)
        starts = _dot_exact_rhs(jnp.broadcast_to(running, (8, ROUTER_LANES)), before)[0:1, :]
        for g in range(N_GROUPS + 1):
            bounds[g] = starts[0, g].astype(jnp.int32)
        dest_cols = [jnp.sum(onehots[s] * (starts + cums[s] - 1.0), axis=-1, keepdims=True) for s in range(n_sub)]
        dest_row = jnp.concatenate(
            [jnp.broadcast_to(dc, (sub, ROUTER_LANES)).T[0:1, :] for dc in dest_cols], axis=1)
        col_id = lax.broadcasted_iota(jnp.int32, (sub, MOE_TILE), 1).astype(F32)
        row_id = lax.broadcasted_iota(jnp.int32, (sub, MOE_TILE), 0).astype(F32)
        for s in range(n_sub):
            rows = slice(s * sub, (s + 1) * sub)
            perm_scr[rows, :] = jnp.where(row_id + float(s * sub) == dest_row, 1.0, 0.0).astype(BF16)
            permt_scr[rows, :] = jnp.where(dest_cols[s] == col_id, 1.0, 0.0).astype(BF16)
        hb = acc_scr[...].astype(BF16)
        g_hi, g_mid, g_lo = _split3(gates_scr[...])
        for s in range(n_sub):
            rows = slice(s * sub, (s + 1) * sub)
            pm = perm_scr[rows, :]
            h_scr[rows, :] = jnp.dot(pm, hb, preferred_element_type=F32).astype(BF16)
        for s in range(n_sub):
            rows = slice(s * sub, (s + 1) * sub)
            pm = perm_scr[rows, :]
            d = lambda y: jnp.dot(pm, y, preferred_element_type=F32)
            gates_scr[rows, :] = d(g_hi) + d(g_mid) + d(g_lo)
            acc_scr[rows, :] = jnp.zeros((sub, acc_scr.shape[1]), F32)

    grp = e // (EXPERTS_PER_GROUP // EXPERTS_PER_STEP)
    first, end = bounds[grp], bounds[grp + 1]
    first_al = (first // BF16_ROWS) * BF16_ROWS
    lane = lax.broadcasted_iota(jnp.int32, (MOE_WINDOW, ROUTER_LANES), 1)
    row = lax.broadcasted_iota(jnp.int32, (MOE_WINDOW, 1), 0)

    def window(w, carry):
        lo = first_al + w * MOE_WINDOW
        start = pl.multiple_of(jnp.minimum(lo, MOE_TILE - MOE_WINDOW), BF16_ROWS)
        rows = pl.ds(start, MOE_WINDOW)
        hb = h_scr[rows, :]
        gates = jnp.where(row + start >= lo, gates_scr[rows, :], 0.0)
        out = acc_scr[rows, :]
        for j in range(EXPERTS_PER_STEP):
            lane_j = N_GROUPS + e * EXPERTS_PER_STEP + j
            ge = jnp.sum(jnp.where(lane == lane_j, gates, 0.0), axis=-1, keepdims=True)
            a = jnp.dot(hb, wg_ref[j], preferred_element_type=F32)
            u = jnp.dot(hb, wu_ref[j], preferred_element_type=F32)
            hid = (a * jax.nn.sigmoid(a)) * u * ge
            out = out + jnp.dot(hid.astype(BF16), wd_ref[j], preferred_element_type=F32)
        acc_scr[rows, :] = out
        return carry

    lax.fori_loop(0, (end - first_al + MOE_WINDOW - 1) // MOE_WINDOW, window, 0)

    @pl.when(e == pl.num_programs(1) - 1)
    def _():
        a_hi, a_lo = _split2(acc_scr[...])
        for s in range(n_sub):
            rows = slice(s * sub, (s + 1) * sub)
            pt = permt_scr[rows, :]
            moe = jnp.dot(pt, a_hi, preferred_element_type=F32) + jnp.dot(pt, a_lo, preferred_element_type=F32)
            z = alpha * x_ref[rows, :] + gate_ref[pl.ds(r, 1), :] * moe
            _store_group(i, n_pt, o_refs, _layer_norm(z, g_ref[...], b_ref[...], LN_EPS), rows)


def _moe_norm(x, mod, layer, rw, rb, w_gate, w_up, w_down, ln_g, ln_b, alpha, geom, split_out):
    n_tok, d = x.shape
    ff = w_gate.shape[-1]
    assert w_gate.dtype == BF16 and w_up.dtype == BF16 and w_down.dtype == BF16
    const = lambda i, e: (0, 0)
    if split_out:
        out_shape = [jax.ShapeDtypeStruct((geom[0], d), F32), jax.ShapeDtypeStruct((n_tok - geom[0], d), F32)]
        out_specs = _group_specs(MOE_TILE, d, geom[0] // MOE_TILE)
    else:
        out_shape = [jax.ShapeDtypeStruct((n_tok, d), F32)]
        out_specs = [pl.BlockSpec((MOE_TILE, d), lambda i, e: (i, 0))]
    return pl.pallas_call(
        functools.partial(_moe_kernel, alpha=alpha, geom=geom),
        out_shape=out_shape,
        grid=(n_tok // MOE_TILE, N_EXPERTS // EXPERTS_PER_STEP),
        in_specs=[pl.BlockSpec((MOE_TILE, d), lambda i, e: (i, 0)),
                  _mod_spec(layer, 3, d), _mod_spec(layer, 4, d), _mod_spec(layer, 5, d),
                  pl.BlockSpec((d, ROUTER_LANES), const),
                  pl.BlockSpec((1, ROUTER_LANES), const),
                  pl.BlockSpec((None, EXPERTS_PER_STEP, d, ff), lambda i, e: (layer, e, 0, 0)),
                  pl.BlockSpec((None, EXPERTS_PER_STEP, d, ff), lambda i, e: (layer, e, 0, 0)),
                  pl.BlockSpec((None, EXPERTS_PER_STEP, ff, d), lambda i, e: (layer, e, 0, 0)),
                  pl.BlockSpec((1, d), const),
                  pl.BlockSpec((1, d), const)],
        out_specs=out_specs,
        scratch_shapes=[pltpu.VMEM((MOE_TILE, d), BF16),
                        pltpu.VMEM((MOE_TILE, ROUTER_LANES), F32),
                        pltpu.VMEM((MOE_TILE, d), F32),
                        pltpu.VMEM((MOE_TILE, MOE_TILE), BF16), pltpu.VMEM((MOE_TILE, MOE_TILE), BF16),
                        pltpu.SMEM((8,), jnp.int32)],
        compiler_params=_cparams(("arbitrary", "arbitrary")),
        name=f"moe_norm_l{layer}",
    )(x, mod, mod, mod, rw, rb, w_gate, w_up, w_down, ln_g.reshape(1, d), ln_b.reshape(1, d))


def _rwkv_kernel(r_ref, k_ref, v_ref, lora_ref, mur_ref, muk_ref, muv_ref, mul_ref,
                 w0_ref, wup_ref, a0_ref, aup_ref, gup_ref, kk_ref, ka_ref, rk_ref, gng_ref, gnb_ref, s0_ref,
                 out_ref, sfin_ref,
                 lw_s, kz_s, bz_s, rr_s, vv_s, kn_s,
                 abar_s, rt_s, wt_s, vst_s, p_s, tm_s, mb_s, mk_s, nkb_s, txr_s, tx_s, gl_s,
                 g_s, h_s, q_s, y0_s, y_s, st_s, *, seq_len, unroll):
    t_len = seq_len
    n_chunks = t_len // CHUNK
    c2 = 2 * CHUNK

    row = lax.broadcasted_iota(jnp.int32, (t_len, 1), 0)

    def tshift(x, mu):
        prev = jnp.where(row == 0, 0.0, pltpu.roll(x, 1, 0))
        nxt = jnp.where(row == t_len - 1, 0.0, pltpu.roll(x, t_len - 1, 0))
        return x + mu * (0.5 * (prev + nxt) - x)

    r = tshift(r_ref[...], mur_ref[...])
    k = tshift(k_ref[...], muk_ref[...])
    v = tshift(v_ref[...], muv_ref[...])
    lo = tshift(lora_ref[...], mul_ref[...])
    wd = jnp.tanh(lo[:, 0:PAIR])
    ad = lo[:, PAIR:2 * PAIR]
    gate = _bdot(jax.nn.sigmoid(lo[:, 2 * PAIR:3 * PAIR]), gup_ref[...])

    ri = lax.broadcasted_iota(jnp.int32, (PAIR, PAIR), 0)
    ci = lax.broadcasted_iota(jnp.int32, (PAIR, PAIR), 1)
    same_head = (ri // HEAD) == (ci // HEAD)
    ones_bd = jnp.where(same_head, 1.0, 0.0).astype(BF16)
    eye = ri == ci

    kk = k * kk_ref[...]
    ssq = _dot_exact_rhs(kk * kk, ones_bd)
    kk = kk * lax.rsqrt(jnp.maximum(ssq, 1e-24))
    wup2 = wup_ref[...].reshape(PAIR, PAIR)
    aup2 = aup_ref[...].reshape(PAIR, PAIR)
    bonus = jnp.zeros((t_len, PAIR), F32)
    for z in range(2):
        zrows = (ri // HEAD) == z
        w_pre = w0_ref[z:z + 1, :] + _bdot(wd, jnp.where(zrows, wup2, 0.0))
        lw_s[z] = -EXP_M05 * jax.nn.sigmoid(w_pre)
        a = jax.nn.sigmoid(a0_ref[z:z + 1, :] + _bdot(ad, jnp.where(zrows, aup2, 0.0)))
        kz = k * (1.0 + (a - 1.0) * ka_ref[...])
        kz_s[z] = kz
        bz_s[z] = kk * a
        bonus = bonus + _dot_exact_rhs(r * kz * rk_ref[...], ones_bd) * v
    rr_s[...] = r
    vv_s[...] = v
    kn_s[...] = kk

    rt_i = ri % CHUNK
    cs_i = ci % CHUNK
    strict = (cs_i < rt_i, cs_i > rt_i)
    incl = (cs_i <= rt_i, cs_i >= rt_i)
    t64 = lax.broadcasted_iota(jnp.int32, (CHUNK, CHUNK), 0)
    s64 = lax.broadcasted_iota(jnp.int32, (CHUNK, CHUNK), 1)
    tri = (jnp.where(s64 <= t64, 1.0, 0.0).astype(BF16), jnp.where(s64 >= t64, 1.0, 0.0).astype(BF16))
    lane_head0 = lax.broadcasted_iota(jnp.int32, (CHUNK, PAIR), 1) < HEAD
    eye_f = jnp.where(eye, 1.0, 0.0)
    same16 = (ri // INV_BLOCK) == (ci // INV_BLOCK)

    def stack(x):
        return jnp.concatenate([jnp.where(lane_head0, x, 0.0), jnp.where(lane_head0, 0.0, x)], axis=0)

    def fold(x):
        return x[:CHUNK] + x[CHUNK:]

    def chunk_rows(c):
        return pl.ds(pl.multiple_of(c * CHUNK, CHUNK), CHUNK)

    def each_item(body):
        def both_directions(c, carry):
            body(0, c)
            body(1, c)
            return carry

        lax.fori_loop(0, n_chunks, both_directions, 0, unroll=unroll)

    def stage_operands(z, c):
        rows = chunk_rows(c)
        r_c, kn_c = rr_s[rows, :], kn_s[rows, :]
        lw_c, kz_c, bz_c = lw_s[z, rows, :], kz_s[z, rows, :], bz_s[z, rows, :]
        lg = _dot_exact_lhs(tri[z], lw_c)
        total = lg[CHUNK - 1:CHUNK, :] if z == 0 else lg[0:1, :]
        e_neg = jnp.exp(-lg)
        e_rem = jnp.exp(total - lg)
        abar = stack(kn_c * jnp.exp(lg - lw_c))
        rt = stack(r_c * jnp.exp(lg))
        big = _bdot_nt(jnp.concatenate([abar, rt], axis=0),
                       jnp.concatenate([stack(bz_c * e_neg), stack(kz_c * e_neg)], axis=0))
        mb = jnp.where(strict[z], big[:c2, :c2], 0.0)
        abar_s[z, c] = abar.astype(BF16)
        rt_s[z, c] = rt
        wt_s[z, c] = jnp.concatenate([stack(kz_c * e_rem), -stack(bz_c * e_rem)], axis=0).T.astype(BF16)
        gl_s[z, c] = jnp.exp(total)
        m0 = jnp.where(same16, mb, 0.0)
        tm_s[z, c] = eye_f - m0
        p_s[z, c] = m0.astype(BF16)
        mb_s[z, c] = mb.astype(BF16)
        mk_s[z, c] = jnp.where(strict[z], big[:c2, c2:], 0.0).astype(BF16)
        nkb_s[z, c] = jnp.concatenate([jnp.where(incl[z], big[c2:, c2:], 0.0),
                                       -jnp.where(incl[z], big[c2:, :c2], 0.0)], axis=1).astype(BF16)

    def stage_values(c, carry):
        vst_s[c] = stack(vv_s[chunk_rows(c), :]).astype(BF16)
        return carry

    lax.fori_loop(0, n_chunks, stage_values, 0, unroll=unroll)
    each_item(stage_operands)

    def stage_square(z, c):
        m0 = p_s[z, c]
        p_s[z, c] = jnp.dot(m0, m0, preferred_element_type=F32).astype(BF16)
        mkv = jnp.dot(mk_s[z, c], vst_s[c], preferred_element_type=F32)
        txr_s[z, c] = jnp.concatenate([abar_s[z, c], mkv.astype(BF16)], axis=1)

    each_item(stage_square)

    n_round = INV_BLOCK.bit_length() - 2
    for j in range(n_round):
        def stage_double(z, c, last=(j == n_round - 1)):
            tm, p = tm_s[z, c], p_s[z, c]
            tm_s[z, c] = tm + jnp.dot(tm.astype(BF16), p, preferred_element_type=F32)
            if not last:
                p_s[z, c] = jnp.dot(p, p, preferred_element_type=F32).astype(BF16)

        each_item(stage_double)

    size = INV_BLOCK
    while size < CHUNK:
        off_diag = ((ri // (2 * size)) == (ci // (2 * size))) & ((ri // size) != (ci // size))

        def stage_merge_rhs(z, c, off_diag=off_diag):
            cross = jnp.where(off_diag, mb_s[z, c], jnp.zeros((), BF16))
            p_s[z, c] = jnp.dot(cross, tm_s[z, c].astype(BF16), preferred_element_type=F32).astype(BF16)

        def stage_merge(z, c):
            tm = tm_s[z, c]
            tm_s[z, c] = tm - jnp.dot(tm.astype(BF16), p_s[z, c], preferred_element_type=F32)

        each_item(stage_merge_rhs)
        each_item(stage_merge)
        size *= 2

    def stage_apply(z, c):
        tx_s[z, c] = jnp.dot(tm_s[z, c].astype(BF16), txr_s[z, c], preferred_element_type=F32)

    each_item(stage_apply)

    def stage_maps(z, c):
        tx = tx_s[z, c].astype(BF16)
        v_st = vst_s[c]
        wt = wt_s[z, c]
        from_tx = jnp.dot(wt[:, c2:], tx, preferred_element_type=F32)
        h_s[z, c] = jnp.dot(wt[:, :c2], v_st, preferred_element_type=F32) + from_tx[:, PAIR:]
        g_s[z, c] = jnp.where(eye, gl_s[z, c], 0.0) + from_tx[:, :PAIR]
        nkb = nkb_s[z, c]
        from_tx = jnp.dot(nkb[:, c2:], tx, preferred_element_type=F32)
        q_s[z, c] = fold(rt_s[z, c] + from_tx[:, :PAIR])
        y0_s[z, c] = fold(jnp.dot(nkb[:, :c2], v_st, preferred_element_type=F32) + from_tx[:, PAIR:])

    each_item(stage_maps)

    zero64 = jnp.zeros((HEAD, HEAD), F32)
    for z in range(2):
        a = jnp.concatenate([jnp.concatenate([s0_ref[z, 0], zero64], axis=1),
                             jnp.concatenate([zero64, s0_ref[z, 1]], axis=1)], axis=0)
        st_s[z] = a.T

    def propagate(i, carry):
        for z in range(2):
            c = i if z == 0 else n_chunks - 1 - i
            s = st_s[z].astype(BF16)
            y_s[z, chunk_rows(c), :] = _bdot(q_s[z, c], s) + y0_s[z, c]
            st_s[z] = _bdot(g_s[z, c], s) + h_s[z, c]
        return carry

    lax.fori_loop(0, n_chunks, propagate, 0)
    for z in range(2):
        a = st_s[z].T
        sfin_ref[z, 0] = a[:HEAD, :HEAD]
        sfin_ref[z, 1] = a[HEAD:, HEAD:]

    y = y_s[0] + y_s[1]
    mu = _dot_exact_rhs(y, ones_bd) * (1.0 / HEAD)
    yc = y - mu
    var = _dot_exact_rhs(yc * yc, ones_bd) * (1.0 / HEAD)
    yn = yc * lax.rsqrt(var + GN_EPS) * gng_ref[...] + gnb_ref[...]
    out_ref[...] = (yn + bonus) * gate


def _rwkv_mix(pa, first_tok, n_seq, seq_len, s0, p):
    a_width = p["w0"].shape[-1]
    n_pairs = a_width // PAIR
    heads = a_width // HEAD
    t0 = first_tok // seq_len
    lora_w = 3 * PAIR
    lora_blk = 3 * a_width // lora_w
    n_chunks = seq_len // CHUNK
    col = lambda off: (lambda s, q: (t0 + s, off + q))
    vec = lambda off: (lambda s, q: (0, off + q))
    in_specs = [
        pl.BlockSpec((seq_len, PAIR), col(0)),
        pl.BlockSpec((seq_len, PAIR), col(n_pairs)),
        pl.BlockSpec((seq_len, PAIR), col(2 * n_pairs)),
        pl.BlockSpec((seq_len, lora_w), lambda s, q: (t0 + s, lora_blk)),
        pl.BlockSpec((1, PAIR), vec(0)),
        pl.BlockSpec((1, PAIR), vec(n_pairs)),
        pl.BlockSpec((1, PAIR), vec(2 * n_pairs)),
        pl.BlockSpec((1, lora_w), lambda s, q: (0, lora_blk)),
        pl.BlockSpec((2, PAIR), vec(0)),
        pl.BlockSpec((2, HEAD, PAIR), lambda s, q: (0, 0, q)),
        pl.BlockSpec((2, PAIR), vec(0)),
        pl.BlockSpec((2, HEAD, PAIR), lambda s, q: (0, 0, q)),
        pl.BlockSpec((PAIR, PAIR), vec(0)),
        pl.BlockSpec((1, PAIR), vec(0)),
        pl.BlockSpec((1, PAIR), vec(0)),
        pl.BlockSpec((1, PAIR), vec(0)),
        pl.BlockSpec((1, PAIR), vec(0)),
        pl.BlockSpec((1, PAIR), vec(0)),
        pl.BlockSpec((None, 2, 2, HEAD, HEAD), lambda s, q: (s, 0, q, 0, 0)),
    ]
    seq2 = lambda: pltpu.VMEM((2, seq_len, PAIR), F32)
    seq1 = lambda: pltpu.VMEM((seq_len, PAIR), F32)
    item = lambda rows, cols, dt: pltpu.VMEM((2, n_chunks, rows, cols), dt)
    out, s_fin = pl.pallas_call(
        functools.partial(_rwkv_kernel, seq_len=seq_len, unroll=min(n_chunks, 4)),
        out_shape=[jax.ShapeDtypeStruct((n_seq * seq_len, a_width), F32),
                   jax.ShapeDtypeStruct((n_seq, 2, heads, HEAD, HEAD), F32)],
        grid=(n_seq, n_pairs),
        in_specs=in_specs,
        out_specs=[pl.BlockSpec((seq_len, PAIR), lambda s, q: (s, q)),
                   pl.BlockSpec((None, 2, 2, HEAD, HEAD), lambda s, q: (s, 0, q, 0, 0))],
        scratch_shapes=[seq2(), seq2(), seq2(), seq1(), seq1(), seq1(),
                        item(PAIR, PAIR, BF16),
                        item(PAIR, PAIR, F32),
                        item(PAIR, 2 * PAIR, BF16),
                        pltpu.VMEM((n_chunks, PAIR, PAIR), BF16),
                        item(PAIR, PAIR, BF16),
                        item(PAIR, PAIR, F32),
                        item(PAIR, PAIR, BF16),
                        item(PAIR, PAIR, BF16),
                        item(PAIR, 2 * PAIR, BF16),
                        item(PAIR, 2 * PAIR, BF16),
                        item(PAIR, 2 * PAIR, F32),
                        item(1, PAIR, F32),
                        item(PAIR, PAIR, F32), item(PAIR, PAIR, F32),
                        item(CHUNK, PAIR, F32), item(CHUNK, PAIR, F32),
                        seq2(), pltpu.VMEM((2, PAIR, PAIR), F32)],
        compiler_params=_cparams(("arbitrary", "arbitrary")),
        name=f"rwkv_mix_t{seq_len}",
    )(pa, pa, pa, pa, p["ts_mu"], p["ts_mu"], p["ts_mu"], p["ts_mu"],
      p["w0"], p["w_up"], p["a0"], p["a_up"], p["g_up"], p["k_k"], p["k_a"], p["r_k"], p["gn_g"], p["gn_b"], s0)
    return out, s_fin


def _conv_kernel(pb_ref, pwb_ref, dw_ref, dwb_ref, g_ref, b_ref, o_ref, pad_s, *, seq_len):
    width = o_ref.shape[1]
    halo = 16
    sub = 64
    pbv = pb_ref[...] + pwb_ref[...]
    glu = pbv[:, :width] * jax.nn.sigmoid(pbv[:, width:])
    pad_s[0:halo, :] = jnp.zeros((halo, width), F32)
    pad_s[halo:halo + seq_len, :] = glu
    pad_s[halo + seq_len:, :] = jnp.zeros((halo, width), F32)
    first = halo - CONV_W // 2
    for c in range(seq_len // sub):
        acc = jnp.zeros((sub, width), F32)
        for d in range(CONV_W):
            start = c * sub + first + d
            acc = acc + pad_s[start:start + sub, :] * dw_ref[d:d + 1, :]
        y = _layer_norm(acc + dwb_ref[...], g_ref[...], b_ref[...], LN_EPS)
        o_ref[c * sub:(c + 1) * sub, :] = y * jax.nn.sigmoid(y)


def _conv_module(pb, first_tok, n_seq, seq_len, pw_b, dw_w, dw_b, cln_g, cln_b):
    width = dw_w.shape[1]
    t0 = first_tok // seq_len
    const = lambda s: (0, 0)
    return pl.pallas_call(
        functools.partial(_conv_kernel, seq_len=seq_len),
        out_shape=jax.ShapeDtypeStruct((n_seq * seq_len, width), F32),
        grid=(n_seq,),
        in_specs=[pl.BlockSpec((seq_len, 2 * width), lambda s: (t0 + s, 0)),
                  pl.BlockSpec((1, 2 * width), const),
                  pl.BlockSpec((CONV_W, width), const),
                  pl.BlockSpec((1, width), const),
                  pl.BlockSpec((1, width), const),
                  pl.BlockSpec((1, width), const)],
        out_specs=pl.BlockSpec((seq_len, width), lambda s: (s, 0)),
        scratch_shapes=[pltpu.VMEM((seq_len + 32, width), F32)],
        compiler_params=_cparams(("arbitrary",)),
        name=f"conv_module_t{seq_len}",
    )(pb, pw_b.reshape(1, -1), dw_w, dw_b.reshape(1, -1), cln_g.reshape(1, -1), cln_b.reshape(1, -1))


def _diff_lambda(lq1, lk1, lq2, lk2, lam_init):
    dot = lambda a, b: jnp.exp(jnp.sum(a[...] * b[...], axis=-1, keepdims=True))
    return dot(lq1, lk1) - dot(lq2, lk2) + lam_init


def _diff_attend(q, keys, vals, lam, lam_init, subln):
    nq = q.shape[0]
    lane_first = lax.broadcasted_iota(jnp.int32, q.shape, 1) < HEAD
    q = q * (HEAD ** -0.5)
    q_st = jnp.concatenate([jnp.where(lane_first, q, 0.0), jnp.where(lane_first, 0.0, q)], axis=0)
    s = _bdot_nt(q_st, keys)
    s = s - jnp.max(s, axis=-1, keepdims=True)
    pexp = jnp.exp(s)
    ov = _bdot(pexp, vals) / jnp.sum(pexp, axis=-1, keepdims=True)
    o = ov[:nq] - lam * ov[nq:]
    o = o * lax.rsqrt(jnp.mean(o * o, axis=-1, keepdims=True) + RMS_EPS)
    return o * subln * (1.0 - lam_init)


def _head_cols(h):
    return slice(h * PAIR, (h + 1) * PAIR)


def _attn_prompt_kernel(q_ref, k_ref, v_ref, lq1, lk1, lq2, lk2, sub_ref, o_ref, *, lam_init):
    lam = _diff_lambda(lq1, lk1, lq2, lk2, lam_init)
    for h in range(q_ref.shape[1] // PAIR):
        c = _head_cols(h)
        o_ref[:, c] = _diff_attend(q_ref[:, c], k_ref[:, c], v_ref[:, c], lam, lam_init, sub_ref[...])


def _rope(x, cos, sin_signed):
    lane = lax.broadcasted_iota(jnp.int32, x.shape, 1)
    quarter = HEAD // 4
    partner = jnp.where((lane // quarter) % 2 == 0,
                        pltpu.roll(x, LANES - quarter, 1), pltpu.roll(x, quarter, 1))
    return x * cos + partner * sin_signed


def _attn_sample_kernel(q_ref, k_ref, v_ref, ck_ref, cv_ref, cq_ref, sq_ref, cosk_ref, sink_ref,
                        lq1, lk1, lq2, lk2, sub_ref, o_ref, *, lam_init):
    lam = _diff_lambda(lq1, lk1, lq2, lk2, lam_init)
    for h in range(q_ref.shape[1] // PAIR):
        c = _head_cols(h)
        q = _rope(q_ref[:, c], cq_ref[...], sq_ref[...])
        kl = _rope(k_ref[:, c], cosk_ref[...], sink_ref[...])
        keys = jnp.concatenate([ck_ref[:, c], kl], axis=0)
        vals = jnp.concatenate([cv_ref[:, c], v_ref[:, c]], axis=0)
        o_ref[:, c] = _diff_attend(q, keys, vals, lam, lam_init, sub_ref[...])


def _rope_tables(n_lat, grid_w):
    quarter = HEAD // 4
    inv = ROPE_BASE ** (-jnp.arange(quarter, dtype=F32) / quarter)
    t = jnp.arange(n_lat)
    row_ang = (t // grid_w).astype(F32)[:, None] * inv
    col_ang = (t % grid_w).astype(F32)[:, None] * inv
    ang = jnp.concatenate([row_ang, row_ang, col_ang, col_ang] * 2, axis=-1)
    sign = jnp.tile(jnp.concatenate([-jnp.ones(quarter, F32), jnp.ones(quarter, F32)]), 2 * HEAD // (2 * quarter))
    return jnp.cos(ang), jnp.sin(ang) * sign


def _attention(q, k_p, v_p, k_s, v_s, cache_k, cache_v, p, lam_init, n_prompt, seq, n_lat_req, n_lat, grid_w):
    inner = q.shape[1]
    vec = lambda n: pl.BlockSpec((1, n), lambda *_: (0, 0))
    lam_args = [p[n].reshape(1, -1) for n in ("lq1", "lk1", "lq2", "lk2")] + [p["subln_g"].reshape(1, -1)]
    lam_specs = [vec(HEAD)] * 4 + [vec(PAIR)]
    blk = pl.BlockSpec((seq, inner), lambda b: (b, 0))
    o_prompt = pl.pallas_call(
        functools.partial(_attn_prompt_kernel, lam_init=lam_init),
        out_shape=jax.ShapeDtypeStruct((n_prompt * seq, inner), F32),
        grid=(n_prompt,),
        in_specs=[blk, blk, blk] + lam_specs,
        out_specs=blk,
        compiler_params=_cparams(("arbitrary",)),
        name="diff_attn_prompt",
    )(q, k_p, v_p, *lam_args)

    past = cache_k.shape[1]
    qt = SUB_TILE
    n_qt = n_lat // qt
    q0 = n_prompt * seq // qt
    cos, sin = _rope_tables(n_lat, grid_w)
    lat = pl.BlockSpec((n_lat, inner), lambda b, t: (b, 0))
    ctx = pl.BlockSpec((None, past, inner), lambda b, t: (b, 0, 0))
    o_sample = pl.pallas_call(
        functools.partial(_attn_sample_kernel, lam_init=lam_init),
        out_shape=jax.ShapeDtypeStruct((n_lat_req * n_lat, inner), F32),
        grid=(n_lat_req, n_qt),
        in_specs=[pl.BlockSpec((qt, inner), lambda b, t: (q0 + b * n_qt + t, 0)),
                  lat, lat, ctx, ctx,
                  pl.BlockSpec((qt, PAIR), lambda b, t: (t, 0)),
                  pl.BlockSpec((qt, PAIR), lambda b, t: (t, 0)),
                  pl.BlockSpec((n_lat, PAIR), lambda b, t: (0, 0)),
                  pl.BlockSpec((n_lat, PAIR), lambda b, t: (0, 0))] + lam_specs,
        out_specs=pl.BlockSpec((qt, inner), lambda b, t: (b * n_qt + t, 0)),
        compiler_params=_cparams(("arbitrary", "arbitrary")),
        name="diff_attn_sample",
    )(q, k_s, v_s, cache_k, cache_v, cos, sin, cos, sin, *lam_args)
    return o_prompt, o_sample


def kernel(x_prompt, x_sample, state_wkv, cache_k, cache_v, c, c_ctx, mod_w, mod_b, ln1_g, ln1_b, ln2_g, ln2_b, ev_in_w, ev_pw_b, ev_ts_mu, ev_w0, ev_w_up, ev_a0, ev_a_up, ev_g_up, ev_k_k, ev_k_a, ev_r_k, ev_gn_g, ev_gn_b, ev_dw_w, ev_dw_b, ev_cln_g, ev_cln_b, ev_out_w, od_in_w, od_lq1, od_lk1, od_lq2, od_lk2, od_subln_g, od_out_w, rg_w, rg_b, re_w, re_b, ffn_w_gate, ffn_w_up, ffn_w_down):
    nbp, seq, d = x_prompt.shape
    nbs, n_lat, _ = x_sample.shape
    depth = mod_w.shape[0]
    n_prompt_tok = nbp * seq
    geom = (n_prompt_tok, n_lat)
    alpha = (2 * depth) ** 0.25
    a_width = ev_w0.shape[-1]
    a_cols = ev_ts_mu.shape[-1]
    heads = a_width // HEAD
    grid_w = 64
    assert n_prompt_tok % MOE_TILE == 0 and n_lat % MOE_TILE == 0 and n_lat % SUB_TILE == 0
    assert 1 + nbs <= 8 and n_prompt_tok % ROW_TILE == 0 and n_lat % ROW_TILE == 0

    xs = (x_prompt.reshape(n_prompt_tok, d), x_sample.reshape(nbs * n_lat, d))
    cond = jnp.concatenate([c_ctx[None, :], c, jnp.zeros((8 - 1 - nbs, d), F32)], axis=0)
    mod = _ada_table(cond, mod_w, mod_b)

    w_gate, w_up, w_down = (w.astype(BF16) for w in (ffn_w_gate, ffn_w_up, ffn_w_down))
    new_wkv, new_k, new_v = [], [], []
    for l in range(depth):
        last = l == depth - 1
        if l % 2 == 0:
            i = l // 2
            pa, pb = _mod_matmul(xs, mod, l, ev_in_w[i], ((a_cols, False), (ev_in_w.shape[-1] - a_cols, False)), geom)
            prm = dict(ts_mu=ev_ts_mu[i][None, :], w0=ev_w0[i], w_up=ev_w_up[i], a0=ev_a0[i], a_up=ev_a_up[i],
                       g_up=ev_g_up[i], k_k=ev_k_k[i][None, :], k_a=ev_k_a[i][None, :],
                       r_k=ev_r_k[i].reshape(1, a_width), gn_g=ev_gn_g[i][None, :], gn_b=ev_gn_b[i][None, :])
            zero_state = jnp.zeros((nbp, 2, heads, HEAD, HEAD), F32)
            oa_p, st = _rwkv_mix(pa, 0, nbp, seq, zero_state, prm)
            oa_s, _ = _rwkv_mix(pa, n_prompt_tok, nbs, n_lat, state_wkv[:, i], prm)
            conv = (ev_pw_b[i], ev_dw_w[i], ev_dw_b[i], ev_cln_g[i], ev_cln_b[i])
            ob_p = _conv_module(pb, 0, nbp, seq, *conv)
            ob_s = _conv_module(pb, n_prompt_tok, nbs, n_lat, *conv)
            hs = [(oa_p, oa_s), (ob_p, ob_s)]
            out_w = ev_out_w[i]
            new_wkv.append(st)
        else:
            j = l // 2
            lam_init = 0.8 - 0.6 * math.exp(-0.3 * l)
            inner = od_in_w.shape[-1] // 3
            q, k_p, k_s, v_p, v_s = _mod_matmul(xs, mod, l, od_in_w[j],
                                                 ((inner, False), (inner, True), (inner, True)), geom)
            prm = dict(lq1=od_lq1[j], lk1=od_lk1[j], lq2=od_lq2[j], lk2=od_lk2[j], subln_g=od_subln_g[j])
            ck = cache_k[:, j].reshape(nbs, -1, inner)
            cv = cache_v[:, j].reshape(nbs, -1, inner)
            hs = [_attention(q, k_p, v_p, k_s, v_s, ck, cv, prm, lam_init, nbp, seq, nbs, n_lat, grid_w)]
            out_w = od_out_w[j]
            new_k.append(k_p.reshape(nbp, seq, inner // PAIR, PAIR))
            new_v.append(v_p.reshape(nbp, seq, inner // PAIR, PAIR))
        x = _out_proj_norm(hs, xs, mod, l, out_w, ln1_g[l], ln1_b[l], alpha, geom)
        rw = jnp.concatenate([rg_w[l], re_w[l],
                              jnp.zeros((d, ROUTER_LANES - N_GROUPS - N_EXPERTS), F32)], axis=1)
        rb = jnp.concatenate([rg_b[l], re_b[l], jnp.zeros((ROUTER_LANES - N_GROUPS - N_EXPERTS,), F32)])[None, :]
        xs = tuple(_moe_norm(x, mod, l, rw, rb, w_gate, w_up, w_down, ln2_g[l], ln2_b[l], alpha, geom,
                             split_out=last))

    y_prompt = xs[0].reshape(nbp, seq, d)
    y_sample = xs[1].reshape(nbs, n_lat, d)
    return (y_prompt, y_sample, jnp.stack(new_wkv, axis=1), jnp.stack(new_k, axis=1), jnp.stack(new_v, axis=1))
```

```python
import functools
import math

import jax
import jax.numpy as jnp
from jax import lax
from jax.experimental import pallas as pl
from jax.experimental.pallas import tpu as pltpu

F32 = jnp.float32
BF16 = jnp.bfloat16

LANES = 128
HEAD = 64
PAIR = 2 * HEAD
CHUNK = 64
INV_BLOCK = 16
ROW_TILE = 512
SUB_TILE = 256
MOE_TILE = 1024
CONV_W = 31
N_GROUPS = 4
EXPERTS_PER_GROUP = 8
N_EXPERTS = N_GROUPS * EXPERTS_PER_GROUP
ROUTER_LANES = 128
EXPERTS_PER_STEP = 4
MOE_WINDOW = 320
BF16_ROWS = 16
LN_EPS = 1e-5
GN_EPS = 64e-5
RMS_EPS = 1e-5
ROPE_BASE = 10000.0
EXP_M05 = math.exp(-0.5)
VMEM_LIMIT = 56 * 1024 * 1024


def _bdot(a, b):
    return jnp.dot(a.astype(BF16), b.astype(BF16), preferred_element_type=F32)


def _bdot_nt(a, b):
    return lax.dot_general(a.astype(BF16), b.astype(BF16), (((1,), (1,)), ((), ())),
                           preferred_element_type=F32)


def _split2(a):
    hi = a.astype(BF16)
    lo = (a - hi.astype(F32)).astype(BF16)
    return hi, lo


def _split3(a):
    hi = a.astype(BF16)
    r1 = a - hi.astype(F32)
    mid = r1.astype(BF16)
    lo = (r1 - mid.astype(F32)).astype(BF16)
    return hi, mid, lo


def _dot_f32ish(a, b):
    ah, al = _split2(a)
    bh, bl = _split2(b)
    d = lambda x, y: jnp.dot(x, y, preferred_element_type=F32)
    return d(ah, bh) + d(ah, bl) + d(al, bh)


def _dot_exact_lhs(m_bf16, a):
    h, m, l = _split3(a)
    d = lambda y: jnp.dot(m_bf16, y, preferred_element_type=F32)
    return d(h) + d(m) + d(l)


def _dot_exact_rhs(a, m_bf16):
    h, m, l = _split3(a)
    d = lambda x: jnp.dot(x, m_bf16, preferred_element_type=F32)
    return d(h) + d(m) + d(l)


def _layer_norm(z, g, b, eps):
    mu = jnp.mean(z, axis=-1, keepdims=True)
    zc = z - mu
    var = jnp.mean(zc * zc, axis=-1, keepdims=True)
    return zc * lax.rsqrt(var + eps) * g + b


def _cond_row(tile, tile_rows, n_prompt_tok, dec_seq):
    tok = tile * tile_rows
    return jnp.where(tok < n_prompt_tok, 0, 1 + (tok - n_prompt_tok) // dec_seq)


def _cast_rows(src_ref, dst_ref, rows_per_step=128):
    n = src_ref.shape[0]
    for s in range(0, n, rows_per_step):
        dst_ref[s:s + rows_per_step, :] = src_ref[s:s + rows_per_step, :].astype(dst_ref.dtype)


def _cparams(sem):
    return pltpu.CompilerParams(dimension_semantics=sem, vmem_limit_bytes=VMEM_LIMIT)


def _ada_kernel(cond_ref, w_ref, b_ref, o_ref):
    c = cond_ref[...]
    s = c * jax.nn.sigmoid(c)
    o_ref[0, 0] = _dot_f32ish(s, w_ref[0]) + b_ref[0, 0]


def _ada_table(cond8, mod_w, mod_b):
    depth, d, _ = mod_w.shape
    return pl.pallas_call(
        _ada_kernel,
        out_shape=jax.ShapeDtypeStruct((depth, 6, 8, d), F32),
        grid=(depth, 6),
        in_specs=[pl.BlockSpec((8, d), lambda l, j: (0, 0)),
                  pl.BlockSpec((1, d, d), lambda l, j: (l, 0, j)),
                  pl.BlockSpec((1, 1, 1, d), lambda l, j: (l, j, 0, 0))],
        out_specs=pl.BlockSpec((1, 1, 8, d), lambda l, j: (l, j, 0, 0)),
        compiler_params=_cparams(("arbitrary", "arbitrary")),
        name="ada_table",
    )(cond8, mod_w, mod_b.reshape(depth, 6, 1, d))


def _mod_spec(layer, which, d):
    return pl.BlockSpec((None, None, 8, d), lambda *_: (layer, which, 0, 0))


def _group_specs(tile, width, n_prompt_tiles):
    return [pl.BlockSpec((tile, width), lambda i, *_: (jnp.minimum(i, n_prompt_tiles - 1), 0)),
            pl.BlockSpec((tile, width), lambda i, *_: (jnp.maximum(i - n_prompt_tiles, 0), 0))]


def _load_group(i, n_prompt_tiles, refs, rows=slice(None)):
    if len(refs) == 1:
        return refs[0][rows, :]
    return jnp.where(i < n_prompt_tiles, refs[0][rows, :], refs[1][rows, :])


def _store_group(i, n_prompt_tiles, refs, value, rows=slice(None)):
    if len(refs) == 1:
        refs[0][rows, :] = value
        return

    @pl.when(i < n_prompt_tiles)
    def _():
        refs[0][rows, :] = value

    @pl.when(i >= n_prompt_tiles)
    def _():
        refs[1][rows, :] = value


def _modmm_kernel(*refs, n_x, outs, geom):
    x_refs = refs[:n_x]
    sh_ref, sc_ref, w_ref = refs[n_x:n_x + 3]
    o_refs, wb = refs[n_x + 3:-1], refs[-1]
    i = pl.program_id(0)
    n_pt = geom[0] // ROW_TILE

    @pl.when(i == 0)
    def _():
        _cast_rows(w_ref, wb)

    r = _cond_row(i, ROW_TILE, *geom)
    h = _load_group(i, n_pt, x_refs) * (1.0 + sc_ref[pl.ds(r, 1), :]) + sh_ref[pl.ds(r, 1), :]
    hb = h.astype(BF16)
    off = 0
    k = 0
    for n, split in outs:
        y = jnp.dot(hb, wb[:, off:off + n], preferred_element_type=F32)
        _store_group(i, n_pt, o_refs[k:k + 1 + split], y)
        off += n
        k += 1 + split


def _mod_matmul(xs, mod, layer, w, outs, geom):
    d, n_out = w.shape
    n_prompt_tok = geom[0]
    n_tok = sum(x.shape[0] for x in xs)
    n_pt = n_prompt_tok // ROW_TILE
    whole = lambda n: [pl.BlockSpec((ROW_TILE, n), lambda i: (i, 0))]
    x_specs = whole(d) if len(xs) == 1 else _group_specs(ROW_TILE, d, n_pt)
    out_shape, out_specs = [], []
    for n, split in outs:
        if split:
            out_shape += [jax.ShapeDtypeStruct((n_prompt_tok, n), F32),
                          jax.ShapeDtypeStruct((n_tok - n_prompt_tok, n), F32)]
            out_specs += _group_specs(ROW_TILE, n, n_pt)
        else:
            out_shape += [jax.ShapeDtypeStruct((n_tok, n), F32)]
            out_specs += whole(n)
    return pl.pallas_call(
        functools.partial(_modmm_kernel, n_x=len(xs), outs=outs, geom=geom),
        out_shape=out_shape,
        grid=(n_tok // ROW_TILE,),
        in_specs=x_specs + [_mod_spec(layer, 0, d), _mod_spec(layer, 1, d),
                            pl.BlockSpec((d, n_out), lambda i: (0, 0), pipeline_mode=pl.Buffered(1))],
        out_specs=out_specs,
        scratch_shapes=[pltpu.VMEM((d, n_out), BF16)],
        compiler_params=_cparams(("arbitrary",)),
        name=f"mod_matmul_l{layer}",
    )(*xs, mod, mod, w)


def _outproj_kernel(*refs, arity, alpha, geom):
    groups, k = [], 0
    for n in arity:
        groups.append(refs[k:k + n])
        k += n
    gate_ref, w_ref, g_ref, b_ref, o_ref, wb = refs[k:]
    h_groups, x_group = groups[:-1], groups[-1]
    i = pl.program_id(0)
    n_pt = geom[0] // ROW_TILE

    @pl.when(i == 0)
    def _():
        _cast_rows(w_ref, wb)

    r = _cond_row(i, ROW_TILE, *geom)
    y = None
    off = 0
    for h_refs in h_groups:
        n = h_refs[0].shape[1]
        h = _load_group(i, n_pt, h_refs)
        part = jnp.dot(h.astype(BF16), wb[off:off + n, :], preferred_element_type=F32)
        y = part if y is None else y + part
        off += n
    z = alpha * _load_group(i, n_pt, x_group) + gate_ref[pl.ds(r, 1), :] * y
    o_ref[...] = _layer_norm(z, g_ref[...], b_ref[...], LN_EPS)


def _out_proj_norm(hs, xs, mod, layer, w, ln_g, ln_b, alpha, geom):
    d = w.shape[1]
    n_tok = sum(x.shape[0] for x in xs)
    n_pt = geom[0] // ROW_TILE
    specs = []
    for grp in list(hs) + [xs]:
        n = grp[0].shape[1]
        specs += [pl.BlockSpec((ROW_TILE, n), lambda i: (i, 0))] if len(grp) == 1 else _group_specs(ROW_TILE, n, n_pt)
    return pl.pallas_call(
        functools.partial(_outproj_kernel, arity=tuple(len(g) for g in list(hs) + [xs]), alpha=alpha, geom=geom),
        out_shape=jax.ShapeDtypeStruct((n_tok, d), F32),
        grid=(n_tok // ROW_TILE,),
        in_specs=specs + [
            _mod_spec(layer, 2, d),
            pl.BlockSpec(w.shape, lambda i: (0, 0), pipeline_mode=pl.Buffered(1)),
            pl.BlockSpec((1, d), lambda i: (0, 0)),
            pl.BlockSpec((1, d), lambda i: (0, 0))],
        out_specs=pl.BlockSpec((ROW_TILE, d), lambda i: (i, 0)),
        scratch_shapes=[pltpu.VMEM(w.shape, BF16)],
        compiler_params=_cparams(("arbitrary",)),
        name=f"out_proj_norm_l{layer}",
    )(*[a for g in hs for a in g], *xs, mod, w, ln_g.reshape(1, d), ln_b.reshape(1, d))


def _route(logits):
    neg = jnp.float32(-3e38)
    big = jnp.int32(1 << 20)
    lane = lax.broadcasted_iota(jnp.int32, logits.shape, 1)
    is_g = lane < N_GROUPS
    gl = jnp.where(is_g, logits, neg)
    gmax = jnp.max(gl, axis=-1, keepdims=True)
    gidx = jnp.min(jnp.where(gl == gmax, lane, big), axis=-1, keepdims=True)
    gsum = jnp.sum(jnp.where(is_g, jnp.exp(gl - gmax), 0.0), axis=-1, keepdims=True)
    g_p = 1.0 / gsum
    lo = N_GROUPS + gidx * EXPERTS_PER_GROUP
    in_grp = (lane >= lo) & (lane < lo + EXPERTS_PER_GROUP)
    el = jnp.where(in_grp, logits, neg)
    v1 = jnp.max(el, axis=-1, keepdims=True)
    i1 = jnp.min(jnp.where(el == v1, lane, big), axis=-1, keepdims=True)
    el2 = jnp.where(lane == i1, neg, el)
    v2 = jnp.max(el2, axis=-1, keepdims=True)
    i2 = jnp.min(jnp.where(el2 == v2, lane, big), axis=-1, keepdims=True)
    e21 = jnp.exp(v2 - v1)
    den = 1.0 + e21
    w1 = g_p / den
    w2 = g_p * e21 / den
    return jnp.where(lane == i1, w1, 0.0) + jnp.where(lane == i2, w2, 0.0), gidx


def _moe_kernel(x_ref, sh_ref, sc_ref, gate_ref, rw_ref, rb_ref, wg_ref, wu_ref, wd_ref, g_ref, b_ref,
                *rest, alpha, geom):
    o_refs = rest[:-6]
    h_scr, gates_scr, acc_scr, perm_scr, permt_scr, bounds = rest[-6:]
    i = pl.program_id(0)
    e = pl.program_id(1)
    r = _cond_row(i, MOE_TILE, *geom)
    n_pt = geom[0] // MOE_TILE
    sub = SUB_TILE
    n_sub = MOE_TILE // sub

    @pl.when(e == 0)
    def _():
        lane = lax.broadcasted_iota(jnp.int32, (sub, ROUTER_LANES), 1)
        ri = lax.broadcasted_iota(jnp.int32, (sub, sub), 0)
        ci = lax.broadcasted_iota(jnp.int32, (sub, sub), 1)
        tril = jnp.where(ci <= ri, 1.0, 0.0).astype(BF16)
        running = jnp.zeros((1, ROUTER_LANES), F32)
        onehots, cums = [], []
        for s in range(n_sub):
            rows = slice(s * sub, (s + 1) * sub)
            h = x_ref[rows, :] * (1.0 + sc_ref[pl.ds(r, 1), :]) + sh_ref[pl.ds(r, 1), :]
            acc_scr[rows, :] = h
            logits = _dot_f32ish(h, rw_ref[...]) + rb_ref[...]
            gates, gidx = _route(logits)
            gates_scr[rows, :] = gates
            onehot = jnp.where(lane == gidx, 1.0, 0.0)
            cum = jnp.dot(tril, onehot.astype(BF16), preferred_element_type=F32) + running
            running = cum[sub - 1:sub, :]
            onehots.append(onehot)
            cums.append(cum)
        gi = lax.broadcasted_iota(jnp.int32, (ROUTER_LANES, ROUTER_LANES), 0)
        gj = lax.broadcasted_iota(jnp.int32, (ROUTER_LANES, ROUTER_LANES), 1)
        before = jnp.where((gi < gj) & (gi < N_GROUPS), 1.0, 0.0).astype(BF16)
        starts = _dot_exact_rhs(jnp.broadcast_to(running, (8, ROUTER_LANES)), before)[0:1, :]
        for g in range(N_GROUPS + 1):
            bounds[g] = starts[0, g].astype(jnp.int32)
        dest_cols = [jnp.sum(onehots[s] * (starts + cums[s] - 1.0), axis=-1, keepdims=True) for s in range(n_sub)]
        dest_row = jnp.concatenate(
            [jnp.broadcast_to(dc, (sub, ROUTER_LANES)).T[0:1, :] for dc in dest_cols], axis=1)
        col_id = lax.broadcasted_iota(jnp.int32, (sub, MOE_TILE), 1).astype(F32)
        row_id = lax.broadcasted_iota(jnp.int32, (sub, MOE_TILE), 0).astype(F32)
        for s in range(n_sub):
            rows = slice(s * sub, (s + 1) * sub)
            perm_scr[rows, :] = jnp.where(row_id + float(s * sub) == dest_row, 1.0, 0.0).astype(BF16)
            permt_scr[rows, :] = jnp.where(dest_cols[s] == col_id, 1.0, 0.0).astype(BF16)
        hb = acc_scr[...].astype(BF16)
        g_hi, g_mid, g_lo = _split3(gates_scr[...])
        for s in range(n_sub):
            rows = slice(s * sub, (s + 1) * sub)
            pm = perm_scr[rows, :]
            h_scr[rows, :] = jnp.dot(pm, hb, preferred_element_type=F32).astype(BF16)
        for s in range(n_sub):
            rows = slice(s * sub, (s + 1) * sub)
            pm = perm_scr[rows, :]
            d = lambda y: jnp.dot(pm, y, preferred_element_type=F32)
            gates_scr[rows, :] = d(g_hi) + d(g_mid) + d(g_lo)
            acc_scr[rows, :] = jnp.zeros((sub, acc_scr.shape[1]), F32)

    grp = e // (EXPERTS_PER_GROUP // EXPERTS_PER_STEP)
    first, end = bounds[grp], bounds[grp + 1]
    first_al = (first // BF16_ROWS) * BF16_ROWS
    lane = lax.broadcasted_iota(jnp.int32, (MOE_WINDOW, ROUTER_LANES), 1)
    row = lax.broadcasted_iota(jnp.int32, (MOE_WINDOW, 1), 0)

    def window(w, carry):
        lo = first_al + w * MOE_WINDOW
        start = pl.multiple_of(jnp.minimum(lo, MOE_TILE - MOE_WINDOW), BF16_ROWS)
        rows = pl.ds(start, MOE_WINDOW)
        hb = h_scr[rows, :]
        gates = jnp.where(row + start >= lo, gates_scr[rows, :], 0.0)
        out = acc_scr[rows, :]
        for j in range(EXPERTS_PER_STEP):
            lane_j = N_GROUPS + e * EXPERTS_PER_STEP + j
            ge = jnp.sum(jnp.where(lane == lane_j, gates, 0.0), axis=-1, keepdims=True)
            a = jnp.dot(hb, wg_ref[j], preferred_element_type=F32)
            u = jnp.dot(hb, wu_ref[j], preferred_element_type=F32)
            hid = (a * jax.nn.sigmoid(a)) * u * ge
            out = out + jnp.dot(hid.astype(BF16), wd_ref[j], preferred_element_type=F32)
        acc_scr[rows, :] = out
        return carry

    lax.fori_loop(0, (end - first_al + MOE_WINDOW - 1) // MOE_WINDOW, window, 0)

    @pl.when(e == pl.num_programs(1) - 1)
    def _():
        a_hi, a_lo = _split2(acc_scr[...])
        for s in range(n_sub):
            rows = slice(s * sub, (s + 1) * sub)
            pt = permt_scr[rows, :]
            moe = jnp.dot(pt, a_hi, preferred_element_type=F32) + jnp.dot(pt, a_lo, preferred_element_type=F32)
            z = alpha * x_ref[rows, :] + gate_ref[pl.ds(r, 1), :] * moe
            _store_group(i, n_pt, o_refs, _layer_norm(z, g_ref[...], b_ref[...], LN_EPS), rows)


def _moe_norm(x, mod, layer, rw, rb, w_gate, w_up, w_down, ln_g, ln_b, alpha, geom, split_out):
    n_tok, d = x.shape
    ff = w_gate.shape[-1]
    assert w_gate.dtype == BF16 and w_up.dtype == BF16 and w_down.dtype == BF16
    const = lambda i, e: (0, 0)
    if split_out:
        out_shape = [jax.ShapeDtypeStruct((geom[0], d), F32), jax.ShapeDtypeStruct((n_tok - geom[0], d), F32)]
        out_specs = _group_specs(MOE_TILE, d, geom[0] // MOE_TILE)
    else:
        out_shape = [jax.ShapeDtypeStruct((n_tok, d), F32)]
        out_specs = [pl.BlockSpec((MOE_TILE, d), lambda i, e: (i, 0))]
    return pl.pallas_call(
        functools.partial(_moe_kernel, alpha=alpha, geom=geom),
        out_shape=out_shape,
        grid=(n_tok // MOE_TILE, N_EXPERTS // EXPERTS_PER_STEP),
        in_specs=[pl.BlockSpec((MOE_TILE, d), lambda i, e: (i, 0)),
                  _mod_spec(layer, 3, d), _mod_spec(layer, 4, d), _mod_spec(layer, 5, d),
                  pl.BlockSpec((d, ROUTER_LANES), const),
                  pl.BlockSpec((1, ROUTER_LANES), const),
                  pl.BlockSpec((None, EXPERTS_PER_STEP, d, ff), lambda i, e: (layer, e, 0, 0)),
                  pl.BlockSpec((None, EXPERTS_PER_STEP, d, ff), lambda i, e: (layer, e, 0, 0)),
                  pl.BlockSpec((None, EXPERTS_PER_STEP, ff, d), lambda i, e: (layer, e, 0, 0)),
                  pl.BlockSpec((1, d), const),
                  pl.BlockSpec((1, d), const)],
        out_specs=out_specs,
        scratch_shapes=[pltpu.VMEM((MOE_TILE, d), BF16),
                        pltpu.VMEM((MOE_TILE, ROUTER_LANES), F32),
                        pltpu.VMEM((MOE_TILE, d), F32),
                        pltpu.VMEM((MOE_TILE, MOE_TILE), BF16), pltpu.VMEM((MOE_TILE, MOE_TILE), BF16),
                        pltpu.SMEM((8,), jnp.int32)],
        compiler_params=_cparams(("arbitrary", "arbitrary")),
        name=f"moe_norm_l{layer}",
    )(x, mod, mod, mod, rw, rb, w_gate, w_up, w_down, ln_g.reshape(1, d), ln_b.reshape(1, d))


def _rwkv_kernel(r_ref, k_ref, v_ref, lora_ref, mur_ref, muk_ref, muv_ref, mul_ref,
                 w0_ref, wup_ref, a0_ref, aup_ref, gup_ref, kk_ref, ka_ref, rk_ref, gng_ref, gnb_ref, s0_ref,
                 *rest, seq_len, unroll, n_ride):
    ride_in, (out_ref, sfin_ref), ride_out = rest[:n_ride], rest[n_ride:n_ride + 2], rest[n_ride + 2:2 * n_ride + 2]
    (lw_s, kz_s, bz_s, rr_s, vv_s, kn_s, abar_s, rt_s, wt_s, vst_s, p_s, tm_s, mb_s, mk_s, nkb_s, txr_s, tx_s, gl_s,
     g_s, h_s, q_s, y0_s, y_s, st_s) = rest[2 * n_ride + 2:]
    for src, dst in zip(ride_in, ride_out):
        dst[...] = src[...].astype(dst.dtype)
    t_len = seq_len
    n_chunks = t_len // CHUNK
    c2 = 2 * CHUNK

    row = lax.broadcasted_iota(jnp.int32, (t_len, 1), 0)

    def tshift(x, mu):
        prev = jnp.where(row == 0, 0.0, pltpu.roll(x, 1, 0))
        nxt = jnp.where(row == t_len - 1, 0.0, pltpu.roll(x, t_len - 1, 0))
        return x + mu * (0.5 * (prev + nxt) - x)

    r = tshift(r_ref[...], mur_ref[...])
    k = tshift(k_ref[...], muk_ref[...])
    v = tshift(v_ref[...], muv_ref[...])
    lo = tshift(lora_ref[...], mul_ref[...])
    wd = jnp.tanh(lo[:, 0:PAIR])
    ad = lo[:, PAIR:2 * PAIR]
    gate = _bdot(jax.nn.sigmoid(lo[:, 2 * PAIR:3 * PAIR]), gup_ref[...])

    ri = lax.broadcasted_iota(jnp.int32, (PAIR, PAIR), 0)
    ci = lax.broadcasted_iota(jnp.int32, (PAIR, PAIR), 1)
    same_head = (ri // HEAD) == (ci // HEAD)
    ones_bd = jnp.where(same_head, 1.0, 0.0).astype(BF16)
    eye = ri == ci

    kk = k * kk_ref[...]
    ssq = _dot_exact_rhs(kk * kk, ones_bd)
    kk = kk * lax.rsqrt(jnp.maximum(ssq, 1e-24))
    wup2 = wup_ref[...].reshape(PAIR, PAIR)
    aup2 = aup_ref[...].reshape(PAIR, PAIR)
    bonus = jnp.zeros((t_len, PAIR), F32)
    for z in range(2):
        zrows = (ri // HEAD) == z
        w_pre = w0_ref[z:z + 1, :] + _bdot(wd, jnp.where(zrows, wup2, 0.0))
        lw_s[z] = -EXP_M05 * jax.nn.sigmoid(w_pre)
        a = jax.nn.sigmoid(a0_ref[z:z + 1, :] + _bdot(ad, jnp.where(zrows, aup2, 0.0)))
        kz = k * (1.0 + (a - 1.0) * ka_ref[...])
        kz_s[z] = kz
        bz_s[z] = kk * a
        bonus = bonus + _dot_exact_rhs(r * kz * rk_ref[...], ones_bd) * v
    rr_s[...] = r
    vv_s[...] = v
    kn_s[...] = kk

    rt_i = ri % CHUNK
    cs_i = ci % CHUNK
    strict = (cs_i < rt_i, cs_i > rt_i)
    incl = (cs_i <= rt_i, cs_i >= rt_i)
    t64 = lax.broadcasted_iota(jnp.int32, (CHUNK, CHUNK), 0)
    s64 = lax.broadcasted_iota(jnp.int32, (CHUNK, CHUNK), 1)
    tri = (jnp.where(s64 <= t64, 1.0, 0.0).astype(BF16), jnp.where(s64 >= t64, 1.0, 0.0).astype(BF16))
    lane_head0 = lax.broadcasted_iota(jnp.int32, (CHUNK, PAIR), 1) < HEAD
    eye_f = jnp.where(eye, 1.0, 0.0)
    same16 = (ri // INV_BLOCK) == (ci // INV_BLOCK)

    def stack(x):
        return jnp.concatenate([jnp.where(lane_head0, x, 0.0), jnp.where(lane_head0, 0.0, x)], axis=0)

    def fold(x):
        return x[:CHUNK] + x[CHUNK:]

    def chunk_rows(c):
        return pl.ds(pl.multiple_of(c * CHUNK, CHUNK), CHUNK)

    def each_item(body):
        def both_directions(c, carry):
            body(0, c)
            body(1, c)
            return carry

        lax.fori_loop(0, n_chunks, both_directions, 0, unroll=unroll)

    def stage_operands(z, c):
        rows = chunk_rows(c)
        r_c, kn_c = rr_s[rows, :], kn_s[rows, :]
        lw_c, kz_c, bz_c = lw_s[z, rows, :], kz_s[z, rows, :], bz_s[z, rows, :]
        lg = _dot_exact_lhs(tri[z], lw_c)
        total = lg[CHUNK - 1:CHUNK, :] if z == 0 else lg[0:1, :]
        e_neg = jnp.exp(-lg)
        e_rem = jnp.exp(total - lg)
        abar = stack(kn_c * jnp.exp(lg - lw_c))
        rt = stack(r_c * jnp.exp(lg))
        big = _bdot_nt(jnp.concatenate([abar, rt], axis=0),
                       jnp.concatenate([stack(bz_c * e_neg), stack(kz_c * e_neg)], axis=0))
        mb = jnp.where(strict[z], big[:c2, :c2], 0.0)
        abar_s[z, c] = abar.astype(BF16)
        rt_s[z, c] = rt
        wt_s[z, c] = jnp.concatenate([stack(kz_c * e_rem), -stack(bz_c * e_rem)], axis=0).T.astype(BF16)
        gl_s[z, c] = jnp.exp(total)
        m0 = jnp.where(same16, mb, 0.0)
        tm_s[z, c] = eye_f - m0
        p_s[z, c] = m0.astype(BF16)
        mb_s[z, c] = mb.astype(BF16)
        mk_s[z, c] = jnp.where(strict[z], big[:c2, c2:], 0.0).astype(BF16)
        nkb_s[z, c] = jnp.concatenate([jnp.where(incl[z], big[c2:, c2:], 0.0),
                                       -jnp.where(incl[z], big[c2:, :c2], 0.0)], axis=1).astype(BF16)

    def stage_values(c, carry):
        vst_s[c] = stack(vv_s[chunk_rows(c), :]).astype(BF16)
        return carry

    lax.fori_loop(0, n_chunks, stage_values, 0, unroll=unroll)
    each_item(stage_operands)

    def stage_square(z, c):
        m0 = p_s[z, c]
        p_s[z, c] = jnp.dot(m0, m0, preferred_element_type=F32).astype(BF16)
        mkv = jnp.dot(mk_s[z, c], vst_s[c], preferred_element_type=F32)
        txr_s[z, c] = jnp.concatenate([abar_s[z, c], mkv.astype(BF16)], axis=1)

    each_item(stage_square)

    n_round = INV_BLOCK.bit_length() - 2
    for j in range(n_round):
        def stage_double(z, c, last=(j == n_round - 1)):
            tm, p = tm_s[z, c], p_s[z, c]
            tm_s[z, c] = tm + jnp.dot(tm.astype(BF16), p, preferred_element_type=F32)
            if not last:
                p_s[z, c] = jnp.dot(p, p, preferred_element_type=F32).astype(BF16)

        each_item(stage_double)

    size = INV_BLOCK
    while size < CHUNK:
        off_diag = ((ri // (2 * size)) == (ci // (2 * size))) & ((ri // size) != (ci // size))

        def stage_merge_rhs(z, c, off_diag=off_diag):
            cross = jnp.where(off_diag, mb_s[z, c], jnp.zeros((), BF16))
            p_s[z, c] = jnp.dot(cross, tm_s[z, c].astype(BF16), preferred_element_type=F32).astype(BF16)

        def stage_merge(z, c):
            tm = tm_s[z, c]
            tm_s[z, c] = tm - jnp.dot(tm.astype(BF16), p_s[z, c], preferred_element_type=F32)

        each_item(stage_merge_rhs)
        each_item(stage_merge)
        size *= 2

    def stage_apply(z, c):
        tx_s[z, c] = jnp.dot(tm_s[z, c].astype(BF16), txr_s[z, c], preferred_element_type=F32)

    each_item(stage_apply)

    def stage_maps(z, c):
        tx = tx_s[z, c].astype(BF16)
        v_st = vst_s[c]
        wt = wt_s[z, c]
        from_tx = jnp.dot(wt[:, c2:], tx, preferred_element_type=F32)
        h_s[z, c] = jnp.dot(wt[:, :c2], v_st, preferred_element_type=F32) + from_tx[:, PAIR:]
        g_s[z, c] = jnp.where(eye, gl_s[z, c], 0.0) + from_tx[:, :PAIR]
        nkb = nkb_s[z, c]
        from_tx = jnp.dot(nkb[:, c2:], tx, preferred_element_type=F32)
        q_s[z, c] = fold(rt_s[z, c] + from_tx[:, :PAIR])
        y0_s[z, c] = fold(jnp.dot(nkb[:, :c2], v_st, preferred_element_type=F32) + from_tx[:, PAIR:])

    each_item(stage_maps)

    zero64 = jnp.zeros((HEAD, HEAD), F32)
    for z in range(2):
        a = jnp.concatenate([jnp.concatenate([s0_ref[z, 0], zero64], axis=1),
                             jnp.concatenate([zero64, s0_ref[z, 1]], axis=1)], axis=0)
        st_s[z] = a.T

    def propagate(i, carry):
        for z in range(2):
            c = i if z == 0 else n_chunks - 1 - i
            s = st_s[z].astype(BF16)
            y_s[z, chunk_rows(c), :] = _bdot(q_s[z, c], s) + y0_s[z, c]
            st_s[z] = _bdot(g_s[z, c], s) + h_s[z, c]
        return carry

    lax.fori_loop(0, n_chunks, propagate, 0)
    for z in range(2):
        a = st_s[z].T
        sfin_ref[z, 0] = a[:HEAD, :HEAD]
        sfin_ref[z, 1] = a[HEAD:, HEAD:]

    y = y_s[0] + y_s[1]
    mu = _dot_exact_rhs(y, ones_bd) * (1.0 / HEAD)
    yc = y - mu
    var = _dot_exact_rhs(yc * yc, ones_bd) * (1.0 / HEAD)
    yn = yc * lax.rsqrt(var + GN_EPS) * gng_ref[...] + gnb_ref[...]
    out_ref[...] = (yn + bonus) * gate


def _rwkv_mix(pa, first_tok, n_seq, seq_len, s0, p, ride=()):
    a_width = p["w0"].shape[-1]
    n_pairs = a_width // PAIR
    heads = a_width // HEAD
    t0 = first_tok // seq_len
    lora_w = 3 * PAIR
    lora_blk = 3 * a_width // lora_w
    n_chunks = seq_len // CHUNK
    col = lambda off: (lambda s, q: (t0 + s, off + q))
    vec = lambda off: (lambda s, q: (0, off + q))
    in_specs = [
        pl.BlockSpec((seq_len, PAIR), col(0)),
        pl.BlockSpec((seq_len, PAIR), col(n_pairs)),
        pl.BlockSpec((seq_len, PAIR), col(2 * n_pairs)),
        pl.BlockSpec((seq_len, lora_w), lambda s, q: (t0 + s, lora_blk)),
        pl.BlockSpec((1, PAIR), vec(0)),
        pl.BlockSpec((1, PAIR), vec(n_pairs)),
        pl.BlockSpec((1, PAIR), vec(2 * n_pairs)),
        pl.BlockSpec((1, lora_w), lambda s, q: (0, lora_blk)),
        pl.BlockSpec((2, PAIR), vec(0)),
        pl.BlockSpec((2, HEAD, PAIR), lambda s, q: (0, 0, q)),
        pl.BlockSpec((2, PAIR), vec(0)),
        pl.BlockSpec((2, HEAD, PAIR), lambda s, q: (0, 0, q)),
        pl.BlockSpec((PAIR, PAIR), vec(0)),
        pl.BlockSpec((1, PAIR), vec(0)),
        pl.BlockSpec((1, PAIR), vec(0)),
        pl.BlockSpec((1, PAIR), vec(0)),
        pl.BlockSpec((1, PAIR), vec(0)),
        pl.BlockSpec((1, PAIR), vec(0)),
        pl.BlockSpec((None, 2, 2, HEAD, HEAD), lambda s, q: (s, 0, q, 0, 0)),
    ]
    seq2 = lambda: pltpu.VMEM((2, seq_len, PAIR), F32)
    seq1 = lambda: pltpu.VMEM((seq_len, PAIR), F32)
    item = lambda rows, cols, dt: pltpu.VMEM((2, n_chunks, rows, cols), dt)
    n_steps = n_seq * n_pairs
    assert all(a.shape[0] % (n_steps * BF16_ROWS) == 0 for a in ride)
    ride_specs = [pl.BlockSpec((a.shape[0] // n_steps, a.shape[1]), lambda s, q: (s * n_pairs + q, 0)) for a in ride]
    out, s_fin, *cast = pl.pallas_call(
        functools.partial(_rwkv_kernel, seq_len=seq_len, unroll=min(n_chunks, 8), n_ride=len(ride)),
        out_shape=[jax.ShapeDtypeStruct((n_seq * seq_len, a_width), F32),
                   jax.ShapeDtypeStruct((n_seq, 2, heads, HEAD, HEAD), F32)]
        + [jax.ShapeDtypeStruct(a.shape, BF16) for a in ride],
        grid=(n_seq, n_pairs),
        in_specs=in_specs + ride_specs,
        out_specs=[pl.BlockSpec((seq_len, PAIR), lambda s, q: (s, q)),
                   pl.BlockSpec((None, 2, 2, HEAD, HEAD), lambda s, q: (s, 0, q, 0, 0))] + ride_specs,
        scratch_shapes=[seq2(), seq2(), seq2(), seq1(), seq1(), seq1(),
                        item(PAIR, PAIR, BF16),
                        item(PAIR, PAIR, F32),
                        item(PAIR, 2 * PAIR, BF16),
                        pltpu.VMEM((n_chunks, PAIR, PAIR), BF16),
                        item(PAIR, PAIR, BF16),
                        item(PAIR, PAIR, F32),
                        item(PAIR, PAIR, BF16),
                        item(PAIR, PAIR, BF16),
                        item(PAIR, 2 * PAIR, BF16),
                        item(PAIR, 2 * PAIR, BF16),
                        item(PAIR, 2 * PAIR, F32),
                        item(1, PAIR, F32),
                        item(PAIR, PAIR, F32), item(PAIR, PAIR, F32),
                        item(CHUNK, PAIR, F32), item(CHUNK, PAIR, F32),
                        seq2(), pltpu.VMEM((2, PAIR, PAIR), F32)],
        compiler_params=_cparams(("arbitrary", "arbitrary")),
        name=f"rwkv_mix_t{seq_len}",
    )(pa, pa, pa, pa, p["ts_mu"], p["ts_mu"], p["ts_mu"], p["ts_mu"],
      p["w0"], p["w_up"], p["a0"], p["a_up"], p["g_up"], p["k_k"], p["k_a"], p["r_k"], p["gn_g"], p["gn_b"], s0, *ride)
    return out, s_fin, cast


def _conv_kernel(pb_ref, pwb_ref, dw_ref, dwb_ref, g_ref, b_ref, o_ref, pad_s, *, seq_len):
    width = o_ref.shape[1]
    halo = 16
    sub = 64
    pbv = pb_ref[...] + pwb_ref[...]
    glu = pbv[:, :width] * jax.nn.sigmoid(pbv[:, width:])
    pad_s[0:halo, :] = jnp.zeros((halo, width), F32)
    pad_s[halo:halo + seq_len, :] = glu
    pad_s[halo + seq_len:, :] = jnp.zeros((halo, width), F32)
    first = halo - CONV_W // 2
    for c in range(seq_len // sub):
        acc = jnp.zeros((sub, width), F32)
        for d in range(CONV_W):
            start = c * sub + first + d
            acc = acc + pad_s[start:start + sub, :] * dw_ref[d:d + 1, :]
        y = _layer_norm(acc + dwb_ref[...], g_ref[...], b_ref[...], LN_EPS)
        o_ref[c * sub:(c + 1) * sub, :] = y * jax.nn.sigmoid(y)


def _conv_module(pb, first_tok, n_seq, seq_len, pw_b, dw_w, dw_b, cln_g, cln_b):
    width = dw_w.shape[1]
    t0 = first_tok // seq_len
    const = lambda s: (0, 0)
    return pl.pallas_call(
        functools.partial(_conv_kernel, seq_len=seq_len),
        out_shape=jax.ShapeDtypeStruct((n_seq * seq_len, width), F32),
        grid=(n_seq,),
        in_specs=[pl.BlockSpec((seq_len, 2 * width), lambda s: (t0 + s, 0)),
                  pl.BlockSpec((1, 2 * width), const),
                  pl.BlockSpec((CONV_W, width), const),
                  pl.BlockSpec((1, width), const),
                  pl.BlockSpec((1, width), const),
                  pl.BlockSpec((1, width), const)],
        out_specs=pl.BlockSpec((seq_len, width), lambda s: (s, 0)),
        scratch_shapes=[pltpu.VMEM((seq_len + 32, width), F32)],
        compiler_params=_cparams(("arbitrary",)),
        name=f"conv_module_t{seq_len}",
    )(pb, pw_b.reshape(1, -1), dw_w, dw_b.reshape(1, -1), cln_g.reshape(1, -1), cln_b.reshape(1, -1))


def _diff_lambda(lq1, lk1, lq2, lk2, lam_init):
    dot = lambda a, b: jnp.exp(jnp.sum(a[...] * b[...], axis=-1, keepdims=True))
    return dot(lq1, lk1) - dot(lq2, lk2) + lam_init


def _diff_attend(q, keys, vals, lam, lam_init, subln):
    nq = q.shape[0]
    lane_first = lax.broadcasted_iota(jnp.int32, q.shape, 1) < HEAD
    q = q * (HEAD ** -0.5)
    q_st = jnp.concatenate([jnp.where(lane_first, q, 0.0), jnp.where(lane_first, 0.0, q)], axis=0)
    s = _bdot_nt(q_st, keys)
    s = s - jnp.max(s, axis=-1, keepdims=True)
    pexp = jnp.exp(s)
    ov = _bdot(pexp, vals) / jnp.sum(pexp, axis=-1, keepdims=True)
    o = ov[:nq] - lam * ov[nq:]
    o = o * lax.rsqrt(jnp.mean(o * o, axis=-1, keepdims=True) + RMS_EPS)
    return o * subln * (1.0 - lam_init)


def _head_cols(h):
    return slice(h * PAIR, (h + 1) * PAIR)


def _attn_prompt_kernel(q_ref, k_ref, v_ref, lq1, lk1, lq2, lk2, sub_ref, o_ref, *, lam_init):
    lam = _diff_lambda(lq1, lk1, lq2, lk2, lam_init)
    for h in range(q_ref.shape[1] // PAIR):
        c = _head_cols(h)
        o_ref[:, c] = _diff_attend(q_ref[:, c], k_ref[:, c], v_ref[:, c], lam, lam_init, sub_ref[...])


def _rope(x, cos, sin_signed):
    lane = lax.broadcasted_iota(jnp.int32, x.shape, 1)
    quarter = HEAD // 4
    partner = jnp.where((lane // quarter) % 2 == 0,
                        pltpu.roll(x, LANES - quarter, 1), pltpu.roll(x, quarter, 1))
    return x * cos + partner * sin_signed


def _attn_sample_kernel(q_ref, k_ref, v_ref, ck_ref, cv_ref, cq_ref, sq_ref, cosk_ref, sink_ref,
                        lq1, lk1, lq2, lk2, sub_ref, o_ref, *, lam_init):
    lam = _diff_lambda(lq1, lk1, lq2, lk2, lam_init)
    for h in range(q_ref.shape[1] // PAIR):
        c = _head_cols(h)
        q = _rope(q_ref[:, c], cq_ref[...], sq_ref[...])
        kl = _rope(k_ref[:, c], cosk_ref[...], sink_ref[...])
        keys = jnp.concatenate([ck_ref[:, c], kl], axis=0)
        vals = jnp.concatenate([cv_ref[:, c], v_ref[:, c]], axis=0)
        o_ref[:, c] = _diff_attend(q, keys, vals, lam, lam_init, sub_ref[...])


def _rope_tables(n_lat, grid_w):
    quarter = HEAD // 4
    inv = ROPE_BASE ** (-jnp.arange(quarter, dtype=F32) / quarter)
    t = jnp.arange(n_lat)
    row_ang = (t // grid_w).astype(F32)[:, None] * inv
    col_ang = (t % grid_w).astype(F32)[:, None] * inv
    ang = jnp.concatenate([row_ang, row_ang, col_ang, col_ang] * 2, axis=-1)
    sign = jnp.tile(jnp.concatenate([-jnp.ones(quarter, F32), jnp.ones(quarter, F32)]), 2 * HEAD // (2 * quarter))
    return jnp.cos(ang), jnp.sin(ang) * sign


def _attention(q, k_p, v_p, k_s, v_s, cache_k, cache_v, p, lam_init, n_prompt, seq, n_lat_req, n_lat, grid_w):
    inner = q.shape[1]
    vec = lambda n: pl.BlockSpec((1, n), lambda *_: (0, 0))
    lam_args = [p[n].reshape(1, -1) for n in ("lq1", "lk1", "lq2", "lk2")] + [p["subln_g"].reshape(1, -1)]
    lam_specs = [vec(HEAD)] * 4 + [vec(PAIR)]
    blk = pl.BlockSpec((seq, inner), lambda b: (b, 0))
    o_prompt = pl.pallas_call(
        functools.partial(_attn_prompt_kernel, lam_init=lam_init),
        out_shape=jax.ShapeDtypeStruct((n_prompt * seq, inner), F32),
        grid=(n_prompt,),
        in_specs=[blk, blk, blk] + lam_specs,
        out_specs=blk,
        compiler_params=_cparams(("arbitrary",)),
        name="diff_attn_prompt",
    )(q, k_p, v_p, *lam_args)

    past = cache_k.shape[1]
    qt = SUB_TILE
    n_qt = n_lat // qt
    q0 = n_prompt * seq // qt
    cos, sin = _rope_tables(n_lat, grid_w)
    lat = pl.BlockSpec((n_lat, inner), lambda b, t: (b, 0))
    ctx = pl.BlockSpec((None, past, inner), lambda b, t: (b, 0, 0))
    o_sample = pl.pallas_call(
        functools.partial(_attn_sample_kernel, lam_init=lam_init),
        out_shape=jax.ShapeDtypeStruct((n_lat_req * n_lat, inner), F32),
        grid=(n_lat_req, n_qt),
        in_specs=[pl.BlockSpec((qt, inner), lambda b, t: (q0 + b * n_qt + t, 0)),
                  lat, lat, ctx, ctx,
                  pl.BlockSpec((qt, PAIR), lambda b, t: (t, 0)),
                  pl.BlockSpec((qt, PAIR), lambda b, t: (t, 0)),
                  pl.BlockSpec((n_lat, PAIR), lambda b, t: (0, 0)),
                  pl.BlockSpec((n_lat, PAIR), lambda b, t: (0, 0))] + lam_specs,
        out_specs=pl.BlockSpec((qt, inner), lambda b, t: (b * n_qt + t, 0)),
        compiler_params=_cparams(("arbitrary", "arbitrary")),
        name="diff_attn_sample",
    )(q, k_s, v_s, cache_k, cache_v, cos, sin, cos, sin, *lam_args)
    return o_prompt, o_sample


def kernel(x_prompt, x_sample, state_wkv, cache_k, cache_v, c, c_ctx, mod_w, mod_b, ln1_g, ln1_b, ln2_g, ln2_b, ev_in_w, ev_pw_b, ev_ts_mu, ev_w0, ev_w_up, ev_a0, ev_a_up, ev_g_up, ev_k_k, ev_k_a, ev_r_k, ev_gn_g, ev_gn_b, ev_dw_w, ev_dw_b, ev_cln_g, ev_cln_b, ev_out_w, od_in_w, od_lq1, od_lk1, od_lq2, od_lk2, od_subln_g, od_out_w, rg_w, rg_b, re_w, re_b, ffn_w_gate, ffn_w_up, ffn_w_down):
    nbp, seq, d = x_prompt.shape
    nbs, n_lat, _ = x_sample.shape
    depth = mod_w.shape[0]
    n_prompt_tok = nbp * seq
    geom = (n_prompt_tok, n_lat)
    alpha = (2 * depth) ** 0.25
    a_width = ev_w0.shape[-1]
    a_cols = ev_ts_mu.shape[-1]
    heads = a_width // HEAD
    grid_w = 64
    assert n_prompt_tok % MOE_TILE == 0 and n_lat % MOE_TILE == 0 and n_lat % SUB_TILE == 0
    assert 1 + nbs <= 8 and n_prompt_tok % ROW_TILE == 0 and n_lat % ROW_TILE == 0

    xs = (x_prompt.reshape(n_prompt_tok, d), x_sample.reshape(nbs * n_lat, d))
    cond = jnp.concatenate([c_ctx[None, :], c, jnp.zeros((8 - 1 - nbs, d), F32)], axis=0)
    mod = _ada_table(cond, mod_w, mod_b)

    new_wkv, new_k, new_v = [], [], []
    for l in range(depth):
        last = l == depth - 1
        if l % 2 == 0:
            i = l // 2
            pa, pb = _mod_matmul(xs, mod, l, ev_in_w[i], ((a_cols, False), (ev_in_w.shape[-1] - a_cols, False)), geom)
            prm = dict(ts_mu=ev_ts_mu[i][None, :], w0=ev_w0[i], w_up=ev_w_up[i], a0=ev_a0[i], a_up=ev_a_up[i],
                       g_up=ev_g_up[i], k_k=ev_k_k[i][None, :], k_a=ev_k_a[i][None, :],
                       r_k=ev_r_k[i].reshape(1, a_width), gn_g=ev_gn_g[i][None, :], gn_b=ev_gn_b[i][None, :])
            zero_state = jnp.zeros((nbp, 2, heads, HEAD, HEAD), F32)
            ride = () if l else tuple(w.reshape(-1, w.shape[-1]) for w in (ffn_w_gate, ffn_w_up, ffn_w_down))
            oa_p, st, cast = _rwkv_mix(pa, 0, nbp, seq, zero_state, prm, ride)
            if cast:
                w_gate, w_up, w_down = (c.reshape(w.shape) for c, w in zip(cast, (ffn_w_gate, ffn_w_up, ffn_w_down)))
            oa_s, _, _ = _rwkv_mix(pa, n_prompt_tok, nbs, n_lat, state_wkv[:, i], prm)
            conv = (ev_pw_b[i], ev_dw_w[i], ev_dw_b[i], ev_cln_g[i], ev_cln_b[i])
            ob_p = _conv_module(pb, 0, nbp, seq, *conv)
            ob_s = _conv_module(pb, n_prompt_tok, nbs, n_lat, *conv)
            hs = [(oa_p, oa_s), (ob_p, ob_s)]
            out_w = ev_out_w[i]
            new_wkv.append(st)
        else:
            j = l // 2
            lam_init = 0.8 - 0.6 * math.exp(-0.3 * l)
            inner = od_in_w.shape[-1] // 3
            q, k_p, k_s, v_p, v_s = _mod_matmul(xs, mod, l, od_in_w[j],
                                                 ((inner, False), (inner, True), (inner, True)), geom)
            prm = dict(lq1=od_lq1[j], lk1=od_lk1[j], lq2=od_lq2[j], lk2=od_lk2[j], subln_g=od_subln_g[j])
            ck = cache_k[:, j].reshape(nbs, -1, inner)
            cv = cache_v[:, j].reshape(nbs, -1, inner)
            hs = [_attention(q, k_p, v_p, k_s, v_s, ck, cv, prm, lam_init, nbp, seq, nbs, n_lat, grid_w)]
            out_w = od_out_w[j]
            new_k.append(k_p.reshape(nbp, seq, inner // PAIR, PAIR))
            new_v.append(v_p.reshape(nbp, seq, inner // PAIR, PAIR))
        x = _out_proj_norm(hs, xs, mod, l, out_w, ln1_g[l], ln1_b[l], alpha, geom)
        rw = jnp.concatenate([rg_w[l], re_w[l],
                              jnp.zeros((d, ROUTER_LANES - N_GROUPS - N_EXPERTS), F32)], axis=1)
        rb = jnp.concatenate([rg_b[l], re_b[l], jnp.zeros((ROUTER_LANES - N_GROUPS - N_EXPERTS,), F32)])[None, :]
        xs = tuple(_moe_norm(x, mod, l, rw, rb, w_gate, w_up, w_down, ln2_g[l], ln2_b[l], alpha, geom,
                             split_out=last))

    y_prompt = xs[0].reshape(nbp, seq, d)
    y_sample = xs[1].reshape(nbs, n_lat, d)
    return (y_prompt, y_sample, jnp.stack(new_wkv, axis=1), jnp.stack(new_k, axis=1), jnp.stack(new_v, axis=1))
```

```python
import functools
import math

import jax
import jax.numpy as jnp
from jax import lax
from jax.experimental import pallas as pl
from jax.experimental.pallas import tpu as pltpu

F32 = jnp.float32
BF16 = jnp.bfloat16

LANES = 128
HEAD = 64
PAIR = 2 * HEAD
CHUNK = 64
RWKV_STEP_ROWS = 512
INV_BLOCK = 16
ROW_TILE = 512
SUB_TILE = 256
MOE_TILE = 1024
CONV_W = 31
N_GROUPS = 4
EXPERTS_PER_GROUP = 8
N_EXPERTS = N_GROUPS * EXPERTS_PER_GROUP
ROUTER_LANES = 128
EXPERTS_PER_STEP = 4
MOE_WINDOWS = (128, 256, 384, 512)
BF16_ROWS = 16
LN_EPS = 1e-5
GN_EPS = 64e-5
RMS_EPS = 1e-5
ROPE_BASE = 10000.0
EXP_M05 = math.exp(-0.5)
VMEM_LIMIT = 56 * 1024 * 1024


def _bdot(a, b):
    return jnp.dot(a.astype(BF16), b.astype(BF16), preferred_element_type=F32)


def _bdot_nt(a, b):
    return lax.dot_general(a.astype(BF16), b.astype(BF16), (((1,), (1,)), ((), ())),
                           preferred_element_type=F32)


def _split2(a):
    hi = a.astype(BF16)
    lo = (a - hi.astype(F32)).astype(BF16)
    return hi, lo


def _split3(a):
    hi = a.astype(BF16)
    r1 = a - hi.astype(F32)
    mid = r1.astype(BF16)
    lo = (r1 - mid.astype(F32)).astype(BF16)
    return hi, mid, lo


def _dot_f32ish(a, b):
    ah, al = _split2(a)
    bh, bl = _split2(b)
    d = lambda x, y: jnp.dot(x, y, preferred_element_type=F32)
    return d(ah, bh) + d(ah, bl) + d(al, bh)


def _dot_exact_lhs(m_bf16, a):
    h, m, l = _split3(a)
    d = lambda y: jnp.dot(m_bf16, y, preferred_element_type=F32)
    return d(h) + d(m) + d(l)


def _dot_exact_rhs(a, m_bf16):
    h, m, l = _split3(a)
    d = lambda x: jnp.dot(x, m_bf16, preferred_element_type=F32)
    return d(h) + d(m) + d(l)


def _layer_norm(z, g, b, eps):
    mu = jnp.mean(z, axis=-1, keepdims=True)
    zc = z - mu
    var = jnp.mean(zc * zc, axis=-1, keepdims=True)
    return zc * lax.rsqrt(var + eps) * g + b


def _cond_row(tile, tile_rows, n_prompt_tok, dec_seq):
    tok = tile * tile_rows
    return jnp.where(tok < n_prompt_tok, 0, 1 + (tok - n_prompt_tok) // dec_seq)


def _cast_rows(src_ref, dst_ref, rows_per_step=128):
    n = src_ref.shape[0]
    for s in range(0, n, rows_per_step):
        dst_ref[s:s + rows_per_step, :] = src_ref[s:s + rows_per_step, :].astype(dst_ref.dtype)


def _cparams(sem):
    return pltpu.CompilerParams(dimension_semantics=sem, vmem_limit_bytes=VMEM_LIMIT)


def _ada_kernel(cond_ref, w_ref, b_ref, o_ref):
    c = cond_ref[...]
    s = c * jax.nn.sigmoid(c)
    o_ref[0, 0] = _dot_f32ish(s, w_ref[0]) + b_ref[0, 0]


def _ada_table(cond8, mod_w, mod_b):
    depth, d, _ = mod_w.shape
    return pl.pallas_call(
        _ada_kernel,
        out_shape=jax.ShapeDtypeStruct((depth, 6, 8, d), F32),
        grid=(depth, 6),
        in_specs=[pl.BlockSpec((8, d), lambda l, j: (0, 0)),
                  pl.BlockSpec((1, d, d), lambda l, j: (l, 0, j)),
                  pl.BlockSpec((1, 1, 1, d), lambda l, j: (l, j, 0, 0))],
        out_specs=pl.BlockSpec((1, 1, 8, d), lambda l, j: (l, j, 0, 0)),
        compiler_params=_cparams(("arbitrary", "arbitrary")),
        name="ada_table",
    )(cond8, mod_w, mod_b.reshape(depth, 6, 1, d))


def _mod_spec(layer, which, d):
    return pl.BlockSpec((None, None, 8, d), lambda *_: (layer, which, 0, 0))


def _group_specs(tile, width, n_prompt_tiles):
    return [pl.BlockSpec((tile, width), lambda i, *_: (jnp.minimum(i, n_prompt_tiles - 1), 0)),
            pl.BlockSpec((tile, width), lambda i, *_: (jnp.maximum(i - n_prompt_tiles, 0), 0))]


def _load_group(i, n_prompt_tiles, refs, rows=slice(None)):
    if len(refs) == 1:
        return refs[0][rows, :]
    return jnp.where(i < n_prompt_tiles, refs[0][rows, :], refs[1][rows, :])


def _store_group(i, n_prompt_tiles, refs, value, rows=slice(None)):
    if len(refs) == 1:
        refs[0][rows, :] = value
        return

    @pl.when(i < n_prompt_tiles)
    def _():
        refs[0][rows, :] = value

    @pl.when(i >= n_prompt_tiles)
    def _():
        refs[1][rows, :] = value


def _modmm_kernel(*refs, n_x, outs, geom):
    x_refs = refs[:n_x]
    sh_ref, sc_ref, w_ref = refs[n_x:n_x + 3]
    o_refs, wb = refs[n_x + 3:-1], refs[-1]
    i = pl.program_id(0)
    n_pt = geom[0] // ROW_TILE

    @pl.when(i == 0)
    def _():
        _cast_rows(w_ref, wb)

    r = _cond_row(i, ROW_TILE, *geom)
    h = _load_group(i, n_pt, x_refs) * (1.0 + sc_ref[pl.ds(r, 1), :]) + sh_ref[pl.ds(r, 1), :]
    hb = h.astype(BF16)
    off = 0
    k = 0
    for n, split in outs:
        y = jnp.dot(hb, wb[:, off:off + n], preferred_element_type=F32)
        _store_group(i, n_pt, o_refs[k:k + 1 + split], y)
        off += n
        k += 1 + split


def _mod_matmul(xs, mod, layer, w, outs, geom):
    d, n_out = w.shape
    n_prompt_tok = geom[0]
    n_tok = sum(x.shape[0] for x in xs)
    n_pt = n_prompt_tok // ROW_TILE
    whole = lambda n: [pl.BlockSpec((ROW_TILE, n), lambda i: (i, 0))]
    x_specs = whole(d) if len(xs) == 1 else _group_specs(ROW_TILE, d, n_pt)
    out_shape, out_specs = [], []
    for n, split in outs:
        if split:
            out_shape += [jax.ShapeDtypeStruct((n_prompt_tok, n), F32),
                          jax.ShapeDtypeStruct((n_tok - n_prompt_tok, n), F32)]
            out_specs += _group_specs(ROW_TILE, n, n_pt)
        else:
            out_shape += [jax.ShapeDtypeStruct((n_tok, n), F32)]
            out_specs += whole(n)
    return pl.pallas_call(
        functools.partial(_modmm_kernel, n_x=len(xs), outs=outs, geom=geom),
        out_shape=out_shape,
        grid=(n_tok // ROW_TILE,),
        in_specs=x_specs + [_mod_spec(layer, 0, d), _mod_spec(layer, 1, d),
                            pl.BlockSpec((d, n_out), lambda i: (0, 0), pipeline_mode=pl.Buffered(1))],
        out_specs=out_specs,
        scratch_shapes=[pltpu.VMEM((d, n_out), BF16)],
        compiler_params=_cparams(("arbitrary",)),
        name=f"mod_matmul_l{layer}",
    )(*xs, mod, mod, w)


def _outproj_kernel(*refs, arity, alpha, geom):
    groups, k = [], 0
    for n in arity:
        groups.append(refs[k:k + n])
        k += n
    gate_ref, w_ref, g_ref, b_ref, o_ref, wb = refs[k:]
    h_groups, x_group = groups[:-1], groups[-1]
    i = pl.program_id(0)
    n_pt = geom[0] // ROW_TILE

    @pl.when(i == 0)
    def _():
        _cast_rows(w_ref, wb)

    r = _cond_row(i, ROW_TILE, *geom)
    y = None
    off = 0
    for h_refs in h_groups:
        n = h_refs[0].shape[1]
        h = _load_group(i, n_pt, h_refs)
        part = jnp.dot(h.astype(BF16), wb[off:off + n, :], preferred_element_type=F32)
        y = part if y is None else y + part
        off += n
    z = alpha * _load_group(i, n_pt, x_group) + gate_ref[pl.ds(r, 1), :] * y
    o_ref[...] = _layer_norm(z, g_ref[...], b_ref[...], LN_EPS)


def _out_proj_norm(hs, xs, mod, layer, w, ln_g, ln_b, alpha, geom):
    d = w.shape[1]
    n_tok = sum(x.shape[0] for x in xs)
    n_pt = geom[0] // ROW_TILE
    specs = []
    for grp in list(hs) + [xs]:
        n = grp[0].shape[1]
        specs += [pl.BlockSpec((ROW_TILE, n), lambda i: (i, 0))] if len(grp) == 1 else _group_specs(ROW_TILE, n, n_pt)
    return pl.pallas_call(
        functools.partial(_outproj_kernel, arity=tuple(len(g) for g in list(hs) + [xs]), alpha=alpha, geom=geom),
        out_shape=jax.ShapeDtypeStruct((n_tok, d), F32),
        grid=(n_tok // ROW_TILE,),
        in_specs=specs + [
            _mod_spec(layer, 2, d),
            pl.BlockSpec(w.shape, lambda i: (0, 0), pipeline_mode=pl.Buffered(1)),
            pl.BlockSpec((1, d), lambda i: (0, 0)),
            pl.BlockSpec((1, d), lambda i: (0, 0))],
        out_specs=pl.BlockSpec((ROW_TILE, d), lambda i: (i, 0)),
        scratch_shapes=[pltpu.VMEM(w.shape, BF16)],
        compiler_params=_cparams(("arbitrary",)),
        name=f"out_proj_norm_l{layer}",
    )(*[a for g in hs for a in g], *xs, mod, w, ln_g.reshape(1, d), ln_b.reshape(1, d))


def _route(logits):
    neg = jnp.float32(-3e38)
    big = jnp.int32(1 << 20)
    lane = lax.broadcasted_iota(jnp.int32, logits.shape, 1)
    is_g = lane < N_GROUPS
    gl = jnp.where(is_g, logits, neg)
    gmax = jnp.max(gl, axis=-1, keepdims=True)
    gidx = jnp.min(jnp.where(gl == gmax, lane, big), axis=-1, keepdims=True)
    gsum = jnp.sum(jnp.where(is_g, jnp.exp(gl - gmax), 0.0), axis=-1, keepdims=True)
    g_p = 1.0 / gsum
    lo = N_GROUPS + gidx * EXPERTS_PER_GROUP
    in_grp = (lane >= lo) & (lane < lo + EXPERTS_PER_GROUP)
    el = jnp.where(in_grp, logits, neg)
    v1 = jnp.max(el, axis=-1, keepdims=True)
    i1 = jnp.min(jnp.where(el == v1, lane, big), axis=-1, keepdims=True)
    el2 = jnp.where(lane == i1, neg, el)
    v2 = jnp.max(el2, axis=-1, keepdims=True)
    i2 = jnp.min(jnp.where(el2 == v2, lane, big), axis=-1, keepdims=True)
    e21 = jnp.exp(v2 - v1)
    den = 1.0 + e21
    w1 = g_p / den
    w2 = g_p * e21 / den
    return jnp.where(lane == i1, w1, 0.0) + jnp.where(lane == i2, w2, 0.0), gidx


def _moe_kernel(x_ref, sh_ref, sc_ref, gate_ref, rw_ref, rb_ref, wg_ref, wu_ref, wd_ref, g_ref, b_ref,
                *rest, alpha, geom):
    o_refs = rest[:-6]
    h_scr, gates_scr, acc_scr, perm_scr, permt_scr, bounds = rest[-6:]
    i = pl.program_id(0)
    e = pl.program_id(1)
    r = _cond_row(i, MOE_TILE, *geom)
    n_pt = geom[0] // MOE_TILE
    sub = SUB_TILE
    n_sub = MOE_TILE // sub

    @pl.when(e == 0)
    def _():
        lane = lax.broadcasted_iota(jnp.int32, (sub, ROUTER_LANES), 1)
        ri = lax.broadcasted_iota(jnp.int32, (sub, sub), 0)
        ci = lax.broadcasted_iota(jnp.int32, (sub, sub), 1)
        tril = jnp.where(ci <= ri, 1.0, 0.0).astype(BF16)
        running = jnp.zeros((1, ROUTER_LANES), F32)
        onehots, cums = [], []
        for s in range(n_sub):
            rows = slice(s * sub, (s + 1) * sub)
            h = x_ref[rows, :] * (1.0 + sc_ref[pl.ds(r, 1), :]) + sh_ref[pl.ds(r, 1), :]
            acc_scr[rows, :] = h
            logits = _dot_f32ish(h, rw_ref[...]) + rb_ref[...]
            gates, gidx = _route(logits)
            gates_scr[rows, :] = gates
            onehot = jnp.where(lane == gidx, 1.0, 0.0)
            cum = jnp.dot(tril, onehot.astype(BF16), preferred_element_type=F32) + running
            running = cum[sub - 1:sub, :]
            onehots.append(onehot)
            cums.append(cum)
        gi = lax.broadcasted_iota(jnp.int32, (ROUTER_LANES, ROUTER_LANES), 0)
        gj = lax.broadcasted_iota(jnp.int32, (ROUTER_LANES, ROUTER_LANES), 1)
        before = jnp.where((gi < gj) & (gi < N_GROUPS), 1.0, 0.0).astype(BF16)
        starts = _dot_exact_rhs(jnp.broadcast_to(running, (8, ROUTER_LANES)), before)[0:1, :]
        for g in range(N_GROUPS + 1):
            bounds[g] = starts[0, g].astype(jnp.int32)
        dest_cols = [jnp.sum(onehots[s] * (starts + cums[s] - 1.0), axis=-1, keepdims=True) for s in range(n_sub)]
        dest_row = jnp.concatenate(
            [jnp.broadcast_to(dc, (sub, ROUTER_LANES)).T[0:1, :] for dc in dest_cols], axis=1)
        col_id = lax.broadcasted_iota(jnp.int32, (sub, MOE_TILE), 1).astype(F32)
        row_id = lax.broadcasted_iota(jnp.int32, (sub, MOE_TILE), 0).astype(F32)
        for s in range(n_sub):
            rows = slice(s * sub, (s + 1) * sub)
            perm_scr[rows, :] = jnp.where(row_id + float(s * sub) == dest_row, 1.0, 0.0).astype(BF16)
            permt_scr[rows, :] = jnp.where(dest_cols[s] == col_id, 1.0, 0.0).astype(BF16)
        hb = acc_scr[...].astype(BF16)
        g_hi, g_mid, g_lo = _split3(gates_scr[...])
        for s in range(n_sub):
            rows = slice(s * sub, (s + 1) * sub)
            pm = perm_scr[rows, :]
            h_scr[rows, :] = jnp.dot(pm, hb, preferred_element_type=F32).astype(BF16)
        for s in range(n_sub):
            rows = slice(s * sub, (s + 1) * sub)
            pm = perm_scr[rows, :]
            d = lambda y: jnp.dot(pm, y, preferred_element_type=F32)
            gates_scr[rows, :] = d(g_hi) + d(g_mid) + d(g_lo)
            acc_scr[rows, :] = jnp.zeros((sub, acc_scr.shape[1]), F32)

    grp = e // (EXPERTS_PER_GROUP // EXPERTS_PER_STEP)
    first, end = bounds[grp], bounds[grp + 1]
    first_al = (first // BF16_ROWS) * BF16_ROWS
    step = MOE_WINDOWS[0]
    assert all(w == (k + 1) * step for k, w in enumerate(MOE_WINDOWS))

    def experts_on(lo, size):
        start = pl.multiple_of(jnp.minimum(lo, MOE_TILE - size), BF16_ROWS)
        rows = pl.ds(start, size)
        lane = lax.broadcasted_iota(jnp.int32, (size, ROUTER_LANES), 1)
        row = lax.broadcasted_iota(jnp.int32, (size, 1), 0)
        hb = h_scr[rows, :]
        gates = jnp.where(row + start >= lo, gates_scr[rows, :], 0.0)
        out = acc_scr[rows, :]
        for j in range(EXPERTS_PER_STEP):
            lane_j = N_GROUPS + e * EXPERTS_PER_STEP + j
            ge = jnp.sum(jnp.where(lane == lane_j, gates, 0.0), axis=-1, keepdims=True)
            a = jnp.dot(hb, wg_ref[j], preferred_element_type=F32)
            u = jnp.dot(hb, wu_ref[j], preferred_element_type=F32)
            hid = (a * jax.nn.sigmoid(a)) * u * ge
            out = out + jnp.dot(hid.astype(BF16), wd_ref[j], preferred_element_type=F32)
        acc_scr[rows, :] = out

    def window(w, carry):
        lo = first_al + w * MOE_WINDOWS[-1]
        n_steps = jnp.minimum((end - lo + step - 1) // step, len(MOE_WINDOWS))
        for k, size in enumerate(MOE_WINDOWS):
            pl.when(n_steps == k + 1)(functools.partial(experts_on, lo, size))
        return carry

    lax.fori_loop(0, (end - first_al + MOE_WINDOWS[-1] - 1) // MOE_WINDOWS[-1], window, 0)

    @pl.when(e == pl.num_programs(1) - 1)
    def _():
        a_hi, a_lo = _split2(acc_scr[...])
        for s in range(n_sub):
            rows = slice(s * sub, (s + 1) * sub)
            pt = permt_scr[rows, :]
            moe = jnp.dot(pt, a_hi, preferred_element_type=F32) + jnp.dot(pt, a_lo, preferred_element_type=F32)
            z = alpha * x_ref[rows, :] + gate_ref[pl.ds(r, 1), :] * moe
            _store_group(i, n_pt, o_refs, _layer_norm(z, g_ref[...], b_ref[...], LN_EPS), rows)


def _moe_norm(x, mod, layer, rw, rb, w_gate, w_up, w_down, ln_g, ln_b, alpha, geom, split_out):
    n_tok, d = x.shape
    ff = w_gate.shape[-1]
    assert w_gate.dtype == BF16 and w_up.dtype == BF16 and w_down.dtype == BF16
    const = lambda i, e: (0, 0)
    if split_out:
        out_shape = [jax.ShapeDtypeStruct((geom[0], d), F32), jax.ShapeDtypeStruct((n_tok - geom[0], d), F32)]
        out_specs = _group_specs(MOE_TILE, d, geom[0] // MOE_TILE)
    else:
        out_shape = [jax.ShapeDtypeStruct((n_tok, d), F32)]
        out_specs = [pl.BlockSpec((MOE_TILE, d), lambda i, e: (i, 0))]
    return pl.pallas_call(
        functools.partial(_moe_kernel, alpha=alpha, geom=geom),
        out_shape=out_shape,
        grid=(n_tok // MOE_TILE, N_EXPERTS // EXPERTS_PER_STEP),
        in_specs=[pl.BlockSpec((MOE_TILE, d), lambda i, e: (i, 0)),
                  _mod_spec(layer, 3, d), _mod_spec(layer, 4, d), _mod_spec(layer, 5, d),
                  pl.BlockSpec((d, ROUTER_LANES), const),
                  pl.BlockSpec((1, ROUTER_LANES), const),
                  pl.BlockSpec((None, EXPERTS_PER_STEP, d, ff), lambda i, e: (layer, e, 0, 0)),
                  pl.BlockSpec((None, EXPERTS_PER_STEP, d, ff), lambda i, e: (layer, e, 0, 0)),
                  pl.BlockSpec((None, EXPERTS_PER_STEP, ff, d), lambda i, e: (layer, e, 0, 0)),
                  pl.BlockSpec((1, d), const),
                  pl.BlockSpec((1, d), const)],
        out_specs=out_specs,
        scratch_shapes=[pltpu.VMEM((MOE_TILE, d), BF16),
                        pltpu.VMEM((MOE_TILE, ROUTER_LANES), F32),
                        pltpu.VMEM((MOE_TILE, d), F32),
                        pltpu.VMEM((MOE_TILE, MOE_TILE), BF16), pltpu.VMEM((MOE_TILE, MOE_TILE), BF16),
                        pltpu.SMEM((8,), jnp.int32)],
        compiler_params=_cparams(("arbitrary", "arbitrary")),
        name=f"moe_norm_l{layer}",
    )(x, mod, mod, mod, rw, rb, w_gate, w_up, w_down, ln_g.reshape(1, d), ln_b.reshape(1, d))


def _rwkv_kernel(r_ref, k_ref, v_ref, lora_ref, mur_ref, muk_ref, muv_ref, mul_ref,
                 w0_ref, wup_ref, a0_ref, aup_ref, gup_ref, kk_ref, ka_ref, rk_ref, gng_ref, gnb_ref, s0_ref,
                 *rest, seq_len, seqs, unroll, n_ride):
    ride_in, (out_ref, sfin_ref), ride_out = rest[:n_ride], rest[n_ride:n_ride + 2], rest[n_ride + 2:2 * n_ride + 2]
    (lw_s, kz_s, bz_s, rr_s, vv_s, kn_s, abar_s, rt_s, wt_s, vst_s, p_s, tm_s, mb_s, mk_s, nkb_s, txr_s, tx_s, gl_s,
     g_s, h_s, q_s, y0_s, y_s, st_s) = rest[2 * n_ride + 2:]
    for src, dst in zip(ride_in, ride_out):
        dst[...] = src[...].astype(dst.dtype)
    t_len = seqs * seq_len
    n_chunks = t_len // CHUNK
    seq_chunks = seq_len // CHUNK
    c2 = 2 * CHUNK

    row = lax.broadcasted_iota(jnp.int32, (t_len, 1), 0) % seq_len

    def tshift(x, mu):
        prev = jnp.where(row == 0, 0.0, pltpu.roll(x, 1, 0))
        nxt = jnp.where(row == seq_len - 1, 0.0, pltpu.roll(x, t_len - 1, 0))
        return x + mu * (0.5 * (prev + nxt) - x)

    r = tshift(r_ref[...], mur_ref[...])
    k = tshift(k_ref[...], muk_ref[...])
    v = tshift(v_ref[...], muv_ref[...])
    lo = tshift(lora_ref[...], mul_ref[...])
    wd = jnp.tanh(lo[:, 0:PAIR])
    ad = lo[:, PAIR:2 * PAIR]
    gate = _bdot(jax.nn.sigmoid(lo[:, 2 * PAIR:3 * PAIR]), gup_ref[...])

    ri = lax.broadcasted_iota(jnp.int32, (PAIR, PAIR), 0)
    ci = lax.broadcasted_iota(jnp.int32, (PAIR, PAIR), 1)
    same_head = (ri // HEAD) == (ci // HEAD)
    ones_bd = jnp.where(same_head, 1.0, 0.0).astype(BF16)
    eye = ri == ci

    kk = k * kk_ref[...]
    ssq = _dot_exact_rhs(kk * kk, ones_bd)
    kk = kk * lax.rsqrt(jnp.maximum(ssq, 1e-24))
    wup2 = wup_ref[...].reshape(PAIR, PAIR)
    aup2 = aup_ref[...].reshape(PAIR, PAIR)
    bonus = jnp.zeros((t_len, PAIR), F32)
    for z in range(2):
        zrows = (ri // HEAD) == z
        w_pre = w0_ref[z:z + 1, :] + _bdot(wd, jnp.where(zrows, wup2, 0.0))
        lw_s[z] = -EXP_M05 * jax.nn.sigmoid(w_pre)
        a = jax.nn.sigmoid(a0_ref[z:z + 1, :] + _bdot(ad, jnp.where(zrows, aup2, 0.0)))
        kz = k * (1.0 + (a - 1.0) * ka_ref[...])
        kz_s[z] = kz
        bz_s[z] = kk * a
        bonus = bonus + _dot_exact_rhs(r * kz * rk_ref[...], ones_bd) * v
    rr_s[...] = r
    vv_s[...] = v
    kn_s[...] = kk

    rt_i = ri % CHUNK
    cs_i = ci % CHUNK
    strict = (cs_i < rt_i, cs_i > rt_i)
    incl = (cs_i <= rt_i, cs_i >= rt_i)
    t64 = lax.broadcasted_iota(jnp.int32, (CHUNK, CHUNK), 0)
    s64 = lax.broadcasted_iota(jnp.int32, (CHUNK, CHUNK), 1)
    tri = (jnp.where(s64 <= t64, 1.0, 0.0).astype(BF16), jnp.where(s64 >= t64, 1.0, 0.0).astype(BF16))
    lane_head0 = lax.broadcasted_iota(jnp.int32, (CHUNK, PAIR), 1) < HEAD
    eye_f = jnp.where(eye, 1.0, 0.0)
    same16 = (ri // INV_BLOCK) == (ci // INV_BLOCK)

    def stack(x):
        return jnp.concatenate([jnp.where(lane_head0, x, 0.0), jnp.where(lane_head0, 0.0, x)], axis=0)

    def fold(x):
        return x[:CHUNK] + x[CHUNK:]

    def chunk_rows(c):
        return pl.ds(pl.multiple_of(c * CHUNK, CHUNK), CHUNK)

    def each_item(body):
        def both_directions(c, carry):
            body(0, c)
            body(1, c)
            return carry

        lax.fori_loop(0, n_chunks, both_directions, 0, unroll=unroll)

    def stage_operands(z, c):
        rows = chunk_rows(c)
        r_c, kn_c = rr_s[rows, :], kn_s[rows, :]
        lw_c, kz_c, bz_c = lw_s[z, rows, :], kz_s[z, rows, :], bz_s[z, rows, :]
        lg = _dot_exact_lhs(tri[z], lw_c)
        total = lg[CHUNK - 1:CHUNK, :] if z == 0 else lg[0:1, :]
        e_neg = jnp.exp(-lg)
        e_rem = jnp.exp(total - lg)
        abar = stack(kn_c * jnp.exp(lg - lw_c))
        rt = stack(r_c * jnp.exp(lg))
        big = _bdot_nt(jnp.concatenate([abar, rt], axis=0),
                       jnp.concatenate([stack(bz_c * e_neg), stack(kz_c * e_neg)], axis=0))
        mb = jnp.where(strict[z], big[:c2, :c2], 0.0)
        abar_s[z, c] = abar.astype(BF16)
        rt_s[z, c] = rt
        wt_s[z, c] = jnp.concatenate([stack(kz_c * e_rem), -stack(bz_c * e_rem)], axis=0).T.astype(BF16)
        gl_s[z, c] = jnp.exp(total)
        m0 = jnp.where(same16, mb, 0.0)
        tm_s[z, c] = eye_f - m0
        p_s[z, c] = m0.astype(BF16)
        mb_s[z, c] = mb.astype(BF16)
        mk_s[z, c] = jnp.where(strict[z], big[:c2, c2:], 0.0).astype(BF16)
        nkb_s[z, c] = jnp.concatenate([jnp.where(incl[z], big[c2:, c2:], 0.0),
                                       -jnp.where(incl[z], big[c2:, :c2], 0.0)], axis=1).astype(BF16)

    def stage_values(c, carry):
        vst_s[c] = stack(vv_s[chunk_rows(c), :]).astype(BF16)
        return carry

    lax.fori_loop(0, n_chunks, stage_values, 0, unroll=unroll)
    each_item(stage_operands)

    def stage_square(z, c):
        m0 = p_s[z, c]
        p_s[z, c] = jnp.dot(m0, m0, preferred_element_type=F32).astype(BF16)
        mkv = jnp.dot(mk_s[z, c], vst_s[c], preferred_element_type=F32)
        txr_s[z, c] = jnp.concatenate([abar_s[z, c], mkv.astype(BF16)], axis=1)

    each_item(stage_square)

    n_round = INV_BLOCK.bit_length() - 2
    for j in range(n_round):
        def stage_double(z, c, last=(j == n_round - 1)):
            tm, p = tm_s[z, c], p_s[z, c]
            tm_s[z, c] = tm + jnp.dot(tm.astype(BF16), p, preferred_element_type=F32)
            if not last:
                p_s[z, c] = jnp.dot(p, p, preferred_element_type=F32).astype(BF16)

        each_item(stage_double)

    size = INV_BLOCK
    while size < CHUNK:
        off_diag = ((ri // (2 * size)) == (ci // (2 * size))) & ((ri // size) != (ci // size))

        def stage_merge_rhs(z, c, off_diag=off_diag):
            cross = jnp.where(off_diag, mb_s[z, c], jnp.zeros((), BF16))
            p_s[z, c] = jnp.dot(cross, tm_s[z, c].astype(BF16), preferred_element_type=F32).astype(BF16)

        def stage_merge(z, c):
            tm = tm_s[z, c]
            tm_s[z, c] = tm - jnp.dot(tm.astype(BF16), p_s[z, c], preferred_element_type=F32)

        each_item(stage_merge_rhs)
        each_item(stage_merge)
        size *= 2

    def stage_apply(z, c):
        tx_s[z, c] = jnp.dot(tm_s[z, c].astype(BF16), txr_s[z, c], preferred_element_type=F32)

    each_item(stage_apply)

    def stage_maps(z, c):
        tx = tx_s[z, c].astype(BF16)
        v_st = vst_s[c]
        wt = wt_s[z, c]
        from_tx = jnp.dot(wt[:, c2:], tx, preferred_element_type=F32)
        h_s[z, c] = jnp.dot(wt[:, :c2], v_st, preferred_element_type=F32) + from_tx[:, PAIR:]
        g_s[z, c] = jnp.where(eye, gl_s[z, c], 0.0) + from_tx[:, :PAIR]
        nkb = nkb_s[z, c]
        from_tx = jnp.dot(nkb[:, c2:], tx, preferred_element_type=F32)
        q_s[z, c] = fold(rt_s[z, c] + from_tx[:, :PAIR])
        y0_s[z, c] = fold(jnp.dot(nkb[:, :c2], v_st, preferred_element_type=F32) + from_tx[:, PAIR:])

    each_item(stage_maps)

    zero64 = jnp.zeros((HEAD, HEAD), F32)
    chains = [(n, z) for n in range(seqs) for z in range(2)]
    for n, z in chains:
        a = jnp.concatenate([jnp.concatenate([s0_ref[n, z, 0], zero64], axis=1),
                             jnp.concatenate([zero64, s0_ref[n, z, 1]], axis=1)], axis=0)
        st_s[n, z] = a.T

    def propagate(i, carry):
        for n, z in chains:
            c = n * seq_chunks + (i if z == 0 else seq_chunks - 1 - i)
            s = st_s[n, z].astype(BF16)
            y_s[z, chunk_rows(c), :] = _bdot(q_s[z, c], s) + y0_s[z, c]
            st_s[n, z] = _bdot(g_s[z, c], s) + h_s[z, c]
        return carry

    lax.fori_loop(0, seq_chunks, propagate, 0)
    for n, z in chains:
        a = st_s[n, z].T
        sfin_ref[n, z, 0] = a[:HEAD, :HEAD]
        sfin_ref[n, z, 1] = a[HEAD:, HEAD:]

    y = y_s[0] + y_s[1]
    mu = _dot_exact_rhs(y, ones_bd) * (1.0 / HEAD)
    yc = y - mu
    var = _dot_exact_rhs(yc * yc, ones_bd) * (1.0 / HEAD)
    yn = yc * lax.rsqrt(var + GN_EPS) * gng_ref[...] + gnb_ref[...]
    out_ref[...] = (yn + bonus) * gate


def _rwkv_mix(pa, first_tok, n_seq, seq_len, s0, p, ride=()):
    a_width = p["w0"].shape[-1]
    n_pairs = a_width // PAIR
    heads = a_width // HEAD
    seqs = max(1, RWKV_STEP_ROWS // seq_len)
    rows = seqs * seq_len
    assert n_seq % seqs == 0 and first_tok % rows == 0
    t0 = first_tok // rows
    lora_w = 3 * PAIR
    lora_blk = 3 * a_width // lora_w
    n_chunks = rows // CHUNK
    col = lambda off: (lambda s, q: (t0 + s, off + q))
    vec = lambda off: (lambda s, q: (0, off + q))
    in_specs = [
        pl.BlockSpec((rows, PAIR), col(0)),
        pl.BlockSpec((rows, PAIR), col(n_pairs)),
        pl.BlockSpec((rows, PAIR), col(2 * n_pairs)),
        pl.BlockSpec((rows, lora_w), lambda s, q: (t0 + s, lora_blk)),
        pl.BlockSpec((1, PAIR), vec(0)),
        pl.BlockSpec((1, PAIR), vec(n_pairs)),
        pl.BlockSpec((1, PAIR), vec(2 * n_pairs)),
        pl.BlockSpec((1, lora_w), lambda s, q: (0, lora_blk)),
        pl.BlockSpec((2, PAIR), vec(0)),
        pl.BlockSpec((2, HEAD, PAIR), lambda s, q: (0, 0, q)),
        pl.BlockSpec((2, PAIR), vec(0)),
        pl.BlockSpec((2, HEAD, PAIR), lambda s, q: (0, 0, q)),
        pl.BlockSpec((PAIR, PAIR), vec(0)),
        pl.BlockSpec((1, PAIR), vec(0)),
        pl.BlockSpec((1, PAIR), vec(0)),
        pl.BlockSpec((1, PAIR), vec(0)),
        pl.BlockSpec((1, PAIR), vec(0)),
        pl.BlockSpec((1, PAIR), vec(0)),
        pl.BlockSpec((seqs, 2, 2, HEAD, HEAD), lambda s, q: (s, 0, q, 0, 0)),
    ]
    seq2 = lambda: pltpu.VMEM((2, rows, PAIR), F32)
    seq1 = lambda: pltpu.VMEM((rows, PAIR), F32)
    item = lambda rows, cols, dt: pltpu.VMEM((2, n_chunks, rows, cols), dt)
    n_steps = n_seq // seqs * n_pairs
    assert all(a.shape[0] % (n_steps * BF16_ROWS) == 0 for a in ride)
    ride_specs = [pl.BlockSpec((a.shape[0] // n_steps, a.shape[1]), lambda s, q: (s * n_pairs + q, 0)) for a in ride]
    out, s_fin, *cast = pl.pallas_call(
        functools.partial(_rwkv_kernel, seq_len=seq_len, seqs=seqs, unroll=min(n_chunks, 8), n_ride=len(ride)),
        out_shape=[jax.ShapeDtypeStruct((n_seq * seq_len, a_width), F32),
                   jax.ShapeDtypeStruct((n_seq, 2, heads, HEAD, HEAD), F32)]
        + [jax.ShapeDtypeStruct(a.shape, BF16) for a in ride],
        grid=(n_seq // seqs, n_pairs),
        in_specs=in_specs + ride_specs,
        out_specs=[pl.BlockSpec((rows, PAIR), lambda s, q: (s, q)),
                   pl.BlockSpec((seqs, 2, 2, HEAD, HEAD), lambda s, q: (s, 0, q, 0, 0))] + ride_specs,
        scratch_shapes=[seq2(), seq2(), seq2(), seq1(), seq1(), seq1(),
                        item(PAIR, PAIR, BF16),
                        item(PAIR, PAIR, F32),
                        item(PAIR, 2 * PAIR, BF16),
                        pltpu.VMEM((n_chunks, PAIR, PAIR), BF16),
                        item(PAIR, PAIR, BF16),
                        item(PAIR, PAIR, F32),
                        item(PAIR, PAIR, BF16),
                        item(PAIR, PAIR, BF16),
                        item(PAIR, 2 * PAIR, BF16),
                        item(PAIR, 2 * PAIR, BF16),
                        item(PAIR, 2 * PAIR, F32),
                        item(1, PAIR, F32),
                        item(PAIR, PAIR, F32), item(PAIR, PAIR, F32),
                        item(CHUNK, PAIR, F32), item(CHUNK, PAIR, F32),
                        seq2(), pltpu.VMEM((seqs, 2, PAIR, PAIR), F32)],
        compiler_params=_cparams(("arbitrary", "arbitrary")),
        name=f"rwkv_mix_t{seq_len}",
    )(pa, pa, pa, pa, p["ts_mu"], p["ts_mu"], p["ts_mu"], p["ts_mu"],
      p["w0"], p["w_up"], p["a0"], p["a_up"], p["g_up"], p["k_k"], p["k_a"], p["r_k"], p["gn_g"], p["gn_b"], s0, *ride)
    return out, s_fin, cast


def _conv_kernel(pb_ref, pwb_ref, dw_ref, dwb_ref, g_ref, b_ref, o_ref, pad_s, *, seq_len):
    width = o_ref.shape[1]
    halo = 16
    sub = 64
    pbv = pb_ref[...] + pwb_ref[...]
    glu = pbv[:, :width] * jax.nn.sigmoid(pbv[:, width:])
    pad_s[0:halo, :] = jnp.zeros((halo, width), F32)
    pad_s[halo:halo + seq_len, :] = glu
    pad_s[halo + seq_len:, :] = jnp.zeros((halo, width), F32)
    first = halo - CONV_W // 2
    for c in range(seq_len // sub):
        acc = jnp.zeros((sub, width), F32)
        for d in range(CONV_W):
            start = c * sub + first + d
            acc = acc + pad_s[start:start + sub, :] * dw_ref[d:d + 1, :]
        y = _layer_norm(acc + dwb_ref[...], g_ref[...], b_ref[...], LN_EPS)
        o_ref[c * sub:(c + 1) * sub, :] = y * jax.nn.sigmoid(y)


def _conv_module(pb, first_tok, n_seq, seq_len, pw_b, dw_w, dw_b, cln_g, cln_b):
    width = dw_w.shape[1]
    t0 = first_tok // seq_len
    const = lambda s: (0, 0)
    return pl.pallas_call(
        functools.partial(_conv_kernel, seq_len=seq_len),
        out_shape=jax.ShapeDtypeStruct((n_seq * seq_len, width), F32),
        grid=(n_seq,),
        in_specs=[pl.BlockSpec((seq_len, 2 * width), lambda s: (t0 + s, 0)),
                  pl.BlockSpec((1, 2 * width), const),
                  pl.BlockSpec((CONV_W, width), const),
                  pl.BlockSpec((1, width), const),
                  pl.BlockSpec((1, width), const),
                  pl.BlockSpec((1, width), const)],
        out_specs=pl.BlockSpec((seq_len, width), lambda s: (s, 0)),
        scratch_shapes=[pltpu.VMEM((seq_len + 32, width), F32)],
        compiler_params=_cparams(("arbitrary",)),
        name=f"conv_module_t{seq_len}",
    )(pb, pw_b.reshape(1, -1), dw_w, dw_b.reshape(1, -1), cln_g.reshape(1, -1), cln_b.reshape(1, -1))


def _diff_lambda(lq1, lk1, lq2, lk2, lam_init):
    dot = lambda a, b: jnp.exp(jnp.sum(a[...] * b[...], axis=-1, keepdims=True))
    return dot(lq1, lk1) - dot(lq2, lk2) + lam_init


def _diff_attend(q, keys, vals, lam, lam_init, subln):
    nq = q.shape[0]
    lane_first = lax.broadcasted_iota(jnp.int32, q.shape, 1) < HEAD
    q = q * (HEAD ** -0.5)
    q_st = jnp.concatenate([jnp.where(lane_first, q, 0.0), jnp.where(lane_first, 0.0, q)], axis=0)
    s = _bdot_nt(q_st, keys)
    s = s - jnp.max(s, axis=-1, keepdims=True)
    pexp = jnp.exp(s)
    ov = _bdot(pexp, vals) / jnp.sum(pexp, axis=-1, keepdims=True)
    o = ov[:nq] - lam * ov[nq:]
    o = o * lax.rsqrt(jnp.mean(o * o, axis=-1, keepdims=True) + RMS_EPS)
    return o * subln * (1.0 - lam_init)


def _head_cols(h):
    return slice(h * PAIR, (h + 1) * PAIR)


def _attn_prompt_kernel(q_ref, k_ref, v_ref, lq1, lk1, lq2, lk2, sub_ref, o_ref, *, lam_init):
    lam = _diff_lambda(lq1, lk1, lq2, lk2, lam_init)
    for h in range(q_ref.shape[1] // PAIR):
        c = _head_cols(h)
        o_ref[:, c] = _diff_attend(q_ref[:, c], k_ref[:, c], v_ref[:, c], lam, lam_init, sub_ref[...])


def _rope(x, cos, sin_signed):
    lane = lax.broadcasted_iota(jnp.int32, x.shape, 1)
    quarter = HEAD // 4
    partner = jnp.where((lane // quarter) % 2 == 0,
                        pltpu.roll(x, LANES - quarter, 1), pltpu.roll(x, quarter, 1))
    return x * cos + partner * sin_signed


def _attn_sample_kernel(q_ref, k_ref, v_ref, ck_ref, cv_ref, cq_ref, sq_ref, cosk_ref, sink_ref,
                        lq1, lk1, lq2, lk2, sub_ref, o_ref, *, lam_init):
    lam = _diff_lambda(lq1, lk1, lq2, lk2, lam_init)
    for h in range(q_ref.shape[1] // PAIR):
        c = _head_cols(h)
        q = _rope(q_ref[:, c], cq_ref[...], sq_ref[...])
        kl = _rope(k_ref[:, c], cosk_ref[...], sink_ref[...])
        keys = jnp.concatenate([ck_ref[:, c], kl], axis=0)
        vals = jnp.concatenate([cv_ref[:, c], v_ref[:, c]], axis=0)
        o_ref[:, c] = _diff_attend(q, keys, vals, lam, lam_init, sub_ref[...])


def _rope_tables(n_lat, grid_w):
    quarter = HEAD // 4
    inv = ROPE_BASE ** (-jnp.arange(quarter, dtype=F32) / quarter)
    t = jnp.arange(n_lat)
    row_ang = (t // grid_w).astype(F32)[:, None] * inv
    col_ang = (t % grid_w).astype(F32)[:, None] * inv
    ang = jnp.concatenate([row_ang, row_ang, col_ang, col_ang] * 2, axis=-1)
    sign = jnp.tile(jnp.concatenate([-jnp.ones(quarter, F32), jnp.ones(quarter, F32)]), 2 * HEAD // (2 * quarter))
    return jnp.cos(ang), jnp.sin(ang) * sign


def _attention(q, k_p, v_p, k_s, v_s, cache_k, cache_v, p, lam_init, n_prompt, seq, n_lat_req, n_lat, grid_w):
    inner = q.shape[1]
    vec = lambda n: pl.BlockSpec((1, n), lambda *_: (0, 0))
    lam_args = [p[n].reshape(1, -1) for n in ("lq1", "lk1", "lq2", "lk2")] + [p["subln_g"].reshape(1, -1)]
    lam_specs = [vec(HEAD)] * 4 + [vec(PAIR)]
    blk = pl.BlockSpec((seq, inner), lambda b: (b, 0))
    o_prompt = pl.pallas_call(
        functools.partial(_attn_prompt_kernel, lam_init=lam_init),
        out_shape=jax.ShapeDtypeStruct((n_prompt * seq, inner), F32),
        grid=(n_prompt,),
        in_specs=[blk, blk, blk] + lam_specs,
        out_specs=blk,
        compiler_params=_cparams(("arbitrary",)),
        name="diff_attn_prompt",
    )(q, k_p, v_p, *lam_args)

    past = cache_k.shape[1]
    qt = SUB_TILE
    n_qt = n_lat // qt
    q0 = n_prompt * seq // qt
    cos, sin = _rope_tables(n_lat, grid_w)
    lat = pl.BlockSpec((n_lat, inner), lambda b, t: (b, 0))
    ctx = pl.BlockSpec((None, past, inner), lambda b, t: (b, 0, 0))
    o_sample = pl.pallas_call(
        functools.partial(_attn_sample_kernel, lam_init=lam_init),
        out_shape=jax.ShapeDtypeStruct((n_lat_req * n_lat, inner), F32),
        grid=(n_lat_req, n_qt),
        in_specs=[pl.BlockSpec((qt, inner), lambda b, t: (q0 + b * n_qt + t, 0)),
                  lat, lat, ctx, ctx,
                  pl.BlockSpec((qt, PAIR), lambda b, t: (t, 0)),
                  pl.BlockSpec((qt, PAIR), lambda b, t: (t, 0)),
                  pl.BlockSpec((n_lat, PAIR), lambda b, t: (0, 0)),
                  pl.BlockSpec((n_lat, PAIR), lambda b, t: (0, 0))] + lam_specs,
        out_specs=pl.BlockSpec((qt, inner), lambda b, t: (b * n_qt + t, 0)),
        compiler_params=_cparams(("arbitrary", "arbitrary")),
        name="diff_attn_sample",
    )(q, k_s, v_s, cache_k, cache_v, cos, sin, cos, sin, *lam_args)
    return o_prompt, o_sample


def kernel(x_prompt, x_sample, state_wkv, cache_k, cache_v, c, c_ctx, mod_w, mod_b, ln1_g, ln1_b, ln2_g, ln2_b, ev_in_w, ev_pw_b, ev_ts_mu, ev_w0, ev_w_up, ev_a0, ev_a_up, ev_g_up, ev_k_k, ev_k_a, ev_r_k, ev_gn_g, ev_gn_b, ev_dw_w, ev_dw_b, ev_cln_g, ev_cln_b, ev_out_w, od_in_w, od_lq1, od_lk1, od_lq2, od_lk2, od_subln_g, od_out_w, rg_w, rg_b, re_w, re_b, ffn_w_gate, ffn_w_up, ffn_w_down):
    nbp, seq, d = x_prompt.shape
    nbs, n_lat, _ = x_sample.shape
    depth = mod_w.shape[0]
    n_prompt_tok = nbp * seq
    geom = (n_prompt_tok, n_lat)
    alpha = (2 * depth) ** 0.25
    a_width = ev_w0.shape[-1]
    a_cols = ev_ts_mu.shape[-1]
    heads = a_width // HEAD
    grid_w = 64
    assert n_prompt_tok % MOE_TILE == 0 and n_lat % MOE_TILE == 0 and n_lat % SUB_TILE == 0
    assert 1 + nbs <= 8 and n_prompt_tok % ROW_TILE == 0 and n_lat % ROW_TILE == 0

    xs = (x_prompt.reshape(n_prompt_tok, d), x_sample.reshape(nbs * n_lat, d))
    cond = jnp.concatenate([c_ctx[None, :], c, jnp.zeros((8 - 1 - nbs, d), F32)], axis=0)
    mod = _ada_table(cond, mod_w, mod_b)

    new_wkv, new_k, new_v = [], [], []
    for l in range(depth):
        last = l == depth - 1
        if l % 2 == 0:
            i = l // 2
            pa, pb = _mod_matmul(xs, mod, l, ev_in_w[i], ((a_cols, False), (ev_in_w.shape[-1] - a_cols, False)), geom)
            prm = dict(ts_mu=ev_ts_mu[i][None, :], w0=ev_w0[i], w_up=ev_w_up[i], a0=ev_a0[i], a_up=ev_a_up[i],
                       g_up=ev_g_up[i], k_k=ev_k_k[i][None, :], k_a=ev_k_a[i][None, :],
                       r_k=ev_r_k[i].reshape(1, a_width), gn_g=ev_gn_g[i][None, :], gn_b=ev_gn_b[i][None, :])
            zero_state = jnp.zeros((nbp, 2, heads, HEAD, HEAD), F32)
            ride = () if l else tuple(w.reshape(-1, w.shape[-1]) for w in (ffn_w_gate, ffn_w_up, ffn_w_down))
            oa_p, st, cast = _rwkv_mix(pa, 0, nbp, seq, zero_state, prm, ride)
            if cast:
                w_gate, w_up, w_down = (c.reshape(w.shape) for c, w in zip(cast, (ffn_w_gate, ffn_w_up, ffn_w_down)))
            oa_s, _, _ = _rwkv_mix(pa, n_prompt_tok, nbs, n_lat, state_wkv[:, i], prm)
            conv = (ev_pw_b[i], ev_dw_w[i], ev_dw_b[i], ev_cln_g[i], ev_cln_b[i])
            ob_p = _conv_module(pb, 0, nbp, seq, *conv)
            ob_s = _conv_module(pb, n_prompt_tok, nbs, n_lat, *conv)
            hs = [(oa_p, oa_s), (ob_p, ob_s)]
            out_w = ev_out_w[i]
            new_wkv.append(st)
        else:
            j = l // 2
            lam_init = 0.8 - 0.6 * math.exp(-0.3 * l)
            inner = od_in_w.shape[-1] // 3
            q, k_p, k_s, v_p, v_s = _mod_matmul(xs, mod, l, od_in_w[j],
                                                 ((inner, False), (inner, True), (inner, True)), geom)
            prm = dict(lq1=od_lq1[j], lk1=od_lk1[j], lq2=od_lq2[j], lk2=od_lk2[j], subln_g=od_subln_g[j])
            ck = cache_k[:, j].reshape(nbs, -1, inner)
            cv = cache_v[:, j].reshape(nbs, -1, inner)
            hs = [_attention(q, k_p, v_p, k_s, v_s, ck, cv, prm, lam_init, nbp, seq, nbs, n_lat, grid_w)]
            out_w = od_out_w[j]
            new_k.append(k_p.reshape(nbp, seq, inner // PAIR, PAIR))
            new_v.append(v_p.reshape(nbp, seq, inner // PAIR, PAIR))
        x = _out_proj_norm(hs, xs, mod, l, out_w, ln1_g[l], ln1_b[l], alpha, geom)
        rw = jnp.concatenate([rg_w[l], re_w[l],
                              jnp.zeros((d, ROUTER_LANES - N_GROUPS - N_EXPERTS), F32)], axis=1)
        rb = jnp.concatenate([rg_b[l], re_b[l], jnp.zeros((ROUTER_LANES - N_GROUPS - N_EXPERTS,), F32)])[None, :]
        xs = tuple(_moe_norm(x, mod, l, rw, rb, w_gate, w_up, w_down, ln2_g[l], ln2_b[l], alpha, geom,
                             split_out=last))

    y_prompt = xs[0].reshape(nbp, seq, d)
    y_sample = xs[1].reshape(nbs, n_lat, d)
    return (y_prompt, y_sample, jnp.stack(new_wkv, axis=1), jnp.stack(new_k, axis=1), jnp.stack(new_v, axis=1))
```

```python
import functools
import math

import jax
import jax.numpy as jnp
from jax import lax
from jax.experimental import pallas as pl
from jax.experimental.pallas import tpu as pltpu

F32 = jnp.float32
BF16 = jnp.bfloat16

LANES = 128
HEAD = 64
PAIR = 2 * HEAD
CHUNK = 64
RWKV_STEP_ROWS = 512
INV_BLOCK = 16
ROW_TILE = 512
SUB_TILE = 256
MOE_TILE = 1024
CONV_W = 31
CONV_TILE = 256
CONV_HALO = 16
N_GROUPS = 4
EXPERTS_PER_GROUP = 8
N_EXPERTS = N_GROUPS * EXPERTS_PER_GROUP
ROUTER_LANES = 128
EXPERTS_PER_STEP = 4
MOE_WINDOWS = (128, 256, 384, 512)
BF16_ROWS = 16
LN_EPS = 1e-5
GN_EPS = 64e-5
RMS_EPS = 1e-5
ROPE_BASE = 10000.0
EXP_M05 = math.exp(-0.5)
VMEM_LIMIT = 56 * 1024 * 1024


def _bdot(a, b):
    return jnp.dot(a.astype(BF16), b.astype(BF16), preferred_element_type=F32)


def _bdot_nt(a, b):
    return lax.dot_general(a.astype(BF16), b.astype(BF16), (((1,), (1,)), ((), ())),
                           preferred_element_type=F32)


def _split2(a):
    hi = a.astype(BF16)
    lo = (a - hi.astype(F32)).astype(BF16)
    return hi, lo


def _split3(a):
    hi = a.astype(BF16)
    r1 = a - hi.astype(F32)
    mid = r1.astype(BF16)
    lo = (r1 - mid.astype(F32)).astype(BF16)
    return hi, mid, lo


def _dot_f32ish(a, b):
    ah, al = _split2(a)
    bh, bl = _split2(b)
    d = lambda x, y: jnp.dot(x, y, preferred_element_type=F32)
    return d(ah, bh) + d(ah, bl) + d(al, bh)


def _dot_exact_lhs(m_bf16, a):
    h, m, l = _split3(a)
    d = lambda y: jnp.dot(m_bf16, y, preferred_element_type=F32)
    return d(h) + d(m) + d(l)


def _dot_ones_rhs(a, m_bf16):
    h, l = _split2(a)
    d = lambda x: jnp.dot(x, m_bf16, preferred_element_type=F32)
    return d(h) + d(l)


def _layer_norm(z, g, b, eps):
    mu = jnp.mean(z, axis=-1, keepdims=True)
    zc = z - mu
    var = jnp.mean(zc * zc, axis=-1, keepdims=True)
    return zc * lax.rsqrt(var + eps) * g + b


def _cond_row(tile, tile_rows, n_prompt_tok, dec_seq):
    tok = tile * tile_rows
    return jnp.where(tok < n_prompt_tok, 0, 1 + (tok - n_prompt_tok) // dec_seq)


def _cast_rows(src_ref, dst_ref, rows_per_step=128):
    n = src_ref.shape[0]
    for s in range(0, n, rows_per_step):
        dst_ref[s:s + rows_per_step, :] = src_ref[s:s + rows_per_step, :].astype(dst_ref.dtype)


def _cparams(sem):
    return pltpu.CompilerParams(dimension_semantics=sem, vmem_limit_bytes=VMEM_LIMIT)


def _ada_kernel(cond_ref, w_ref, b_ref, o_ref):
    c = cond_ref[...]
    s = c * jax.nn.sigmoid(c)
    o_ref[0, 0] = _dot_f32ish(s, w_ref[0]) + b_ref[0, 0]


def _ada_table(cond8, mod_w, mod_b):
    depth, d, _ = mod_w.shape
    return pl.pallas_call(
        _ada_kernel,
        out_shape=jax.ShapeDtypeStruct((depth, 6, 8, d), F32),
        grid=(depth, 6),
        in_specs=[pl.BlockSpec((8, d), lambda l, j: (0, 0)),
                  pl.BlockSpec((1, d, d), lambda l, j: (l, 0, j)),
                  pl.BlockSpec((1, 1, 1, d), lambda l, j: (l, j, 0, 0))],
        out_specs=pl.BlockSpec((1, 1, 8, d), lambda l, j: (l, j, 0, 0)),
        compiler_params=_cparams(("arbitrary", "arbitrary")),
        name="ada_table",
    )(cond8, mod_w, mod_b.reshape(depth, 6, 1, d))


def _mod_spec(layer, which, d):
    return pl.BlockSpec((None, None, 8, d), lambda *_: (layer, which, 0, 0))


def _group_specs(tile, width, n_prompt_tiles):
    return [pl.BlockSpec((tile, width), lambda i, *_: (jnp.minimum(i, n_prompt_tiles - 1), 0)),
            pl.BlockSpec((tile, width), lambda i, *_: (jnp.maximum(i - n_prompt_tiles, 0), 0))]


def _load_group(i, n_prompt_tiles, refs, rows=slice(None)):
    if len(refs) == 1:
        return refs[0][rows, :]
    return jnp.where(i < n_prompt_tiles, refs[0][rows, :], refs[1][rows, :])


def _store_group(i, n_prompt_tiles, refs, value, rows=slice(None)):
    if len(refs) == 1:
        refs[0][rows, :] = value
        return

    @pl.when(i < n_prompt_tiles)
    def _():
        refs[0][rows, :] = value

    @pl.when(i >= n_prompt_tiles)
    def _():
        refs[1][rows, :] = value


def _modmm_kernel(*refs, n_x, outs, geom):
    x_refs = refs[:n_x]
    sh_ref, sc_ref, w_ref = refs[n_x:n_x + 3]
    o_refs, wb = refs[n_x + 3:-1], refs[-1]
    i = pl.program_id(0)
    n_pt = geom[0] // ROW_TILE

    @pl.when(i == 0)
    def _():
        _cast_rows(w_ref, wb)

    r = _cond_row(i, ROW_TILE, *geom)
    h = _load_group(i, n_pt, x_refs) * (1.0 + sc_ref[pl.ds(r, 1), :]) + sh_ref[pl.ds(r, 1), :]
    hb = h.astype(BF16)
    off = 0
    k = 0
    for n, split in outs:
        y = jnp.dot(hb, wb[:, off:off + n], preferred_element_type=F32)
        _store_group(i, n_pt, o_refs[k:k + 1 + split], y)
        off += n
        k += 1 + split


def _mod_matmul(xs, mod, layer, w, outs, geom):
    d, n_out = w.shape
    n_prompt_tok = geom[0]
    n_tok = sum(x.shape[0] for x in xs)
    n_pt = n_prompt_tok // ROW_TILE
    whole = lambda n: [pl.BlockSpec((ROW_TILE, n), lambda i: (i, 0))]
    x_specs = whole(d) if len(xs) == 1 else _group_specs(ROW_TILE, d, n_pt)
    out_shape, out_specs = [], []
    for n, split in outs:
        if split:
            out_shape += [jax.ShapeDtypeStruct((n_prompt_tok, n), F32),
                          jax.ShapeDtypeStruct((n_tok - n_prompt_tok, n), F32)]
            out_specs += _group_specs(ROW_TILE, n, n_pt)
        else:
            out_shape += [jax.ShapeDtypeStruct((n_tok, n), F32)]
            out_specs += whole(n)
    return pl.pallas_call(
        functools.partial(_modmm_kernel, n_x=len(xs), outs=outs, geom=geom),
        out_shape=out_shape,
        grid=(n_tok // ROW_TILE,),
        in_specs=x_specs + [_mod_spec(layer, 0, d), _mod_spec(layer, 1, d),
                            pl.BlockSpec((d, n_out), lambda i: (0, 0), pipeline_mode=pl.Buffered(1))],
        out_specs=out_specs,
        scratch_shapes=[pltpu.VMEM((d, n_out), BF16)],
        compiler_params=_cparams(("arbitrary",)),
        name=f"mod_matmul_l{layer}",
    )(*xs, mod, mod, w)


def _outproj_kernel(*refs, arity, alpha, geom):
    groups, k = [], 0
    for n in arity:
        groups.append(refs[k:k + n])
        k += n
    gate_ref, w_ref, g_ref, b_ref, o_ref, wb = refs[k:]
    h_groups, x_group = groups[:-1], groups[-1]
    i = pl.program_id(0)
    n_pt = geom[0] // ROW_TILE

    @pl.when(i == 0)
    def _():
        _cast_rows(w_ref, wb)

    r = _cond_row(i, ROW_TILE, *geom)
    y = None
    off = 0
    for h_refs in h_groups:
        n = h_refs[0].shape[1]
        h = _load_group(i, n_pt, h_refs)
        part = jnp.dot(h.astype(BF16), wb[off:off + n, :], preferred_element_type=F32)
        y = part if y is None else y + part
        off += n
    z = alpha * _load_group(i, n_pt, x_group) + gate_ref[pl.ds(r, 1), :] * y
    o_ref[...] = _layer_norm(z, g_ref[...], b_ref[...], LN_EPS)


def _out_proj_norm(hs, xs, mod, layer, w, ln_g, ln_b, alpha, geom):
    d = w.shape[1]
    n_tok = sum(x.shape[0] for x in xs)
    n_pt = geom[0] // ROW_TILE
    specs = []
    for grp in list(hs) + [xs]:
        n = grp[0].shape[1]
        specs += [pl.BlockSpec((ROW_TILE, n), lambda i: (i, 0))] if len(grp) == 1 else _group_specs(ROW_TILE, n, n_pt)
    return pl.pallas_call(
        functools.partial(_outproj_kernel, arity=tuple(len(g) for g in list(hs) + [xs]), alpha=alpha, geom=geom),
        out_shape=jax.ShapeDtypeStruct((n_tok, d), F32),
        grid=(n_tok // ROW_TILE,),
        in_specs=specs + [
            _mod_spec(layer, 2, d),
            pl.BlockSpec(w.shape, lambda i: (0, 0), pipeline_mode=pl.Buffered(1)),
            pl.BlockSpec((1, d), lambda i: (0, 0)),
            pl.BlockSpec((1, d), lambda i: (0, 0))],
        out_specs=pl.BlockSpec((ROW_TILE, d), lambda i: (i, 0)),
        scratch_shapes=[pltpu.VMEM(w.shape, BF16)],
        compiler_params=_cparams(("arbitrary",)),
        name=f"out_proj_norm_l{layer}",
    )(*[a for g in hs for a in g], *xs, mod, w, ln_g.reshape(1, d), ln_b.reshape(1, d))


def _route(logits):
    neg = jnp.float32(-3e38)
    big = jnp.int32(1 << 20)
    lane = lax.broadcasted_iota(jnp.int32, logits.shape, 1)
    is_g = lane < N_GROUPS
    gl = jnp.where(is_g, logits, neg)
    gmax = jnp.max(gl, axis=-1, keepdims=True)
    gidx = jnp.min(jnp.where(gl == gmax, lane, big), axis=-1, keepdims=True)
    gsum = jnp.sum(jnp.where(is_g, jnp.exp(gl - gmax), 0.0), axis=-1, keepdims=True)
    g_p = 1.0 / gsum
    lo = N_GROUPS + gidx * EXPERTS_PER_GROUP
    in_grp = (lane >= lo) & (lane < lo + EXPERTS_PER_GROUP)
    el = jnp.where(in_grp, logits, neg)
    v1 = jnp.max(el, axis=-1, keepdims=True)
    i1 = jnp.min(jnp.where(el == v1, lane, big), axis=-1, keepdims=True)
    el2 = jnp.where(lane == i1, neg, el)
    v2 = jnp.max(el2, axis=-1, keepdims=True)
    i2 = jnp.min(jnp.where(el2 == v2, lane, big), axis=-1, keepdims=True)
    e21 = jnp.exp(v2 - v1)
    den = 1.0 + e21
    w1 = g_p / den
    w2 = g_p * e21 / den
    return jnp.where(lane == i1, w1, 0.0) + jnp.where(lane == i2, w2, 0.0), gidx


def _moe_kernel(x_ref, sh_ref, sc_ref, gate_ref, rw_ref, rb_ref, wg_ref, wu_ref, wd_ref, g_ref, b_ref,
                *rest, alpha, geom):
    o_refs = rest[:-6]
    h_scr, gates_scr, acc_scr, perm_scr, permt_scr, bounds = rest[-6:]
    i = pl.program_id(0)
    e = pl.program_id(1)
    r = _cond_row(i, MOE_TILE, *geom)
    n_pt = geom[0] // MOE_TILE
    sub = SUB_TILE
    n_sub = MOE_TILE // sub

    @pl.when(e == 0)
    def _():
        lane = lax.broadcasted_iota(jnp.int32, (sub, ROUTER_LANES), 1)
        ri = lax.broadcasted_iota(jnp.int32, (sub, sub), 0)
        ci = lax.broadcasted_iota(jnp.int32, (sub, sub), 1)
        tril = jnp.where(ci <= ri, 1.0, 0.0).astype(BF16)
        running = jnp.zeros((1, ROUTER_LANES), F32)
        onehots, cums = [], []
        for s in range(n_sub):
            rows = slice(s * sub, (s + 1) * sub)
            h = x_ref[rows, :] * (1.0 + sc_ref[pl.ds(r, 1), :]) + sh_ref[pl.ds(r, 1), :]
            acc_scr[rows, :] = h
            logits = _dot_f32ish(h, rw_ref[...]) + rb_ref[...]
            gates, gidx = _route(logits)
            gates_scr[rows, :] = gates
            onehot = jnp.where(lane == gidx, 1.0, 0.0)
            cum = jnp.dot(tril, onehot.astype(BF16), preferred_element_type=F32) + running
            running = cum[sub - 1:sub, :]
            onehots.append(onehot)
            cums.append(cum)
        gi = lax.broadcasted_iota(jnp.int32, (ROUTER_LANES, ROUTER_LANES), 0)
        gj = lax.broadcasted_iota(jnp.int32, (ROUTER_LANES, ROUTER_LANES), 1)
        before = jnp.where((gi < gj) & (gi < N_GROUPS), 1.0, 0.0).astype(BF16)
        starts = _dot_ones_rhs(jnp.broadcast_to(running, (8, ROUTER_LANES)), before)[0:1, :]
        for g in range(N_GROUPS + 1):
            bounds[g] = starts[0, g].astype(jnp.int32)
        dest_cols = [jnp.sum(onehots[s] * (starts + cums[s] - 1.0), axis=-1, keepdims=True) for s in range(n_sub)]
        dest_row = jnp.concatenate(
            [jnp.broadcast_to(dc, (sub, ROUTER_LANES)).T[0:1, :] for dc in dest_cols], axis=1)
        col_id = lax.broadcasted_iota(jnp.int32, (sub, MOE_TILE), 1).astype(F32)
        row_id = lax.broadcasted_iota(jnp.int32, (sub, MOE_TILE), 0).astype(F32)
        for s in range(n_sub):
            rows = slice(s * sub, (s + 1) * sub)
            perm_scr[rows, :] = jnp.where(row_id + float(s * sub) == dest_row, 1.0, 0.0).astype(BF16)
            permt_scr[rows, :] = jnp.where(dest_cols[s] == col_id, 1.0, 0.0).astype(BF16)
        perm = perm_scr[...]
        by_perm = lambda y: jnp.dot(perm, y, preferred_element_type=F32)
        h_scr[...] = by_perm(acc_scr[...].astype(BF16)).astype(BF16)
        g_hi, g_mid, g_lo = _split3(gates_scr[...])
        gates_scr[...] = by_perm(g_hi) + by_perm(g_mid) + by_perm(g_lo)
        acc_scr[...] = jnp.zeros(acc_scr.shape, F32)

    grp = e // (EXPERTS_PER_GROUP // EXPERTS_PER_STEP)
    first, end = bounds[grp], bounds[grp + 1]
    first_al = (first // BF16_ROWS) * BF16_ROWS
    step = MOE_WINDOWS[0]
    assert all(w == (k + 1) * step for k, w in enumerate(MOE_WINDOWS))

    def experts_on(lo, size):
        start = pl.multiple_of(jnp.minimum(lo, MOE_TILE - size), BF16_ROWS)
        rows = pl.ds(start, size)
        lane = lax.broadcasted_iota(jnp.int32, (size, ROUTER_LANES), 1)
        row = lax.broadcasted_iota(jnp.int32, (size, 1), 0)
        hb = h_scr[rows, :]
        gates = jnp.where(row + start >= lo, gates_scr[rows, :], 0.0)
        out = acc_scr[rows, :]
        for j in range(EXPERTS_PER_STEP):
            lane_j = N_GROUPS + e * EXPERTS_PER_STEP + j
            ge = jnp.sum(jnp.where(lane == lane_j, gates, 0.0), axis=-1, keepdims=True)
            a = jnp.dot(hb, wg_ref[j], preferred_element_type=F32)
            u = jnp.dot(hb, wu_ref[j], preferred_element_type=F32)
            hid = (a * jax.nn.sigmoid(a)) * u * ge
            out = out + jnp.dot(hid.astype(BF16), wd_ref[j], preferred_element_type=F32)
        acc_scr[rows, :] = out

    def window(w, carry):
        lo = first_al + w * MOE_WINDOWS[-1]
        n_steps = jnp.minimum((end - lo + step - 1) // step, len(MOE_WINDOWS))
        for k, size in enumerate(MOE_WINDOWS):
            pl.when(n_steps == k + 1)(functools.partial(experts_on, lo, size))
        return carry

    lax.fori_loop(0, (end - first_al + MOE_WINDOWS[-1] - 1) // MOE_WINDOWS[-1], window, 0)

    @pl.when(e == pl.num_programs(1) - 1)
    def _():
        a_hi, a_lo = _split2(acc_scr[...])
        permt = permt_scr[...]
        acc_scr[...] = (jnp.dot(permt, a_hi, preferred_element_type=F32)
                        + jnp.dot(permt, a_lo, preferred_element_type=F32))
        for s in range(n_sub):
            rows = slice(s * sub, (s + 1) * sub)
            z = alpha * x_ref[rows, :] + gate_ref[pl.ds(r, 1), :] * acc_scr[rows, :]
            _store_group(i, n_pt, o_refs, _layer_norm(z, g_ref[...], b_ref[...], LN_EPS), rows)


def _moe_norm(x, mod, layer, rw, rb, w_gate, w_up, w_down, ln_g, ln_b, alpha, geom, split_out):
    n_tok, d = x.shape
    ff = w_gate.shape[-1]
    assert w_gate.dtype == BF16 and w_up.dtype == BF16 and w_down.dtype == BF16
    const = lambda i, e: (0, 0)
    if split_out:
        out_shape = [jax.ShapeDtypeStruct((geom[0], d), F32), jax.ShapeDtypeStruct((n_tok - geom[0], d), F32)]
        out_specs = _group_specs(MOE_TILE, d, geom[0] // MOE_TILE)
    else:
        out_shape = [jax.ShapeDtypeStruct((n_tok, d), F32)]
        out_specs = [pl.BlockSpec((MOE_TILE, d), lambda i, e: (i, 0))]
    return pl.pallas_call(
        functools.partial(_moe_kernel, alpha=alpha, geom=geom),
        out_shape=out_shape,
        grid=(n_tok // MOE_TILE, N_EXPERTS // EXPERTS_PER_STEP),
        in_specs=[pl.BlockSpec((MOE_TILE, d), lambda i, e: (i, 0)),
                  _mod_spec(layer, 3, d), _mod_spec(layer, 4, d), _mod_spec(layer, 5, d),
                  pl.BlockSpec((d, ROUTER_LANES), const),
                  pl.BlockSpec((1, ROUTER_LANES), const),
                  pl.BlockSpec((None, EXPERTS_PER_STEP, d, ff), lambda i, e: (layer, e, 0, 0)),
                  pl.BlockSpec((None, EXPERTS_PER_STEP, d, ff), lambda i, e: (layer, e, 0, 0)),
                  pl.BlockSpec((None, EXPERTS_PER_STEP, ff, d), lambda i, e: (layer, e, 0, 0)),
                  pl.BlockSpec((1, d), const),
                  pl.BlockSpec((1, d), const)],
        out_specs=out_specs,
        scratch_shapes=[pltpu.VMEM((MOE_TILE, d), BF16),
                        pltpu.VMEM((MOE_TILE, ROUTER_LANES), F32),
                        pltpu.VMEM((MOE_TILE, d), F32),
                        pltpu.VMEM((MOE_TILE, MOE_TILE), BF16), pltpu.VMEM((MOE_TILE, MOE_TILE), BF16),
                        pltpu.SMEM((8,), jnp.int32)],
        compiler_params=_cparams(("arbitrary", "arbitrary")),
        name=f"moe_norm_l{layer}",
    )(x, mod, mod, mod, rw, rb, w_gate, w_up, w_down, ln_g.reshape(1, d), ln_b.reshape(1, d))


def _rwkv_kernel(r_ref, k_ref, v_ref, lora_ref, mur_ref, muk_ref, muv_ref, mul_ref,
                 w0_ref, wup_ref, a0_ref, aup_ref, gup_ref, kk_ref, ka_ref, rk_ref, gng_ref, gnb_ref, s0_ref,
                 *rest, seq_len, seqs, unroll, n_ride):
    ride_in, (out_ref, sfin_ref), ride_out = rest[:n_ride], rest[n_ride:n_ride + 2], rest[n_ride + 2:2 * n_ride + 2]
    (lw_s, kz_s, bz_s, rr_s, vv_s, kn_s, abar_s, rt_s, wt_s, vst_s, p_s, tm_s, mb_s, mk_s, nkb_s, txr_s, tx_s, gl_s,
     g_s, h_s, q_s, y0_s, y_s, st_s) = rest[2 * n_ride + 2:]
    for src, dst in zip(ride_in, ride_out):
        dst[...] = src[...].astype(dst.dtype)
    t_len = seqs * seq_len
    n_chunks = t_len // CHUNK
    seq_chunks = seq_len // CHUNK
    c2 = 2 * CHUNK

    row = lax.broadcasted_iota(jnp.int32, (t_len, 1), 0) % seq_len

    def tshift(x, mu):
        prev = jnp.where(row == 0, 0.0, pltpu.roll(x, 1, 0))
        nxt = jnp.where(row == seq_len - 1, 0.0, pltpu.roll(x, t_len - 1, 0))
        return x + mu * (0.5 * (prev + nxt) - x)

    r = tshift(r_ref[...], mur_ref[...])
    k = tshift(k_ref[...], muk_ref[...])
    v = tshift(v_ref[...], muv_ref[...])
    lo = tshift(lora_ref[...], mul_ref[...])
    wd = jnp.tanh(lo[:, 0:PAIR])
    ad = lo[:, PAIR:2 * PAIR]
    gate = _bdot(jax.nn.sigmoid(lo[:, 2 * PAIR:3 * PAIR]), gup_ref[...])

    ri = lax.broadcasted_iota(jnp.int32, (PAIR, PAIR), 0)
    ci = lax.broadcasted_iota(jnp.int32, (PAIR, PAIR), 1)
    same_head = (ri // HEAD) == (ci // HEAD)
    ones_bd = jnp.where(same_head, 1.0, 0.0).astype(BF16)
    eye = ri == ci

    kk = k * kk_ref[...]
    ssq = _dot_ones_rhs(kk * kk, ones_bd)
    kk = kk * lax.rsqrt(jnp.maximum(ssq, 1e-24))
    wup2 = wup_ref[...].reshape(PAIR, PAIR)
    aup2 = aup_ref[...].reshape(PAIR, PAIR)
    bonus = jnp.zeros((t_len, PAIR), F32)
    for z in range(2):
        zrows = (ri // HEAD) == z
        w_pre = w0_ref[z:z + 1, :] + _bdot(wd, jnp.where(zrows, wup2, 0.0))
        lw_s[z] = -EXP_M05 * jax.nn.sigmoid(w_pre)
        a = jax.nn.sigmoid(a0_ref[z:z + 1, :] + _bdot(ad, jnp.where(zrows, aup2, 0.0)))
        kz = k * (1.0 + (a - 1.0) * ka_ref[...])
        kz_s[z] = kz
        bz_s[z] = kk * a
        bonus = bonus + _dot_ones_rhs(r * kz * rk_ref[...], ones_bd) * v
    rr_s[...] = r
    vv_s[...] = v
    kn_s[...] = kk

    rt_i = ri % CHUNK
    cs_i = ci % CHUNK
    strict = (cs_i < rt_i, cs_i > rt_i)
    incl = (cs_i <= rt_i, cs_i >= rt_i)
    t64 = lax.broadcasted_iota(jnp.int32, (CHUNK, CHUNK), 0)
    s64 = lax.broadcasted_iota(jnp.int32, (CHUNK, CHUNK), 1)
    tri = (jnp.where(s64 <= t64, 1.0, 0.0).astype(BF16), jnp.where(s64 >= t64, 1.0, 0.0).astype(BF16))
    lane_head0 = lax.broadcasted_iota(jnp.int32, (CHUNK, PAIR), 1) < HEAD
    eye_f = jnp.where(eye, 1.0, 0.0)
    same16 = (ri // INV_BLOCK) == (ci // INV_BLOCK)

    def stack(x):
        return jnp.concatenate([jnp.where(lane_head0, x, 0.0), jnp.where(lane_head0, 0.0, x)], axis=0)

    def fold(x):
        return x[:CHUNK] + x[CHUNK:]

    def chunk_rows(c):
        return pl.ds(pl.multiple_of(c * CHUNK, CHUNK), CHUNK)

    def each_item(body):
        def both_directions(c, carry):
            body(0, c)
            body(1, c)
            return carry

        lax.fori_loop(0, n_chunks, both_directions, 0, unroll=unroll)

    def stage_operands(z, c):
        rows = chunk_rows(c)
        r_c, kn_c = rr_s[rows, :], kn_s[rows, :]
        lw_c, kz_c, bz_c = lw_s[z, rows, :], kz_s[z, rows, :], bz_s[z, rows, :]
        lg = _dot_exact_lhs(tri[z], lw_c)
        total = lg[CHUNK - 1:CHUNK, :] if z == 0 else lg[0:1, :]
        e_neg = jnp.exp(-lg)
        e_rem = jnp.exp(total - lg)
        abar = stack(kn_c * jnp.exp(lg - lw_c))
        rt = stack(r_c * jnp.exp(lg))
        big = _bdot_nt(jnp.concatenate([abar, rt], axis=0),
                       jnp.concatenate([stack(bz_c * e_neg), stack(kz_c * e_neg)], axis=0))
        mb = jnp.where(strict[z], big[:c2, :c2], 0.0)
        abar_s[z, c] = abar.astype(BF16)
        rt_s[z, c] = rt
        wt_s[z, c] = jnp.concatenate([stack(kz_c * e_rem), -stack(bz_c * e_rem)], axis=0).T.astype(BF16)
        gl_s[z, c] = jnp.exp(total)
        m0 = jnp.where(same16, mb, 0.0)
        tm_s[z, c] = eye_f - m0
        p_s[z, c] = m0.astype(BF16)
        mb_s[z, c] = mb.astype(BF16)
        mk_s[z, c] = jnp.where(strict[z], big[:c2, c2:], 0.0).astype(BF16)
        nkb_s[z, c] = jnp.concatenate([jnp.where(incl[z], big[c2:, c2:], 0.0),
                                       -jnp.where(incl[z], big[c2:, :c2], 0.0)], axis=1).astype(BF16)

    def stage_values(c, carry):
        vst_s[c] = stack(vv_s[chunk_rows(c), :]).astype(BF16)
        return carry

    lax.fori_loop(0, n_chunks, stage_values, 0, unroll=unroll)
    each_item(stage_operands)

    def stage_square(z, c):
        m0 = p_s[z, c]
        p_s[z, c] = jnp.dot(m0, m0, preferred_element_type=F32).astype(BF16)
        mkv = jnp.dot(mk_s[z, c], vst_s[c], preferred_element_type=F32)
        txr_s[z, c] = jnp.concatenate([abar_s[z, c], mkv.astype(BF16)], axis=1)

    each_item(stage_square)

    n_round = INV_BLOCK.bit_length() - 2
    for j in range(n_round):
        def stage_double(z, c, last=(j == n_round - 1)):
            tm, p = tm_s[z, c], p_s[z, c]
            tm_s[z, c] = tm + jnp.dot(tm.astype(BF16), p, preferred_element_type=F32)
            if not last:
                p_s[z, c] = jnp.dot(p, p, preferred_element_type=F32).astype(BF16)

        each_item(stage_double)

    size = INV_BLOCK
    while size < CHUNK:
        off_diag = ((ri // (2 * size)) == (ci // (2 * size))) & ((ri // size) != (ci // size))

        def stage_merge_rhs(z, c, off_diag=off_diag):
            cross = jnp.where(off_diag, mb_s[z, c], jnp.zeros((), BF16))
            p_s[z, c] = jnp.dot(cross, tm_s[z, c].astype(BF16), preferred_element_type=F32).astype(BF16)

        def stage_merge(z, c):
            tm = tm_s[z, c]
            tm_s[z, c] = tm - jnp.dot(tm.astype(BF16), p_s[z, c], preferred_element_type=F32)

        each_item(stage_merge_rhs)
        each_item(stage_merge)
        size *= 2

    def stage_apply(z, c):
        tx_s[z, c] = jnp.dot(tm_s[z, c].astype(BF16), txr_s[z, c], preferred_element_type=F32)

    each_item(stage_apply)

    def stage_maps(z, c):
        tx = tx_s[z, c].astype(BF16)
        v_st = vst_s[c]
        wt = wt_s[z, c]
        from_tx = jnp.dot(wt[:, c2:], tx, preferred_element_type=F32)
        h_s[z, c] = jnp.dot(wt[:, :c2], v_st, preferred_element_type=F32) + from_tx[:, PAIR:]
        g_s[z, c] = jnp.where(eye, gl_s[z, c], 0.0) + from_tx[:, :PAIR]
        nkb = nkb_s[z, c]
        from_tx = jnp.dot(nkb[:, c2:], tx, preferred_element_type=F32)
        q_s[z, c] = fold(rt_s[z, c] + from_tx[:, :PAIR])
        y0_s[z, c] = fold(jnp.dot(nkb[:, :c2], v_st, preferred_element_type=F32) + from_tx[:, PAIR:])

    each_item(stage_maps)

    zero64 = jnp.zeros((HEAD, HEAD), F32)
    chains = [(n, z) for n in range(seqs) for z in range(2)]
    for n, z in chains:
        a = jnp.concatenate([jnp.concatenate([s0_ref[n, z, 0], zero64], axis=1),
                             jnp.concatenate([zero64, s0_ref[n, z, 1]], axis=1)], axis=0)
        st_s[n, z] = a.T

    def propagate(i, carry):
        for n, z in chains:
            c = n * seq_chunks + (i if z == 0 else seq_chunks - 1 - i)
            s = st_s[n, z].astype(BF16)
            y_s[z, chunk_rows(c), :] = _bdot(q_s[z, c], s) + y0_s[z, c]
            st_s[n, z] = _bdot(g_s[z, c], s) + h_s[z, c]
        return carry

    lax.fori_loop(0, seq_chunks, propagate, 0)
    for n, z in chains:
        a = st_s[n, z].T
        sfin_ref[n, z, 0] = a[:HEAD, :HEAD]
        sfin_ref[n, z, 1] = a[HEAD:, HEAD:]

    y = y_s[0] + y_s[1]
    mu = _dot_ones_rhs(y, ones_bd) * (1.0 / HEAD)
    yc = y - mu
    var = _dot_ones_rhs(yc * yc, ones_bd) * (1.0 / HEAD)
    yn = yc * lax.rsqrt(var + GN_EPS) * gng_ref[...] + gnb_ref[...]
    out_ref[...] = (yn + bonus) * gate


def _rwkv_mix(pa, first_tok, n_seq, seq_len, s0, p, ride=()):
    a_width = p["w0"].shape[-1]
    n_pairs = a_width // PAIR
    heads = a_width // HEAD
    seqs = max(1, RWKV_STEP_ROWS // seq_len)
    rows = seqs * seq_len
    assert n_seq % seqs == 0 and first_tok % rows == 0
    t0 = first_tok // rows
    lora_w = 3 * PAIR
    lora_blk = 3 * a_width // lora_w
    n_chunks = rows // CHUNK
    col = lambda off: (lambda s, q: (t0 + s, off + q))
    vec = lambda off: (lambda s, q: (0, off + q))
    in_specs = [
        pl.BlockSpec((rows, PAIR), col(0)),
        pl.BlockSpec((rows, PAIR), col(n_pairs)),
        pl.BlockSpec((rows, PAIR), col(2 * n_pairs)),
        pl.BlockSpec((rows, lora_w), lambda s, q: (t0 + s, lora_blk)),
        pl.BlockSpec((1, PAIR), vec(0)),
        pl.BlockSpec((1, PAIR), vec(n_pairs)),
        pl.BlockSpec((1, PAIR), vec(2 * n_pairs)),
        pl.BlockSpec((1, lora_w), lambda s, q: (0, lora_blk)),
        pl.BlockSpec((2, PAIR), vec(0)),
        pl.BlockSpec((2, HEAD, PAIR), lambda s, q: (0, 0, q)),
        pl.BlockSpec((2, PAIR), vec(0)),
        pl.BlockSpec((2, HEAD, PAIR), lambda s, q: (0, 0, q)),
        pl.BlockSpec((PAIR, PAIR), vec(0)),
        pl.BlockSpec((1, PAIR), vec(0)),
        pl.BlockSpec((1, PAIR), vec(0)),
        pl.BlockSpec((1, PAIR), vec(0)),
        pl.BlockSpec((1, PAIR), vec(0)),
        pl.BlockSpec((1, PAIR), vec(0)),
        pl.BlockSpec((seqs, 2, 2, HEAD, HEAD), lambda s, q: (s, 0, q, 0, 0)),
    ]
    seq2 = lambda: pltpu.VMEM((2, rows, PAIR), F32)
    seq1 = lambda: pltpu.VMEM((rows, PAIR), F32)
    item = lambda rows, cols, dt: pltpu.VMEM((2, n_chunks, rows, cols), dt)
    n_steps = n_seq // seqs * n_pairs
    assert all(a.shape[0] % (n_steps * BF16_ROWS) == 0 for a in ride)
    ride_specs = [pl.BlockSpec((a.shape[0] // n_steps, a.shape[1]), lambda s, q: (s * n_pairs + q, 0)) for a in ride]
    out, s_fin, *cast = pl.pallas_call(
        functools.partial(_rwkv_kernel, seq_len=seq_len, seqs=seqs, unroll=min(n_chunks, 16), n_ride=len(ride)),
        out_shape=[jax.ShapeDtypeStruct((n_seq * seq_len, a_width), F32),
                   jax.ShapeDtypeStruct((n_seq, 2, heads, HEAD, HEAD), F32)]
        + [jax.ShapeDtypeStruct(a.shape, BF16) for a in ride],
        grid=(n_seq // seqs, n_pairs),
        in_specs=in_specs + ride_specs,
        out_specs=[pl.BlockSpec((rows, PAIR), lambda s, q: (s, q)),
                   pl.BlockSpec((seqs, 2, 2, HEAD, HEAD), lambda s, q: (s, 0, q, 0, 0))] + ride_specs,
        scratch_shapes=[seq2(), seq2(), seq2(), seq1(), seq1(), seq1(),
                        item(PAIR, PAIR, BF16),
                        item(PAIR, PAIR, F32),
                        item(PAIR, 2 * PAIR, BF16),
                        pltpu.VMEM((n_chunks, PAIR, PAIR), BF16),
                        item(PAIR, PAIR, BF16),
                        item(PAIR, PAIR, F32),
                        item(PAIR, PAIR, BF16),
                        item(PAIR, PAIR, BF16),
                        item(PAIR, 2 * PAIR, BF16),
                        item(PAIR, 2 * PAIR, BF16),
                        item(PAIR, 2 * PAIR, F32),
                        item(1, PAIR, F32),
                        item(PAIR, PAIR, F32), item(PAIR, PAIR, F32),
                        item(CHUNK, PAIR, F32), item(CHUNK, PAIR, F32),
                        seq2(), pltpu.VMEM((seqs, 2, PAIR, PAIR), F32)],
        compiler_params=_cparams(("arbitrary", "arbitrary")),
        name=f"rwkv_mix_t{seq_len}",
    )(pa, pa, pa, pa, p["ts_mu"], p["ts_mu"], p["ts_mu"], p["ts_mu"],
      p["w0"], p["w_up"], p["a0"], p["a_up"], p["g_up"], p["k_k"], p["k_a"], p["r_k"], p["gn_g"], p["gn_b"], s0, *ride)
    return out, s_fin, cast


def _conv_kernel(pb_ref, prev_ref, next_ref, pwb_ref, dw_ref, dwb_ref, g_ref, b_ref, o_ref, pad_s, *, geom, seq):
    width = o_ref.shape[1]
    sub = 64
    n_prompt_tok, dec_seq = geom
    tok = pl.program_id(0) * CONV_TILE
    in_prompt = tok < n_prompt_tok
    seq_len = jnp.where(in_prompt, seq, dec_seq)
    off = jnp.where(in_prompt, tok, tok - n_prompt_tok) % seq_len

    def glu(ref):
        pbv = ref[...] + pwb_ref[...]
        return pbv[:, :width] * jax.nn.sigmoid(pbv[:, width:])

    pad_s[0:CONV_HALO, :] = jnp.where(off > 0, glu(prev_ref), 0.0)
    pad_s[CONV_HALO:CONV_HALO + CONV_TILE, :] = glu(pb_ref)
    pad_s[CONV_HALO + CONV_TILE:, :] = jnp.where(off + CONV_TILE < seq_len, glu(next_ref), 0.0)
    first = CONV_HALO - CONV_W // 2
    for c in range(CONV_TILE // sub):
        acc = jnp.zeros((sub, width), F32)
        for d in range(CONV_W):
            start = c * sub + first + d
            acc = acc + pad_s[start:start + sub, :] * dw_ref[d:d + 1, :]
        y = _layer_norm(acc + dwb_ref[...], g_ref[...], b_ref[...], LN_EPS)
        o_ref[c * sub:(c + 1) * sub, :] = y * jax.nn.sigmoid(y)


def _conv_module(pb, geom, seq, pw_b, dw_w, dw_b, cln_g, cln_b):
    n_tok = pb.shape[0]
    width = dw_w.shape[1]
    per_tile = CONV_TILE // CONV_HALO
    last_halo = n_tok // CONV_HALO - 1
    const = lambda i: (0, 0)
    return pl.pallas_call(
        functools.partial(_conv_kernel, geom=geom, seq=seq),
        out_shape=jax.ShapeDtypeStruct((n_tok, width), F32),
        grid=(n_tok // CONV_TILE,),
        in_specs=[pl.BlockSpec((CONV_TILE, 2 * width), lambda i: (i, 0)),
                  pl.BlockSpec((CONV_HALO, 2 * width), lambda i: (jnp.maximum(i * per_tile - 1, 0), 0)),
                  pl.BlockSpec((CONV_HALO, 2 * width), lambda i: (jnp.minimum((i + 1) * per_tile, last_halo), 0)),
                  pl.BlockSpec((1, 2 * width), const),
                  pl.BlockSpec((CONV_W, width), const),
                  pl.BlockSpec((1, width), const),
                  pl.BlockSpec((1, width), const),
                  pl.BlockSpec((1, width), const)],
        out_specs=pl.BlockSpec((CONV_TILE, width), lambda i: (i, 0)),
        scratch_shapes=[pltpu.VMEM((CONV_TILE + 2 * CONV_HALO, width), F32)],
        compiler_params=_cparams(("arbitrary",)),
        name="conv_module",
    )(pb, pb, pb, pw_b.reshape(1, -1), dw_w, dw_b.reshape(1, -1), cln_g.reshape(1, -1), cln_b.reshape(1, -1))


def _diff_lambda(lq1, lk1, lq2, lk2, lam_init):
    dot = lambda a, b: jnp.exp(jnp.sum(a[...] * b[...], axis=-1, keepdims=True))
    return dot(lq1, lk1) - dot(lq2, lk2) + lam_init


def _diff_attend(q, keys, vals, lam, lam_init, subln):
    nq = q.shape[0]
    lane_first = lax.broadcasted_iota(jnp.int32, q.shape, 1) < HEAD
    q = q * (HEAD ** -0.5)
    q_st = jnp.concatenate([jnp.where(lane_first, q, 0.0), jnp.where(lane_first, 0.0, q)], axis=0)
    s = _bdot_nt(q_st, keys)
    s = s - jnp.max(s, axis=-1, keepdims=True)
    pexp = jnp.exp(s)
    ov = _bdot(pexp, vals) / jnp.sum(pexp, axis=-1, keepdims=True)
    o = ov[:nq] - lam * ov[nq:]
    o = o * lax.rsqrt(jnp.mean(o * o, axis=-1, keepdims=True) + RMS_EPS)
    return o * subln * (1.0 - lam_init)


def _head_cols(h):
    return slice(h * PAIR, (h + 1) * PAIR)


def _attn_prompt_kernel(q_ref, k_ref, v_ref, lq1, lk1, lq2, lk2, sub_ref, o_ref, *, lam_init):
    lam = _diff_lambda(lq1, lk1, lq2, lk2, lam_init)
    for h in range(q_ref.shape[1] // PAIR):
        c = _head_cols(h)
        o_ref[:, c] = _diff_attend(q_ref[:, c], k_ref[:, c], v_ref[:, c], lam, lam_init, sub_ref[...])


def _rope(x, cos, sin_signed):
    lane = lax.broadcasted_iota(jnp.int32, x.shape, 1)
    quarter = HEAD // 4
    partner = jnp.where((lane // quarter) % 2 == 0,
                        pltpu.roll(x, LANES - quarter, 1), pltpu.roll(x, quarter, 1))
    return x * cos + partner * sin_signed


def _attn_sample_kernel(q_ref, k_ref, v_ref, ck_ref, cv_ref, cq_ref, sq_ref, cosk_ref, sink_ref,
                        lq1, lk1, lq2, lk2, sub_ref, o_ref, *, lam_init):
    lam = _diff_lambda(lq1, lk1, lq2, lk2, lam_init)
    for h in range(q_ref.shape[1] // PAIR):
        c = _head_cols(h)
        q = _rope(q_ref[:, c], cq_ref[...], sq_ref[...])
        kl = _rope(k_ref[:, c], cosk_ref[...], sink_ref[...])
        keys = jnp.concatenate([ck_ref[:, c], kl], axis=0)
        vals = jnp.concatenate([cv_ref[:, c], v_ref[:, c]], axis=0)
        o_ref[:, c] = _diff_attend(q, keys, vals, lam, lam_init, sub_ref[...])


def _rope_tables(n_lat, grid_w):
    quarter = HEAD // 4
    inv = ROPE_BASE ** (-jnp.arange(quarter, dtype=F32) / quarter)
    t = jnp.arange(n_lat)
    row_ang = (t // grid_w).astype(F32)[:, None] * inv
    col_ang = (t % grid_w).astype(F32)[:, None] * inv
    ang = jnp.concatenate([row_ang, row_ang, col_ang, col_ang] * 2, axis=-1)
    sign = jnp.tile(jnp.concatenate([-jnp.ones(quarter, F32), jnp.ones(quarter, F32)]), 2 * HEAD // (2 * quarter))
    return jnp.cos(ang), jnp.sin(ang) * sign


def _attention(q, k_p, v_p, k_s, v_s, cache_k, cache_v, p, lam_init, n_prompt, seq, n_lat_req, n_lat, grid_w):
    inner = q.shape[1]
    vec = lambda n: pl.BlockSpec((1, n), lambda *_: (0, 0))
    lam_args = [p[n].reshape(1, -1) for n in ("lq1", "lk1", "lq2", "lk2")] + [p["subln_g"].reshape(1, -1)]
    lam_specs = [vec(HEAD)] * 4 + [vec(PAIR)]
    blk = pl.BlockSpec((seq, inner), lambda b: (b, 0))
    o_prompt = pl.pallas_call(
        functools.partial(_attn_prompt_kernel, lam_init=lam_init),
        out_shape=jax.ShapeDtypeStruct((n_prompt * seq, inner), F32),
        grid=(n_prompt,),
        in_specs=[blk, blk, blk] + lam_specs,
        out_specs=blk,
        compiler_params=_cparams(("arbitrary",)),
        name="diff_attn_prompt",
    )(q, k_p, v_p, *lam_args)

    past = cache_k.shape[1]
    qt = SUB_TILE
    n_qt = n_lat // qt
    q0 = n_prompt * seq // qt
    cos, sin = _rope_tables(n_lat, grid_w)
    lat = pl.BlockSpec((n_lat, inner), lambda b, t: (b, 0))
    ctx = pl.BlockSpec((None, past, inner), lambda b, t: (b, 0, 0))
    o_sample = pl.pallas_call(
        functools.partial(_attn_sample_kernel, lam_init=lam_init),
        out_shape=jax.ShapeDtypeStruct((n_lat_req * n_lat, inner), F32),
        grid=(n_lat_req, n_qt),
        in_specs=[pl.BlockSpec((qt, inner), lambda b, t: (q0 + b * n_qt + t, 0)),
                  lat, lat, ctx, ctx,
                  pl.BlockSpec((qt, PAIR), lambda b, t: (t, 0)),
                  pl.BlockSpec((qt, PAIR), lambda b, t: (t, 0)),
                  pl.BlockSpec((n_lat, PAIR), lambda b, t: (0, 0)),
                  pl.BlockSpec((n_lat, PAIR), lambda b, t: (0, 0))] + lam_specs,
        out_specs=pl.BlockSpec((qt, inner), lambda b, t: (b * n_qt + t, 0)),
        compiler_params=_cparams(("arbitrary", "arbitrary")),
        name="diff_attn_sample",
    )(q, k_s, v_s, cache_k, cache_v, cos, sin, cos, sin, *lam_args)
    return o_prompt, o_sample


def kernel(x_prompt, x_sample, state_wkv, cache_k, cache_v, c, c_ctx, mod_w, mod_b, ln1_g, ln1_b, ln2_g, ln2_b, ev_in_w, ev_pw_b, ev_ts_mu, ev_w0, ev_w_up, ev_a0, ev_a_up, ev_g_up, ev_k_k, ev_k_a, ev_r_k, ev_gn_g, ev_gn_b, ev_dw_w, ev_dw_b, ev_cln_g, ev_cln_b, ev_out_w, od_in_w, od_lq1, od_lk1, od_lq2, od_lk2, od_subln_g, od_out_w, rg_w, rg_b, re_w, re_b, ffn_w_gate, ffn_w_up, ffn_w_down):
    nbp, seq, d = x_prompt.shape
    nbs, n_lat, _ = x_sample.shape
    depth = mod_w.shape[0]
    n_prompt_tok = nbp * seq
    geom = (n_prompt_tok, n_lat)
    alpha = (2 * depth) ** 0.25
    a_width = ev_w0.shape[-1]
    a_cols = ev_ts_mu.shape[-1]
    heads = a_width // HEAD
    grid_w = 64
    assert n_prompt_tok % MOE_TILE == 0 and n_lat % MOE_TILE == 0 and n_lat % SUB_TILE == 0
    assert seq % CONV_TILE == 0 and n_lat % CONV_TILE == 0 and CONV_HALO >= CONV_W // 2
    assert 1 + nbs <= 8 and n_prompt_tok % ROW_TILE == 0 and n_lat % ROW_TILE == 0

    xs = (x_prompt.reshape(n_prompt_tok, d), x_sample.reshape(nbs * n_lat, d))
    cond = jnp.concatenate([c_ctx[None, :], c, jnp.zeros((8 - 1 - nbs, d), F32)], axis=0)
    mod = _ada_table(cond, mod_w, mod_b)

    new_wkv, new_k, new_v = [], [], []
    for l in range(depth):
        last = l == depth - 1
        if l % 2 == 0:
            i = l // 2
            pa, pb = _mod_matmul(xs, mod, l, ev_in_w[i], ((a_cols, False), (ev_in_w.shape[-1] - a_cols, False)), geom)
            prm = dict(ts_mu=ev_ts_mu[i][None, :], w0=ev_w0[i], w_up=ev_w_up[i], a0=ev_a0[i], a_up=ev_a_up[i],
                       g_up=ev_g_up[i], k_k=ev_k_k[i][None, :], k_a=ev_k_a[i][None, :],
                       r_k=ev_r_k[i].reshape(1, a_width), gn_g=ev_gn_g[i][None, :], gn_b=ev_gn_b[i][None, :])
            zero_state = jnp.zeros((nbp, 2, heads, HEAD, HEAD), F32)
            ride = () if l else tuple(w.reshape(-1, w.shape[-1]) for w in (ffn_w_gate, ffn_w_up, ffn_w_down))
            oa_p, st, cast = _rwkv_mix(pa, 0, nbp, seq, zero_state, prm, ride)
            if cast:
                w_gate, w_up, w_down = (c.reshape(w.shape) for c, w in zip(cast, (ffn_w_gate, ffn_w_up, ffn_w_down)))
            oa_s, _, _ = _rwkv_mix(pa, n_prompt_tok, nbs, n_lat, state_wkv[:, i], prm)
            conv = (ev_pw_b[i], ev_dw_w[i], ev_dw_b[i], ev_cln_g[i], ev_cln_b[i])
            hs = [(oa_p, oa_s), (_conv_module(pb, geom, seq, *conv),)]
            out_w = ev_out_w[i]
            new_wkv.append(st)
        else:
            j = l // 2
            lam_init = 0.8 - 0.6 * math.exp(-0.3 * l)
            inner = od_in_w.shape[-1] // 3
            q, k_p, k_s, v_p, v_s = _mod_matmul(xs, mod, l, od_in_w[j],
                                                 ((inner, False), (inner, True), (inner, True)), geom)
            prm = dict(lq1=od_lq1[j], lk1=od_lk1[j], lq2=od_lq2[j], lk2=od_lk2[j], subln_g=od_subln_g[j])
            ck = cache_k[:, j].reshape(nbs, -1, inner)
            cv = cache_v[:, j].reshape(nbs, -1, inner)
            hs = [_attention(q, k_p, v_p, k_s, v_s, ck, cv, prm, lam_init, nbp, seq, nbs, n_lat, grid_w)]
            out_w = od_out_w[j]
            new_k.append(k_p.reshape(nbp, seq, inner // PAIR, PAIR))
            new_v.append(v_p.reshape(nbp, seq, inner // PAIR, PAIR))
        x = _out_proj_norm(hs, xs, mod, l, out_w, ln1_g[l], ln1_b[l], alpha, geom)
        rw = jnp.concatenate([rg_w[l], re_w[l],
                              jnp.zeros((d, ROUTER_LANES - N_GROUPS - N_EXPERTS), F32)], axis=1)
        rb = jnp.concatenate([rg_b[l], re_b[l], jnp.zeros((ROUTER_LANES - N_GROUPS - N_EXPERTS,), F32)])[None, :]
        xs = tuple(_moe_norm(x, mod, l, rw, rb, w_gate, w_up, w_down, ln2_g[l], ln2_b[l], alpha, geom,
                             split_out=last))

    y_prompt = xs[0].reshape(nbp, seq, d)
    y_sample = xs[1].reshape(nbs, n_lat, d)
    return (y_prompt, y_sample, jnp.stack(new_wkv, axis=1), jnp.stack(new_k, axis=1), jnp.stack(new_v, axis=1))
```

```python
import functools
import math

import jax
import jax.numpy as jnp
from jax import lax
from jax.experimental import pallas as pl
from jax.experimental.pallas import tpu as pltpu

F32 = jnp.float32
BF16 = jnp.bfloat16

LANES = 128
SUBLANES = 8
HEAD = 64
PAIR = 2 * HEAD
CHUNK = 64
RWKV_STEP_ROWS = 512
INV_BLOCK = 16
ROW_TILE = 512
SUB_TILE = 256
MOE_TILE = 1024
ADA_PER_STEP = 2
CONV_W = 31
CONV_TILE = 256
CONV_HALO = 16
N_GROUPS = 4
EXPERTS_PER_GROUP = 8
N_EXPERTS = N_GROUPS * EXPERTS_PER_GROUP
ROUTER_LANES = 128
EXPERTS_PER_STEP = 4
MOE_WINDOWS = (128, 256, 384, 512)
BF16_ROWS = 16
LN_EPS = 1e-5
GN_EPS = 64e-5
RMS_EPS = 1e-5
ROPE_BASE = 10000.0
EXP_M05 = math.exp(-0.5)
VMEM_LIMIT = 56 * 1024 * 1024


def _bdot(a, b):
    return jnp.dot(a.astype(BF16), b.astype(BF16), preferred_element_type=F32)


def _bdot_nt(a, b):
    return lax.dot_general(a.astype(BF16), b.astype(BF16), (((1,), (1,)), ((), ())),
                           preferred_element_type=F32)


def _split2(a):
    hi = a.astype(BF16)
    lo = (a - hi.astype(F32)).astype(BF16)
    return hi, lo


def _split3(a):
    hi = a.astype(BF16)
    r1 = a - hi.astype(F32)
    mid = r1.astype(BF16)
    lo = (r1 - mid.astype(F32)).astype(BF16)
    return hi, mid, lo


def _dot_f32ish(a, b):
    ah, al = _split2(a)
    bh, bl = _split2(b)
    d = lambda x, y: jnp.dot(x, y, preferred_element_type=F32)
    return d(ah, bh) + d(ah, bl) + d(al, bh)


def _dot_exact_lhs(m_bf16, a):
    h, m, l = _split3(a)
    d = lambda y: jnp.dot(m_bf16, y, preferred_element_type=F32)
    return d(h) + d(m) + d(l)


def _dot_ones_rhs(a, m_bf16):
    h, l = _split2(a)
    d = lambda x: jnp.dot(x, m_bf16, preferred_element_type=F32)
    return d(h) + d(l)


def _layer_norm(z, g, b, eps):
    mu = jnp.mean(z, axis=-1, keepdims=True)
    zc = z - mu
    var = jnp.mean(zc * zc, axis=-1, keepdims=True)
    return zc * lax.rsqrt(var + eps) * g + b


def _cond_row(tile, tile_rows, n_prompt_tok, dec_seq):
    tok = tile * tile_rows
    return jnp.where(tok < n_prompt_tok, 0, 1 + (tok - n_prompt_tok) // dec_seq)


def _cast_rows(src_ref, dst_ref, rows_per_step=128):
    n = src_ref.shape[0]
    for s in range(0, n, rows_per_step):
        dst_ref[s:s + rows_per_step, :] = src_ref[s:s + rows_per_step, :].astype(dst_ref.dtype)


def _cparams(sem):
    return pltpu.CompilerParams(dimension_semantics=sem, vmem_limit_bytes=VMEM_LIMIT)


def _ada_kernel(cond_ref, w_ref, b_ref, o_ref):
    c = cond_ref[...]
    s = c * jax.nn.sigmoid(c)
    d = o_ref.shape[-1]
    y = _dot_f32ish(s, w_ref[0])
    for j in range(ADA_PER_STEP):
        o_ref[0, j] = y[:, j * d:(j + 1) * d] + b_ref[0, j]


def _ada_table(cond8, mod_w, mod_b):
    depth, d, _ = mod_w.shape
    return pl.pallas_call(
        _ada_kernel,
        out_shape=jax.ShapeDtypeStruct((depth, 6, 8, d), F32),
        grid=(depth, 6 // ADA_PER_STEP),
        in_specs=[pl.BlockSpec((8, d), lambda l, j: (0, 0)),
                  pl.BlockSpec((1, d, ADA_PER_STEP * d), lambda l, j: (l, 0, j)),
                  pl.BlockSpec((1, ADA_PER_STEP, 1, d), lambda l, j: (l, j, 0, 0))],
        out_specs=pl.BlockSpec((1, ADA_PER_STEP, 8, d), lambda l, j: (l, j, 0, 0)),
        compiler_params=_cparams(("arbitrary", "arbitrary")),
        name="ada_table",
    )(cond8, mod_w, mod_b.reshape(depth, 6, 1, d))


def _mod_spec(layer, which, d):
    return pl.BlockSpec((None, None, 8, d), lambda *_: (layer, which, 0, 0))


def _group_specs(tile, width, n_prompt_tiles):
    return [pl.BlockSpec((tile, width), lambda i, *_: (jnp.minimum(i, n_prompt_tiles - 1), 0)),
            pl.BlockSpec((tile, width), lambda i, *_: (jnp.maximum(i - n_prompt_tiles, 0), 0))]


def _load_group(i, n_prompt_tiles, refs, rows=slice(None)):
    if len(refs) == 1:
        return refs[0][rows, :]
    return jnp.where(i < n_prompt_tiles, refs[0][rows, :], refs[1][rows, :])


def _store_group(i, n_prompt_tiles, refs, value, rows=slice(None)):
    if len(refs) == 1:
        refs[0][rows, :] = value
        return

    @pl.when(i < n_prompt_tiles)
    def _():
        refs[0][rows, :] = value

    @pl.when(i >= n_prompt_tiles)
    def _():
        refs[1][rows, :] = value


def _modmm_kernel(*refs, n_x, outs, geom):
    x_refs = refs[:n_x]
    sh_ref, sc_ref, w_ref = refs[n_x:n_x + 3]
    o_refs, wb = refs[n_x + 3:-1], refs[-1]
    i = pl.program_id(0)
    n_pt = geom[0] // ROW_TILE

    @pl.when(i == 0)
    def _():
        _cast_rows(w_ref, wb)

    r = _cond_row(i, ROW_TILE, *geom)
    h = _load_group(i, n_pt, x_refs) * (1.0 + sc_ref[pl.ds(r, 1), :]) + sh_ref[pl.ds(r, 1), :]
    hb = h.astype(BF16)
    off = 0
    k = 0
    for n, split in outs:
        y = jnp.dot(hb, wb[:, off:off + n], preferred_element_type=F32)
        _store_group(i, n_pt, o_refs[k:k + 1 + split], y)
        off += n
        k += 1 + split


def _mod_matmul(xs, mod, layer, w, outs, geom):
    d, n_out = w.shape
    n_prompt_tok = geom[0]
    n_tok = sum(x.shape[0] for x in xs)
    n_pt = n_prompt_tok // ROW_TILE
    whole = lambda n: [pl.BlockSpec((ROW_TILE, n), lambda i: (i, 0))]
    x_specs = whole(d) if len(xs) == 1 else _group_specs(ROW_TILE, d, n_pt)
    out_shape, out_specs = [], []
    for n, split in outs:
        if split:
            out_shape += [jax.ShapeDtypeStruct((n_prompt_tok, n), F32),
                          jax.ShapeDtypeStruct((n_tok - n_prompt_tok, n), F32)]
            out_specs += _group_specs(ROW_TILE, n, n_pt)
        else:
            out_shape += [jax.ShapeDtypeStruct((n_tok, n), F32)]
            out_specs += whole(n)
    return pl.pallas_call(
        functools.partial(_modmm_kernel, n_x=len(xs), outs=outs, geom=geom),
        out_shape=out_shape,
        grid=(n_tok // ROW_TILE,),
        in_specs=x_specs + [_mod_spec(layer, 0, d), _mod_spec(layer, 1, d),
                            pl.BlockSpec((d, n_out), lambda i: (0, 0), pipeline_mode=pl.Buffered(1))],
        out_specs=out_specs,
        scratch_shapes=[pltpu.VMEM((d, n_out), BF16)],
        compiler_params=_cparams(("arbitrary",)),
        name=f"mod_matmul_l{layer}",
    )(*xs, mod, mod, w)


def _outproj_kernel(*refs, arity, alpha, geom):
    groups, k = [], 0
    for n in arity:
        groups.append(refs[k:k + n])
        k += n
    gate_ref, w_ref, g_ref, b_ref, o_ref, wb = refs[k:]
    h_groups, x_group = groups[:-1], groups[-1]
    i = pl.program_id(0)
    n_pt = geom[0] // ROW_TILE

    @pl.when(i == 0)
    def _():
        _cast_rows(w_ref, wb)

    r = _cond_row(i, ROW_TILE, *geom)
    y = None
    off = 0
    for h_refs in h_groups:
        n = h_refs[0].shape[1]
        h = _load_group(i, n_pt, h_refs)
        part = jnp.dot(h.astype(BF16), wb[off:off + n, :], preferred_element_type=F32)
        y = part if y is None else y + part
        off += n
    z = alpha * _load_group(i, n_pt, x_group) + gate_ref[pl.ds(r, 1), :] * y
    o_ref[...] = _layer_norm(z, g_ref[...], b_ref[...], LN_EPS)


def _out_proj_norm(hs, xs, mod, layer, w, ln_g, ln_b, alpha, geom):
    d = w.shape[1]
    n_tok = sum(x.shape[0] for x in xs)
    n_pt = geom[0] // ROW_TILE
    specs = []
    for grp in list(hs) + [xs]:
        n = grp[0].shape[1]
        specs += [pl.BlockSpec((ROW_TILE, n), lambda i: (i, 0))] if len(grp) == 1 else _group_specs(ROW_TILE, n, n_pt)
    return pl.pallas_call(
        functools.partial(_outproj_kernel, arity=tuple(len(g) for g in list(hs) + [xs]), alpha=alpha, geom=geom),
        out_shape=jax.ShapeDtypeStruct((n_tok, d), F32),
        grid=(n_tok // ROW_TILE,),
        in_specs=specs + [
            _mod_spec(layer, 2, d),
            pl.BlockSpec(w.shape, lambda i: (0, 0), pipeline_mode=pl.Buffered(1)),
            pl.BlockSpec((1, d), lambda i: (0, 0)),
            pl.BlockSpec((1, d), lambda i: (0, 0))],
        out_specs=pl.BlockSpec((ROW_TILE, d), lambda i: (i, 0)),
        scratch_shapes=[pltpu.VMEM(w.shape, BF16)],
        compiler_params=_cparams(("arbitrary",)),
        name=f"out_proj_norm_l{layer}",
    )(*[a for g in hs for a in g], *xs, mod, w, ln_g.reshape(1, d), ln_b.reshape(1, d))


def _route(logits):
    neg = jnp.float32(-3e38)
    big = jnp.int32(1 << 20)
    lane = lax.broadcasted_iota(jnp.int32, logits.shape, 1)
    is_g = lane < N_GROUPS
    gl = jnp.where(is_g, logits, neg)
    gmax = jnp.max(gl, axis=-1, keepdims=True)
    gidx = jnp.min(jnp.where(gl == gmax, lane, big), axis=-1, keepdims=True)
    gsum = jnp.sum(jnp.where(is_g, jnp.exp(gl - gmax), 0.0), axis=-1, keepdims=True)
    g_p = 1.0 / gsum
    lo = N_GROUPS + gidx * EXPERTS_PER_GROUP
    in_grp = (lane >= lo) & (lane < lo + EXPERTS_PER_GROUP)
    el = jnp.where(in_grp, logits, neg)
    v1 = jnp.max(el, axis=-1, keepdims=True)
    i1 = jnp.min(jnp.where(el == v1, lane, big), axis=-1, keepdims=True)
    el2 = jnp.where(lane == i1, neg, el)
    v2 = jnp.max(el2, axis=-1, keepdims=True)
    i2 = jnp.min(jnp.where(el2 == v2, lane, big), axis=-1, keepdims=True)
    e21 = jnp.exp(v2 - v1)
    den = 1.0 + e21
    w1 = g_p / den
    w2 = g_p * e21 / den
    return jnp.where(lane == i1, w1, 0.0) + jnp.where(lane == i2, w2, 0.0), gidx


def _moe_kernel(x_ref, sh_ref, sc_ref, gate_ref, rw_ref, rb_ref, wg_ref, wu_ref, wd_ref, g_ref, b_ref,
                *rest, alpha, geom):
    o_refs = rest[:-6]
    h_scr, gates_scr, acc_scr, perm_scr, permt_scr, bounds = rest[-6:]
    i = pl.program_id(0)
    e = pl.program_id(1)
    r = _cond_row(i, MOE_TILE, *geom)
    n_pt = geom[0] // MOE_TILE
    sub = SUB_TILE
    n_sub = MOE_TILE // sub

    @pl.when(e == 0)
    def _():
        lane = lax.broadcasted_iota(jnp.int32, (sub, ROUTER_LANES), 1)
        ri = lax.broadcasted_iota(jnp.int32, (sub, sub), 0)
        ci = lax.broadcasted_iota(jnp.int32, (sub, sub), 1)
        tril = jnp.where(ci <= ri, 1.0, 0.0).astype(BF16)
        running = jnp.zeros((1, ROUTER_LANES), F32)
        onehots, cums = [], []
        for s in range(n_sub):
            rows = slice(s * sub, (s + 1) * sub)
            h = x_ref[rows, :] * (1.0 + sc_ref[pl.ds(r, 1), :]) + sh_ref[pl.ds(r, 1), :]
            acc_scr[rows, :] = h
            logits = _dot_f32ish(h, rw_ref[...]) + rb_ref[...]
            gates, gidx = _route(logits)
            gates_scr[rows, :] = gates
            onehot = jnp.where(lane == gidx, 1.0, 0.0)
            cum = jnp.dot(tril, onehot.astype(BF16), preferred_element_type=F32) + running
            running = cum[sub - 1:sub, :]
            onehots.append(onehot)
            cums.append(cum)
        gi = lax.broadcasted_iota(jnp.int32, (ROUTER_LANES, ROUTER_LANES), 0)
        gj = lax.broadcasted_iota(jnp.int32, (ROUTER_LANES, ROUTER_LANES), 1)
        before = jnp.where((gi < gj) & (gi < N_GROUPS), 1.0, 0.0).astype(BF16)
        starts = _dot_ones_rhs(jnp.broadcast_to(running, (8, ROUTER_LANES)), before)[0:1, :]
        for g in range(N_GROUPS + 1):
            bounds[g] = starts[0, g].astype(jnp.int32)
        dest_cols = [jnp.sum(onehots[s] * (starts + cums[s] - 1.0), axis=-1, keepdims=True) for s in range(n_sub)]
        dest_row = jnp.concatenate(
            [jnp.broadcast_to(dc, (sub, ROUTER_LANES)).T[0:1, :] for dc in dest_cols], axis=1)
        col_id = lax.broadcasted_iota(jnp.int32, (sub, MOE_TILE), 1).astype(F32)
        row_id = lax.broadcasted_iota(jnp.int32, (sub, MOE_TILE), 0).astype(F32)
        for s in range(n_sub):
            rows = slice(s * sub, (s + 1) * sub)
            perm_scr[rows, :] = jnp.where(row_id + float(s * sub) == dest_row, 1.0, 0.0).astype(BF16)
            permt_scr[rows, :] = jnp.where(dest_cols[s] == col_id, 1.0, 0.0).astype(BF16)
        perm = perm_scr[...]
        by_perm = lambda y: jnp.dot(perm, y, preferred_element_type=F32)
        h_scr[...] = by_perm(acc_scr[...].astype(BF16)).astype(BF16)
        g_hi, g_lo = _split2(gates_scr[...])
        gates_scr[...] = by_perm(g_hi) + by_perm(g_lo)
        acc_scr[...] = jnp.zeros(acc_scr.shape, F32)

    grp = e // (EXPERTS_PER_GROUP // EXPERTS_PER_STEP)
    first, end = bounds[grp], bounds[grp + 1]
    first_al = (first // BF16_ROWS) * BF16_ROWS
    step = MOE_WINDOWS[0]
    assert all(w == (k + 1) * step for k, w in enumerate(MOE_WINDOWS))

    def experts_on(lo, size):
        start = pl.multiple_of(jnp.minimum(lo, MOE_TILE - size), BF16_ROWS)
        rows = pl.ds(start, size)
        lane = lax.broadcasted_iota(jnp.int32, (size, ROUTER_LANES), 1)
        row = lax.broadcasted_iota(jnp.int32, (size, 1), 0)
        hb = h_scr[rows, :]
        gates = jnp.where(row + start >= lo, gates_scr[rows, :], 0.0)
        out = acc_scr[rows, :]
        for j in range(EXPERTS_PER_STEP):
            lane_j = N_GROUPS + e * EXPERTS_PER_STEP + j
            ge = jnp.sum(jnp.where(lane == lane_j, gates, 0.0), axis=-1, keepdims=True)
            a = jnp.dot(hb, wg_ref[j], preferred_element_type=F32)
            u = jnp.dot(hb, wu_ref[j], preferred_element_type=F32)
            hid = (a * jax.nn.sigmoid(a)) * u * ge
            out = out + jnp.dot(hid.astype(BF16), wd_ref[j], preferred_element_type=F32)
        acc_scr[rows, :] = out

    def window(w, carry):
        lo = first_al + w * MOE_WINDOWS[-1]
        n_steps = jnp.minimum((end - lo + step - 1) // step, len(MOE_WINDOWS))
        for k, size in enumerate(MOE_WINDOWS):
            pl.when(n_steps == k + 1)(functools.partial(experts_on, lo, size))
        return carry

    lax.fori_loop(0, (end - first_al + MOE_WINDOWS[-1] - 1) // MOE_WINDOWS[-1], window, 0)

    @pl.when(e == pl.num_programs(1) - 1)
    def _():
        a_hi, a_lo = _split2(acc_scr[...])
        permt = permt_scr[...]
        acc_scr[...] = (jnp.dot(permt, a_hi, preferred_element_type=F32)
                        + jnp.dot(permt, a_lo, preferred_element_type=F32))
        for s in range(n_sub):
            rows = slice(s * sub, (s + 1) * sub)
            z = alpha * x_ref[rows, :] + gate_ref[pl.ds(r, 1), :] * acc_scr[rows, :]
            _store_group(i, n_pt, o_refs, _layer_norm(z, g_ref[...], b_ref[...], LN_EPS), rows)


def _moe_norm(x, mod, layer, rw, rb, w_gate, w_up, w_down, ln_g, ln_b, alpha, geom, split_out):
    n_tok, d = x.shape
    ff = w_gate.shape[-1]
    assert w_gate.dtype == BF16 and w_up.dtype == BF16 and w_down.dtype == BF16
    const = lambda i, e: (0, 0)
    if split_out:
        out_shape = [jax.ShapeDtypeStruct((geom[0], d), F32), jax.ShapeDtypeStruct((n_tok - geom[0], d), F32)]
        out_specs = _group_specs(MOE_TILE, d, geom[0] // MOE_TILE)
    else:
        out_shape = [jax.ShapeDtypeStruct((n_tok, d), F32)]
        out_specs = [pl.BlockSpec((MOE_TILE, d), lambda i, e: (i, 0))]
    return pl.pallas_call(
        functools.partial(_moe_kernel, alpha=alpha, geom=geom),
        out_shape=out_shape,
        grid=(n_tok // MOE_TILE, N_EXPERTS // EXPERTS_PER_STEP),
        in_specs=[pl.BlockSpec((MOE_TILE, d), lambda i, e: (i, 0)),
                  _mod_spec(layer, 3, d), _mod_spec(layer, 4, d), _mod_spec(layer, 5, d),
                  pl.BlockSpec((d, ROUTER_LANES), const),
                  pl.BlockSpec((1, ROUTER_LANES), const),
                  pl.BlockSpec((None, EXPERTS_PER_STEP, d, ff), lambda i, e: (layer, e, 0, 0)),
                  pl.BlockSpec((None, EXPERTS_PER_STEP, d, ff), lambda i, e: (layer, e, 0, 0)),
                  pl.BlockSpec((None, EXPERTS_PER_STEP, ff, d), lambda i, e: (layer, e, 0, 0)),
                  pl.BlockSpec((1, d), const),
                  pl.BlockSpec((1, d), const)],
        out_specs=out_specs,
        scratch_shapes=[pltpu.VMEM((MOE_TILE, d), BF16),
                        pltpu.VMEM((MOE_TILE, ROUTER_LANES), F32),
                        pltpu.VMEM((MOE_TILE, d), F32),
                        pltpu.VMEM((MOE_TILE, MOE_TILE), BF16), pltpu.VMEM((MOE_TILE, MOE_TILE), BF16),
                        pltpu.SMEM((8,), jnp.int32)],
        compiler_params=_cparams(("arbitrary", "arbitrary")),
        name=f"moe_norm_l{layer}",
    )(x, mod, mod, mod, rw, rb, w_gate, w_up, w_down, ln_g.reshape(1, d), ln_b.reshape(1, d))


def _rwkv_kernel(r_ref, k_ref, v_ref, lora_ref, mur_ref, muk_ref, muv_ref, mul_ref,
                 w0_ref, wup_ref, a0_ref, aup_ref, gup_ref, kk_ref, ka_ref, rk_ref, gng_ref, gnb_ref, s0_ref,
                 *rest, seq_len, seqs, unroll, n_ride):
    ride_in, (out_ref, sfin_ref), ride_out = rest[:n_ride], rest[n_ride:n_ride + 2], rest[n_ride + 2:2 * n_ride + 2]
    (lw_s, kz_s, bz_s, rr_s, vv_s, kn_s, abar_s, rt_s, wt_s, vst_s, p_s, tm_s, mb_s, mk_s, nkb_s, txr_s, tx_s, gl_s,
     g_s, h_s, q_s, y0_s, y_s, st_s) = rest[2 * n_ride + 2:]
    for src, dst in zip(ride_in, ride_out):
        dst[...] = src[...].astype(dst.dtype)
    t_len = seqs * seq_len
    n_chunks = t_len // CHUNK
    seq_chunks = seq_len // CHUNK
    c2 = 2 * CHUNK

    row = lax.broadcasted_iota(jnp.int32, (t_len, 1), 0) % seq_len

    def tshift(x, mu):
        prev = jnp.where(row == 0, 0.0, pltpu.roll(x, 1, 0))
        nxt = jnp.where(row == seq_len - 1, 0.0, pltpu.roll(x, t_len - 1, 0))
        return x + mu * (0.5 * (prev + nxt) - x)

    r = tshift(r_ref[...], mur_ref[...])
    k = tshift(k_ref[...], muk_ref[...])
    v = tshift(v_ref[...], muv_ref[...])
    lo = tshift(lora_ref[...], mul_ref[...])
    wd = jnp.tanh(lo[:, 0:PAIR])
    ad = lo[:, PAIR:2 * PAIR]
    gate = _bdot(jax.nn.sigmoid(lo[:, 2 * PAIR:3 * PAIR]), gup_ref[...])

    ri = lax.broadcasted_iota(jnp.int32, (PAIR, PAIR), 0)
    ci = lax.broadcasted_iota(jnp.int32, (PAIR, PAIR), 1)
    same_head = (ri // HEAD) == (ci // HEAD)
    ones_bd = jnp.where(same_head, 1.0, 0.0).astype(BF16)
    eye = ri == ci

    kk = k * kk_ref[...]
    ssq = _dot_ones_rhs(kk * kk, ones_bd)
    kk = kk * lax.rsqrt(jnp.maximum(ssq, 1e-24))
    wup2 = wup_ref[...].reshape(PAIR, PAIR)
    aup2 = aup_ref[...].reshape(PAIR, PAIR)
    bonus = jnp.zeros((t_len, PAIR), F32)
    for z in range(2):
        zrows = (ri // HEAD) == z
        w_pre = w0_ref[z:z + 1, :] + _bdot(wd, jnp.where(zrows, wup2, 0.0))
        lw_s[z] = -EXP_M05 * jax.nn.sigmoid(w_pre)
        a = jax.nn.sigmoid(a0_ref[z:z + 1, :] + _bdot(ad, jnp.where(zrows, aup2, 0.0)))
        kz = k * (1.0 + (a - 1.0) * ka_ref[...])
        kz_s[z] = kz
        bz_s[z] = kk * a
        bonus = bonus + _dot_ones_rhs(r * kz * rk_ref[...], ones_bd) * v
    rr_s[...] = r
    vv_s[...] = v
    kn_s[...] = kk

    rt_i = ri % CHUNK
    cs_i = ci % CHUNK
    strict = (cs_i < rt_i, cs_i > rt_i)
    incl = (cs_i <= rt_i, cs_i >= rt_i)
    t64 = lax.broadcasted_iota(jnp.int32, (CHUNK, CHUNK), 0)
    s64 = lax.broadcasted_iota(jnp.int32, (CHUNK, CHUNK), 1)
    tri = (jnp.where(s64 <= t64, 1.0, 0.0).astype(BF16), jnp.where(s64 >= t64, 1.0, 0.0).astype(BF16))
    lane_head0 = lax.broadcasted_iota(jnp.int32, (CHUNK, PAIR), 1) < HEAD
    eye_f = jnp.where(eye, 1.0, 0.0)
    same16 = (ri // INV_BLOCK) == (ci // INV_BLOCK)

    def stack(x):
        return jnp.concatenate([jnp.where(lane_head0, x, 0.0), jnp.where(lane_head0, 0.0, x)], axis=0)

    def fold(x):
        return x[:CHUNK] + x[CHUNK:]

    def chunk_rows(c):
        return pl.ds(pl.multiple_of(c * CHUNK, CHUNK), CHUNK)

    def each_item(body):
        def both_directions(c, carry):
            body(0, c)
            body(1, c)
            return carry

        lax.fori_loop(0, n_chunks, both_directions, 0, unroll=unroll)

    def stage_operands(z, c):
        rows = chunk_rows(c)
        r_c, kn_c = rr_s[rows, :], kn_s[rows, :]
        lw_c, kz_c, bz_c = lw_s[z, rows, :], kz_s[z, rows, :], bz_s[z, rows, :]
        lg = _dot_exact_lhs(tri[z], lw_c)
        total = lg[CHUNK - 1:CHUNK, :] if z == 0 else lg[0:1, :]
        e_neg = jnp.exp(-lg)
        e_rem = jnp.exp(total - lg)
        abar = stack(kn_c * jnp.exp(lg - lw_c))
        rt = stack(r_c * jnp.exp(lg))
        big = _bdot_nt(jnp.concatenate([abar, rt], axis=0),
                       jnp.concatenate([stack(bz_c * e_neg), stack(kz_c * e_neg)], axis=0))
        mb = jnp.where(strict[z], big[:c2, :c2], 0.0)
        abar_s[z, c] = abar.astype(BF16)
        rt_s[z, c] = rt
        wt_s[z, c] = jnp.concatenate([stack(kz_c * e_rem), -stack(bz_c * e_rem)], axis=0).T.astype(BF16)
        gl_s[z, c] = jnp.exp(total)
        m0 = jnp.where(same16, mb, 0.0)
        tm_s[z, c] = eye_f - m0
        p_s[z, c] = m0.astype(BF16)
        mb_s[z, c] = mb.astype(BF16)
        mk_s[z, c] = jnp.where(strict[z], big[:c2, c2:], 0.0).astype(BF16)
        nkb_s[z, c] = jnp.concatenate([jnp.where(incl[z], big[c2:, c2:], 0.0),
                                       -jnp.where(incl[z], big[c2:, :c2], 0.0)], axis=1).astype(BF16)

    def stage_values(c, carry):
        vst_s[c] = stack(vv_s[chunk_rows(c), :]).astype(BF16)
        return carry

    lax.fori_loop(0, n_chunks, stage_values, 0, unroll=unroll)
    each_item(stage_operands)

    def stage_square(z, c):
        m0 = p_s[z, c]
        p_s[z, c] = jnp.dot(m0, m0, preferred_element_type=F32).astype(BF16)
        mkv = jnp.dot(mk_s[z, c], vst_s[c], preferred_element_type=F32)
        txr_s[z, c] = jnp.concatenate([abar_s[z, c], mkv.astype(BF16)], axis=1)

    each_item(stage_square)

    n_round = INV_BLOCK.bit_length() - 2
    for j in range(n_round):
        def stage_double(z, c, last=(j == n_round - 1)):
            tm, p = tm_s[z, c], p_s[z, c]
            tm_s[z, c] = tm + jnp.dot(tm.astype(BF16), p, preferred_element_type=F32)
            if not last:
                p_s[z, c] = jnp.dot(p, p, preferred_element_type=F32).astype(BF16)

        each_item(stage_double)

    size = INV_BLOCK
    while size < CHUNK:
        off_diag = ((ri // (2 * size)) == (ci // (2 * size))) & ((ri // size) != (ci // size))

        def stage_merge_rhs(z, c, off_diag=off_diag):
            cross = jnp.where(off_diag, mb_s[z, c], jnp.zeros((), BF16))
            p_s[z, c] = jnp.dot(cross, tm_s[z, c].astype(BF16), preferred_element_type=F32).astype(BF16)

        def stage_merge(z, c):
            tm = tm_s[z, c]
            tm_s[z, c] = tm - jnp.dot(tm.astype(BF16), p_s[z, c], preferred_element_type=F32)

        each_item(stage_merge_rhs)
        each_item(stage_merge)
        size *= 2

    def stage_apply(z, c):
        tx_s[z, c] = jnp.dot(tm_s[z, c].astype(BF16), txr_s[z, c], preferred_element_type=F32)

    each_item(stage_apply)

    def stage_maps(z, c):
        tx = tx_s[z, c].astype(BF16)
        v_st = vst_s[c]
        wt = wt_s[z, c]
        from_tx = jnp.dot(wt[:, c2:], tx, preferred_element_type=F32)
        h_s[z, c] = jnp.dot(wt[:, :c2], v_st, preferred_element_type=F32) + from_tx[:, PAIR:]
        g_s[z, c] = jnp.where(eye, gl_s[z, c], 0.0) + from_tx[:, :PAIR]
        nkb = nkb_s[z, c]
        from_tx = jnp.dot(nkb[:, c2:], tx, preferred_element_type=F32)
        q_s[z, c] = fold(rt_s[z, c] + from_tx[:, :PAIR])
        y0_s[z, c] = fold(jnp.dot(nkb[:, :c2], v_st, preferred_element_type=F32) + from_tx[:, PAIR:])

    each_item(stage_maps)

    zero64 = jnp.zeros((HEAD, HEAD), F32)
    chains = [(n, z) for n in range(seqs) for z in range(2)]
    for n, z in chains:
        a = jnp.concatenate([jnp.concatenate([s0_ref[n, z, 0], zero64], axis=1),
                             jnp.concatenate([zero64, s0_ref[n, z, 1]], axis=1)], axis=0)
        st_s[n, z] = a.T

    def propagate(i, carry):
        for n, z in chains:
            c = n * seq_chunks + (i if z == 0 else seq_chunks - 1 - i)
            s = st_s[n, z].astype(BF16)
            y_s[z, chunk_rows(c), :] = _bdot(q_s[z, c], s) + y0_s[z, c]
            st_s[n, z] = _bdot(g_s[z, c], s) + h_s[z, c]
        return carry

    lax.fori_loop(0, seq_chunks, propagate, 0)
    for n, z in chains:
        a = st_s[n, z].T
        sfin_ref[n, z, 0] = a[:HEAD, :HEAD]
        sfin_ref[n, z, 1] = a[HEAD:, HEAD:]

    y = y_s[0] + y_s[1]
    mu = _dot_ones_rhs(y, ones_bd) * (1.0 / HEAD)
    yc = y - mu
    var = _dot_ones_rhs(yc * yc, ones_bd) * (1.0 / HEAD)
    yn = yc * lax.rsqrt(var + GN_EPS) * gng_ref[...] + gnb_ref[...]
    out_ref[...] = (yn + bonus) * gate


def _rwkv_mix(pa, first_tok, n_seq, seq_len, s0, p, ride=()):
    a_width = p["w0"].shape[-1]
    n_pairs = a_width // PAIR
    heads = a_width // HEAD
    seqs = max(1, RWKV_STEP_ROWS // seq_len)
    rows = seqs * seq_len
    assert n_seq % seqs == 0 and first_tok % rows == 0
    t0 = first_tok // rows
    lora_w = 3 * PAIR
    lora_blk = 3 * a_width // lora_w
    n_chunks = rows // CHUNK
    col = lambda off: (lambda s, q: (t0 + s, off + q))
    vec = lambda off: (lambda s, q: (0, off + q))
    in_specs = [
        pl.BlockSpec((rows, PAIR), col(0)),
        pl.BlockSpec((rows, PAIR), col(n_pairs)),
        pl.BlockSpec((rows, PAIR), col(2 * n_pairs)),
        pl.BlockSpec((rows, lora_w), lambda s, q: (t0 + s, lora_blk)),
        pl.BlockSpec((1, PAIR), vec(0)),
        pl.BlockSpec((1, PAIR), vec(n_pairs)),
        pl.BlockSpec((1, PAIR), vec(2 * n_pairs)),
        pl.BlockSpec((1, lora_w), lambda s, q: (0, lora_blk)),
        pl.BlockSpec((2, PAIR), vec(0)),
        pl.BlockSpec((2, HEAD, PAIR), lambda s, q: (0, 0, q)),
        pl.BlockSpec((2, PAIR), vec(0)),
        pl.BlockSpec((2, HEAD, PAIR), lambda s, q: (0, 0, q)),
        pl.BlockSpec((PAIR, PAIR), vec(0)),
        pl.BlockSpec((1, PAIR), vec(0)),
        pl.BlockSpec((1, PAIR), vec(0)),
        pl.BlockSpec((1, PAIR), vec(0)),
        pl.BlockSpec((1, PAIR), vec(0)),
        pl.BlockSpec((1, PAIR), vec(0)),
        pl.BlockSpec((seqs, 2, 2, HEAD, HEAD), lambda s, q: (s, 0, q, 0, 0)),
    ]
    seq2 = lambda: pltpu.VMEM((2, rows, PAIR), F32)
    seq1 = lambda: pltpu.VMEM((rows, PAIR), F32)
    item = lambda rows, cols, dt: pltpu.VMEM((2, n_chunks, rows, cols), dt)
    n_steps = n_seq // seqs * n_pairs
    assert all(a.shape[0] % (n_steps * BF16_ROWS) == 0 for a in ride)
    ride_specs = [pl.BlockSpec((a.shape[0] // n_steps, a.shape[1]), lambda s, q: (s * n_pairs + q, 0)) for a in ride]
    out, s_fin, *cast = pl.pallas_call(
        functools.partial(_rwkv_kernel, seq_len=seq_len, seqs=seqs, unroll=min(n_chunks, 16), n_ride=len(ride)),
        out_shape=[jax.ShapeDtypeStruct((n_seq * seq_len, a_width), F32),
                   jax.ShapeDtypeStruct((n_seq, 2, heads, HEAD, HEAD), F32)]
        + [jax.ShapeDtypeStruct(a.shape, BF16) for a in ride],
        grid=(n_seq // seqs, n_pairs),
        in_specs=in_specs + ride_specs,
        out_specs=[pl.BlockSpec((rows, PAIR), lambda s, q: (s, q)),
                   pl.BlockSpec((seqs, 2, 2, HEAD, HEAD), lambda s, q: (s, 0, q, 0, 0))] + ride_specs,
        scratch_shapes=[seq2(), seq2(), seq2(), seq1(), seq1(), seq1(),
                        item(PAIR, PAIR, BF16),
                        item(PAIR, PAIR, F32),
                        item(PAIR, 2 * PAIR, BF16),
                        pltpu.VMEM((n_chunks, PAIR, PAIR), BF16),
                        item(PAIR, PAIR, BF16),
                        item(PAIR, PAIR, F32),
                        item(PAIR, PAIR, BF16),
                        item(PAIR, PAIR, BF16),
                        item(PAIR, 2 * PAIR, BF16),
                        item(PAIR, 2 * PAIR, BF16),
                        item(PAIR, 2 * PAIR, F32),
                        item(1, PAIR, F32),
                        item(PAIR, PAIR, F32), item(PAIR, PAIR, F32),
                        item(CHUNK, PAIR, F32), item(CHUNK, PAIR, F32),
                        seq2(), pltpu.VMEM((seqs, 2, PAIR, PAIR), F32)],
        compiler_params=_cparams(("arbitrary", "arbitrary")),
        name=f"rwkv_mix_t{seq_len}",
    )(pa, pa, pa, pa, p["ts_mu"], p["ts_mu"], p["ts_mu"], p["ts_mu"],
      p["w0"], p["w_up"], p["a0"], p["a_up"], p["g_up"], p["k_k"], p["k_a"], p["r_k"], p["gn_g"], p["gn_b"], s0, *ride)
    return out, s_fin, cast


def _conv_kernel(pb_ref, prev_ref, next_ref, pwb_ref, dw_ref, dwb_ref, g_ref, b_ref, o_ref, pad_s, shift_s, *, geom, seq):
    width = o_ref.shape[1]
    sub = 64
    n_prompt_tok, dec_seq = geom
    tok = pl.program_id(0) * CONV_TILE
    in_prompt = tok < n_prompt_tok
    seq_len = jnp.where(in_prompt, seq, dec_seq)
    off = jnp.where(in_prompt, tok, tok - n_prompt_tok) % seq_len

    def glu(ref):
        pbv = ref[...] + pwb_ref[...]
        return pbv[:, :width] * jax.nn.sigmoid(pbv[:, width:])

    pad_s[0:CONV_HALO, :] = jnp.where(off > 0, glu(prev_ref), 0.0)
    pad_s[CONV_HALO:CONV_HALO + CONV_TILE, :] = glu(pb_ref)
    pad_s[CONV_HALO + CONV_TILE:, :] = jnp.where(off + CONV_TILE < seq_len, glu(next_ref), 0.0)
    keep = pad_s.shape[0] - SUBLANES
    for phase in range(SUBLANES):
        shift_s[phase, 0:keep, :] = pad_s[phase:phase + keep, :]
    first = CONV_HALO - CONV_W // 2
    for c in range(CONV_TILE // sub):
        acc = jnp.zeros((sub, width), F32)
        for d in range(CONV_W):
            phase = (first + d) % SUBLANES
            start = c * sub + first + d - phase
            acc = acc + shift_s[phase, start:start + sub, :] * dw_ref[d:d + 1, :]
        y = _layer_norm(acc + dwb_ref[...], g_ref[...], b_ref[...], LN_EPS)
        o_ref[c * sub:(c + 1) * sub, :] = y * jax.nn.sigmoid(y)


def _conv_module(pb, geom, seq, pw_b, dw_w, dw_b, cln_g, cln_b):
    n_tok = pb.shape[0]
    width = dw_w.shape[1]
    per_tile = CONV_TILE // CONV_HALO
    last_halo = n_tok // CONV_HALO - 1
    const = lambda i: (0, 0)
    return pl.pallas_call(
        functools.partial(_conv_kernel, geom=geom, seq=seq),
        out_shape=jax.ShapeDtypeStruct((n_tok, width), F32),
        grid=(n_tok // CONV_TILE,),
        in_specs=[pl.BlockSpec((CONV_TILE, 2 * width), lambda i: (i, 0)),
                  pl.BlockSpec((CONV_HALO, 2 * width), lambda i: (jnp.maximum(i * per_tile - 1, 0), 0)),
                  pl.BlockSpec((CONV_HALO, 2 * width), lambda i: (jnp.minimum((i + 1) * per_tile, last_halo), 0)),
                  pl.BlockSpec((1, 2 * width), const),
                  pl.BlockSpec((CONV_W, width), const),
                  pl.BlockSpec((1, width), const),
                  pl.BlockSpec((1, width), const),
                  pl.BlockSpec((1, width), const)],
        out_specs=pl.BlockSpec((CONV_TILE, width), lambda i: (i, 0)),
        scratch_shapes=[pltpu.VMEM((CONV_TILE + 2 * CONV_HALO, width), F32),
                        pltpu.VMEM((SUBLANES, CONV_TILE + 2 * CONV_HALO, width), F32)],
        compiler_params=_cparams(("arbitrary",)),
        name="conv_module",
    )(pb, pb, pb, pw_b.reshape(1, -1), dw_w, dw_b.reshape(1, -1), cln_g.reshape(1, -1), cln_b.reshape(1, -1))


def _diff_lambda(lq1, lk1, lq2, lk2, lam_init):
    dot = lambda a, b: jnp.exp(jnp.sum(a[...] * b[...], axis=-1, keepdims=True))
    return dot(lq1, lk1) - dot(lq2, lk2) + lam_init


def _diff_attend(q, keys, vals, lam, lam_init, subln):
    nq = q.shape[0]
    lane_first = lax.broadcasted_iota(jnp.int32, q.shape, 1) < HEAD
    q = q * (HEAD ** -0.5)
    q_st = jnp.concatenate([jnp.where(lane_first, q, 0.0), jnp.where(lane_first, 0.0, q)], axis=0)
    s = _bdot_nt(q_st, keys)
    s = s - jnp.max(s, axis=-1, keepdims=True)
    pexp = jnp.exp(s)
    ov = _bdot(pexp, vals) / jnp.sum(pexp, axis=-1, keepdims=True)
    o = ov[:nq] - lam * ov[nq:]
    o = o * lax.rsqrt(jnp.mean(o * o, axis=-1, keepdims=True) + RMS_EPS)
    return o * subln * (1.0 - lam_init)


def _head_cols(h):
    return slice(h * PAIR, (h + 1) * PAIR)


def _attn_prompt_kernel(q_ref, k_ref, v_ref, lq1, lk1, lq2, lk2, sub_ref, o_ref, *, lam_init):
    lam = _diff_lambda(lq1, lk1, lq2, lk2, lam_init)
    for h in range(q_ref.shape[1] // PAIR):
        c = _head_cols(h)
        o_ref[:, c] = _diff_attend(q_ref[:, c], k_ref[:, c], v_ref[:, c], lam, lam_init, sub_ref[...])


def _rope(x, cos, sin_signed):
    lane = lax.broadcasted_iota(jnp.int32, x.shape, 1)
    quarter = HEAD // 4
    partner = jnp.where((lane // quarter) % 2 == 0,
                        pltpu.roll(x, LANES - quarter, 1), pltpu.roll(x, quarter, 1))
    return x * cos + partner * sin_signed


def _attn_sample_kernel(q_ref, k_ref, v_ref, ck_ref, cv_ref, cq_ref, sq_ref, cosk_ref, sink_ref,
                        lq1, lk1, lq2, lk2, sub_ref, o_ref, *, lam_init):
    lam = _diff_lambda(lq1, lk1, lq2, lk2, lam_init)
    for h in range(q_ref.shape[1] // PAIR):
        c = _head_cols(h)
        q = _rope(q_ref[:, c], cq_ref[...], sq_ref[...])
        kl = _rope(k_ref[:, c], cosk_ref[...], sink_ref[...])
        keys = jnp.concatenate([ck_ref[:, c], kl], axis=0)
        vals = jnp.concatenate([cv_ref[:, c], v_ref[:, c]], axis=0)
        o_ref[:, c] = _diff_attend(q, keys, vals, lam, lam_init, sub_ref[...])


def _rope_tables(n_lat, grid_w):
    quarter = HEAD // 4
    inv = ROPE_BASE ** (-jnp.arange(quarter, dtype=F32) / quarter)
    t = jnp.arange(n_lat)
    row_ang = (t // grid_w).astype(F32)[:, None] * inv
    col_ang = (t % grid_w).astype(F32)[:, None] * inv
    ang = jnp.concatenate([row_ang, row_ang, col_ang, col_ang] * 2, axis=-1)
    sign = jnp.tile(jnp.concatenate([-jnp.ones(quarter, F32), jnp.ones(quarter, F32)]), 2 * HEAD // (2 * quarter))
    return jnp.cos(ang), jnp.sin(ang) * sign


def _attention(q, k_p, v_p, k_s, v_s, cache_k, cache_v, p, lam_init, n_prompt, seq, n_lat_req, n_lat, grid_w):
    inner = q.shape[1]
    vec = lambda n: pl.BlockSpec((1, n), lambda *_: (0, 0))
    lam_args = [p[n].reshape(1, -1) for n in ("lq1", "lk1", "lq2", "lk2")] + [p["subln_g"].reshape(1, -1)]
    lam_specs = [vec(HEAD)] * 4 + [vec(PAIR)]
    blk = pl.BlockSpec((seq, inner), lambda b: (b, 0))
    o_prompt = pl.pallas_call(
        functools.partial(_attn_prompt_kernel, lam_init=lam_init),
        out_shape=jax.ShapeDtypeStruct((n_prompt * seq, inner), F32),
        grid=(n_prompt,),
        in_specs=[blk, blk, blk] + lam_specs,
        out_specs=blk,
        compiler_params=_cparams(("arbitrary",)),
        name="diff_attn_prompt",
    )(q, k_p, v_p, *lam_args)

    past = cache_k.shape[1]
    qt = SUB_TILE
    n_qt = n_lat // qt
    q0 = n_prompt * seq // qt
    cos, sin = _rope_tables(n_lat, grid_w)
    lat = pl.BlockSpec((n_lat, inner), lambda b, t: (b, 0))
    ctx = pl.BlockSpec((None, past, inner), lambda b, t: (b, 0, 0))
    o_sample = pl.pallas_call(
        functools.partial(_attn_sample_kernel, lam_init=lam_init),
        out_shape=jax.ShapeDtypeStruct((n_lat_req * n_lat, inner), F32),
        grid=(n_lat_req, n_qt),
        in_specs=[pl.BlockSpec((qt, inner), lambda b, t: (q0 + b * n_qt + t, 0)),
                  lat, lat, ctx, ctx,
                  pl.BlockSpec((qt, PAIR), lambda b, t: (t, 0)),
                  pl.BlockSpec((qt, PAIR), lambda b, t: (t, 0)),
                  pl.BlockSpec((n_lat, PAIR), lambda b, t: (0, 0)),
                  pl.BlockSpec((n_lat, PAIR), lambda b, t: (0, 0))] + lam_specs,
        out_specs=pl.BlockSpec((qt, inner), lambda b, t: (b * n_qt + t, 0)),
        compiler_params=_cparams(("arbitrary", "arbitrary")),
        name="diff_attn_sample",
    )(q, k_s, v_s, cache_k, cache_v, cos, sin, cos, sin, *lam_args)
    return o_prompt, o_sample


def kernel(x_prompt, x_sample, state_wkv, cache_k, cache_v, c, c_ctx, mod_w, mod_b, ln1_g, ln1_b, ln2_g, ln2_b, ev_in_w, ev_pw_b, ev_ts_mu, ev_w0, ev_w_up, ev_a0, ev_a_up, ev_g_up, ev_k_k, ev_k_a, ev_r_k, ev_gn_g, ev_gn_b, ev_dw_w, ev_dw_b, ev_cln_g, ev_cln_b, ev_out_w, od_in_w, od_lq1, od_lk1, od_lq2, od_lk2, od_subln_g, od_out_w, rg_w, rg_b, re_w, re_b, ffn_w_gate, ffn_w_up, ffn_w_down):
    nbp, seq, d = x_prompt.shape
    nbs, n_lat, _ = x_sample.shape
    depth = mod_w.shape[0]
    n_prompt_tok = nbp * seq
    geom = (n_prompt_tok, n_lat)
    alpha = (2 * depth) ** 0.25
    a_width = ev_w0.shape[-1]
    a_cols = ev_ts_mu.shape[-1]
    heads = a_width // HEAD
    grid_w = 64
    assert n_prompt_tok % MOE_TILE == 0 and n_lat % MOE_TILE == 0 and n_lat % SUB_TILE == 0
    assert seq % CONV_TILE == 0 and n_lat % CONV_TILE == 0 and CONV_HALO >= CONV_W // 2
    assert 1 + nbs <= 8 and n_prompt_tok % ROW_TILE == 0 and n_lat % ROW_TILE == 0

    xs = (x_prompt.reshape(n_prompt_tok, d), x_sample.reshape(nbs * n_lat, d))
    cond = jnp.concatenate([c_ctx[None, :], c, jnp.zeros((8 - 1 - nbs, d), F32)], axis=0)
    mod = _ada_table(cond, mod_w, mod_b)

    new_wkv, new_k, new_v = [], [], []
    for l in range(depth):
        last = l == depth - 1
        if l % 2 == 0:
            i = l // 2
            pa, pb = _mod_matmul(xs, mod, l, ev_in_w[i], ((a_cols, False), (ev_in_w.shape[-1] - a_cols, False)), geom)
            prm = dict(ts_mu=ev_ts_mu[i][None, :], w0=ev_w0[i], w_up=ev_w_up[i], a0=ev_a0[i], a_up=ev_a_up[i],
                       g_up=ev_g_up[i], k_k=ev_k_k[i][None, :], k_a=ev_k_a[i][None, :],
                       r_k=ev_r_k[i].reshape(1, a_width), gn_g=ev_gn_g[i][None, :], gn_b=ev_gn_b[i][None, :])
            zero_state = jnp.zeros((nbp, 2, heads, HEAD, HEAD), F32)
            ride = () if l else tuple(w.reshape(-1, w.shape[-1]) for w in (ffn_w_gate, ffn_w_up, ffn_w_down))
            oa_p, st, cast = _rwkv_mix(pa, 0, nbp, seq, zero_state, prm, ride)
            if cast:
                w_gate, w_up, w_down = (c.reshape(w.shape) for c, w in zip(cast, (ffn_w_gate, ffn_w_up, ffn_w_down)))
            oa_s, _, _ = _rwkv_mix(pa, n_prompt_tok, nbs, n_lat, state_wkv[:, i], prm)
            conv = (ev_pw_b[i], ev_dw_w[i], ev_dw_b[i], ev_cln_g[i], ev_cln_b[i])
            hs = [(oa_p, oa_s), (_conv_module(pb, geom, seq, *conv),)]
            out_w = ev_out_w[i]
            new_wkv.append(st)
        else:
            j = l // 2
            lam_init = 0.8 - 0.6 * math.exp(-0.3 * l)
            inner = od_in_w.shape[-1] // 3
            q, k_p, k_s, v_p, v_s = _mod_matmul(xs, mod, l, od_in_w[j],
                                                 ((inner, False), (inner, True), (inner, True)), geom)
            prm = dict(lq1=od_lq1[j], lk1=od_lk1[j], lq2=od_lq2[j], lk2=od_lk2[j], subln_g=od_subln_g[j])
            ck = cache_k[:, j].reshape(nbs, -1, inner)
            cv = cache_v[:, j].reshape(nbs, -1, inner)
            hs = [_attention(q, k_p, v_p, k_s, v_s, ck, cv, prm, lam_init, nbp, seq, nbs, n_lat, grid_w)]
            out_w = od_out_w[j]
            new_k.append(k_p.reshape(nbp, seq, inner // PAIR, PAIR))
            new_v.append(v_p.reshape(nbp, seq, inner // PAIR, PAIR))
        x = _out_proj_norm(hs, xs, mod, l, out_w, ln1_g[l], ln1_b[l], alpha, geom)
        rw = jnp.concatenate([rg_w[l], re_w[l],
                              jnp.zeros((d, ROUTER_LANES - N_GROUPS - N_EXPERTS), F32)], axis=1)
        rb = jnp.concatenate([rg_b[l], re_b[l], jnp.zeros((ROUTER_LANES - N_GROUPS - N_EXPERTS,), F32)])[None, :]
        xs = tuple(_moe_norm(x, mod, l, rw, rb, w_gate, w_up, w_down, ln2_g[l], ln2_b[l], alpha, geom,
                             split_out=last))

    y_prompt = xs[0].reshape(nbp, seq, d)
    y_sample = xs[1].reshape(nbs, n_lat, d)
    return (y_prompt, y_sample, jnp.stack(new_wkv, axis=1), jnp.stack(new_k, axis=1), jnp.stack(new_v, axis=1))
```

```python
import functools
import math

import jax
import jax.numpy as jnp
from jax import lax
from jax.experimental import pallas as pl
from jax.experimental.pallas import tpu as pltpu

F32 = jnp.float32
BF16 = jnp.bfloat16

LANES = 128
SUBLANES = 8
HEAD = 64
PAIR = 2 * HEAD
CHUNK = 64
RWKV_STEP_ROWS = 512
INV_BLOCK = 16
ROW_TILE = 512
SUB_TILE = 256
MOE_TILE = 1024
ADA_PER_STEP = 2
CONV_W = 31
CONV_TILE = 256
CONV_HALO = 16
N_GROUPS = 4
EXPERTS_PER_GROUP = 8
N_EXPERTS = N_GROUPS * EXPERTS_PER_GROUP
ROUTER_LANES = 128
EXPERTS_PER_STEP = 4
MOE_WINDOWS = (128, 256, 384, 512)
BF16_ROWS = 16
LN_EPS = 1e-5
GN_EPS = 64e-5
RMS_EPS = 1e-5
ROPE_BASE = 10000.0
EXP_M05 = math.exp(-0.5)
VMEM_LIMIT = 56 * 1024 * 1024


def _bdot(a, b):
    return jnp.dot(a.astype(BF16), b.astype(BF16), preferred_element_type=F32)


def _bdot_nt(a, b):
    return lax.dot_general(a.astype(BF16), b.astype(BF16), (((1,), (1,)), ((), ())),
                           preferred_element_type=F32)


def _split2(a):
    hi = a.astype(BF16)
    lo = (a - hi.astype(F32)).astype(BF16)
    return hi, lo


def _dot_f32ish(a, b):
    ah, al = _split2(a)
    bh, bl = _split2(b)
    d = lambda x, y: jnp.dot(x, y, preferred_element_type=F32)
    return d(ah, bh) + d(ah, bl) + d(al, bh)


def _dot_ones_lhs(m_bf16, a):
    h, l = _split2(a)
    d = lambda y: jnp.dot(m_bf16, y, preferred_element_type=F32)
    return d(h) + d(l)


def _dot_ones_rhs(a, m_bf16):
    h, l = _split2(a)
    d = lambda x: jnp.dot(x, m_bf16, preferred_element_type=F32)
    return d(h) + d(l)


def _layer_norm(z, g, b, eps):
    mu = jnp.mean(z, axis=-1, keepdims=True)
    zc = z - mu
    var = jnp.mean(zc * zc, axis=-1, keepdims=True)
    return zc * lax.rsqrt(var + eps) * g + b


def _cond_row(tile, tile_rows, n_prompt_tok, dec_seq):
    tok = tile * tile_rows
    return jnp.where(tok < n_prompt_tok, 0, 1 + (tok - n_prompt_tok) // dec_seq)


def _cast_rows(src_ref, dst_ref, rows_per_step=128):
    n = src_ref.shape[0]
    for s in range(0, n, rows_per_step):
        dst_ref[s:s + rows_per_step, :] = src_ref[s:s + rows_per_step, :].astype(dst_ref.dtype)


def _cparams(sem):
    return pltpu.CompilerParams(dimension_semantics=sem, vmem_limit_bytes=VMEM_LIMIT)


def _ada_kernel(cond_ref, w_ref, b_ref, o_ref):
    c = cond_ref[...]
    s = c * jax.nn.sigmoid(c)
    d = o_ref.shape[-1]
    y = _dot_f32ish(s, w_ref[0])
    for j in range(ADA_PER_STEP):
        o_ref[0, j] = y[:, j * d:(j + 1) * d] + b_ref[0, j]


def _ada_table(cond8, mod_w, mod_b):
    depth, d, _ = mod_w.shape
    return pl.pallas_call(
        _ada_kernel,
        out_shape=jax.ShapeDtypeStruct((depth, 6, 8, d), F32),
        grid=(depth, 6 // ADA_PER_STEP),
        in_specs=[pl.BlockSpec((8, d), lambda l, j: (0, 0)),
                  pl.BlockSpec((1, d, ADA_PER_STEP * d), lambda l, j: (l, 0, j)),
                  pl.BlockSpec((1, ADA_PER_STEP, 1, d), lambda l, j: (l, j, 0, 0))],
        out_specs=pl.BlockSpec((1, ADA_PER_STEP, 8, d), lambda l, j: (l, j, 0, 0)),
        compiler_params=_cparams(("arbitrary", "arbitrary")),
        name="ada_table",
    )(cond8, mod_w, mod_b.reshape(depth, 6, 1, d))


def _mod_spec(layer, which, d):
    return pl.BlockSpec((None, None, 8, d), lambda *_: (layer, which, 0, 0))


def _group_specs(tile, width, n_prompt_tiles):
    return [pl.BlockSpec((tile, width), lambda i, *_: (jnp.minimum(i, n_prompt_tiles - 1), 0)),
            pl.BlockSpec((tile, width), lambda i, *_: (jnp.maximum(i - n_prompt_tiles, 0), 0))]


def _load_group(i, n_prompt_tiles, refs, rows=slice(None)):
    if len(refs) == 1:
        return refs[0][rows, :]
    return jnp.where(i < n_prompt_tiles, refs[0][rows, :], refs[1][rows, :])


def _store_group(i, n_prompt_tiles, refs, value, rows=slice(None)):
    if len(refs) == 1:
        refs[0][rows, :] = value
        return

    @pl.when(i < n_prompt_tiles)
    def _():
        refs[0][rows, :] = value

    @pl.when(i >= n_prompt_tiles)
    def _():
        refs[1][rows, :] = value


def _modmm_kernel(*refs, n_x, outs, geom):
    x_refs = refs[:n_x]
    sh_ref, sc_ref, w_ref = refs[n_x:n_x + 3]
    o_refs, wb = refs[n_x + 3:-1], refs[-1]
    i = pl.program_id(0)
    n_pt = geom[0] // ROW_TILE

    @pl.when(i == 0)
    def _():
        _cast_rows(w_ref, wb)

    r = _cond_row(i, ROW_TILE, *geom)
    h = _load_group(i, n_pt, x_refs) * (1.0 + sc_ref[pl.ds(r, 1), :]) + sh_ref[pl.ds(r, 1), :]
    hb = h.astype(BF16)
    off = 0
    k = 0
    for n, split in outs:
        y = jnp.dot(hb, wb[:, off:off + n], preferred_element_type=F32)
        _store_group(i, n_pt, o_refs[k:k + 1 + split], y)
        off += n
        k += 1 + split


def _mod_matmul(xs, mod, layer, w, outs, geom):
    d, n_out = w.shape
    n_prompt_tok = geom[0]
    n_tok = sum(x.shape[0] for x in xs)
    n_pt = n_prompt_tok // ROW_TILE
    whole = lambda n: [pl.BlockSpec((ROW_TILE, n), lambda i: (i, 0))]
    x_specs = whole(d) if len(xs) == 1 else _group_specs(ROW_TILE, d, n_pt)
    out_shape, out_specs = [], []
    for n, split in outs:
        if split:
            out_shape += [jax.ShapeDtypeStruct((n_prompt_tok, n), F32),
                          jax.ShapeDtypeStruct((n_tok - n_prompt_tok, n), F32)]
            out_specs += _group_specs(ROW_TILE, n, n_pt)
        else:
            out_shape += [jax.ShapeDtypeStruct((n_tok, n), F32)]
            out_specs += whole(n)
    return pl.pallas_call(
        functools.partial(_modmm_kernel, n_x=len(xs), outs=outs, geom=geom),
        out_shape=out_shape,
        grid=(n_tok // ROW_TILE,),
        in_specs=x_specs + [_mod_spec(layer, 0, d), _mod_spec(layer, 1, d),
                            pl.BlockSpec((d, n_out), lambda i: (0, 0), pipeline_mode=pl.Buffered(1))],
        out_specs=out_specs,
        scratch_shapes=[pltpu.VMEM((d, n_out), BF16)],
        compiler_params=_cparams(("arbitrary",)),
        name=f"mod_matmul_l{layer}",
    )(*xs, mod, mod, w)


def _outproj_kernel(*refs, arity, alpha, geom):
    groups, k = [], 0
    for n in arity:
        groups.append(refs[k:k + n])
        k += n
    gate_ref, w_ref, g_ref, b_ref, o_ref, wb = refs[k:]
    h_groups, x_group = groups[:-1], groups[-1]
    i = pl.program_id(0)
    n_pt = geom[0] // ROW_TILE

    @pl.when(i == 0)
    def _():
        _cast_rows(w_ref, wb)

    r = _cond_row(i, ROW_TILE, *geom)
    y = None
    off = 0
    for h_refs in h_groups:
        n = h_refs[0].shape[1]
        h = _load_group(i, n_pt, h_refs)
        part = jnp.dot(h.astype(BF16), wb[off:off + n, :], preferred_element_type=F32)
        y = part if y is None else y + part
        off += n
    z = alpha * _load_group(i, n_pt, x_group) + gate_ref[pl.ds(r, 1), :] * y
    o_ref[...] = _layer_norm(z, g_ref[...], b_ref[...], LN_EPS)


def _out_proj_norm(hs, xs, mod, layer, w, ln_g, ln_b, alpha, geom):
    d = w.shape[1]
    n_tok = sum(x.shape[0] for x in xs)
    n_pt = geom[0] // ROW_TILE
    specs = []
    for grp in list(hs) + [xs]:
        n = grp[0].shape[1]
        specs += [pl.BlockSpec((ROW_TILE, n), lambda i: (i, 0))] if len(grp) == 1 else _group_specs(ROW_TILE, n, n_pt)
    return pl.pallas_call(
        functools.partial(_outproj_kernel, arity=tuple(len(g) for g in list(hs) + [xs]), alpha=alpha, geom=geom),
        out_shape=jax.ShapeDtypeStruct((n_tok, d), F32),
        grid=(n_tok // ROW_TILE,),
        in_specs=specs + [
            _mod_spec(layer, 2, d),
            pl.BlockSpec(w.shape, lambda i: (0, 0), pipeline_mode=pl.Buffered(1)),
            pl.BlockSpec((1, d), lambda i: (0, 0)),
            pl.BlockSpec((1, d), lambda i: (0, 0))],
        out_specs=pl.BlockSpec((ROW_TILE, d), lambda i: (i, 0)),
        scratch_shapes=[pltpu.VMEM(w.shape, BF16)],
        compiler_params=_cparams(("arbitrary",)),
        name=f"out_proj_norm_l{layer}",
    )(*[a for g in hs for a in g], *xs, mod, w, ln_g.reshape(1, d), ln_b.reshape(1, d))


def _route(logits):
    neg = jnp.float32(-3e38)
    big = jnp.int32(1 << 20)
    lane = lax.broadcasted_iota(jnp.int32, logits.shape, 1)
    is_g = lane < N_GROUPS
    gl = jnp.where(is_g, logits, neg)
    gmax = jnp.max(gl, axis=-1, keepdims=True)
    gidx = jnp.min(jnp.where(gl == gmax, lane, big), axis=-1, keepdims=True)
    gsum = jnp.sum(jnp.where(is_g, jnp.exp(gl - gmax), 0.0), axis=-1, keepdims=True)
    g_p = 1.0 / gsum
    lo = N_GROUPS + gidx * EXPERTS_PER_GROUP
    in_grp = (lane >= lo) & (lane < lo + EXPERTS_PER_GROUP)
    el = jnp.where(in_grp, logits, neg)
    v1 = jnp.max(el, axis=-1, keepdims=True)
    i1 = jnp.min(jnp.where(el == v1, lane, big), axis=-1, keepdims=True)
    el2 = jnp.where(lane == i1, neg, el)
    v2 = jnp.max(el2, axis=-1, keepdims=True)
    i2 = jnp.min(jnp.where(el2 == v2, lane, big), axis=-1, keepdims=True)
    e21 = jnp.exp(v2 - v1)
    den = 1.0 + e21
    w1 = g_p / den
    w2 = g_p * e21 / den
    return jnp.where(lane == i1, w1, 0.0) + jnp.where(lane == i2, w2, 0.0), gidx


def _moe_kernel(x_ref, sh_ref, sc_ref, gate_ref, rw_ref, rb_ref, wg_ref, wu_ref, wd_ref, g_ref, b_ref,
                *rest, alpha, geom):
    o_refs = rest[:-7]
    hu_scr, h_scr, gates_scr, acc_scr, perm_scr, permt_scr, bounds = rest[-7:]
    i = pl.program_id(0)
    e = pl.program_id(1)
    r = _cond_row(i, MOE_TILE, *geom)
    n_pt = geom[0] // MOE_TILE
    sub = SUB_TILE
    n_sub = MOE_TILE // sub

    @pl.when(e == 0)
    def _():
        lane = lax.broadcasted_iota(jnp.int32, (sub, ROUTER_LANES), 1)
        ri = lax.broadcasted_iota(jnp.int32, (sub, sub), 0)
        ci = lax.broadcasted_iota(jnp.int32, (sub, sub), 1)
        tril = jnp.where(ci <= ri, 1.0, 0.0).astype(BF16)
        running = jnp.zeros((1, ROUTER_LANES), F32)
        onehots, cums = [], []
        for s in range(n_sub):
            rows = slice(s * sub, (s + 1) * sub)
            h = x_ref[rows, :] * (1.0 + sc_ref[pl.ds(r, 1), :]) + sh_ref[pl.ds(r, 1), :]
            hu_scr[rows, :] = h.astype(BF16)
            logits = _dot_f32ish(h, rw_ref[...]) + rb_ref[...]
            gates, gidx = _route(logits)
            gates_scr[rows, :] = gates
            onehot = jnp.where(lane == gidx, 1.0, 0.0)
            cum = jnp.dot(tril, onehot.astype(BF16), preferred_element_type=F32) + running
            running = cum[sub - 1:sub, :]
            onehots.append(onehot)
            cums.append(cum)
        gi = lax.broadcasted_iota(jnp.int32, (ROUTER_LANES, ROUTER_LANES), 0)
        gj = lax.broadcasted_iota(jnp.int32, (ROUTER_LANES, ROUTER_LANES), 1)
        before = jnp.where((gi < gj) & (gi < N_GROUPS), 1.0, 0.0).astype(BF16)
        starts = _dot_ones_rhs(jnp.broadcast_to(running, (8, ROUTER_LANES)), before)[0:1, :]
        for g in range(N_GROUPS + 1):
            bounds[g] = starts[0, g].astype(jnp.int32)
        dest_cols = [jnp.sum(onehots[s] * (starts + cums[s] - 1.0), axis=-1, keepdims=True) for s in range(n_sub)]
        dest_row = jnp.concatenate(
            [jnp.broadcast_to(dc, (sub, ROUTER_LANES)).T[0:1, :] for dc in dest_cols], axis=1)
        col_id = lax.broadcasted_iota(jnp.int32, (sub, MOE_TILE), 1).astype(F32)
        row_id = lax.broadcasted_iota(jnp.int32, (sub, MOE_TILE), 0).astype(F32)
        for s in range(n_sub):
            rows = slice(s * sub, (s + 1) * sub)
            perm_scr[rows, :] = jnp.where(row_id + float(s * sub) == dest_row, 1.0, 0.0).astype(BF16)
            permt_scr[rows, :] = jnp.where(dest_cols[s] == col_id, 1.0, 0.0).astype(BF16)
        perm = perm_scr[...]
        by_perm = lambda y: jnp.dot(perm, y, preferred_element_type=F32)
        h_scr[...] = by_perm(hu_scr[...]).astype(BF16)
        g_hi, g_lo = _split2(gates_scr[...])
        gates_scr[...] = by_perm(g_hi) + by_perm(g_lo)
        acc_scr[...] = jnp.zeros(acc_scr.shape, F32)

    grp = e // (EXPERTS_PER_GROUP // EXPERTS_PER_STEP)
    first, end = bounds[grp], bounds[grp + 1]
    first_al = (first // BF16_ROWS) * BF16_ROWS
    step = MOE_WINDOWS[0]
    assert all(w == (k + 1) * step for k, w in enumerate(MOE_WINDOWS))

    def experts_on(lo, size):
        start = pl.multiple_of(jnp.minimum(lo, MOE_TILE - size), BF16_ROWS)
        rows = pl.ds(start, size)
        lane = lax.broadcasted_iota(jnp.int32, (size, ROUTER_LANES), 1)
        row = lax.broadcasted_iota(jnp.int32, (size, 1), 0)
        hb = h_scr[rows, :]
        gates = jnp.where(row + start >= lo, gates_scr[rows, :], 0.0)
        out = acc_scr[rows, :]
        for j in range(EXPERTS_PER_STEP):
            lane_j = N_GROUPS + e * EXPERTS_PER_STEP + j
            ge = jnp.sum(jnp.where(lane == lane_j, gates, 0.0), axis=-1, keepdims=True)
            a = jnp.dot(hb, wg_ref[j], preferred_element_type=F32)
            u = jnp.dot(hb, wu_ref[j], preferred_element_type=F32)
            hid = (a * jax.nn.sigmoid(a)) * u * ge
            out = out + jnp.dot(hid.astype(BF16), wd_ref[j], preferred_element_type=F32)
        acc_scr[rows, :] = out

    def window(w, carry):
        lo = first_al + w * MOE_WINDOWS[-1]
        n_steps = jnp.minimum((end - lo + step - 1) // step, len(MOE_WINDOWS))
        for k, size in enumerate(MOE_WINDOWS):
            pl.when(n_steps == k + 1)(functools.partial(experts_on, lo, size))
        return carry

    lax.fori_loop(0, (end - first_al + MOE_WINDOWS[-1] - 1) // MOE_WINDOWS[-1], window, 0)

    @pl.when(e == pl.num_programs(1) - 1)
    def _():
        acc_scr[...] = jnp.dot(permt_scr[...], acc_scr[...].astype(BF16), preferred_element_type=F32)
        for s in range(n_sub):
            rows = slice(s * sub, (s + 1) * sub)
            z = alpha * x_ref[rows, :] + gate_ref[pl.ds(r, 1), :] * acc_scr[rows, :]
            _store_group(i, n_pt, o_refs, _layer_norm(z, g_ref[...], b_ref[...], LN_EPS), rows)


def _moe_norm(x, mod, layer, rw, rb, w_gate, w_up, w_down, ln_g, ln_b, alpha, geom, split_out):
    n_tok, d = x.shape
    ff = w_gate.shape[-1]
    assert w_gate.dtype == BF16 and w_up.dtype == BF16 and w_down.dtype == BF16
    const = lambda i, e: (0, 0)
    if split_out:
        out_shape = [jax.ShapeDtypeStruct((geom[0], d), F32), jax.ShapeDtypeStruct((n_tok - geom[0], d), F32)]
        out_specs = _group_specs(MOE_TILE, d, geom[0] // MOE_TILE)
    else:
        out_shape = [jax.ShapeDtypeStruct((n_tok, d), F32)]
        out_specs = [pl.BlockSpec((MOE_TILE, d), lambda i, e: (i, 0))]
    return pl.pallas_call(
        functools.partial(_moe_kernel, alpha=alpha, geom=geom),
        out_shape=out_shape,
        grid=(n_tok // MOE_TILE, N_EXPERTS // EXPERTS_PER_STEP),
        in_specs=[pl.BlockSpec((MOE_TILE, d), lambda i, e: (i, 0)),
                  _mod_spec(layer, 3, d), _mod_spec(layer, 4, d), _mod_spec(layer, 5, d),
                  pl.BlockSpec((d, ROUTER_LANES), const),
                  pl.BlockSpec((1, ROUTER_LANES), const),
                  pl.BlockSpec((None, EXPERTS_PER_STEP, d, ff), lambda i, e: (layer, e, 0, 0)),
                  pl.BlockSpec((None, EXPERTS_PER_STEP, d, ff), lambda i, e: (layer, e, 0, 0)),
                  pl.BlockSpec((None, EXPERTS_PER_STEP, ff, d), lambda i, e: (layer, e, 0, 0)),
                  pl.BlockSpec((1, d), const),
                  pl.BlockSpec((1, d), const)],
        out_specs=out_specs,
        scratch_shapes=[pltpu.VMEM((MOE_TILE, d), BF16), pltpu.VMEM((MOE_TILE, d), BF16),
                        pltpu.VMEM((MOE_TILE, ROUTER_LANES), F32),
                        pltpu.VMEM((MOE_TILE, d), F32),
                        pltpu.VMEM((MOE_TILE, MOE_TILE), BF16), pltpu.VMEM((MOE_TILE, MOE_TILE), BF16),
                        pltpu.SMEM((8,), jnp.int32)],
        compiler_params=_cparams(("arbitrary", "arbitrary")),
        name=f"moe_norm_l{layer}",
    )(x, mod, mod, mod, rw, rb, w_gate, w_up, w_down, ln_g.reshape(1, d), ln_b.reshape(1, d))


def _rwkv_kernel(r_ref, k_ref, v_ref, lora_ref, mur_ref, muk_ref, muv_ref, mul_ref,
                 w0_ref, wup_ref, a0_ref, aup_ref, gup_ref, kk_ref, ka_ref, rk_ref, gng_ref, gnb_ref, s0_ref,
                 *rest, seq_len, seqs, unroll, n_ride):
    ride_in, (out_ref, sfin_ref), ride_out = rest[:n_ride], rest[n_ride:n_ride + 2], rest[n_ride + 2:2 * n_ride + 2]
    (lw_s, kz_s, bz_s, rr_s, vv_s, kn_s, abar_s, rt_s, wt_s, vst_s, p_s, tm_s, mb_s, mk_s, nkb_s, txr_s, tx_s, gl_s,
     g_s, h_s, q_s, y0_s, y_s, st_s) = rest[2 * n_ride + 2:]
    for src, dst in zip(ride_in, ride_out):
        dst[...] = src[...].astype(dst.dtype)
    t_len = seqs * seq_len
    n_chunks = t_len // CHUNK
    seq_chunks = seq_len // CHUNK
    c2 = 2 * CHUNK

    row = lax.broadcasted_iota(jnp.int32, (t_len, 1), 0) % seq_len

    def tshift(x, mu):
        prev = jnp.where(row == 0, 0.0, pltpu.roll(x, 1, 0))
        nxt = jnp.where(row == seq_len - 1, 0.0, pltpu.roll(x, t_len - 1, 0))
        return x + mu * (0.5 * (prev + nxt) - x)

    r = tshift(r_ref[...], mur_ref[...])
    k = tshift(k_ref[...], muk_ref[...])
    v = tshift(v_ref[...], muv_ref[...])
    lo = tshift(lora_ref[...], mul_ref[...])
    wd = jnp.tanh(lo[:, 0:PAIR])
    ad = lo[:, PAIR:2 * PAIR]
    gate = _bdot(jax.nn.sigmoid(lo[:, 2 * PAIR:3 * PAIR]), gup_ref[...])

    ri = lax.broadcasted_iota(jnp.int32, (PAIR, PAIR), 0)
    ci = lax.broadcasted_iota(jnp.int32, (PAIR, PAIR), 1)
    same_head = (ri // HEAD) == (ci // HEAD)
    ones_bd = jnp.where(same_head, 1.0, 0.0).astype(BF16)
    eye = ri == ci

    kk = k * kk_ref[...]
    ssq = _dot_ones_rhs(kk * kk, ones_bd)
    kk = kk * lax.rsqrt(jnp.maximum(ssq, 1e-24))
    wup2 = wup_ref[...].reshape(PAIR, PAIR)
    aup2 = aup_ref[...].reshape(PAIR, PAIR)
    bonus = jnp.zeros((t_len, PAIR), F32)
    for z in range(2):
        zrows = (ri // HEAD) == z
        w_pre = w0_ref[z:z + 1, :] + _bdot(wd, jnp.where(zrows, wup2, 0.0))
        lw_s[z] = -EXP_M05 * jax.nn.sigmoid(w_pre)
        a = jax.nn.sigmoid(a0_ref[z:z + 1, :] + _bdot(ad, jnp.where(zrows, aup2, 0.0)))
        kz = k * (1.0 + (a - 1.0) * ka_ref[...])
        kz_s[z] = kz
        bz_s[z] = kk * a
        bonus = bonus + _dot_ones_rhs(r * kz * rk_ref[...], ones_bd) * v
    rr_s[...] = r
    vv_s[...] = v
    kn_s[...] = kk

    rt_i = ri % CHUNK
    cs_i = ci % CHUNK
    strict = (cs_i < rt_i, cs_i > rt_i)
    incl = (cs_i <= rt_i, cs_i >= rt_i)
    t64 = lax.broadcasted_iota(jnp.int32, (CHUNK, CHUNK), 0)
    s64 = lax.broadcasted_iota(jnp.int32, (CHUNK, CHUNK), 1)
    tri = (jnp.where(s64 <= t64, 1.0, 0.0).astype(BF16), jnp.where(s64 >= t64, 1.0, 0.0).astype(BF16))
    lane_head0 = lax.broadcasted_iota(jnp.int32, (CHUNK, PAIR), 1) < HEAD
    eye_f = jnp.where(eye, 1.0, 0.0)
    same16 = (ri // INV_BLOCK) == (ci // INV_BLOCK)

    def stack(x):
        return jnp.concatenate([jnp.where(lane_head0, x, 0.0), jnp.where(lane_head0, 0.0, x)], axis=0)

    def fold(x):
        return x[:CHUNK] + x[CHUNK:]

    def chunk_rows(c):
        return pl.ds(pl.multiple_of(c * CHUNK, CHUNK), CHUNK)

    def each_item(body):
        def both_directions(c, carry):
            body(0, c)
            body(1, c)
            return carry

        lax.fori_loop(0, n_chunks, both_directions, 0, unroll=unroll)

    def stage_operands(z, c):
        rows = chunk_rows(c)
        r_c, kn_c = rr_s[rows, :], kn_s[rows, :]
        lw_c, kz_c, bz_c = lw_s[z, rows, :], kz_s[z, rows, :], bz_s[z, rows, :]
        lg = _dot_ones_lhs(tri[z], lw_c)
        total = lg[CHUNK - 1:CHUNK, :] if z == 0 else lg[0:1, :]
        e_neg = jnp.exp(-lg)
        e_rem = jnp.exp(total - lg)
        abar = stack(kn_c * jnp.exp(lg - lw_c))
        rt = stack(r_c * jnp.exp(lg))
        big = _bdot_nt(jnp.concatenate([abar, rt], axis=0),
                       jnp.concatenate([stack(bz_c * e_neg), stack(kz_c * e_neg)], axis=0))
        mb = jnp.where(strict[z], big[:c2, :c2], 0.0)
        abar_s[z, c] = abar.astype(BF16)
        rt_s[z, c] = rt
        wt_s[z, c] = jnp.concatenate([stack(kz_c * e_rem), -stack(bz_c * e_rem)], axis=0).T.astype(BF16)
        gl_s[z, c] = jnp.exp(total)
        m0 = jnp.where(same16, mb, 0.0)
        tm_s[z, c] = eye_f - m0
        p_s[z, c] = m0.astype(BF16)
        mb_s[z, c] = mb.astype(BF16)
        mk_s[z, c] = jnp.where(strict[z], big[:c2, c2:], 0.0).astype(BF16)
        nkb_s[z, c] = jnp.concatenate([jnp.where(incl[z], big[c2:, c2:], 0.0),
                                       -jnp.where(incl[z], big[c2:, :c2], 0.0)], axis=1).astype(BF16)

    def stage_values(c, carry):
        vst_s[c] = stack(vv_s[chunk_rows(c), :]).astype(BF16)
        return carry

    lax.fori_loop(0, n_chunks, stage_values, 0, unroll=unroll)
    each_item(stage_operands)

    def stage_square(z, c):
        m0 = p_s[z, c]
        p_s[z, c] = jnp.dot(m0, m0, preferred_element_type=F32).astype(BF16)
        mkv = jnp.dot(mk_s[z, c], vst_s[c], preferred_element_type=F32)
        txr_s[z, c] = jnp.concatenate([abar_s[z, c], mkv.astype(BF16)], axis=1)

    each_item(stage_square)

    n_round = INV_BLOCK.bit_length() - 2
    for j in range(n_round):
        def stage_double(z, c, last=(j == n_round - 1)):
            tm, p = tm_s[z, c], p_s[z, c]
            tm_s[z, c] = tm + jnp.dot(tm.astype(BF16), p, preferred_element_type=F32)
            if not last:
                p_s[z, c] = jnp.dot(p, p, preferred_element_type=F32).astype(BF16)

        each_item(stage_double)

    size = INV_BLOCK
    while size < CHUNK:
        off_diag = ((ri // (2 * size)) == (ci // (2 * size))) & ((ri // size) != (ci // size))

        def stage_merge_rhs(z, c, off_diag=off_diag):
            cross = jnp.where(off_diag, mb_s[z, c], jnp.zeros((), BF16))
            p_s[z, c] = jnp.dot(cross, tm_s[z, c].astype(BF16), preferred_element_type=F32).astype(BF16)

        def stage_merge(z, c):
            tm = tm_s[z, c]
            tm_s[z, c] = tm - jnp.dot(tm.astype(BF16), p_s[z, c], preferred_element_type=F32)

        each_item(stage_merge_rhs)
        each_item(stage_merge)
        size *= 2

    def stage_apply(z, c):
        tx_s[z, c] = jnp.dot(tm_s[z, c].astype(BF16), txr_s[z, c], preferred_element_type=F32)

    each_item(stage_apply)

    def stage_maps(z, c):
        tx = tx_s[z, c].astype(BF16)
        v_st = vst_s[c]
        wt = wt_s[z, c]
        from_tx = jnp.dot(wt[:, c2:], tx, preferred_element_type=F32)
        h_s[z, c] = jnp.dot(wt[:, :c2], v_st, preferred_element_type=F32) + from_tx[:, PAIR:]
        g_s[z, c] = jnp.where(eye, gl_s[z, c], 0.0) + from_tx[:, :PAIR]
        nkb = nkb_s[z, c]
        from_tx = jnp.dot(nkb[:, c2:], tx, preferred_element_type=F32)
        q_s[z, c] = fold(rt_s[z, c] + from_tx[:, :PAIR])
        y0_s[z, c] = fold(jnp.dot(nkb[:, :c2], v_st, preferred_element_type=F32) + from_tx[:, PAIR:])

    each_item(stage_maps)

    zero64 = jnp.zeros((HEAD, HEAD), F32)
    chains = [(n, z) for n in range(seqs) for z in range(2)]
    for n, z in chains:
        a = jnp.concatenate([jnp.concatenate([s0_ref[n, z, 0], zero64], axis=1),
                             jnp.concatenate([zero64, s0_ref[n, z, 1]], axis=1)], axis=0)
        st_s[n, z] = a.T

    def propagate(i, carry):
        for n, z in chains:
            c = n * seq_chunks + (i if z == 0 else seq_chunks - 1 - i)
            s = st_s[n, z].astype(BF16)
            y_s[z, chunk_rows(c), :] = _bdot(q_s[z, c], s) + y0_s[z, c]
            st_s[n, z] = _bdot(g_s[z, c], s) + h_s[z, c]
        return carry

    lax.fori_loop(0, seq_chunks, propagate, 0)
    for n, z in chains:
        a = st_s[n, z].T
        sfin_ref[n, z, 0] = a[:HEAD, :HEAD]
        sfin_ref[n, z, 1] = a[HEAD:, HEAD:]

    y = y_s[0] + y_s[1]
    mu = _dot_ones_rhs(y, ones_bd) * (1.0 / HEAD)
    yc = y - mu
    var = _dot_ones_rhs(yc * yc, ones_bd) * (1.0 / HEAD)
    yn = yc * lax.rsqrt(var + GN_EPS) * gng_ref[...] + gnb_ref[...]
    out_ref[...] = (yn + bonus) * gate


def _rwkv_mix(pa, first_tok, n_seq, seq_len, s0, p, ride=()):
    a_width = p["w0"].shape[-1]
    n_pairs = a_width // PAIR
    heads = a_width // HEAD
    seqs = max(1, RWKV_STEP_ROWS // seq_len)
    rows = seqs * seq_len
    assert n_seq % seqs == 0 and first_tok % rows == 0
    t0 = first_tok // rows
    lora_w = 3 * PAIR
    lora_blk = 3 * a_width // lora_w
    n_chunks = rows // CHUNK
    col = lambda off: (lambda s, q: (t0 + s, off + q))
    vec = lambda off: (lambda s, q: (0, off + q))
    in_specs = [
        pl.BlockSpec((rows, PAIR), col(0)),
        pl.BlockSpec((rows, PAIR), col(n_pairs)),
        pl.BlockSpec((rows, PAIR), col(2 * n_pairs)),
        pl.BlockSpec((rows, lora_w), lambda s, q: (t0 + s, lora_blk)),
        pl.BlockSpec((1, PAIR), vec(0)),
        pl.BlockSpec((1, PAIR), vec(n_pairs)),
        pl.BlockSpec((1, PAIR), vec(2 * n_pairs)),
        pl.BlockSpec((1, lora_w), lambda s, q: (0, lora_blk)),
        pl.BlockSpec((2, PAIR), vec(0)),
        pl.BlockSpec((2, HEAD, PAIR), lambda s, q: (0, 0, q)),
        pl.BlockSpec((2, PAIR), vec(0)),
        pl.BlockSpec((2, HEAD, PAIR), lambda s, q: (0, 0, q)),
        pl.BlockSpec((PAIR, PAIR), vec(0)),
        pl.BlockSpec((1, PAIR), vec(0)),
        pl.BlockSpec((1, PAIR), vec(0)),
        pl.BlockSpec((1, PAIR), vec(0)),
        pl.BlockSpec((1, PAIR), vec(0)),
        pl.BlockSpec((1, PAIR), vec(0)),
        pl.BlockSpec((seqs, 2, 2, HEAD, HEAD), lambda s, q: (s, 0, q, 0, 0)),
    ]
    seq2 = lambda: pltpu.VMEM((2, rows, PAIR), F32)
    seq1 = lambda: pltpu.VMEM((rows, PAIR), F32)
    item = lambda rows, cols, dt: pltpu.VMEM((2, n_chunks, rows, cols), dt)
    n_steps = n_seq // seqs * n_pairs
    assert all(a.shape[0] % (n_steps * BF16_ROWS) == 0 for a in ride)
    ride_specs = [pl.BlockSpec((a.shape[0] // n_steps, a.shape[1]), lambda s, q: (s * n_pairs + q, 0)) for a in ride]
    out, s_fin, *cast = pl.pallas_call(
        functools.partial(_rwkv_kernel, seq_len=seq_len, seqs=seqs, unroll=min(n_chunks, 16), n_ride=len(ride)),
        out_shape=[jax.ShapeDtypeStruct((n_seq * seq_len, a_width), F32),
                   jax.ShapeDtypeStruct((n_seq, 2, heads, HEAD, HEAD), F32)]
        + [jax.ShapeDtypeStruct(a.shape, BF16) for a in ride],
        grid=(n_seq // seqs, n_pairs),
        in_specs=in_specs + ride_specs,
        out_specs=[pl.BlockSpec((rows, PAIR), lambda s, q: (s, q)),
                   pl.BlockSpec((seqs, 2, 2, HEAD, HEAD), lambda s, q: (s, 0, q, 0, 0))] + ride_specs,
        scratch_shapes=[seq2(), seq2(), seq2(), seq1(), seq1(), seq1(),
                        item(PAIR, PAIR, BF16),
                        item(PAIR, PAIR, F32),
                        item(PAIR, 2 * PAIR, BF16),
                        pltpu.VMEM((n_chunks, PAIR, PAIR), BF16),
                        item(PAIR, PAIR, BF16),
                        item(PAIR, PAIR, F32),
                        item(PAIR, PAIR, BF16),
                        item(PAIR, PAIR, BF16),
                        item(PAIR, 2 * PAIR, BF16),
                        item(PAIR, 2 * PAIR, BF16),
                        item(PAIR, 2 * PAIR, F32),
                        item(1, PAIR, F32),
                        item(PAIR, PAIR, F32), item(PAIR, PAIR, F32),
                        item(CHUNK, PAIR, F32), item(CHUNK, PAIR, F32),
                        seq2(), pltpu.VMEM((seqs, 2, PAIR, PAIR), F32)],
        compiler_params=_cparams(("arbitrary", "arbitrary")),
        name=f"rwkv_mix_t{seq_len}",
    )(pa, pa, pa, pa, p["ts_mu"], p["ts_mu"], p["ts_mu"], p["ts_mu"],
      p["w0"], p["w_up"], p["a0"], p["a_up"], p["g_up"], p["k_k"], p["k_a"], p["r_k"], p["gn_g"], p["gn_b"], s0, *ride)
    return out, s_fin, cast


def _conv_kernel(pb_ref, prev_ref, next_ref, pwb_ref, dw_ref, dwb_ref, g_ref, b_ref, o_ref, pad_s, shift_s, *, geom, seq):
    width = o_ref.shape[1]
    sub = 64
    n_prompt_tok, dec_seq = geom
    tok = pl.program_id(0) * CONV_TILE
    in_prompt = tok < n_prompt_tok
    seq_len = jnp.where(in_prompt, seq, dec_seq)
    off = jnp.where(in_prompt, tok, tok - n_prompt_tok) % seq_len

    def glu(ref):
        pbv = ref[...] + pwb_ref[...]
        return pbv[:, :width] * jax.nn.sigmoid(pbv[:, width:])

    pad_s[0:CONV_HALO, :] = jnp.where(off > 0, glu(prev_ref), 0.0)
    pad_s[CONV_HALO:CONV_HALO + CONV_TILE, :] = glu(pb_ref)
    pad_s[CONV_HALO + CONV_TILE:, :] = jnp.where(off + CONV_TILE < seq_len, glu(next_ref), 0.0)
    keep = pad_s.shape[0] - SUBLANES
    for phase in range(SUBLANES):
        shift_s[phase, 0:keep, :] = pad_s[phase:phase + keep, :]
    first = CONV_HALO - CONV_W // 2
    for c in range(CONV_TILE // sub):
        acc = jnp.zeros((sub, width), F32)
        for d in range(CONV_W):
            phase = (first + d) % SUBLANES
            start = c * sub + first + d - phase
            acc = acc + shift_s[phase, start:start + sub, :] * dw_ref[d:d + 1, :]
        y = _layer_norm(acc + dwb_ref[...], g_ref[...], b_ref[...], LN_EPS)
        o_ref[c * sub:(c + 1) * sub, :] = y * jax.nn.sigmoid(y)


def _conv_module(pb, geom, seq, pw_b, dw_w, dw_b, cln_g, cln_b):
    n_tok = pb.shape[0]
    width = dw_w.shape[1]
    per_tile = CONV_TILE // CONV_HALO
    last_halo = n_tok // CONV_HALO - 1
    const = lambda i: (0, 0)
    return pl.pallas_call(
        functools.partial(_conv_kernel, geom=geom, seq=seq),
        out_shape=jax.ShapeDtypeStruct((n_tok, width), F32),
        grid=(n_tok // CONV_TILE,),
        in_specs=[pl.BlockSpec((CONV_TILE, 2 * width), lambda i: (i, 0)),
                  pl.BlockSpec((CONV_HALO, 2 * width), lambda i: (jnp.maximum(i * per_tile - 1, 0), 0)),
                  pl.BlockSpec((CONV_HALO, 2 * width), lambda i: (jnp.minimum((i + 1) * per_tile, last_halo), 0)),
                  pl.BlockSpec((1, 2 * width), const),
                  pl.BlockSpec((CONV_W, width), const),
                  pl.BlockSpec((1, width), const),
                  pl.BlockSpec((1, width), const),
                  pl.BlockSpec((1, width), const)],
        out_specs=pl.BlockSpec((CONV_TILE, width), lambda i: (i, 0)),
        scratch_shapes=[pltpu.VMEM((CONV_TILE + 2 * CONV_HALO, width), F32),
                        pltpu.VMEM((SUBLANES, CONV_TILE + 2 * CONV_HALO, width), F32)],
        compiler_params=_cparams(("arbitrary",)),
        name="conv_module",
    )(pb, pb, pb, pw_b.reshape(1, -1), dw_w, dw_b.reshape(1, -1), cln_g.reshape(1, -1), cln_b.reshape(1, -1))


def _diff_lambda(lq1, lk1, lq2, lk2, lam_init):
    dot = lambda a, b: jnp.exp(jnp.sum(a[...] * b[...], axis=-1, keepdims=True))
    return dot(lq1, lk1) - dot(lq2, lk2) + lam_init


def _diff_attend(q, keys, vals, lam, lam_init, subln):
    nq = q.shape[0]
    lane_first = lax.broadcasted_iota(jnp.int32, q.shape, 1) < HEAD
    q = q * (HEAD ** -0.5)
    q_st = jnp.concatenate([jnp.where(lane_first, q, 0.0), jnp.where(lane_first, 0.0, q)], axis=0)
    s = _bdot_nt(q_st, keys)
    s = s - jnp.max(s, axis=-1, keepdims=True)
    pexp = jnp.exp(s)
    ov = _bdot(pexp, vals) / jnp.sum(pexp, axis=-1, keepdims=True)
    o = ov[:nq] - lam * ov[nq:]
    o = o * lax.rsqrt(jnp.mean(o * o, axis=-1, keepdims=True) + RMS_EPS)
    return o * subln * (1.0 - lam_init)


def _head_cols(h):
    return slice(h * PAIR, (h + 1) * PAIR)


def _attn_prompt_kernel(q_ref, k_ref, v_ref, lq1, lk1, lq2, lk2, sub_ref, o_ref, *, lam_init):
    lam = _diff_lambda(lq1, lk1, lq2, lk2, lam_init)
    for h in range(q_ref.shape[1] // PAIR):
        c = _head_cols(h)
        o_ref[:, c] = _diff_attend(q_ref[:, c], k_ref[:, c], v_ref[:, c], lam, lam_init, sub_ref[...])


def _rope(x, cos, sin_signed):
    lane = lax.broadcasted_iota(jnp.int32, x.shape, 1)
    quarter = HEAD // 4
    partner = jnp.where((lane // quarter) % 2 == 0,
                        pltpu.roll(x, LANES - quarter, 1), pltpu.roll(x, quarter, 1))
    return x * cos + partner * sin_signed


def _attn_sample_kernel(q_ref, k_ref, v_ref, ck_ref, cv_ref, cq_ref, sq_ref, cosk_ref, sink_ref,
                        lq1, lk1, lq2, lk2, sub_ref, o_ref, *, lam_init):
    lam = _diff_lambda(lq1, lk1, lq2, lk2, lam_init)
    for h in range(q_ref.shape[1] // PAIR):
        c = _head_cols(h)
        q = _rope(q_ref[:, c], cq_ref[...], sq_ref[...])
        kl = _rope(k_ref[:, c], cosk_ref[...], sink_ref[...])
        keys = jnp.concatenate([ck_ref[:, c], kl], axis=0)
        vals = jnp.concatenate([cv_ref[:, c], v_ref[:, c]], axis=0)
        o_ref[:, c] = _diff_attend(q, keys, vals, lam, lam_init, sub_ref[...])


def _rope_tables(n_lat, grid_w):
    quarter = HEAD // 4
    inv = ROPE_BASE ** (-jnp.arange(quarter, dtype=F32) / quarter)
    t = jnp.arange(n_lat)
    row_ang = (t // grid_w).astype(F32)[:, None] * inv
    col_ang = (t % grid_w).astype(F32)[:, None] * inv
    ang = jnp.concatenate([row_ang, row_ang, col_ang, col_ang] * 2, axis=-1)
    sign = jnp.tile(jnp.concatenate([-jnp.ones(quarter, F32), jnp.ones(quarter, F32)]), 2 * HEAD // (2 * quarter))
    return jnp.cos(ang), jnp.sin(ang) * sign


def _attention(q, k_p, v_p, k_s, v_s, cache_k, cache_v, p, lam_init, n_prompt, seq, n_lat_req, n_lat, grid_w):
    inner = q.shape[1]
    vec = lambda n: pl.BlockSpec((1, n), lambda *_: (0, 0))
    lam_args = [p[n].reshape(1, -1) for n in ("lq1", "lk1", "lq2", "lk2")] + [p["subln_g"].reshape(1, -1)]
    lam_specs = [vec(HEAD)] * 4 + [vec(PAIR)]
    blk = pl.BlockSpec((seq, inner), lambda b: (b, 0))
    o_prompt = pl.pallas_call(
        functools.partial(_attn_prompt_kernel, lam_init=lam_init),
        out_shape=jax.ShapeDtypeStruct((n_prompt * seq, inner), F32),
        grid=(n_prompt,),
        in_specs=[blk, blk, blk] + lam_specs,
        out_specs=blk,
        compiler_params=_cparams(("arbitrary",)),
        name="diff_attn_prompt",
    )(q, k_p, v_p, *lam_args)

    past = cache_k.shape[1]
    qt = SUB_TILE
    n_qt = n_lat // qt
    q0 = n_prompt * seq // qt
    cos, sin = _rope_tables(n_lat, grid_w)
    lat = pl.BlockSpec((n_lat, inner), lambda b, t: (b, 0))
    ctx = pl.BlockSpec((None, past, inner), lambda b, t: (b, 0, 0))
    o_sample = pl.pallas_call(
        functools.partial(_attn_sample_kernel, lam_init=lam_init),
        out_shape=jax.ShapeDtypeStruct((n_lat_req * n_lat, inner), F32),
        grid=(n_lat_req, n_qt),
        in_specs=[pl.BlockSpec((qt, inner), lambda b, t: (q0 + b * n_qt + t, 0)),
                  lat, lat, ctx, ctx,
                  pl.BlockSpec((qt, PAIR), lambda b, t: (t, 0)),
                  pl.BlockSpec((qt, PAIR), lambda b, t: (t, 0)),
                  pl.BlockSpec((n_lat, PAIR), lambda b, t: (0, 0)),
                  pl.BlockSpec((n_lat, PAIR), lambda b, t: (0, 0))] + lam_specs,
        out_specs=pl.BlockSpec((qt, inner), lambda b, t: (b * n_qt + t, 0)),
        compiler_params=_cparams(("arbitrary", "arbitrary")),
        name="diff_attn_sample",
    )(q, k_s, v_s, cache_k, cache_v, cos, sin, cos, sin, *lam_args)
    return o_prompt, o_sample


def kernel(x_prompt, x_sample, state_wkv, cache_k, cache_v, c, c_ctx, mod_w, mod_b, ln1_g, ln1_b, ln2_g, ln2_b, ev_in_w, ev_pw_b, ev_ts_mu, ev_w0, ev_w_up, ev_a0, ev_a_up, ev_g_up, ev_k_k, ev_k_a, ev_r_k, ev_gn_g, ev_gn_b, ev_dw_w, ev_dw_b, ev_cln_g, ev_cln_b, ev_out_w, od_in_w, od_lq1, od_lk1, od_lq2, od_lk2, od_subln_g, od_out_w, rg_w, rg_b, re_w, re_b, ffn_w_gate, ffn_w_up, ffn_w_down):
    nbp, seq, d = x_prompt.shape
    nbs, n_lat, _ = x_sample.shape
    depth = mod_w.shape[0]
    n_prompt_tok = nbp * seq
    geom = (n_prompt_tok, n_lat)
    alpha = (2 * depth) ** 0.25
    a_width = ev_w0.shape[-1]
    a_cols = ev_ts_mu.shape[-1]
    heads = a_width // HEAD
    grid_w = 64
    assert n_prompt_tok % MOE_TILE == 0 and n_lat % MOE_TILE == 0 and n_lat % SUB_TILE == 0
    assert seq % CONV_TILE == 0 and n_lat % CONV_TILE == 0 and CONV_HALO >= CONV_W // 2
    assert 1 + nbs <= 8 and n_prompt_tok % ROW_TILE == 0 and n_lat % ROW_TILE == 0

    xs = (x_prompt.reshape(n_prompt_tok, d), x_sample.reshape(nbs * n_lat, d))
    cond = jnp.concatenate([c_ctx[None, :], c, jnp.zeros((8 - 1 - nbs, d), F32)], axis=0)
    mod = _ada_table(cond, mod_w, mod_b)

    new_wkv, new_k, new_v = [], [], []
    for l in range(depth):
        last = l == depth - 1
        if l % 2 == 0:
            i = l // 2
            pa, pb = _mod_matmul(xs, mod, l, ev_in_w[i], ((a_cols, False), (ev_in_w.shape[-1] - a_cols, False)), geom)
            prm = dict(ts_mu=ev_ts_mu[i][None, :], w0=ev_w0[i], w_up=ev_w_up[i], a0=ev_a0[i], a_up=ev_a_up[i],
                       g_up=ev_g_up[i], k_k=ev_k_k[i][None, :], k_a=ev_k_a[i][None, :],
                       r_k=ev_r_k[i].reshape(1, a_width), gn_g=ev_gn_g[i][None, :], gn_b=ev_gn_b[i][None, :])
            zero_state = jnp.zeros((nbp, 2, heads, HEAD, HEAD), F32)
            ride = () if l else tuple(w.reshape(-1, w.shape[-1]) for w in (ffn_w_gate, ffn_w_up, ffn_w_down))
            oa_p, st, cast = _rwkv_mix(pa, 0, nbp, seq, zero_state, prm, ride)
            if cast:
                w_gate, w_up, w_down = (c.reshape(w.shape) for c, w in zip(cast, (ffn_w_gate, ffn_w_up, ffn_w_down)))
            oa_s, _, _ = _rwkv_mix(pa, n_prompt_tok, nbs, n_lat, state_wkv[:, i], prm)
            conv = (ev_pw_b[i], ev_dw_w[i], ev_dw_b[i], ev_cln_g[i], ev_cln_b[i])
            hs = [(oa_p, oa_s), (_conv_module(pb, geom, seq, *conv),)]
            out_w = ev_out_w[i]
            new_wkv.append(st)
        else:
            j = l // 2
            lam_init = 0.8 - 0.6 * math.exp(-0.3 * l)
            inner = od_in_w.shape[-1] // 3
            q, k_p, k_s, v_p, v_s = _mod_matmul(xs, mod, l, od_in_w[j],
                                                 ((inner, False), (inner, True), (inner, True)), geom)
            prm = dict(lq1=od_lq1[j], lk1=od_lk1[j], lq2=od_lq2[j], lk2=od_lk2[j], subln_g=od_subln_g[j])
            ck = cache_k[:, j].reshape(nbs, -1, inner)
            cv = cache_v[:, j].reshape(nbs, -1, inner)
            hs = [_attention(q, k_p, v_p, k_s, v_s, ck, cv, prm, lam_init, nbp, seq, nbs, n_lat, grid_w)]
            out_w = od_out_w[j]
            new_k.append(k_p.reshape(nbp, seq, inner // PAIR, PAIR))
            new_v.append(v_p.reshape(nbp, seq, inner // PAIR, PAIR))
        x = _out_proj_norm(hs, xs, mod, l, out_w, ln1_g[l], ln1_b[l], alpha, geom)
        rw = jnp.concatenate([rg_w[l], re_w[l],
                              jnp.zeros((d, ROUTER_LANES - N_GROUPS - N_EXPERTS), F32)], axis=1)
        rb = jnp.concatenate([rg_b[l], re_b[l], jnp.zeros((ROUTER_LANES - N_GROUPS - N_EXPERTS,), F32)])[None, :]
        xs = tuple(_moe_norm(x, mod, l, rw, rb, w_gate, w_up, w_down, ln2_g[l], ln2_b[l], alpha, geom,
                             split_out=last))

    y_prompt = xs[0].reshape(nbp, seq, d)
    y_sample = xs[1].reshape(nbs, n_lat, d)
    return (y_prompt, y_sample, jnp.stack(new_wkv, axis=1), jnp.stack(new_k, axis=1), jnp.stack(new_v, axis=1))
```

```python
import functools
import math

import jax
import jax.numpy as jnp
from jax import lax
from jax.experimental import pallas as pl
from jax.experimental.pallas import tpu as pltpu

F32 = jnp.float32
BF16 = jnp.bfloat16

LANES = 128
SUBLANES = 8
HEAD = 64
PAIR = 2 * HEAD
CHUNK = 64
RWKV_STEP_ROWS = 512
INV_BLOCK = 16
PROJ_TILES = (1024, 512, 256)
PROJ_VMEM_BUDGET = 48 * 1024 * 1024
SUB_TILE = 256
MOE_TILE = 1024
ADA_PER_STEP = 2
CONV_W = 31
CONV_TILE = 256
CONV_HALO = 16
N_GROUPS = 4
EXPERTS_PER_GROUP = 8
N_EXPERTS = N_GROUPS * EXPERTS_PER_GROUP
ROUTER_LANES = 128
EXPERTS_PER_STEP = 4
MOE_WINDOWS = (128, 256, 384, 512)
BF16_ROWS = 16
LN_EPS = 1e-5
GN_EPS = 64e-5
RMS_EPS = 1e-5
ROPE_BASE = 10000.0
EXP_M05 = math.exp(-0.5)
VMEM_LIMIT = 56 * 1024 * 1024


def _bdot(a, b):
    return jnp.dot(a.astype(BF16), b.astype(BF16), preferred_element_type=F32)


def _bdot_nt(a, b):
    return lax.dot_general(a.astype(BF16), b.astype(BF16), (((1,), (1,)), ((), ())),
                           preferred_element_type=F32)


def _split2(a):
    hi = a.astype(BF16)
    lo = (a - hi.astype(F32)).astype(BF16)
    return hi, lo


def _dot_f32ish(a, b):
    ah, al = _split2(a)
    bh, bl = _split2(b)
    d = lambda x, y: jnp.dot(x, y, preferred_element_type=F32)
    return d(ah, bh) + d(ah, bl) + d(al, bh)


def _dot_ones_lhs(m_bf16, a):
    h, l = _split2(a)
    d = lambda y: jnp.dot(m_bf16, y, preferred_element_type=F32)
    return d(h) + d(l)


def _dot_ones_rhs(a, m_bf16):
    h, l = _split2(a)
    d = lambda x: jnp.dot(x, m_bf16, preferred_element_type=F32)
    return d(h) + d(l)


def _layer_norm(z, g, b, eps):
    mu = jnp.mean(z, axis=-1, keepdims=True)
    zc = z - mu
    var = jnp.mean(zc * zc, axis=-1, keepdims=True)
    return zc * lax.rsqrt(var + eps) * g + b


def _cond_row(tile, tile_rows, n_prompt_tok, dec_seq):
    tok = tile * tile_rows
    return jnp.where(tok < n_prompt_tok, 0, 1 + (tok - n_prompt_tok) // dec_seq)


def _cparams(sem):
    return pltpu.CompilerParams(dimension_semantics=sem, vmem_limit_bytes=VMEM_LIMIT)


def _ada_kernel(cond_ref, w_ref, b_ref, *rest, n_ride):
    ride_in, o_ref, ride_out = rest[:n_ride], rest[n_ride], rest[n_ride + 1:]

    @pl.when(pl.program_id(1) == 0)
    def _():
        for src, dst in zip(ride_in, ride_out):
            dst[...] = src[...].astype(dst.dtype)

    c = cond_ref[...]
    s = c * jax.nn.sigmoid(c)
    d = o_ref.shape[-1]
    y = _dot_f32ish(s, w_ref[0])
    for j in range(ADA_PER_STEP):
        o_ref[0, j] = y[:, j * d:(j + 1) * d] + b_ref[0, j]


def _ada_table(cond8, mod_w, mod_b, ride=()):
    depth, d, _ = mod_w.shape
    assert all(a.shape[0] % (depth * BF16_ROWS) == 0 for a in ride)
    ride_specs = [pl.BlockSpec((a.shape[0] // depth, a.shape[1]), lambda l, j: (l, 0)) for a in ride]
    mod, *cast = pl.pallas_call(
        functools.partial(_ada_kernel, n_ride=len(ride)),
        out_shape=[jax.ShapeDtypeStruct((depth, 6, 8, d), F32)] + [jax.ShapeDtypeStruct(a.shape, BF16) for a in ride],
        grid=(depth, 6 // ADA_PER_STEP),
        in_specs=[pl.BlockSpec((8, d), lambda l, j: (0, 0)),
                  pl.BlockSpec((1, d, ADA_PER_STEP * d), lambda l, j: (l, 0, j)),
                  pl.BlockSpec((1, ADA_PER_STEP, 1, d), lambda l, j: (l, j, 0, 0))] + ride_specs,
        out_specs=[pl.BlockSpec((1, ADA_PER_STEP, 8, d), lambda l, j: (l, j, 0, 0))] + ride_specs,
        compiler_params=_cparams(("arbitrary", "arbitrary")),
        name="ada_table",
    )(cond8, mod_w, mod_b.reshape(depth, 6, 1, d), *ride)
    return mod, cast


def _mod_spec(layer, which, d):
    return pl.BlockSpec((None, None, 8, d), lambda *_: (layer, which, 0, 0))


def _group_specs(tile, width, n_prompt_tiles):
    return [pl.BlockSpec((tile, width), lambda i, *_: (jnp.minimum(i, n_prompt_tiles - 1), 0)),
            pl.BlockSpec((tile, width), lambda i, *_: (jnp.maximum(i - n_prompt_tiles, 0), 0))]


def _load_group(i, n_prompt_tiles, refs, rows=slice(None)):
    if len(refs) == 1:
        return refs[0][rows, :]
    return jnp.where(i < n_prompt_tiles, refs[0][rows, :], refs[1][rows, :])


def _store_group(i, n_prompt_tiles, refs, value, rows=slice(None)):
    if len(refs) == 1:
        refs[0][rows, :] = value
        return

    @pl.when(i < n_prompt_tiles)
    def _():
        refs[0][rows, :] = value

    @pl.when(i >= n_prompt_tiles)
    def _():
        refs[1][rows, :] = value


def _proj_tile(cols_f32, w_elems, geom):
    for tile in PROJ_TILES:
        fits = 2 * 4 * tile * cols_f32 + 2 * w_elems <= PROJ_VMEM_BUDGET
        if fits and geom[0] % tile == 0 and geom[1] % tile == 0:
            return tile
    raise ValueError("no projection tile fits")


def _modmm_kernel(*refs, n_x, outs, geom, tile):
    x_refs = refs[:n_x]
    sh_ref, sc_ref, w_ref = refs[n_x:n_x + 3]
    o_refs = refs[n_x + 3:]
    i = pl.program_id(0)
    n_pt = geom[0] // tile
    r = _cond_row(i, tile, *geom)
    h = _load_group(i, n_pt, x_refs) * (1.0 + sc_ref[pl.ds(r, 1), :]) + sh_ref[pl.ds(r, 1), :]
    hb = h.astype(BF16)
    off = 0
    k = 0
    for n, split in outs:
        y = jnp.dot(hb, w_ref[:, off:off + n], preferred_element_type=F32)
        _store_group(i, n_pt, o_refs[k:k + 1 + split], y)
        off += n
        k += 1 + split


def _mod_matmul(xs, mod, layer, w, outs, geom):
    d, n_out = w.shape
    n_prompt_tok = geom[0]
    n_tok = sum(x.shape[0] for x in xs)
    tile = _proj_tile(len(xs) * d + sum(n * (1 + split) for n, split in outs), w.size, geom)
    n_pt = n_prompt_tok // tile
    whole = lambda n: [pl.BlockSpec((tile, n), lambda i: (i, 0))]
    x_specs = whole(d) if len(xs) == 1 else _group_specs(tile, d, n_pt)
    out_shape, out_specs = [], []
    for n, split in outs:
        if split:
            out_shape += [jax.ShapeDtypeStruct((n_prompt_tok, n), F32),
                          jax.ShapeDtypeStruct((n_tok - n_prompt_tok, n), F32)]
            out_specs += _group_specs(tile, n, n_pt)
        else:
            out_shape += [jax.ShapeDtypeStruct((n_tok, n), F32)]
            out_specs += whole(n)
    return pl.pallas_call(
        functools.partial(_modmm_kernel, n_x=len(xs), outs=outs, geom=geom, tile=tile),
        out_shape=out_shape,
        grid=(n_tok // tile,),
        in_specs=x_specs + [_mod_spec(layer, 0, d), _mod_spec(layer, 1, d),
                            pl.BlockSpec((d, n_out), lambda i: (0, 0), pipeline_mode=pl.Buffered(1))],
        out_specs=out_specs,
        compiler_params=_cparams(("arbitrary",)),
        name=f"mod_matmul_l{layer}",
    )(*xs, mod, mod, w)


def _outproj_kernel(*refs, arity, alpha, geom, tile):
    groups, k = [], 0
    for n in arity:
        groups.append(refs[k:k + n])
        k += n
    gate_ref, w_ref, g_ref, b_ref, o_ref = refs[k:]
    h_groups, x_group = groups[:-1], groups[-1]
    i = pl.program_id(0)
    n_pt = geom[0] // tile
    r = _cond_row(i, tile, *geom)
    y = None
    off = 0
    for h_refs in h_groups:
        n = h_refs[0].shape[1]
        h = _load_group(i, n_pt, h_refs)
        part = jnp.dot(h.astype(BF16), w_ref[off:off + n, :], preferred_element_type=F32)
        y = part if y is None else y + part
        off += n
    z = alpha * _load_group(i, n_pt, x_group) + gate_ref[pl.ds(r, 1), :] * y
    o_ref[...] = _layer_norm(z, g_ref[...], b_ref[...], LN_EPS)


def _out_proj_norm(hs, xs, mod, layer, w, ln_g, ln_b, alpha, geom):
    d = w.shape[1]
    n_tok = sum(x.shape[0] for x in xs)
    tile = _proj_tile(d + sum(len(g) * g[0].shape[1] for g in list(hs) + [xs]), w.size, geom)
    n_pt = geom[0] // tile
    specs = []
    for grp in list(hs) + [xs]:
        n = grp[0].shape[1]
        specs += [pl.BlockSpec((tile, n), lambda i: (i, 0))] if len(grp) == 1 else _group_specs(tile, n, n_pt)
    return pl.pallas_call(
        functools.partial(_outproj_kernel, arity=tuple(len(g) for g in list(hs) + [xs]), alpha=alpha, geom=geom,
                          tile=tile),
        out_shape=jax.ShapeDtypeStruct((n_tok, d), F32),
        grid=(n_tok // tile,),
        in_specs=specs + [
            _mod_spec(layer, 2, d),
            pl.BlockSpec(w.shape, lambda i: (0, 0), pipeline_mode=pl.Buffered(1)),
            pl.BlockSpec((1, d), lambda i: (0, 0)),
            pl.BlockSpec((1, d), lambda i: (0, 0))],
        out_specs=pl.BlockSpec((tile, d), lambda i: (i, 0)),
        compiler_params=_cparams(("arbitrary",)),
        name=f"out_proj_norm_l{layer}",
    )(*[a for g in hs for a in g], *xs, mod, w, ln_g.reshape(1, d), ln_b.reshape(1, d))


def _route(logits):
    neg = jnp.float32(-3e38)
    big = jnp.int32(1 << 20)
    lane = lax.broadcasted_iota(jnp.int32, logits.shape, 1)
    is_g = lane < N_GROUPS
    gl = jnp.where(is_g, logits, neg)
    gmax = jnp.max(gl, axis=-1, keepdims=True)
    gidx = jnp.min(jnp.where(gl == gmax, lane, big), axis=-1, keepdims=True)
    gsum = jnp.sum(jnp.where(is_g, jnp.exp(gl - gmax), 0.0), axis=-1, keepdims=True)
    g_p = 1.0 / gsum
    lo = N_GROUPS + gidx * EXPERTS_PER_GROUP
    in_grp = (lane >= lo) & (lane < lo + EXPERTS_PER_GROUP)
    el = jnp.where(in_grp, logits, neg)
    v1 = jnp.max(el, axis=-1, keepdims=True)
    i1 = jnp.min(jnp.where(el == v1, lane, big), axis=-1, keepdims=True)
    el2 = jnp.where(lane == i1, neg, el)
    v2 = jnp.max(el2, axis=-1, keepdims=True)
    i2 = jnp.min(jnp.where(el2 == v2, lane, big), axis=-1, keepdims=True)
    e21 = jnp.exp(v2 - v1)
    den = 1.0 + e21
    w1 = g_p / den
    w2 = g_p * e21 / den
    return jnp.where(lane == i1, w1, 0.0) + jnp.where(lane == i2, w2, 0.0), gidx


def _moe_kernel(x_ref, sh_ref, sc_ref, gate_ref, rw_ref, rb_ref, wg_ref, wu_ref, wd_ref, g_ref, b_ref,
                *rest, alpha, geom):
    o_refs = rest[:-7]
    hu_scr, h_scr, gates_scr, acc_scr, perm_scr, permt_scr, bounds = rest[-7:]
    i = pl.program_id(0)
    e = pl.program_id(1)
    r = _cond_row(i, MOE_TILE, *geom)
    n_pt = geom[0] // MOE_TILE
    sub = SUB_TILE
    n_sub = MOE_TILE // sub

    @pl.when(e == 0)
    def _():
        lane = lax.broadcasted_iota(jnp.int32, (sub, ROUTER_LANES), 1)
        ri = lax.broadcasted_iota(jnp.int32, (sub, sub), 0)
        ci = lax.broadcasted_iota(jnp.int32, (sub, sub), 1)
        tril = jnp.where(ci <= ri, 1.0, 0.0).astype(BF16)
        running = jnp.zeros((1, ROUTER_LANES), F32)
        onehots, cums = [], []
        for s in range(n_sub):
            rows = slice(s * sub, (s + 1) * sub)
            h = x_ref[rows, :] * (1.0 + sc_ref[pl.ds(r, 1), :]) + sh_ref[pl.ds(r, 1), :]
            hu_scr[rows, :] = h.astype(BF16)
            logits = _dot_f32ish(h, rw_ref[...]) + rb_ref[...]
            gates, gidx = _route(logits)
            gates_scr[rows, :] = gates
            onehot = jnp.where(lane == gidx, 1.0, 0.0)
            cum = jnp.dot(tril, onehot.astype(BF16), preferred_element_type=F32) + running
            running = cum[sub - 1:sub, :]
            onehots.append(onehot)
            cums.append(cum)
        gi = lax.broadcasted_iota(jnp.int32, (ROUTER_LANES, ROUTER_LANES), 0)
        gj = lax.broadcasted_iota(jnp.int32, (ROUTER_LANES, ROUTER_LANES), 1)
        before = jnp.where((gi < gj) & (gi < N_GROUPS), 1.0, 0.0).astype(BF16)
        starts = _dot_ones_rhs(jnp.broadcast_to(running, (8, ROUTER_LANES)), before)[0:1, :]
        for g in range(N_GROUPS + 1):
            bounds[g] = starts[0, g].astype(jnp.int32)
        dest_cols = [jnp.sum(onehots[s] * (starts + cums[s] - 1.0), axis=-1, keepdims=True) for s in range(n_sub)]
        dest_row = jnp.concatenate(
            [jnp.broadcast_to(dc, (sub, ROUTER_LANES)).T[0:1, :] for dc in dest_cols], axis=1)
        col_id = lax.broadcasted_iota(jnp.int32, (sub, MOE_TILE), 1).astype(F32)
        row_id = lax.broadcasted_iota(jnp.int32, (sub, MOE_TILE), 0).astype(F32)
        for s in range(n_sub):
            rows = slice(s * sub, (s + 1) * sub)
            perm_scr[rows, :] = jnp.where(row_id + float(s * sub) == dest_row, 1.0, 0.0).astype(BF16)
            permt_scr[rows, :] = jnp.where(dest_cols[s] == col_id, 1.0, 0.0).astype(BF16)
        perm = perm_scr[...]
        by_perm = lambda y: jnp.dot(perm, y, preferred_element_type=F32)
        h_scr[...] = by_perm(hu_scr[...]).astype(BF16)
        g_hi, g_lo = _split2(gates_scr[...])
        gates_scr[...] = by_perm(g_hi) + by_perm(g_lo)
        acc_scr[...] = jnp.zeros(acc_scr.shape, F32)

    grp = e // (EXPERTS_PER_GROUP // EXPERTS_PER_STEP)
    first, end = bounds[grp], bounds[grp + 1]
    first_al = (first // BF16_ROWS) * BF16_ROWS
    step = MOE_WINDOWS[0]
    assert all(w == (k + 1) * step for k, w in enumerate(MOE_WINDOWS))

    def experts_on(lo, size):
        start = pl.multiple_of(jnp.minimum(lo, MOE_TILE - size), BF16_ROWS)
        rows = pl.ds(start, size)
        lane = lax.broadcasted_iota(jnp.int32, (size, ROUTER_LANES), 1)
        row = lax.broadcasted_iota(jnp.int32, (size, 1), 0)
        hb = h_scr[rows, :]
        gates = jnp.where(row + start >= lo, gates_scr[rows, :], 0.0)
        out = acc_scr[rows, :]
        for j in range(EXPERTS_PER_STEP):
            lane_j = N_GROUPS + e * EXPERTS_PER_STEP + j
            ge = jnp.sum(jnp.where(lane == lane_j, gates, 0.0), axis=-1, keepdims=True)
            a = jnp.dot(hb, wg_ref[j], preferred_element_type=F32)
            u = jnp.dot(hb, wu_ref[j], preferred_element_type=F32)
            hid = (a * jax.nn.sigmoid(a)) * u * ge
            out = out + jnp.dot(hid.astype(BF16), wd_ref[j], preferred_element_type=F32)
        acc_scr[rows, :] = out

    def window(w, carry):
        lo = first_al + w * MOE_WINDOWS[-1]
        n_steps = jnp.minimum((end - lo + step - 1) // step, len(MOE_WINDOWS))
        for k, size in enumerate(MOE_WINDOWS):
            pl.when(n_steps == k + 1)(functools.partial(experts_on, lo, size))
        return carry

    lax.fori_loop(0, (end - first_al + MOE_WINDOWS[-1] - 1) // MOE_WINDOWS[-1], window, 0)

    @pl.when(e == pl.num_programs(1) - 1)
    def _():
        acc_scr[...] = jnp.dot(permt_scr[...], acc_scr[...].astype(BF16), preferred_element_type=F32)
        for s in range(n_sub):
            rows = slice(s * sub, (s + 1) * sub)
            z = alpha * x_ref[rows, :] + gate_ref[pl.ds(r, 1), :] * acc_scr[rows, :]
            _store_group(i, n_pt, o_refs, _layer_norm(z, g_ref[...], b_ref[...], LN_EPS), rows)


def _moe_norm(x, mod, layer, rw, rb, w_gate, w_up, w_down, ln_g, ln_b, alpha, geom, split_out):
    n_tok, d = x.shape
    ff = w_gate.shape[-1]
    assert w_gate.dtype == BF16 and w_up.dtype == BF16 and w_down.dtype == BF16
    const = lambda i, e: (0, 0)
    if split_out:
        out_shape = [jax.ShapeDtypeStruct((geom[0], d), F32), jax.ShapeDtypeStruct((n_tok - geom[0], d), F32)]
        out_specs = _group_specs(MOE_TILE, d, geom[0] // MOE_TILE)
    else:
        out_shape = [jax.ShapeDtypeStruct((n_tok, d), F32)]
        out_specs = [pl.BlockSpec((MOE_TILE, d), lambda i, e: (i, 0))]
    return pl.pallas_call(
        functools.partial(_moe_kernel, alpha=alpha, geom=geom),
        out_shape=out_shape,
        grid=(n_tok // MOE_TILE, N_EXPERTS // EXPERTS_PER_STEP),
        in_specs=[pl.BlockSpec((MOE_TILE, d), lambda i, e: (i, 0)),
                  _mod_spec(layer, 3, d), _mod_spec(layer, 4, d), _mod_spec(layer, 5, d),
                  pl.BlockSpec((d, ROUTER_LANES), const),
                  pl.BlockSpec((1, ROUTER_LANES), const),
                  pl.BlockSpec((None, EXPERTS_PER_STEP, d, ff), lambda i, e: (layer, e, 0, 0)),
                  pl.BlockSpec((None, EXPERTS_PER_STEP, d, ff), lambda i, e: (layer, e, 0, 0)),
                  pl.BlockSpec((None, EXPERTS_PER_STEP, ff, d), lambda i, e: (layer, e, 0, 0)),
                  pl.BlockSpec((1, d), const),
                  pl.BlockSpec((1, d), const)],
        out_specs=out_specs,
        scratch_shapes=[pltpu.VMEM((MOE_TILE, d), BF16), pltpu.VMEM((MOE_TILE, d), BF16),
                        pltpu.VMEM((MOE_TILE, ROUTER_LANES), F32),
                        pltpu.VMEM((MOE_TILE, d), F32),
                        pltpu.VMEM((MOE_TILE, MOE_TILE), BF16), pltpu.VMEM((MOE_TILE, MOE_TILE), BF16),
                        pltpu.SMEM((8,), jnp.int32)],
        compiler_params=_cparams(("arbitrary", "arbitrary")),
        name=f"moe_norm_l{layer}",
    )(x, mod, mod, mod, rw, rb, w_gate, w_up, w_down, ln_g.reshape(1, d), ln_b.reshape(1, d))


def _rwkv_kernel(r_ref, k_ref, v_ref, lora_ref, mur_ref, muk_ref, muv_ref, mul_ref,
                 w0_ref, wup_ref, a0_ref, aup_ref, gup_ref, kk_ref, ka_ref, rk_ref, gng_ref, gnb_ref, s0_ref,
                 *rest, seq_len, seqs, unroll, n_ride):
    ride_in, (out_ref, sfin_ref), ride_out = rest[:n_ride], rest[n_ride:n_ride + 2], rest[n_ride + 2:2 * n_ride + 2]
    (lw_s, kz_s, bz_s, rr_s, vv_s, kn_s, abar_s, rt_s, wt_s, vst_s, p_s, tm_s, mb_s, mk_s, nkb_s, txr_s, tx_s, gl_s,
     g_s, h_s, q_s, y0_s, y_s, st_s) = rest[2 * n_ride + 2:]
    for src, dst in zip(ride_in, ride_out):
        dst[...] = src[...].astype(dst.dtype)
    t_len = seqs * seq_len
    n_chunks = t_len // CHUNK
    seq_chunks = seq_len // CHUNK
    c2 = 2 * CHUNK

    row = lax.broadcasted_iota(jnp.int32, (t_len, 1), 0) % seq_len

    def tshift(x, mu):
        prev = jnp.where(row == 0, 0.0, pltpu.roll(x, 1, 0))
        nxt = jnp.where(row == seq_len - 1, 0.0, pltpu.roll(x, t_len - 1, 0))
        return x + mu * (0.5 * (prev + nxt) - x)

    r = tshift(r_ref[...], mur_ref[...])
    k = tshift(k_ref[...], muk_ref[...])
    v = tshift(v_ref[...], muv_ref[...])
    lo = tshift(lora_ref[...], mul_ref[...])
    wd = jnp.tanh(lo[:, 0:PAIR])
    ad = lo[:, PAIR:2 * PAIR]
    gate = _bdot(jax.nn.sigmoid(lo[:, 2 * PAIR:3 * PAIR]), gup_ref[...])

    ri = lax.broadcasted_iota(jnp.int32, (PAIR, PAIR), 0)
    ci = lax.broadcasted_iota(jnp.int32, (PAIR, PAIR), 1)
    same_head = (ri // HEAD) == (ci // HEAD)
    ones_bd = jnp.where(same_head, 1.0, 0.0).astype(BF16)
    eye = ri == ci

    kk = k * kk_ref[...]
    ssq = _dot_ones_rhs(kk * kk, ones_bd)
    kk = kk * lax.rsqrt(jnp.maximum(ssq, 1e-24))
    wup2 = wup_ref[...].reshape(PAIR, PAIR)
    aup2 = aup_ref[...].reshape(PAIR, PAIR)
    bonus = jnp.zeros((t_len, PAIR), F32)
    for z in range(2):
        zrows = (ri // HEAD) == z
        w_pre = w0_ref[z:z + 1, :] + _bdot(wd, jnp.where(zrows, wup2, 0.0))
        lw_s[z] = -EXP_M05 * jax.nn.sigmoid(w_pre)
        a = jax.nn.sigmoid(a0_ref[z:z + 1, :] + _bdot(ad, jnp.where(zrows, aup2, 0.0)))
        kz = k * (1.0 + (a - 1.0) * ka_ref[...])
        kz_s[z] = kz
        bz_s[z] = kk * a
        bonus = bonus + _dot_ones_rhs(r * kz * rk_ref[...], ones_bd) * v
    rr_s[...] = r
    vv_s[...] = v
    kn_s[...] = kk

    rt_i = ri % CHUNK
    cs_i = ci % CHUNK
    strict = (cs_i < rt_i, cs_i > rt_i)
    incl = (cs_i <= rt_i, cs_i >= rt_i)
    t64 = lax.broadcasted_iota(jnp.int32, (CHUNK, CHUNK), 0)
    s64 = lax.broadcasted_iota(jnp.int32, (CHUNK, CHUNK), 1)
    tri = (jnp.where(s64 <= t64, 1.0, 0.0).astype(BF16), jnp.where(s64 >= t64, 1.0, 0.0).astype(BF16))
    lane_head0 = lax.broadcasted_iota(jnp.int32, (CHUNK, PAIR), 1) < HEAD
    eye_f = jnp.where(eye, 1.0, 0.0)
    same16 = (ri // INV_BLOCK) == (ci // INV_BLOCK)

    def stack(x):
        return jnp.concatenate([jnp.where(lane_head0, x, 0.0), jnp.where(lane_head0, 0.0, x)], axis=0)

    def fold(x):
        return x[:CHUNK] + x[CHUNK:]

    def chunk_rows(c):
        return pl.ds(pl.multiple_of(c * CHUNK, CHUNK), CHUNK)

    def each_item(body):
        def both_directions(c, carry):
            body(0, c)
            body(1, c)
            return carry

        lax.fori_loop(0, n_chunks, both_directions, 0, unroll=unroll)

    def stage_operands(z, c):
        rows = chunk_rows(c)
        r_c, kn_c = rr_s[rows, :], kn_s[rows, :]
        lw_c, kz_c, bz_c = lw_s[z, rows, :], kz_s[z, rows, :], bz_s[z, rows, :]
        lg = _dot_ones_lhs(tri[z], lw_c)
        total = lg[CHUNK - 1:CHUNK, :] if z == 0 else lg[0:1, :]
        e_neg = jnp.exp(-lg)
        e_rem = jnp.exp(total - lg)
        abar = stack(kn_c * jnp.exp(lg - lw_c))
        rt = stack(r_c * jnp.exp(lg))
        big = _bdot_nt(jnp.concatenate([abar, rt], axis=0),
                       jnp.concatenate([stack(bz_c * e_neg), stack(kz_c * e_neg)], axis=0))
        mb = jnp.where(strict[z], big[:c2, :c2], 0.0)
        abar_s[z, c] = abar.astype(BF16)
        rt_s[z, c] = rt
        wt_s[z, c] = jnp.concatenate([stack(kz_c * e_rem), -stack(bz_c * e_rem)], axis=0).T.astype(BF16)
        gl_s[z, c] = jnp.exp(total)
        m0 = jnp.where(same16, mb, 0.0)
        tm_s[z, c] = eye_f - m0
        p_s[z, c] = m0.astype(BF16)
        mb_s[z, c] = mb.astype(BF16)
        mk_s[z, c] = jnp.where(strict[z], big[:c2, c2:], 0.0).astype(BF16)
        nkb_s[z, c] = jnp.concatenate([jnp.where(incl[z], big[c2:, c2:], 0.0),
                                       -jnp.where(incl[z], big[c2:, :c2], 0.0)], axis=1).astype(BF16)

    def stage_values(c, carry):
        vst_s[c] = stack(vv_s[chunk_rows(c), :]).astype(BF16)
        return carry

    lax.fori_loop(0, n_chunks, stage_values, 0, unroll=unroll)
    each_item(stage_operands)

    def stage_square(z, c):
        m0 = p_s[z, c]
        p_s[z, c] = jnp.dot(m0, m0, preferred_element_type=F32).astype(BF16)
        mkv = jnp.dot(mk_s[z, c], vst_s[c], preferred_element_type=F32)
        txr_s[z, c] = jnp.concatenate([abar_s[z, c], mkv.astype(BF16)], axis=1)

    each_item(stage_square)

    n_round = INV_BLOCK.bit_length() - 2
    for j in range(n_round):
        def stage_double(z, c, last=(j == n_round - 1)):
            tm, p = tm_s[z, c], p_s[z, c]
            tm_s[z, c] = tm + jnp.dot(tm.astype(BF16), p, preferred_element_type=F32)
            if not last:
                p_s[z, c] = jnp.dot(p, p, preferred_element_type=F32).astype(BF16)

        each_item(stage_double)

    size = INV_BLOCK
    while size < CHUNK:
        off_diag = ((ri // (2 * size)) == (ci // (2 * size))) & ((ri // size) != (ci // size))

        def stage_merge_rhs(z, c, off_diag=off_diag):
            cross = jnp.where(off_diag, mb_s[z, c], jnp.zeros((), BF16))
            p_s[z, c] = jnp.dot(cross, tm_s[z, c].astype(BF16), preferred_element_type=F32).astype(BF16)

        def stage_merge(z, c):
            tm = tm_s[z, c]
            tm_s[z, c] = tm - jnp.dot(tm.astype(BF16), p_s[z, c], preferred_element_type=F32)

        each_item(stage_merge_rhs)
        each_item(stage_merge)
        size *= 2

    def stage_apply(z, c):
        tx_s[z, c] = jnp.dot(tm_s[z, c].astype(BF16), txr_s[z, c], preferred_element_type=F32)

    each_item(stage_apply)

    def stage_maps(z, c):
        tx = tx_s[z, c].astype(BF16)
        v_st = vst_s[c]
        wt = wt_s[z, c]
        from_tx = jnp.dot(wt[:, c2:], tx, preferred_element_type=F32)
        h_s[z, c] = jnp.dot(wt[:, :c2], v_st, preferred_element_type=F32) + from_tx[:, PAIR:]
        g_s[z, c] = jnp.where(eye, gl_s[z, c], 0.0) + from_tx[:, :PAIR]
        nkb = nkb_s[z, c]
        from_tx = jnp.dot(nkb[:, c2:], tx, preferred_element_type=F32)
        q_s[z, c] = fold(rt_s[z, c] + from_tx[:, :PAIR])
        y0_s[z, c] = fold(jnp.dot(nkb[:, :c2], v_st, preferred_element_type=F32) + from_tx[:, PAIR:])

    each_item(stage_maps)

    zero64 = jnp.zeros((HEAD, HEAD), F32)
    chains = [(n, z) for n in range(seqs) for z in range(2)]
    for n, z in chains:
        a = jnp.concatenate([jnp.concatenate([s0_ref[n, z, 0], zero64], axis=1),
                             jnp.concatenate([zero64, s0_ref[n, z, 1]], axis=1)], axis=0)
        st_s[n, z] = a.T

    def propagate(i, carry):
        for n, z in chains:
            c = n * seq_chunks + (i if z == 0 else seq_chunks - 1 - i)
            s = st_s[n, z].astype(BF16)
            y_s[z, chunk_rows(c), :] = _bdot(q_s[z, c], s) + y0_s[z, c]
            st_s[n, z] = _bdot(g_s[z, c], s) + h_s[z, c]
        return carry

    lax.fori_loop(0, seq_chunks, propagate, 0)
    for n, z in chains:
        a = st_s[n, z].T
        sfin_ref[n, z, 0] = a[:HEAD, :HEAD]
        sfin_ref[n, z, 1] = a[HEAD:, HEAD:]

    y = y_s[0] + y_s[1]
    mu = _dot_ones_rhs(y, ones_bd) * (1.0 / HEAD)
    yc = y - mu
    var = _dot_ones_rhs(yc * yc, ones_bd) * (1.0 / HEAD)
    yn = yc * lax.rsqrt(var + GN_EPS) * gng_ref[...] + gnb_ref[...]
    out_ref[...] = (yn + bonus) * gate


def _rwkv_mix(pa, first_tok, n_seq, seq_len, s0, p, ride=()):
    a_width = p["w0"].shape[-1]
    n_pairs = a_width // PAIR
    heads = a_width // HEAD
    seqs = max(1, RWKV_STEP_ROWS // seq_len)
    rows = seqs * seq_len
    assert n_seq % seqs == 0 and first_tok % rows == 0
    t0 = first_tok // rows
    lora_w = 3 * PAIR
    lora_blk = 3 * a_width // lora_w
    n_chunks = rows // CHUNK
    col = lambda off: (lambda s, q: (t0 + s, off + q))
    vec = lambda off: (lambda s, q: (0, off + q))
    in_specs = [
        pl.BlockSpec((rows, PAIR), col(0)),
        pl.BlockSpec((rows, PAIR), col(n_pairs)),
        pl.BlockSpec((rows, PAIR), col(2 * n_pairs)),
        pl.BlockSpec((rows, lora_w), lambda s, q: (t0 + s, lora_blk)),
        pl.BlockSpec((1, PAIR), vec(0)),
        pl.BlockSpec((1, PAIR), vec(n_pairs)),
        pl.BlockSpec((1, PAIR), vec(2 * n_pairs)),
        pl.BlockSpec((1, lora_w), lambda s, q: (0, lora_blk)),
        pl.BlockSpec((2, PAIR), vec(0)),
        pl.BlockSpec((2, HEAD, PAIR), lambda s, q: (0, 0, q)),
        pl.BlockSpec((2, PAIR), vec(0)),
        pl.BlockSpec((2, HEAD, PAIR), lambda s, q: (0, 0, q)),
        pl.BlockSpec((PAIR, PAIR), vec(0)),
        pl.BlockSpec((1, PAIR), vec(0)),
        pl.BlockSpec((1, PAIR), vec(0)),
        pl.BlockSpec((1, PAIR), vec(0)),
        pl.BlockSpec((1, PAIR), vec(0)),
        pl.BlockSpec((1, PAIR), vec(0)),
        pl.BlockSpec((seqs, 2, 2, HEAD, HEAD), lambda s, q: (s, 0, q, 0, 0)),
    ]
    seq2 = lambda: pltpu.VMEM((2, rows, PAIR), F32)
    seq1 = lambda: pltpu.VMEM((rows, PAIR), F32)
    item = lambda rows, cols, dt: pltpu.VMEM((2, n_chunks, rows, cols), dt)
    n_steps = n_seq // seqs * n_pairs
    assert all(a.shape[0] % (n_steps * BF16_ROWS) == 0 for a in ride)
    ride_specs = [pl.BlockSpec((a.shape[0] // n_steps, a.shape[1]), lambda s, q: (s * n_pairs + q, 0)) for a in ride]
    out, s_fin, *cast = pl.pallas_call(
        functools.partial(_rwkv_kernel, seq_len=seq_len, seqs=seqs, unroll=min(n_chunks, 16), n_ride=len(ride)),
        out_shape=[jax.ShapeDtypeStruct((n_seq * seq_len, a_width), F32),
                   jax.ShapeDtypeStruct((n_seq, 2, heads, HEAD, HEAD), F32)]
        + [jax.ShapeDtypeStruct(a.shape, BF16) for a in ride],
        grid=(n_seq // seqs, n_pairs),
        in_specs=in_specs + ride_specs,
        out_specs=[pl.BlockSpec((rows, PAIR), lambda s, q: (s, q)),
                   pl.BlockSpec((seqs, 2, 2, HEAD, HEAD), lambda s, q: (s, 0, q, 0, 0))] + ride_specs,
        scratch_shapes=[seq2(), seq2(), seq2(), seq1(), seq1(), seq1(),
                        item(PAIR, PAIR, BF16),
                        item(PAIR, PAIR, F32),
                        item(PAIR, 2 * PAIR, BF16),
                        pltpu.VMEM((n_chunks, PAIR, PAIR), BF16),
                        item(PAIR, PAIR, BF16),
                        item(PAIR, PAIR, F32),
                        item(PAIR, PAIR, BF16),
                        item(PAIR, PAIR, BF16),
                        item(PAIR, 2 * PAIR, BF16),
                        item(PAIR, 2 * PAIR, BF16),
                        item(PAIR, 2 * PAIR, F32),
                        item(1, PAIR, F32),
                        item(PAIR, PAIR, F32), item(PAIR, PAIR, F32),
                        item(CHUNK, PAIR, F32), item(CHUNK, PAIR, F32),
                        seq2(), pltpu.VMEM((seqs, 2, PAIR, PAIR), F32)],
        compiler_params=_cparams(("arbitrary", "arbitrary")),
        name=f"rwkv_mix_t{seq_len}",
    )(pa, pa, pa, pa, p["ts_mu"], p["ts_mu"], p["ts_mu"], p["ts_mu"],
      p["w0"], p["w_up"], p["a0"], p["a_up"], p["g_up"], p["k_k"], p["k_a"], p["r_k"], p["gn_g"], p["gn_b"], s0, *ride)
    return out, s_fin, cast


def _conv_kernel(pb_ref, prev_ref, next_ref, pwb_ref, dw_ref, dwb_ref, g_ref, b_ref, o_ref, pad_s, shift_s, *, geom, seq):
    width = o_ref.shape[1]
    sub = 64
    n_prompt_tok, dec_seq = geom
    tok = pl.program_id(0) * CONV_TILE
    in_prompt = tok < n_prompt_tok
    seq_len = jnp.where(in_prompt, seq, dec_seq)
    off = jnp.where(in_prompt, tok, tok - n_prompt_tok) % seq_len

    def glu(ref):
        pbv = ref[...] + pwb_ref[...]
        return pbv[:, :width] * jax.nn.sigmoid(pbv[:, width:])

    pad_s[0:CONV_HALO, :] = jnp.where(off > 0, glu(prev_ref), 0.0)
    pad_s[CONV_HALO:CONV_HALO + CONV_TILE, :] = glu(pb_ref)
    pad_s[CONV_HALO + CONV_TILE:, :] = jnp.where(off + CONV_TILE < seq_len, glu(next_ref), 0.0)
    keep = pad_s.shape[0] - SUBLANES
    for phase in range(SUBLANES):
        shift_s[phase, 0:keep, :] = pad_s[phase:phase + keep, :]
    first = CONV_HALO - CONV_W // 2
    for c in range(CONV_TILE // sub):
        acc = jnp.zeros((sub, width), F32)
        for d in range(CONV_W):
            phase = (first + d) % SUBLANES
            start = c * sub + first + d - phase
            acc = acc + shift_s[phase, start:start + sub, :] * dw_ref[d:d + 1, :]
        y = _layer_norm(acc + dwb_ref[...], g_ref[...], b_ref[...], LN_EPS)
        o_ref[c * sub:(c + 1) * sub, :] = y * jax.nn.sigmoid(y)


def _conv_module(pb, geom, seq, pw_b, dw_w, dw_b, cln_g, cln_b):
    n_tok = pb.shape[0]
    width = dw_w.shape[1]
    per_tile = CONV_TILE // CONV_HALO
    last_halo = n_tok // CONV_HALO - 1
    const = lambda i: (0, 0)
    return pl.pallas_call(
        functools.partial(_conv_kernel, geom=geom, seq=seq),
        out_shape=jax.ShapeDtypeStruct((n_tok, width), F32),
        grid=(n_tok // CONV_TILE,),
        in_specs=[pl.BlockSpec((CONV_TILE, 2 * width), lambda i: (i, 0)),
                  pl.BlockSpec((CONV_HALO, 2 * width), lambda i: (jnp.maximum(i * per_tile - 1, 0), 0)),
                  pl.BlockSpec((CONV_HALO, 2 * width), lambda i: (jnp.minimum((i + 1) * per_tile, last_halo), 0)),
                  pl.BlockSpec((1, 2 * width), const),
                  pl.BlockSpec((CONV_W, width), const),
                  pl.BlockSpec((1, width), const),
                  pl.BlockSpec((1, width), const),
                  pl.BlockSpec((1, width), const)],
        out_specs=pl.BlockSpec((CONV_TILE, width), lambda i: (i, 0)),
        scratch_shapes=[pltpu.VMEM((CONV_TILE + 2 * CONV_HALO, width), F32),
                        pltpu.VMEM((SUBLANES, CONV_TILE + 2 * CONV_HALO, width), F32)],
        compiler_params=_cparams(("arbitrary",)),
        name="conv_module",
    )(pb, pb, pb, pw_b.reshape(1, -1), dw_w, dw_b.reshape(1, -1), cln_g.reshape(1, -1), cln_b.reshape(1, -1))


def _diff_lambda(lq1, lk1, lq2, lk2, lam_init):
    dot = lambda a, b: jnp.exp(jnp.sum(a[...] * b[...], axis=-1, keepdims=True))
    return dot(lq1, lk1) - dot(lq2, lk2) + lam_init


def _diff_attend(q, keys, vals, lam, lam_init, subln):
    nq = q.shape[0]
    lane_first = lax.broadcasted_iota(jnp.int32, q.shape, 1) < HEAD
    q = q * (HEAD ** -0.5)
    q_st = jnp.concatenate([jnp.where(lane_first, q, 0.0), jnp.where(lane_first, 0.0, q)], axis=0)
    s = _bdot_nt(q_st, keys)
    s = s - jnp.max(s, axis=-1, keepdims=True)
    pexp = jnp.exp(s)
    ov = _bdot(pexp, vals) / jnp.sum(pexp, axis=-1, keepdims=True)
    o = ov[:nq] - lam * ov[nq:]
    o = o * lax.rsqrt(jnp.mean(o * o, axis=-1, keepdims=True) + RMS_EPS)
    return o * subln * (1.0 - lam_init)


def _head_cols(h):
    return slice(h * PAIR, (h + 1) * PAIR)


def _attn_prompt_kernel(q_ref, k_ref, v_ref, lq1, lk1, lq2, lk2, sub_ref, o_ref, *, lam_init):
    lam = _diff_lambda(lq1, lk1, lq2, lk2, lam_init)
    for h in range(q_ref.shape[1] // PAIR):
        c = _head_cols(h)
        o_ref[:, c] = _diff_attend(q_ref[:, c], k_ref[:, c], v_ref[:, c], lam, lam_init, sub_ref[...])


def _rope(x, cos, sin_signed):
    lane = lax.broadcasted_iota(jnp.int32, x.shape, 1)
    quarter = HEAD // 4
    partner = jnp.where((lane // quarter) % 2 == 0,
                        pltpu.roll(x, LANES - quarter, 1), pltpu.roll(x, quarter, 1))
    return x * cos + partner * sin_signed


def _attn_sample_kernel(q_ref, k_ref, v_ref, ck_ref, cv_ref, cq_ref, sq_ref, cosk_ref, sink_ref,
                        lq1, lk1, lq2, lk2, sub_ref, o_ref, *, lam_init):
    lam = _diff_lambda(lq1, lk1, lq2, lk2, lam_init)
    for h in range(q_ref.shape[1] // PAIR):
        c = _head_cols(h)
        q = _rope(q_ref[:, c], cq_ref[...], sq_ref[...])
        kl = _rope(k_ref[:, c], cosk_ref[...], sink_ref[...])
        keys = jnp.concatenate([ck_ref[:, c], kl], axis=0)
        vals = jnp.concatenate([cv_ref[:, c], v_ref[:, c]], axis=0)
        o_ref[:, c] = _diff_attend(q, keys, vals, lam, lam_init, sub_ref[...])


def _rope_tables(n_lat, grid_w):
    quarter = HEAD // 4
    inv = ROPE_BASE ** (-jnp.arange(quarter, dtype=F32) / quarter)
    t = jnp.arange(n_lat)
    row_ang = (t // grid_w).astype(F32)[:, None] * inv
    col_ang = (t % grid_w).astype(F32)[:, None] * inv
    ang = jnp.concatenate([row_ang, row_ang, col_ang, col_ang] * 2, axis=-1)
    sign = jnp.tile(jnp.concatenate([-jnp.ones(quarter, F32), jnp.ones(quarter, F32)]), 2 * HEAD // (2 * quarter))
    return jnp.cos(ang), jnp.sin(ang) * sign


def _attention(q, k_p, v_p, k_s, v_s, cache_k, cache_v, p, lam_init, n_prompt, seq, n_lat_req, n_lat, grid_w):
    inner = q.shape[1]
    vec = lambda n: pl.BlockSpec((1, n), lambda *_: (0, 0))
    lam_args = [p[n].reshape(1, -1) for n in ("lq1", "lk1", "lq2", "lk2")] + [p["subln_g"].reshape(1, -1)]
    lam_specs = [vec(HEAD)] * 4 + [vec(PAIR)]
    blk = pl.BlockSpec((seq, inner), lambda b: (b, 0))
    o_prompt = pl.pallas_call(
        functools.partial(_attn_prompt_kernel, lam_init=lam_init),
        out_shape=jax.ShapeDtypeStruct((n_prompt * seq, inner), F32),
        grid=(n_prompt,),
        in_specs=[blk, blk, blk] + lam_specs,
        out_specs=blk,
        compiler_params=_cparams(("arbitrary",)),
        name="diff_attn_prompt",
    )(q, k_p, v_p, *lam_args)

    past = cache_k.shape[1]
    qt = SUB_TILE
    n_qt = n_lat // qt
    q0 = n_prompt * seq // qt
    cos, sin = _rope_tables(n_lat, grid_w)
    lat = pl.BlockSpec((n_lat, inner), lambda b, t: (b, 0))
    ctx = pl.BlockSpec((None, past, inner), lambda b, t: (b, 0, 0))
    o_sample = pl.pallas_call(
        functools.partial(_attn_sample_kernel, lam_init=lam_init),
        out_shape=jax.ShapeDtypeStruct((n_lat_req * n_lat, inner), F32),
        grid=(n_lat_req, n_qt),
        in_specs=[pl.BlockSpec((qt, inner), lambda b, t: (q0 + b * n_qt + t, 0)),
                  lat, lat, ctx, ctx,
                  pl.BlockSpec((qt, PAIR), lambda b, t: (t, 0)),
                  pl.BlockSpec((qt, PAIR), lambda b, t: (t, 0)),
                  pl.BlockSpec((n_lat, PAIR), lambda b, t: (0, 0)),
                  pl.BlockSpec((n_lat, PAIR), lambda b, t: (0, 0))] + lam_specs,
        out_specs=pl.BlockSpec((qt, inner), lambda b, t: (b * n_qt + t, 0)),
        compiler_params=_cparams(("arbitrary", "arbitrary")),
        name="diff_attn_sample",
    )(q, k_s, v_s, cache_k, cache_v, cos, sin, cos, sin, *lam_args)
    return o_prompt, o_sample


def kernel(x_prompt, x_sample, state_wkv, cache_k, cache_v, c, c_ctx, mod_w, mod_b, ln1_g, ln1_b, ln2_g, ln2_b, ev_in_w, ev_pw_b, ev_ts_mu, ev_w0, ev_w_up, ev_a0, ev_a_up, ev_g_up, ev_k_k, ev_k_a, ev_r_k, ev_gn_g, ev_gn_b, ev_dw_w, ev_dw_b, ev_cln_g, ev_cln_b, ev_out_w, od_in_w, od_lq1, od_lk1, od_lq2, od_lk2, od_subln_g, od_out_w, rg_w, rg_b, re_w, re_b, ffn_w_gate, ffn_w_up, ffn_w_down):
    nbp, seq, d = x_prompt.shape
    nbs, n_lat, _ = x_sample.shape
    depth = mod_w.shape[0]
    n_prompt_tok = nbp * seq
    geom = (n_prompt_tok, n_lat)
    alpha = (2 * depth) ** 0.25
    a_width = ev_w0.shape[-1]
    a_cols = ev_ts_mu.shape[-1]
    heads = a_width // HEAD
    grid_w = 64
    assert n_prompt_tok % MOE_TILE == 0 and n_lat % MOE_TILE == 0 and n_lat % SUB_TILE == 0
    assert seq % CONV_TILE == 0 and n_lat % CONV_TILE == 0 and CONV_HALO >= CONV_W // 2
    assert 1 + nbs <= 8

    xs = (x_prompt.reshape(n_prompt_tok, d), x_sample.reshape(nbs * n_lat, d))
    cond = jnp.concatenate([c_ctx[None, :], c, jnp.zeros((8 - 1 - nbs, d), F32)], axis=0)
    flat = lambda w: w.reshape(-1, w.shape[-1])
    mod, (ev_in_b,) = _ada_table(cond, mod_w, mod_b, (flat(ev_in_w),))
    ev_in_b = ev_in_b.reshape(ev_in_w.shape)
    late = (ffn_w_gate, ffn_w_up, ffn_w_down, ev_out_w, od_in_w, od_out_w)

    new_wkv, new_k, new_v = [], [], []
    for l in range(depth):
        last = l == depth - 1
        if l % 2 == 0:
            i = l // 2
            pa, pb = _mod_matmul(xs, mod, l, ev_in_b[i], ((a_cols, False), (ev_in_w.shape[-1] - a_cols, False)), geom)
            prm = dict(ts_mu=ev_ts_mu[i][None, :], w0=ev_w0[i], w_up=ev_w_up[i], a0=ev_a0[i], a_up=ev_a_up[i],
                       g_up=ev_g_up[i], k_k=ev_k_k[i][None, :], k_a=ev_k_a[i][None, :],
                       r_k=ev_r_k[i].reshape(1, a_width), gn_g=ev_gn_g[i][None, :], gn_b=ev_gn_b[i][None, :])
            zero_state = jnp.zeros((nbp, 2, heads, HEAD, HEAD), F32)
            oa_p, st, cast = _rwkv_mix(pa, 0, nbp, seq, zero_state, prm, () if l else tuple(flat(w) for w in late))
            if cast:
                w_gate, w_up, w_down, ev_out_b, od_in_b, od_out_b = (c.reshape(w.shape) for c, w in zip(cast, late))
            oa_s, _, _ = _rwkv_mix(pa, n_prompt_tok, nbs, n_lat, state_wkv[:, i], prm)
            conv = (ev_pw_b[i], ev_dw_w[i], ev_dw_b[i], ev_cln_g[i], ev_cln_b[i])
            hs = [(oa_p, oa_s), (_conv_module(pb, geom, seq, *conv),)]
            out_w = ev_out_b[i]
            new_wkv.append(st)
        else:
            j = l // 2
            lam_init = 0.8 - 0.6 * math.exp(-0.3 * l)
            inner = od_in_w.shape[-1] // 3
            q, k_p, k_s, v_p, v_s = _mod_matmul(xs, mod, l, od_in_b[j],
                                                 ((inner, False), (inner, True), (inner, True)), geom)
            prm = dict(lq1=od_lq1[j], lk1=od_lk1[j], lq2=od_lq2[j], lk2=od_lk2[j], subln_g=od_subln_g[j])
            ck = cache_k[:, j].reshape(nbs, -1, inner)
            cv = cache_v[:, j].reshape(nbs, -1, inner)
            hs = [_attention(q, k_p, v_p, k_s, v_s, ck, cv, prm, lam_init, nbp, seq, nbs, n_lat, grid_w)]
            out_w = od_out_b[j]
            new_k.append(k_p.reshape(nbp, seq, inner // PAIR, PAIR))
            new_v.append(v_p.reshape(nbp, seq, inner // PAIR, PAIR))
        x = _out_proj_norm(hs, xs, mod, l, out_w, ln1_g[l], ln1_b[l], alpha, geom)
        rw = jnp.concatenate([rg_w[l], re_w[l],
                              jnp.zeros((d, ROUTER_LANES - N_GROUPS - N_EXPERTS), F32)], axis=1)
        rb = jnp.concatenate([rg_b[l], re_b[l], jnp.zeros((ROUTER_LANES - N_GROUPS - N_EXPERTS,), F32)])[None, :]
        xs = tuple(_moe_norm(x, mod, l, rw, rb, w_gate, w_up, w_down, ln2_g[l], ln2_b[l], alpha, geom,
                             split_out=last))

    y_prompt = xs[0].reshape(nbp, seq, d)
    y_sample = xs[1].reshape(nbs, n_lat, d)
    return (y_prompt, y_sample, jnp.stack(new_wkv, axis=1), jnp.stack(new_k, axis=1), jnp.stack(new_v, axis=1))
```

```python
import functools
import math

import jax
import jax.numpy as jnp
from jax import lax
from jax.experimental import pallas as pl
from jax.experimental.pallas import tpu as pltpu

F32 = jnp.float32
BF16 = jnp.bfloat16

LANES = 128
SUBLANES = 8
HEAD = 64
PAIR = 2 * HEAD
CHUNK = 64
RWKV_STEP_ROWS = 512
INV_BLOCK = 16
PROJ_TILES = (1024, 512, 256)
PROJ_VMEM_BUDGET = 48 * 1024 * 1024
SUB_TILE = 256
MOE_TILE = 1024
ADA_PER_STEP = 2
CONV_W = 31
CONV_TILE = 256
CONV_HALO = 16
N_GROUPS = 4
EXPERTS_PER_GROUP = 8
N_EXPERTS = N_GROUPS * EXPERTS_PER_GROUP
ROUTER_LANES = 128
EXPERTS_PER_STEP = 4
MOE_WEIGHT_SLOTS = 3
MOE_WINDOWS = (128, 256, 384, 512)
BF16_ROWS = 16
LN_EPS = 1e-5
GN_EPS = 64e-5
RMS_EPS = 1e-5
ROPE_BASE = 10000.0
EXP_M05 = math.exp(-0.5)
VMEM_LIMIT = 60 * 1024 * 1024


def _bdot(a, b):
    return jnp.dot(a.astype(BF16), b.astype(BF16), preferred_element_type=F32)


def _bdot_nt(a, b):
    return lax.dot_general(a.astype(BF16), b.astype(BF16), (((1,), (1,)), ((), ())),
                           preferred_element_type=F32)


def _split2(a):
    hi = a.astype(BF16)
    lo = (a - hi.astype(F32)).astype(BF16)
    return hi, lo


def _dot_f32ish(a, b):
    ah, al = _split2(a)
    bh, bl = _split2(b)
    d = lambda x, y: jnp.dot(x, y, preferred_element_type=F32)
    return d(ah, bh) + d(ah, bl) + d(al, bh)


def _dot_ones_lhs(m_bf16, a):
    h, l = _split2(a)
    d = lambda y: jnp.dot(m_bf16, y, preferred_element_type=F32)
    return d(h) + d(l)


def _dot_ones_rhs(a, m_bf16):
    h, l = _split2(a)
    d = lambda x: jnp.dot(x, m_bf16, preferred_element_type=F32)
    return d(h) + d(l)


def _layer_norm(z, g, b, eps):
    mu = jnp.mean(z, axis=-1, keepdims=True)
    zc = z - mu
    var = jnp.mean(zc * zc, axis=-1, keepdims=True)
    return zc * lax.rsqrt(var + eps) * g + b


def _cond_row(tile, tile_rows, n_prompt_tok, dec_seq):
    tok = tile * tile_rows
    return jnp.where(tok < n_prompt_tok, 0, 1 + (tok - n_prompt_tok) // dec_seq)


def _cparams(sem):
    return pltpu.CompilerParams(dimension_semantics=sem, vmem_limit_bytes=VMEM_LIMIT)


def _ada_kernel(cond_ref, w_ref, b_ref, *rest, n_ride):
    ride_in, o_ref, ride_out = rest[:n_ride], rest[n_ride], rest[n_ride + 1:]

    @pl.when(pl.program_id(1) == 0)
    def _():
        for src, dst in zip(ride_in, ride_out):
            dst[...] = src[...].astype(dst.dtype)

    c = cond_ref[...]
    s = c * jax.nn.sigmoid(c)
    d = o_ref.shape[-1]
    y = _dot_f32ish(s, w_ref[0])
    for j in range(ADA_PER_STEP):
        o_ref[0, j] = y[:, j * d:(j + 1) * d] + b_ref[0, j]


def _ada_table(cond8, mod_w, mod_b, ride=()):
    depth, d, _ = mod_w.shape
    assert all(a.shape[0] % (depth * BF16_ROWS) == 0 for a in ride)
    ride_specs = [pl.BlockSpec((a.shape[0] // depth, a.shape[1]), lambda l, j: (l, 0)) for a in ride]
    mod, *cast = pl.pallas_call(
        functools.partial(_ada_kernel, n_ride=len(ride)),
        out_shape=[jax.ShapeDtypeStruct((depth, 6, 8, d), F32)] + [jax.ShapeDtypeStruct(a.shape, BF16) for a in ride],
        grid=(depth, 6 // ADA_PER_STEP),
        in_specs=[pl.BlockSpec((8, d), lambda l, j: (0, 0)),
                  pl.BlockSpec((1, d, ADA_PER_STEP * d), lambda l, j: (l, 0, j)),
                  pl.BlockSpec((1, ADA_PER_STEP, 1, d), lambda l, j: (l, j, 0, 0))] + ride_specs,
        out_specs=[pl.BlockSpec((1, ADA_PER_STEP, 8, d), lambda l, j: (l, j, 0, 0))] + ride_specs,
        compiler_params=_cparams(("arbitrary", "arbitrary")),
        name="ada_table",
    )(cond8, mod_w, mod_b.reshape(depth, 6, 1, d), *ride)
    return mod, cast


def _mod_spec(layer, which, d):
    return pl.BlockSpec((None, None, 8, d), lambda *_: (layer, which, 0, 0))


def _group_specs(tile, width, n_prompt_tiles):
    return [pl.BlockSpec((tile, width), lambda i, *_: (jnp.minimum(i, n_prompt_tiles - 1), 0)),
            pl.BlockSpec((tile, width), lambda i, *_: (jnp.maximum(i - n_prompt_tiles, 0), 0))]


def _load_group(i, n_prompt_tiles, refs, rows=slice(None)):
    if len(refs) == 1:
        return refs[0][rows, :]
    return jnp.where(i < n_prompt_tiles, refs[0][rows, :], refs[1][rows, :])


def _store_group(i, n_prompt_tiles, refs, value, rows=slice(None)):
    if len(refs) == 1:
        refs[0][rows, :] = value
        return

    @pl.when(i < n_prompt_tiles)
    def _():
        refs[0][rows, :] = value

    @pl.when(i >= n_prompt_tiles)
    def _():
        refs[1][rows, :] = value


def _proj_tile(cols_f32, w_elems, geom):
    for tile in PROJ_TILES:
        fits = 2 * 4 * tile * cols_f32 + 2 * w_elems <= PROJ_VMEM_BUDGET
        if fits and geom[0] % tile == 0 and geom[1] % tile == 0:
            return tile
    raise ValueError("no projection tile fits")


def _modmm_kernel(*refs, n_x, outs, geom, tile):
    x_refs = refs[:n_x]
    sh_ref, sc_ref, w_ref = refs[n_x:n_x + 3]
    o_refs = refs[n_x + 3:]
    i = pl.program_id(0)
    n_pt = geom[0] // tile
    r = _cond_row(i, tile, *geom)
    h = _load_group(i, n_pt, x_refs) * (1.0 + sc_ref[pl.ds(r, 1), :]) + sh_ref[pl.ds(r, 1), :]
    hb = h.astype(BF16)
    off = 0
    k = 0
    for n, split in outs:
        y = jnp.dot(hb, w_ref[:, off:off + n], preferred_element_type=F32)
        _store_group(i, n_pt, o_refs[k:k + 1 + split], y)
        off += n
        k += 1 + split


def _mod_matmul(xs, mod, layer, w, outs, geom):
    d, n_out = w.shape
    n_prompt_tok = geom[0]
    n_tok = sum(x.shape[0] for x in xs)
    tile = _proj_tile(len(xs) * d + sum(n * (1 + split) for n, split in outs), w.size, geom)
    n_pt = n_prompt_tok // tile
    whole = lambda n: [pl.BlockSpec((tile, n), lambda i: (i, 0))]
    x_specs = whole(d) if len(xs) == 1 else _group_specs(tile, d, n_pt)
    out_shape, out_specs = [], []
    for n, split in outs:
        if split:
            out_shape += [jax.ShapeDtypeStruct((n_prompt_tok, n), F32),
                          jax.ShapeDtypeStruct((n_tok - n_prompt_tok, n), F32)]
            out_specs += _group_specs(tile, n, n_pt)
        else:
            out_shape += [jax.ShapeDtypeStruct((n_tok, n), F32)]
            out_specs += whole(n)
    return pl.pallas_call(
        functools.partial(_modmm_kernel, n_x=len(xs), outs=outs, geom=geom, tile=tile),
        out_shape=out_shape,
        grid=(n_tok // tile,),
        in_specs=x_specs + [_mod_spec(layer, 0, d), _mod_spec(layer, 1, d),
                            pl.BlockSpec((d, n_out), lambda i: (0, 0), pipeline_mode=pl.Buffered(1))],
        out_specs=out_specs,
        compiler_params=_cparams(("arbitrary",)),
        name=f"mod_matmul_l{layer}",
    )(*xs, mod, mod, w)


def _outproj_kernel(*refs, arity, alpha, geom, tile):
    groups, k = [], 0
    for n in arity:
        groups.append(refs[k:k + n])
        k += n
    gate_ref, w_ref, g_ref, b_ref, o_ref = refs[k:]
    h_groups, x_group = groups[:-1], groups[-1]
    i = pl.program_id(0)
    n_pt = geom[0] // tile
    r = _cond_row(i, tile, *geom)
    y = None
    off = 0
    for h_refs in h_groups:
        n = h_refs[0].shape[1]
        h = _load_group(i, n_pt, h_refs)
        part = jnp.dot(h.astype(BF16), w_ref[off:off + n, :], preferred_element_type=F32)
        y = part if y is None else y + part
        off += n
    z = alpha * _load_group(i, n_pt, x_group) + gate_ref[pl.ds(r, 1), :] * y
    o_ref[...] = _layer_norm(z, g_ref[...], b_ref[...], LN_EPS)


def _out_proj_norm(hs, xs, mod, layer, w, ln_g, ln_b, alpha, geom):
    d = w.shape[1]
    n_tok = sum(x.shape[0] for x in xs)
    tile = _proj_tile(d + sum(len(g) * g[0].shape[1] for g in list(hs) + [xs]), w.size, geom)
    n_pt = geom[0] // tile
    specs = []
    for grp in list(hs) + [xs]:
        n = grp[0].shape[1]
        specs += [pl.BlockSpec((tile, n), lambda i: (i, 0))] if len(grp) == 1 else _group_specs(tile, n, n_pt)
    return pl.pallas_call(
        functools.partial(_outproj_kernel, arity=tuple(len(g) for g in list(hs) + [xs]), alpha=alpha, geom=geom,
                          tile=tile),
        out_shape=jax.ShapeDtypeStruct((n_tok, d), F32),
        grid=(n_tok // tile,),
        in_specs=specs + [
            _mod_spec(layer, 2, d),
            pl.BlockSpec(w.shape, lambda i: (0, 0), pipeline_mode=pl.Buffered(1)),
            pl.BlockSpec((1, d), lambda i: (0, 0)),
            pl.BlockSpec((1, d), lambda i: (0, 0))],
        out_specs=pl.BlockSpec((tile, d), lambda i: (i, 0)),
        compiler_params=_cparams(("arbitrary",)),
        name=f"out_proj_norm_l{layer}",
    )(*[a for g in hs for a in g], *xs, mod, w, ln_g.reshape(1, d), ln_b.reshape(1, d))


def _route(logits):
    neg = jnp.float32(-3e38)
    big = jnp.int32(1 << 20)
    lane = lax.broadcasted_iota(jnp.int32, logits.shape, 1)
    is_g = lane < N_GROUPS
    gl = jnp.where(is_g, logits, neg)
    gmax = jnp.max(gl, axis=-1, keepdims=True)
    gidx = jnp.min(jnp.where(gl == gmax, lane, big), axis=-1, keepdims=True)
    gsum = jnp.sum(jnp.where(is_g, jnp.exp(gl - gmax), 0.0), axis=-1, keepdims=True)
    g_p = 1.0 / gsum
    lo = N_GROUPS + gidx * EXPERTS_PER_GROUP
    in_grp = (lane >= lo) & (lane < lo + EXPERTS_PER_GROUP)
    el = jnp.where(in_grp, logits, neg)
    v1 = jnp.max(el, axis=-1, keepdims=True)
    i1 = jnp.min(jnp.where(el == v1, lane, big), axis=-1, keepdims=True)
    el2 = jnp.where(lane == i1, neg, el)
    v2 = jnp.max(el2, axis=-1, keepdims=True)
    i2 = jnp.min(jnp.where(el2 == v2, lane, big), axis=-1, keepdims=True)
    e21 = jnp.exp(v2 - v1)
    den = 1.0 + e21
    w1 = g_p / den
    w2 = g_p * e21 / den
    return jnp.where(lane == i1, w1, 0.0) + jnp.where(lane == i2, w2, 0.0), gidx


def _moe_kernel(x_ref, sh_ref, sc_ref, gate_ref, rw_ref, rb_ref, wg_ref, wu_ref, wd_ref, g_ref, b_ref,
                *rest, alpha, geom, layer):
    o_refs = rest[:-10]
    hu_scr, h_scr, gates_scr, acc_scr, perm_scr, bounds, wg_buf, wu_buf, wd_buf, w_sem = rest[-10:]
    i = pl.program_id(0)
    e = pl.program_id(1)
    n_e = pl.num_programs(1)

    def weight_copies(step):
        blk = pl.ds((step % n_e) * EXPERTS_PER_STEP, EXPERTS_PER_STEP)
        slot = step % MOE_WEIGHT_SLOTS
        return [pltpu.make_async_copy(src.at[layer, blk], dst.at[slot], w_sem.at[k, slot])
                for k, (src, dst) in enumerate(((wg_ref, wg_buf), (wu_ref, wu_buf), (wd_ref, wd_buf)))]

    step = i * n_e + e
    n_steps = pl.num_programs(0) * n_e

    @pl.when(step == 0)
    def _():
        for ahead in range(MOE_WEIGHT_SLOTS - 1):
            for cp in weight_copies(ahead):
                cp.start()

    @pl.when(step + MOE_WEIGHT_SLOTS - 1 < n_steps)
    def _():
        for cp in weight_copies(step + MOE_WEIGHT_SLOTS - 1):
            cp.start()
    r = _cond_row(i, MOE_TILE, *geom)
    n_pt = geom[0] // MOE_TILE
    sub = SUB_TILE
    n_sub = MOE_TILE // sub

    @pl.when(e == 0)
    def _():
        lane = lax.broadcasted_iota(jnp.int32, (sub, ROUTER_LANES), 1)
        ri = lax.broadcasted_iota(jnp.int32, (sub, sub), 0)
        ci = lax.broadcasted_iota(jnp.int32, (sub, sub), 1)
        tril = jnp.where(ci <= ri, 1.0, 0.0).astype(BF16)
        running = jnp.zeros((1, ROUTER_LANES), F32)
        onehots, cums = [], []
        for s in range(n_sub):
            rows = slice(s * sub, (s + 1) * sub)
            h = x_ref[rows, :] * (1.0 + sc_ref[pl.ds(r, 1), :]) + sh_ref[pl.ds(r, 1), :]
            hu_scr[rows, :] = h.astype(BF16)
            logits = _dot_f32ish(h, rw_ref[...]) + rb_ref[...]
            gates, gidx = _route(logits)
            gates_scr[rows, :] = gates
            onehot = jnp.where(lane == gidx, 1.0, 0.0)
            cum = jnp.dot(tril, onehot.astype(BF16), preferred_element_type=F32) + running
            running = cum[sub - 1:sub, :]
            onehots.append(onehot)
            cums.append(cum)
        gi = lax.broadcasted_iota(jnp.int32, (ROUTER_LANES, ROUTER_LANES), 0)
        gj = lax.broadcasted_iota(jnp.int32, (ROUTER_LANES, ROUTER_LANES), 1)
        before = jnp.where((gi < gj) & (gi < N_GROUPS), 1.0, 0.0).astype(BF16)
        starts = _dot_ones_rhs(jnp.broadcast_to(running, (8, ROUTER_LANES)), before)[0:1, :]
        for g in range(N_GROUPS + 1):
            bounds[g] = starts[0, g].astype(jnp.int32)
        dest_cols = [jnp.sum(onehots[s] * (starts + cums[s] - 1.0), axis=-1, keepdims=True) for s in range(n_sub)]
        dest_row = jnp.concatenate(
            [jnp.broadcast_to(dc, (sub, ROUTER_LANES)).T[0:1, :] for dc in dest_cols], axis=1)
        row_id = lax.broadcasted_iota(jnp.int32, (sub, MOE_TILE), 0).astype(F32)
        for s in range(n_sub):
            rows = slice(s * sub, (s + 1) * sub)
            perm_scr[rows, :] = jnp.where(row_id + float(s * sub) == dest_row, 1.0, 0.0).astype(BF16)
        perm = perm_scr[...]
        by_perm = lambda y: jnp.dot(perm, y, preferred_element_type=F32)
        h_scr[...] = by_perm(hu_scr[...]).astype(BF16)
        g_hi, g_lo = _split2(gates_scr[...])
        gates_scr[...] = by_perm(g_hi) + by_perm(g_lo)
        acc_scr[...] = jnp.zeros(acc_scr.shape, F32)

    for cp in weight_copies(step):
        cp.wait()
    slot = step % MOE_WEIGHT_SLOTS
    grp = e // (EXPERTS_PER_GROUP // EXPERTS_PER_STEP)
    first, end = bounds[grp], bounds[grp + 1]
    first_al = (first // BF16_ROWS) * BF16_ROWS
    step = MOE_WINDOWS[0]
    assert all(w == (k + 1) * step for k, w in enumerate(MOE_WINDOWS))

    def experts_on(lo, size):
        start = pl.multiple_of(jnp.minimum(lo, MOE_TILE - size), BF16_ROWS)
        rows = pl.ds(start, size)
        lane = lax.broadcasted_iota(jnp.int32, (size, ROUTER_LANES), 1)
        row = lax.broadcasted_iota(jnp.int32, (size, 1), 0)
        hb = h_scr[rows, :]
        gates = jnp.where(row + start >= lo, gates_scr[rows, :], 0.0)
        out = acc_scr[rows, :]
        for j in range(EXPERTS_PER_STEP):
            lane_j = N_GROUPS + e * EXPERTS_PER_STEP + j
            ge = jnp.sum(jnp.where(lane == lane_j, gates, 0.0), axis=-1, keepdims=True)
            a = jnp.dot(hb, wg_buf[slot, j], preferred_element_type=F32)
            u = jnp.dot(hb, wu_buf[slot, j], preferred_element_type=F32)
            hid = (a * jax.nn.sigmoid(a)) * u * ge
            out = out + jnp.dot(hid.astype(BF16), wd_buf[slot, j], preferred_element_type=F32)
        acc_scr[rows, :] = out

    def window(w, carry):
        lo = first_al + w * MOE_WINDOWS[-1]
        n_steps = jnp.minimum((end - lo + step - 1) // step, len(MOE_WINDOWS))
        for k, size in enumerate(MOE_WINDOWS):
            pl.when(n_steps == k + 1)(functools.partial(experts_on, lo, size))
        return carry

    lax.fori_loop(0, (end - first_al + MOE_WINDOWS[-1] - 1) // MOE_WINDOWS[-1], window, 0)

    @pl.when(e == pl.num_programs(1) - 1)
    def _():
        acc_scr[...] = lax.dot_general(perm_scr[...], acc_scr[...].astype(BF16), (((0,), (0,)), ((), ())),
                                       preferred_element_type=F32)
        for s in range(n_sub):
            rows = slice(s * sub, (s + 1) * sub)
            z = alpha * x_ref[rows, :] + gate_ref[pl.ds(r, 1), :] * acc_scr[rows, :]
            _store_group(i, n_pt, o_refs, _layer_norm(z, g_ref[...], b_ref[...], LN_EPS), rows)


def _moe_norm(x, mod, layer, rw, rb, w_gate, w_up, w_down, ln_g, ln_b, alpha, geom, split_out):
    n_tok, d = x.shape
    ff = w_gate.shape[-1]
    assert w_gate.dtype == BF16 and w_up.dtype == BF16 and w_down.dtype == BF16
    const = lambda i, e: (0, 0)
    if split_out:
        out_shape = [jax.ShapeDtypeStruct((geom[0], d), F32), jax.ShapeDtypeStruct((n_tok - geom[0], d), F32)]
        out_specs = _group_specs(MOE_TILE, d, geom[0] // MOE_TILE)
    else:
        out_shape = [jax.ShapeDtypeStruct((n_tok, d), F32)]
        out_specs = [pl.BlockSpec((MOE_TILE, d), lambda i, e: (i, 0))]
    return pl.pallas_call(
        functools.partial(_moe_kernel, alpha=alpha, geom=geom, layer=layer),
        out_shape=out_shape,
        grid=(n_tok // MOE_TILE, N_EXPERTS // EXPERTS_PER_STEP),
        in_specs=[pl.BlockSpec((MOE_TILE, d), lambda i, e: (i, 0)),
                  _mod_spec(layer, 3, d), _mod_spec(layer, 4, d), _mod_spec(layer, 5, d),
                  pl.BlockSpec((d, ROUTER_LANES), const),
                  pl.BlockSpec((1, ROUTER_LANES), const),
                  pl.BlockSpec(memory_space=pl.ANY), pl.BlockSpec(memory_space=pl.ANY),
                  pl.BlockSpec(memory_space=pl.ANY),
                  pl.BlockSpec((1, d), const),
                  pl.BlockSpec((1, d), const)],
        out_specs=out_specs,
        scratch_shapes=[pltpu.VMEM((MOE_TILE, d), BF16), pltpu.VMEM((MOE_TILE, d), BF16),
                        pltpu.VMEM((MOE_TILE, ROUTER_LANES), F32),
                        pltpu.VMEM((MOE_TILE, d), F32),
                        pltpu.VMEM((MOE_TILE, MOE_TILE), BF16),
                        pltpu.SMEM((8,), jnp.int32),
                        pltpu.VMEM((MOE_WEIGHT_SLOTS, EXPERTS_PER_STEP, d, ff), BF16),
                        pltpu.VMEM((MOE_WEIGHT_SLOTS, EXPERTS_PER_STEP, d, ff), BF16),
                        pltpu.VMEM((MOE_WEIGHT_SLOTS, EXPERTS_PER_STEP, ff, d), BF16),
                        pltpu.SemaphoreType.DMA((3, MOE_WEIGHT_SLOTS))],
        compiler_params=_cparams(("arbitrary", "arbitrary")),
        name=f"moe_norm_l{layer}",
    )(x, mod, mod, mod, rw, rb, w_gate, w_up, w_down, ln_g.reshape(1, d), ln_b.reshape(1, d))


def _rwkv_kernel(r_ref, k_ref, v_ref, lora_ref, mur_ref, muk_ref, muv_ref, mul_ref,
                 w0_ref, wup_ref, a0_ref, aup_ref, gup_ref, kk_ref, ka_ref, rk_ref, gng_ref, gnb_ref, s0_ref,
                 *rest, seq_len, seqs, unroll, n_ride):
    ride_in, (out_ref, sfin_ref), ride_out = rest[:n_ride], rest[n_ride:n_ride + 2], rest[n_ride + 2:2 * n_ride + 2]
    (lw_s, kz_s, bz_s, rr_s, vv_s, kn_s, abar_s, rt_s, wt_s, vst_s, p_s, tm_s, mb_s, mk_s, nkb_s, txr_s, tx_s, gl_s,
     g_s, h_s, q_s, y0_s, y_s, st_s) = rest[2 * n_ride + 2:]
    for src, dst in zip(ride_in, ride_out):
        dst[...] = src[...].astype(dst.dtype)
    t_len = seqs * seq_len
    n_chunks = t_len // CHUNK
    seq_chunks = seq_len // CHUNK
    c2 = 2 * CHUNK

    row = lax.broadcasted_iota(jnp.int32, (t_len, 1), 0) % seq_len

    def tshift(x, mu):
        prev = jnp.where(row == 0, 0.0, pltpu.roll(x, 1, 0))
        nxt = jnp.where(row == seq_len - 1, 0.0, pltpu.roll(x, t_len - 1, 0))
        return x + mu * (0.5 * (prev + nxt) - x)

    r = tshift(r_ref[...], mur_ref[...])
    k = tshift(k_ref[...], muk_ref[...])
    v = tshift(v_ref[...], muv_ref[...])
    lo = tshift(lora_ref[...], mul_ref[...])
    wd = jnp.tanh(lo[:, 0:PAIR])
    ad = lo[:, PAIR:2 * PAIR]
    gate = _bdot(jax.nn.sigmoid(lo[:, 2 * PAIR:3 * PAIR]), gup_ref[...])

    ri = lax.broadcasted_iota(jnp.int32, (PAIR, PAIR), 0)
    ci = lax.broadcasted_iota(jnp.int32, (PAIR, PAIR), 1)
    same_head = (ri // HEAD) == (ci // HEAD)
    ones_bd = jnp.where(same_head, 1.0, 0.0).astype(BF16)
    eye = ri == ci

    kk = k * kk_ref[...]
    ssq = _dot_ones_rhs(kk * kk, ones_bd)
    kk = kk * lax.rsqrt(jnp.maximum(ssq, 1e-24))
    wup2 = wup_ref[...].reshape(PAIR, PAIR)
    aup2 = aup_ref[...].reshape(PAIR, PAIR)
    bonus = jnp.zeros((t_len, PAIR), F32)
    for z in range(2):
        zrows = (ri // HEAD) == z
        w_pre = w0_ref[z:z + 1, :] + _bdot(wd, jnp.where(zrows, wup2, 0.0))
        lw_s[z] = -EXP_M05 * jax.nn.sigmoid(w_pre)
        a = jax.nn.sigmoid(a0_ref[z:z + 1, :] + _bdot(ad, jnp.where(zrows, aup2, 0.0)))
        kz = k * (1.0 + (a - 1.0) * ka_ref[...])
        kz_s[z] = kz
        bz_s[z] = kk * a
        bonus = bonus + _dot_ones_rhs(r * kz * rk_ref[...], ones_bd) * v
    rr_s[...] = r
    vv_s[...] = v
    kn_s[...] = kk

    rt_i = ri % CHUNK
    cs_i = ci % CHUNK
    strict = (cs_i < rt_i, cs_i > rt_i)
    incl = (cs_i <= rt_i, cs_i >= rt_i)
    t64 = lax.broadcasted_iota(jnp.int32, (CHUNK, CHUNK), 0)
    s64 = lax.broadcasted_iota(jnp.int32, (CHUNK, CHUNK), 1)
    tri = (jnp.where(s64 <= t64, 1.0, 0.0).astype(BF16), jnp.where(s64 >= t64, 1.0, 0.0).astype(BF16))
    lane_head0 = lax.broadcasted_iota(jnp.int32, (CHUNK, PAIR), 1) < HEAD
    eye_f = jnp.where(eye, 1.0, 0.0)
    same16 = (ri // INV_BLOCK) == (ci // INV_BLOCK)

    def stack(x):
        return jnp.concatenate([jnp.where(lane_head0, x, 0.0), jnp.where(lane_head0, 0.0, x)], axis=0)

    def fold(x):
        return x[:CHUNK] + x[CHUNK:]

    def chunk_rows(c):
        return pl.ds(pl.multiple_of(c * CHUNK, CHUNK), CHUNK)

    def each_item(body):
        def both_directions(c, carry):
            body(0, c)
            body(1, c)
            return carry

        lax.fori_loop(0, n_chunks, both_directions, 0, unroll=unroll)

    def stage_operands(z, c):
        rows = chunk_rows(c)
        r_c, kn_c = rr_s[rows, :], kn_s[rows, :]
        lw_c, kz_c, bz_c = lw_s[z, rows, :], kz_s[z, rows, :], bz_s[z, rows, :]
        lg = _dot_ones_lhs(tri[z], lw_c)
        total = lg[CHUNK - 1:CHUNK, :] if z == 0 else lg[0:1, :]
        e_neg = jnp.exp(-lg)
        e_rem = jnp.exp(total - lg)
        abar = stack(kn_c * jnp.exp(lg - lw_c))
        rt = stack(r_c * jnp.exp(lg))
        big = _bdot_nt(jnp.concatenate([abar, rt], axis=0),
                       jnp.concatenate([stack(bz_c * e_neg), stack(kz_c * e_neg)], axis=0))
        mb = jnp.where(strict[z], big[:c2, :c2], 0.0)
        abar_s[z, c] = abar.astype(BF16)
        rt_s[z, c] = rt
        wt_s[z, c] = jnp.concatenate([stack(kz_c * e_rem), -stack(bz_c * e_rem)], axis=0).T.astype(BF16)
        gl_s[z, c] = jnp.exp(total)
        m0 = jnp.where(same16, mb, 0.0)
        tm_s[z, c] = eye_f - m0
        p_s[z, c] = m0.astype(BF16)
        mb_s[z, c] = mb.astype(BF16)
        mk_s[z, c] = jnp.where(strict[z], big[:c2, c2:], 0.0).astype(BF16)
        nkb_s[z, c] = jnp.concatenate([jnp.where(incl[z], big[c2:, c2:], 0.0),
                                       -jnp.where(incl[z], big[c2:, :c2], 0.0)], axis=1).astype(BF16)

    def stage_values(c, carry):
        vst_s[c] = stack(vv_s[chunk_rows(c), :]).astype(BF16)
        return carry

    lax.fori_loop(0, n_chunks, stage_values, 0, unroll=unroll)
    each_item(stage_operands)

    def stage_square(z, c):
        m0 = p_s[z, c]
        p_s[z, c] = jnp.dot(m0, m0, preferred_element_type=F32).astype(BF16)
        mkv = jnp.dot(mk_s[z, c], vst_s[c], preferred_element_type=F32)
        txr_s[z, c] = jnp.concatenate([abar_s[z, c], mkv.astype(BF16)], axis=1)

    each_item(stage_square)

    n_round = INV_BLOCK.bit_length() - 2
    for j in range(n_round):
        def stage_double(z, c, last=(j == n_round - 1)):
            tm, p = tm_s[z, c], p_s[z, c]
            tm_s[z, c] = tm + jnp.dot(tm.astype(BF16), p, preferred_element_type=F32)
            if not last:
                p_s[z, c] = jnp.dot(p, p, preferred_element_type=F32).astype(BF16)

        each_item(stage_double)

    size = INV_BLOCK
    while size < CHUNK:
        off_diag = ((ri // (2 * size)) == (ci // (2 * size))) & ((ri // size) != (ci // size))

        def stage_merge_rhs(z, c, off_diag=off_diag):
            cross = jnp.where(off_diag, mb_s[z, c], jnp.zeros((), BF16))
            p_s[z, c] = jnp.dot(cross, tm_s[z, c].astype(BF16), preferred_element_type=F32).astype(BF16)

        def stage_merge(z, c):
            tm = tm_s[z, c]
            tm_s[z, c] = tm - jnp.dot(tm.astype(BF16), p_s[z, c], preferred_element_type=F32)

        each_item(stage_merge_rhs)
        each_item(stage_merge)
        size *= 2

    def stage_apply(z, c):
        tx_s[z, c] = jnp.dot(tm_s[z, c].astype(BF16), txr_s[z, c], preferred_element_type=F32)

    each_item(stage_apply)

    def stage_maps(z, c):
        tx = tx_s[z, c].astype(BF16)
        v_st = vst_s[c]
        wt = wt_s[z, c]
        from_tx = jnp.dot(wt[:, c2:], tx, preferred_element_type=F32)
        h_s[z, c] = jnp.dot(wt[:, :c2], v_st, preferred_element_type=F32) + from_tx[:, PAIR:]
        g_s[z, c] = jnp.where(eye, gl_s[z, c], 0.0) + from_tx[:, :PAIR]
        nkb = nkb_s[z, c]
        from_tx = jnp.dot(nkb[:, c2:], tx, preferred_element_type=F32)
        q_s[z, c] = fold(rt_s[z, c] + from_tx[:, :PAIR])
        y0_s[z, c] = fold(jnp.dot(nkb[:, :c2], v_st, preferred_element_type=F32) + from_tx[:, PAIR:])

    each_item(stage_maps)

    zero64 = jnp.zeros((HEAD, HEAD), F32)
    chains = [(n, z) for n in range(seqs) for z in range(2)]
    for n, z in chains:
        a = jnp.concatenate([jnp.concatenate([s0_ref[n, z, 0], zero64], axis=1),
                             jnp.concatenate([zero64, s0_ref[n, z, 1]], axis=1)], axis=0)
        st_s[n, z] = a.T

    def propagate(i, carry):
        for n, z in chains:
            c = n * seq_chunks + (i if z == 0 else seq_chunks - 1 - i)
            s = st_s[n, z].astype(BF16)
            y_s[z, chunk_rows(c), :] = _bdot(q_s[z, c], s) + y0_s[z, c]
            st_s[n, z] = _bdot(g_s[z, c], s) + h_s[z, c]
        return carry

    lax.fori_loop(0, seq_chunks, propagate, 0)
    for n, z in chains:
        a = st_s[n, z].T
        sfin_ref[n, z, 0] = a[:HEAD, :HEAD]
        sfin_ref[n, z, 1] = a[HEAD:, HEAD:]

    y = y_s[0] + y_s[1]
    mu = _dot_ones_rhs(y, ones_bd) * (1.0 / HEAD)
    yc = y - mu
    var = _dot_ones_rhs(yc * yc, ones_bd) * (1.0 / HEAD)
    yn = yc * lax.rsqrt(var + GN_EPS) * gng_ref[...] + gnb_ref[...]
    out_ref[...] = (yn + bonus) * gate


def _rwkv_mix(pa, first_tok, n_seq, seq_len, s0, p, ride=()):
    a_width = p["w0"].shape[-1]
    n_pairs = a_width // PAIR
    heads = a_width // HEAD
    seqs = max(1, RWKV_STEP_ROWS // seq_len)
    rows = seqs * seq_len
    assert n_seq % seqs == 0 and first_tok % rows == 0
    t0 = first_tok // rows
    lora_w = 3 * PAIR
    lora_blk = 3 * a_width // lora_w
    n_chunks = rows // CHUNK
    col = lambda off: (lambda s, q: (t0 + s, off + q))
    vec = lambda off: (lambda s, q: (0, off + q))
    in_specs = [
        pl.BlockSpec((rows, PAIR), col(0)),
        pl.BlockSpec((rows, PAIR), col(n_pairs)),
        pl.BlockSpec((rows, PAIR), col(2 * n_pairs)),
        pl.BlockSpec((rows, lora_w), lambda s, q: (t0 + s, lora_blk)),
        pl.BlockSpec((1, PAIR), vec(0)),
        pl.BlockSpec((1, PAIR), vec(n_pairs)),
        pl.BlockSpec((1, PAIR), vec(2 * n_pairs)),
        pl.BlockSpec((1, lora_w), lambda s, q: (0, lora_blk)),
        pl.BlockSpec((2, PAIR), vec(0)),
        pl.BlockSpec((2, HEAD, PAIR), lambda s, q: (0, 0, q)),
        pl.BlockSpec((2, PAIR), vec(0)),
        pl.BlockSpec((2, HEAD, PAIR), lambda s, q: (0, 0, q)),
        pl.BlockSpec((PAIR, PAIR), vec(0)),
        pl.BlockSpec((1, PAIR), vec(0)),
        pl.BlockSpec((1, PAIR), vec(0)),
        pl.BlockSpec((1, PAIR), vec(0)),
        pl.BlockSpec((1, PAIR), vec(0)),
        pl.BlockSpec((1, PAIR), vec(0)),
        pl.BlockSpec((seqs, 2, 2, HEAD, HEAD), lambda s, q: (s, 0, q, 0, 0)),
    ]
    seq2 = lambda: pltpu.VMEM((2, rows, PAIR), F32)
    seq1 = lambda: pltpu.VMEM((rows, PAIR), F32)
    item = lambda rows, cols, dt: pltpu.VMEM((2, n_chunks, rows, cols), dt)
    n_steps = n_seq // seqs * n_pairs
    assert all(a.shape[0] % (n_steps * BF16_ROWS) == 0 for a in ride)
    ride_specs = [pl.BlockSpec((a.shape[0] // n_steps, a.shape[1]), lambda s, q: (s * n_pairs + q, 0)) for a in ride]
    out, s_fin, *cast = pl.pallas_call(
        functools.partial(_rwkv_kernel, seq_len=seq_len, seqs=seqs, unroll=min(n_chunks, 16), n_ride=len(ride)),
        out_shape=[jax.ShapeDtypeStruct((n_seq * seq_len, a_width), F32),
                   jax.ShapeDtypeStruct((n_seq, 2, heads, HEAD, HEAD), F32)]
        + [jax.ShapeDtypeStruct(a.shape, BF16) for a in ride],
        grid=(n_seq // seqs, n_pairs),
        in_specs=in_specs + ride_specs,
        out_specs=[pl.BlockSpec((rows, PAIR), lambda s, q: (s, q)),
                   pl.BlockSpec((seqs, 2, 2, HEAD, HEAD), lambda s, q: (s, 0, q, 0, 0))] + ride_specs,
        scratch_shapes=[seq2(), seq2(), seq2(), seq1(), seq1(), seq1(),
                        item(PAIR, PAIR, BF16),
                        item(PAIR, PAIR, F32),
                        item(PAIR, 2 * PAIR, BF16),
                        pltpu.VMEM((n_chunks, PAIR, PAIR), BF16),
                        item(PAIR, PAIR, BF16),
                        item(PAIR, PAIR, F32),
                        item(PAIR, PAIR, BF16),
                        item(PAIR, PAIR, BF16),
                        item(PAIR, 2 * PAIR, BF16),
                        item(PAIR, 2 * PAIR, BF16),
                        item(PAIR, 2 * PAIR, F32),
                        item(1, PAIR, F32),
                        item(PAIR, PAIR, F32), item(PAIR, PAIR, F32),
                        item(CHUNK, PAIR, F32), item(CHUNK, PAIR, F32),
                        seq2(), pltpu.VMEM((seqs, 2, PAIR, PAIR), F32)],
        compiler_params=_cparams(("arbitrary", "arbitrary")),
        name=f"rwkv_mix_t{seq_len}",
    )(pa, pa, pa, pa, p["ts_mu"], p["ts_mu"], p["ts_mu"], p["ts_mu"],
      p["w0"], p["w_up"], p["a0"], p["a_up"], p["g_up"], p["k_k"], p["k_a"], p["r_k"], p["gn_g"], p["gn_b"], s0, *ride)
    return out, s_fin, cast


def _conv_kernel(pb_ref, prev_ref, next_ref, pwb_ref, dw_ref, dwb_ref, g_ref, b_ref, o_ref, pad_s, shift_s, *, geom, seq):
    width = o_ref.shape[1]
    sub = 64
    n_prompt_tok, dec_seq = geom
    tok = pl.program_id(0) * CONV_TILE
    in_prompt = tok < n_prompt_tok
    seq_len = jnp.where(in_prompt, seq, dec_seq)
    off = jnp.where(in_prompt, tok, tok - n_prompt_tok) % seq_len

    def glu(ref):
        pbv = ref[...] + pwb_ref[...]
        return pbv[:, :width] * jax.nn.sigmoid(pbv[:, width:])

    pad_s[0:CONV_HALO, :] = jnp.where(off > 0, glu(prev_ref), 0.0)
    pad_s[CONV_HALO:CONV_HALO + CONV_TILE, :] = glu(pb_ref)
    pad_s[CONV_HALO + CONV_TILE:, :] = jnp.where(off + CONV_TILE < seq_len, glu(next_ref), 0.0)
    keep = pad_s.shape[0] - SUBLANES
    for phase in range(SUBLANES):
        shift_s[phase, 0:keep, :] = pad_s[phase:phase + keep, :]
    first = CONV_HALO - CONV_W // 2
    for c in range(CONV_TILE // sub):
        acc = jnp.zeros((sub, width), F32)
        for d in range(CONV_W):
            phase = (first + d) % SUBLANES
            start = c * sub + first + d - phase
            acc = acc + shift_s[phase, start:start + sub, :] * dw_ref[d:d + 1, :]
        y = _layer_norm(acc + dwb_ref[...], g_ref[...], b_ref[...], LN_EPS)
        o_ref[c * sub:(c + 1) * sub, :] = y * jax.nn.sigmoid(y)


def _conv_module(pb, geom, seq, pw_b, dw_w, dw_b, cln_g, cln_b):
    n_tok = pb.shape[0]
    width = dw_w.shape[1]
    per_tile = CONV_TILE // CONV_HALO
    last_halo = n_tok // CONV_HALO - 1
    const = lambda i: (0, 0)
    return pl.pallas_call(
        functools.partial(_conv_kernel, geom=geom, seq=seq),
        out_shape=jax.ShapeDtypeStruct((n_tok, width), F32),
        grid=(n_tok // CONV_TILE,),
        in_specs=[pl.BlockSpec((CONV_TILE, 2 * width), lambda i: (i, 0)),
                  pl.BlockSpec((CONV_HALO, 2 * width), lambda i: (jnp.maximum(i * per_tile - 1, 0), 0)),
                  pl.BlockSpec((CONV_HALO, 2 * width), lambda i: (jnp.minimum((i + 1) * per_tile, last_halo), 0)),
                  pl.BlockSpec((1, 2 * width), const),
                  pl.BlockSpec((CONV_W, width), const),
                  pl.BlockSpec((1, width), const),
                  pl.BlockSpec((1, width), const),
                  pl.BlockSpec((1, width), const)],
        out_specs=pl.BlockSpec((CONV_TILE, width), lambda i: (i, 0)),
        scratch_shapes=[pltpu.VMEM((CONV_TILE + 2 * CONV_HALO, width), F32),
                        pltpu.VMEM((SUBLANES, CONV_TILE + 2 * CONV_HALO, width), F32)],
        compiler_params=_cparams(("arbitrary",)),
        name="conv_module",
    )(pb, pb, pb, pw_b.reshape(1, -1), dw_w, dw_b.reshape(1, -1), cln_g.reshape(1, -1), cln_b.reshape(1, -1))


def _diff_lambda(lq1, lk1, lq2, lk2, lam_init):
    dot = lambda a, b: jnp.exp(jnp.sum(a[...] * b[...], axis=-1, keepdims=True))
    return dot(lq1, lk1) - dot(lq2, lk2) + lam_init


def _diff_attend(q, keys, vals, lam, lam_init, subln):
    nq = q.shape[0]
    lane_first = lax.broadcasted_iota(jnp.int32, q.shape, 1) < HEAD
    q = q * (HEAD ** -0.5)
    q_st = jnp.concatenate([jnp.where(lane_first, q, 0.0), jnp.where(lane_first, 0.0, q)], axis=0)
    s = _bdot_nt(q_st, keys)
    s = s - jnp.max(s, axis=-1, keepdims=True)
    pexp = jnp.exp(s)
    ov = _bdot(pexp, vals) / jnp.sum(pexp, axis=-1, keepdims=True)
    o = ov[:nq] - lam * ov[nq:]
    o = o * lax.rsqrt(jnp.mean(o * o, axis=-1, keepdims=True) + RMS_EPS)
    return o * subln * (1.0 - lam_init)


def _head_cols(h):
    return slice(h * PAIR, (h + 1) * PAIR)


def _attn_prompt_kernel(q_ref, k_ref, v_ref, lq1, lk1, lq2, lk2, sub_ref, o_ref, *, lam_init):
    lam = _diff_lambda(lq1, lk1, lq2, lk2, lam_init)
    for h in range(q_ref.shape[1] // PAIR):
        c = _head_cols(h)
        o_ref[:, c] = _diff_attend(q_ref[:, c], k_ref[:, c], v_ref[:, c], lam, lam_init, sub_ref[...])


def _rope(x, cos, sin_signed):
    lane = lax.broadcasted_iota(jnp.int32, x.shape, 1)
    quarter = HEAD // 4
    partner = jnp.where((lane // quarter) % 2 == 0,
                        pltpu.roll(x, LANES - quarter, 1), pltpu.roll(x, quarter, 1))
    return x * cos + partner * sin_signed


def _attn_sample_kernel(q_ref, k_ref, v_ref, ck_ref, cv_ref, cq_ref, sq_ref, cosk_ref, sink_ref,
                        lq1, lk1, lq2, lk2, sub_ref, o_ref, *, lam_init):
    lam = _diff_lambda(lq1, lk1, lq2, lk2, lam_init)
    for h in range(q_ref.shape[1] // PAIR):
        c = _head_cols(h)
        q = _rope(q_ref[:, c], cq_ref[...], sq_ref[...])
        kl = _rope(k_ref[:, c], cosk_ref[...], sink_ref[...])
        keys = jnp.concatenate([ck_ref[:, c], kl], axis=0)
        vals = jnp.concatenate([cv_ref[:, c], v_ref[:, c]], axis=0)
        o_ref[:, c] = _diff_attend(q, keys, vals, lam, lam_init, sub_ref[...])


def _rope_tables(n_lat, grid_w):
    quarter = HEAD // 4
    inv = ROPE_BASE ** (-jnp.arange(quarter, dtype=F32) / quarter)
    t = jnp.arange(n_lat)
    row_ang = (t // grid_w).astype(F32)[:, None] * inv
    col_ang = (t % grid_w).astype(F32)[:, None] * inv
    ang = jnp.concatenate([row_ang, row_ang, col_ang, col_ang] * 2, axis=-1)
    sign = jnp.tile(jnp.concatenate([-jnp.ones(quarter, F32), jnp.ones(quarter, F32)]), 2 * HEAD // (2 * quarter))
    return jnp.cos(ang), jnp.sin(ang) * sign


def _attention(q, k_p, v_p, k_s, v_s, cache_k, cache_v, p, lam_init, n_prompt, seq, n_lat_req, n_lat, grid_w):
    inner = q.shape[1]
    vec = lambda n: pl.BlockSpec((1, n), lambda *_: (0, 0))
    lam_args = [p[n].reshape(1, -1) for n in ("lq1", "lk1", "lq2", "lk2")] + [p["subln_g"].reshape(1, -1)]
    lam_specs = [vec(HEAD)] * 4 + [vec(PAIR)]
    blk = pl.BlockSpec((seq, inner), lambda b: (b, 0))
    o_prompt = pl.pallas_call(
        functools.partial(_attn_prompt_kernel, lam_init=lam_init),
        out_shape=jax.ShapeDtypeStruct((n_prompt * seq, inner), F32),
        grid=(n_prompt,),
        in_specs=[blk, blk, blk] + lam_specs,
        out_specs=blk,
        compiler_params=_cparams(("arbitrary",)),
        name="diff_attn_prompt",
    )(q, k_p, v_p, *lam_args)

    past = cache_k.shape[1]
    qt = SUB_TILE
    n_qt = n_lat // qt
    q0 = n_prompt * seq // qt
    cos, sin = _rope_tables(n_lat, grid_w)
    lat = pl.BlockSpec((n_lat, inner), lambda b, t: (b, 0))
    ctx = pl.BlockSpec((None, past, inner), lambda b, t: (b, 0, 0))
    o_sample = pl.pallas_call(
        functools.partial(_attn_sample_kernel, lam_init=lam_init),
        out_shape=jax.ShapeDtypeStruct((n_lat_req * n_lat, inner), F32),
        grid=(n_lat_req, n_qt),
        in_specs=[pl.BlockSpec((qt, inner), lambda b, t: (q0 + b * n_qt + t, 0)),
                  lat, lat, ctx, ctx,
                  pl.BlockSpec((qt, PAIR), lambda b, t: (t, 0)),
                  pl.BlockSpec((qt, PAIR), lambda b, t: (t, 0)),
                  pl.BlockSpec((n_lat, PAIR), lambda b, t: (0, 0)),
                  pl.BlockSpec((n_lat, PAIR), lambda b, t: (0, 0))] + lam_specs,
        out_specs=pl.BlockSpec((qt, inner), lambda b, t: (b * n_qt + t, 0)),
        compiler_params=_cparams(("arbitrary", "arbitrary")),
        name="diff_attn_sample",
    )(q, k_s, v_s, cache_k, cache_v, cos, sin, cos, sin, *lam_args)
    return o_prompt, o_sample


def kernel(x_prompt, x_sample, state_wkv, cache_k, cache_v, c, c_ctx, mod_w, mod_b, ln1_g, ln1_b, ln2_g, ln2_b, ev_in_w, ev_pw_b, ev_ts_mu, ev_w0, ev_w_up, ev_a0, ev_a_up, ev_g_up, ev_k_k, ev_k_a, ev_r_k, ev_gn_g, ev_gn_b, ev_dw_w, ev_dw_b, ev_cln_g, ev_cln_b, ev_out_w, od_in_w, od_lq1, od_lk1, od_lq2, od_lk2, od_subln_g, od_out_w, rg_w, rg_b, re_w, re_b, ffn_w_gate, ffn_w_up, ffn_w_down):
    nbp, seq, d = x_prompt.shape
    nbs, n_lat, _ = x_sample.shape
    depth = mod_w.shape[0]
    n_prompt_tok = nbp * seq
    geom = (n_prompt_tok, n_lat)
    alpha = (2 * depth) ** 0.25
    a_width = ev_w0.shape[-1]
    a_cols = ev_ts_mu.shape[-1]
    heads = a_width // HEAD
    grid_w = 64
    assert n_prompt_tok % MOE_TILE == 0 and n_lat % MOE_TILE == 0 and n_lat % SUB_TILE == 0
    assert seq % CONV_TILE == 0 and n_lat % CONV_TILE == 0 and CONV_HALO >= CONV_W // 2
    assert 1 + nbs <= 8

    xs = (x_prompt.reshape(n_prompt_tok, d), x_sample.reshape(nbs * n_lat, d))
    cond = jnp.concatenate([c_ctx[None, :], c, jnp.zeros((8 - 1 - nbs, d), F32)], axis=0)
    flat = lambda w: w.reshape(-1, w.shape[-1])
    mod, (ev_in_b,) = _ada_table(cond, mod_w, mod_b, (flat(ev_in_w),))
    ev_in_b = ev_in_b.reshape(ev_in_w.shape)
    late = (ffn_w_gate, ffn_w_up, ffn_w_down, ev_out_w, od_in_w, od_out_w)

    new_wkv, new_k, new_v = [], [], []
    for l in range(depth):
        last = l == depth - 1
        if l % 2 == 0:
            i = l // 2
            pa, pb = _mod_matmul(xs, mod, l, ev_in_b[i], ((a_cols, False), (ev_in_w.shape[-1] - a_cols, False)), geom)
            prm = dict(ts_mu=ev_ts_mu[i][None, :], w0=ev_w0[i], w_up=ev_w_up[i], a0=ev_a0[i], a_up=ev_a_up[i],
                       g_up=ev_g_up[i], k_k=ev_k_k[i][None, :], k_a=ev_k_a[i][None, :],
                       r_k=ev_r_k[i].reshape(1, a_width), gn_g=ev_gn_g[i][None, :], gn_b=ev_gn_b[i][None, :])
            zero_state = jnp.zeros((nbp, 2, heads, HEAD, HEAD), F32)
            oa_p, st, cast = _rwkv_mix(pa, 0, nbp, seq, zero_state, prm, () if l else tuple(flat(w) for w in late))
            if cast:
                w_gate, w_up, w_down, ev_out_b, od_in_b, od_out_b = (c.reshape(w.shape) for c, w in zip(cast, late))
            oa_s, _, _ = _rwkv_mix(pa, n_prompt_tok, nbs, n_lat, state_wkv[:, i], prm)
            conv = (ev_pw_b[i], ev_dw_w[i], ev_dw_b[i], ev_cln_g[i], ev_cln_b[i])
            hs = [(oa_p, oa_s), (_conv_module(pb, geom, seq, *conv),)]
            out_w = ev_out_b[i]
            new_wkv.append(st)
        else:
            j = l // 2
            lam_init = 0.8 - 0.6 * math.exp(-0.3 * l)
            inner = od_in_w.shape[-1] // 3
            q, k_p, k_s, v_p, v_s = _mod_matmul(xs, mod, l, od_in_b[j],
                                                 ((inner, False), (inner, True), (inner, True)), geom)
            prm = dict(lq1=od_lq1[j], lk1=od_lk1[j], lq2=od_lq2[j], lk2=od_lk2[j], subln_g=od_subln_g[j])
            ck = cache_k[:, j].reshape(nbs, -1, inner)
            cv = cache_v[:, j].reshape(nbs, -1, inner)
            hs = [_attention(q, k_p, v_p, k_s, v_s, ck, cv, prm, lam_init, nbp, seq, nbs, n_lat, grid_w)]
            out_w = od_out_b[j]
            new_k.append(k_p.reshape(nbp, seq, inner // PAIR, PAIR))
            new_v.append(v_p.reshape(nbp, seq, inner // PAIR, PAIR))
        x = _out_proj_norm(hs, xs, mod, l, out_w, ln1_g[l], ln1_b[l], alpha, geom)
        rw = jnp.concatenate([rg_w[l], re_w[l],
                              jnp.zeros((d, ROUTER_LANES - N_GROUPS - N_EXPERTS), F32)], axis=1)
        rb = jnp.concatenate([rg_b[l], re_b[l], jnp.zeros((ROUTER_LANES - N_GROUPS - N_EXPERTS,), F32)])[None, :]
        xs = tuple(_moe_norm(x, mod, l, rw, rb, w_gate, w_up, w_down, ln2_g[l], ln2_b[l], alpha, geom,
                             split_out=last))

    y_prompt = xs[0].reshape(nbp, seq, d)
    y_sample = xs[1].reshape(nbs, n_lat, d)
    return (y_prompt, y_sample, jnp.stack(new_wkv, axis=1), jnp.stack(new_k, axis=1), jnp.stack(new_v, axis=1))
```

```python
import functools
import math

import jax
import jax.numpy as jnp
from jax import lax
from jax.experimental import pallas as pl
from jax.experimental.pallas import tpu as pltpu

F32 = jnp.float32
BF16 = jnp.bfloat16

LANES = 128
SUBLANES = 8
HEAD = 64
PAIR = 2 * HEAD
CHUNK = 64
RWKV_STEP_ROWS = 512
INV_BLOCK = 16
PROJ_TILES = (1024, 512, 256)
PROJ_VMEM_BUDGET = 48 * 1024 * 1024
SUB_TILE = 256
MOE_TILE = 1024
ADA_PER_STEP = 2
CONV_W = 31
CONV_TILE = 256
CONV_HALO = 16
N_GROUPS = 4
EXPERTS_PER_GROUP = 8
N_EXPERTS = N_GROUPS * EXPERTS_PER_GROUP
ROUTER_LANES = 128
EXPERTS_PER_STEP = 4
MOE_WEIGHT_SLOTS = 3
MOE_WINDOWS = (128, 256, 384, 512)
BF16_ROWS = 16
LN_EPS = 1e-5
GN_EPS = 64e-5
RMS_EPS = 1e-5
ROPE_BASE = 10000.0
EXP_M05 = math.exp(-0.5)
VMEM_LIMIT = 60 * 1024 * 1024


def _bdot(a, b):
    return jnp.dot(a.astype(BF16), b.astype(BF16), preferred_element_type=F32)


def _bdot_nt(a, b):
    return lax.dot_general(a.astype(BF16), b.astype(BF16), (((1,), (1,)), ((), ())),
                           preferred_element_type=F32)


def _split2(a):
    hi = a.astype(BF16)
    lo = (a - hi.astype(F32)).astype(BF16)
    return hi, lo


def _dot_f32ish(a, b):
    ah, al = _split2(a)
    bh, bl = _split2(b)
    d = lambda x, y: jnp.dot(x, y, preferred_element_type=F32)
    return d(ah, bh) + d(ah, bl) + d(al, bh)


def _dot_ones_lhs(m_bf16, a):
    h, l = _split2(a)
    d = lambda y: jnp.dot(m_bf16, y, preferred_element_type=F32)
    return d(h) + d(l)


def _dot_ones_rhs(a, m_bf16):
    h, l = _split2(a)
    d = lambda x: jnp.dot(x, m_bf16, preferred_element_type=F32)
    return d(h) + d(l)


def _layer_norm(z, g, b, eps):
    mu = jnp.mean(z, axis=-1, keepdims=True)
    zc = z - mu
    var = jnp.mean(zc * zc, axis=-1, keepdims=True)
    return zc * lax.rsqrt(var + eps) * g + b


def _cond_row(tile, tile_rows, n_prompt_tok, dec_seq):
    tok = tile * tile_rows
    return jnp.where(tok < n_prompt_tok, 0, 1 + (tok - n_prompt_tok) // dec_seq)


def _cparams(sem):
    return pltpu.CompilerParams(dimension_semantics=sem, vmem_limit_bytes=VMEM_LIMIT)


def _ada_kernel(cond_ref, w_ref, b_ref, *rest, n_ride):
    ride_in, o_ref, ride_out = rest[:n_ride], rest[n_ride], rest[n_ride + 1:]

    @pl.when(pl.program_id(1) == 0)
    def _():
        for src, dst in zip(ride_in, ride_out):
            dst[...] = src[...].astype(dst.dtype)

    c = cond_ref[...]
    s = c * jax.nn.sigmoid(c)
    d = o_ref.shape[-1]
    y = _dot_f32ish(s, w_ref[0])
    for j in range(ADA_PER_STEP):
        o_ref[0, j] = y[:, j * d:(j + 1) * d] + b_ref[0, j]


def _ada_table(cond8, mod_w, mod_b, ride=()):
    depth, d, _ = mod_w.shape
    assert all(a.shape[0] % (depth * BF16_ROWS) == 0 for a in ride)
    ride_specs = [pl.BlockSpec((a.shape[0] // depth, a.shape[1]), lambda l, j: (l, 0)) for a in ride]
    mod, *cast = pl.pallas_call(
        functools.partial(_ada_kernel, n_ride=len(ride)),
        out_shape=[jax.ShapeDtypeStruct((depth, 6, 8, d), F32)] + [jax.ShapeDtypeStruct(a.shape, BF16) for a in ride],
        grid=(depth, 6 // ADA_PER_STEP),
        in_specs=[pl.BlockSpec((8, d), lambda l, j: (0, 0)),
                  pl.BlockSpec((1, d, ADA_PER_STEP * d), lambda l, j: (l, 0, j)),
                  pl.BlockSpec((1, ADA_PER_STEP, 1, d), lambda l, j: (l, j, 0, 0))] + ride_specs,
        out_specs=[pl.BlockSpec((1, ADA_PER_STEP, 8, d), lambda l, j: (l, j, 0, 0))] + ride_specs,
        compiler_params=_cparams(("arbitrary", "arbitrary")),
        name="ada_table",
    )(cond8, mod_w, mod_b.reshape(depth, 6, 1, d), *ride)
    return mod, cast


def _mod_spec(layer, which, d):
    return pl.BlockSpec((None, None, 8, d), lambda *_: (layer, which, 0, 0))


def _group_specs(tile, width, n_prompt_tiles):
    return [pl.BlockSpec((tile, width), lambda i, *_: (jnp.minimum(i, n_prompt_tiles - 1), 0)),
            pl.BlockSpec((tile, width), lambda i, *_: (jnp.maximum(i - n_prompt_tiles, 0), 0))]


def _load_group(i, n_prompt_tiles, refs, rows=slice(None)):
    if len(refs) == 1:
        return refs[0][rows, :]
    return jnp.where(i < n_prompt_tiles, refs[0][rows, :], refs[1][rows, :])


def _store_group(i, n_prompt_tiles, refs, value, rows=slice(None)):
    if len(refs) == 1:
        refs[0][rows, :] = value
        return

    @pl.when(i < n_prompt_tiles)
    def _():
        refs[0][rows, :] = value

    @pl.when(i >= n_prompt_tiles)
    def _():
        refs[1][rows, :] = value


def _proj_tile(cols_f32, w_elems, geom):
    for tile in PROJ_TILES:
        fits = 2 * 4 * tile * cols_f32 + 2 * w_elems <= PROJ_VMEM_BUDGET
        if fits and geom[0] % tile == 0 and geom[1] % tile == 0:
            return tile
    raise ValueError("no projection tile fits")


def _modmm_kernel(*refs, n_x, outs, geom, tile):
    x_refs = refs[:n_x]
    sh_ref, sc_ref, w_ref = refs[n_x:n_x + 3]
    o_refs = refs[n_x + 3:]
    i = pl.program_id(0)
    n_pt = geom[0] // tile
    r = _cond_row(i, tile, *geom)
    h = _load_group(i, n_pt, x_refs) * (1.0 + sc_ref[pl.ds(r, 1), :]) + sh_ref[pl.ds(r, 1), :]
    hb = h.astype(BF16)
    off = 0
    k = 0
    for n, split in outs:
        y = jnp.dot(hb, w_ref[:, off:off + n], preferred_element_type=F32)
        _store_group(i, n_pt, o_refs[k:k + 1 + split], y)
        off += n
        k += 1 + split


def _mod_matmul(xs, mod, layer, w, outs, geom):
    d, n_out = w.shape
    n_prompt_tok = geom[0]
    n_tok = sum(x.shape[0] for x in xs)
    tile = _proj_tile(len(xs) * d + sum(n * (1 + split) for n, split in outs), w.size, geom)
    n_pt = n_prompt_tok // tile
    whole = lambda n: [pl.BlockSpec((tile, n), lambda i: (i, 0))]
    x_specs = whole(d) if len(xs) == 1 else _group_specs(tile, d, n_pt)
    out_shape, out_specs = [], []
    for n, split in outs:
        if split:
            out_shape += [jax.ShapeDtypeStruct((n_prompt_tok, n), F32),
                          jax.ShapeDtypeStruct((n_tok - n_prompt_tok, n), F32)]
            out_specs += _group_specs(tile, n, n_pt)
        else:
            out_shape += [jax.ShapeDtypeStruct((n_tok, n), F32)]
            out_specs += whole(n)
    return pl.pallas_call(
        functools.partial(_modmm_kernel, n_x=len(xs), outs=outs, geom=geom, tile=tile),
        out_shape=out_shape,
        grid=(n_tok // tile,),
        in_specs=x_specs + [_mod_spec(layer, 0, d), _mod_spec(layer, 1, d),
                            pl.BlockSpec((d, n_out), lambda i: (0, 0), pipeline_mode=pl.Buffered(1))],
        out_specs=out_specs,
        compiler_params=_cparams(("arbitrary",)),
        name=f"mod_matmul_l{layer}",
    )(*xs, mod, mod, w)


def _outproj_kernel(*refs, arity, alpha, geom, tile):
    groups, k = [], 0
    for n in arity:
        groups.append(refs[k:k + n])
        k += n
    gate_ref, w_ref, g_ref, b_ref, o_ref = refs[k:]
    h_groups, x_group = groups[:-1], groups[-1]
    i = pl.program_id(0)
    n_pt = geom[0] // tile
    r = _cond_row(i, tile, *geom)
    y = None
    off = 0
    for h_refs in h_groups:
        n = h_refs[0].shape[1]
        h = _load_group(i, n_pt, h_refs)
        part = jnp.dot(h.astype(BF16), w_ref[off:off + n, :], preferred_element_type=F32)
        y = part if y is None else y + part
        off += n
    z = alpha * _load_group(i, n_pt, x_group) + gate_ref[pl.ds(r, 1), :] * y
    o_ref[...] = _layer_norm(z, g_ref[...], b_ref[...], LN_EPS)


def _out_proj_norm(hs, xs, mod, layer, w, ln_g, ln_b, alpha, geom):
    d = w.shape[1]
    n_tok = sum(x.shape[0] for x in xs)
    tile = _proj_tile(d + sum(len(g) * g[0].shape[1] for g in list(hs) + [xs]), w.size, geom)
    n_pt = geom[0] // tile
    specs = []
    for grp in list(hs) + [xs]:
        n = grp[0].shape[1]
        specs += [pl.BlockSpec((tile, n), lambda i: (i, 0))] if len(grp) == 1 else _group_specs(tile, n, n_pt)
    return pl.pallas_call(
        functools.partial(_outproj_kernel, arity=tuple(len(g) for g in list(hs) + [xs]), alpha=alpha, geom=geom,
                          tile=tile),
        out_shape=jax.ShapeDtypeStruct((n_tok, d), F32),
        grid=(n_tok // tile,),
        in_specs=specs + [
            _mod_spec(layer, 2, d),
            pl.BlockSpec(w.shape, lambda i: (0, 0), pipeline_mode=pl.Buffered(1)),
            pl.BlockSpec((1, d), lambda i: (0, 0)),
            pl.BlockSpec((1, d), lambda i: (0, 0))],
        out_specs=pl.BlockSpec((tile, d), lambda i: (i, 0)),
        compiler_params=_cparams(("arbitrary",)),
        name=f"out_proj_norm_l{layer}",
    )(*[a for g in hs for a in g], *xs, mod, w, ln_g.reshape(1, d), ln_b.reshape(1, d))


def _route(logits):
    neg = jnp.float32(-3e38)
    big = jnp.int32(1 << 20)
    lane = lax.broadcasted_iota(jnp.int32, logits.shape, 1)
    is_g = lane < N_GROUPS
    gl = jnp.where(is_g, logits, neg)
    gmax = jnp.max(gl, axis=-1, keepdims=True)
    gidx = jnp.min(jnp.where(gl == gmax, lane, big), axis=-1, keepdims=True)
    gsum = jnp.sum(jnp.where(is_g, jnp.exp(gl - gmax), 0.0), axis=-1, keepdims=True)
    g_p = 1.0 / gsum
    lo = N_GROUPS + gidx * EXPERTS_PER_GROUP
    in_grp = (lane >= lo) & (lane < lo + EXPERTS_PER_GROUP)
    el = jnp.where(in_grp, logits, neg)
    v1 = jnp.max(el, axis=-1, keepdims=True)
    i1 = jnp.min(jnp.where(el == v1, lane, big), axis=-1, keepdims=True)
    el2 = jnp.where(lane == i1, neg, el)
    v2 = jnp.max(el2, axis=-1, keepdims=True)
    i2 = jnp.min(jnp.where(el2 == v2, lane, big), axis=-1, keepdims=True)
    e21 = jnp.exp(v2 - v1)
    den = 1.0 + e21
    w1 = g_p / den
    w2 = g_p * e21 / den
    return jnp.where(lane == i1, w1, 0.0) + jnp.where(lane == i2, w2, 0.0), gidx


def _moe_kernel(x_ref, sh_ref, sc_ref, gate_ref, rw_ref, rb_ref, wg_ref, wu_ref, wd_ref, g_ref, b_ref,
                *rest, alpha, geom, layer):
    o_refs = rest[:-10]
    hu_scr, h_scr, gates_scr, acc_scr, perm_scr, bounds, wg_buf, wu_buf, wd_buf, w_sem = rest[-10:]
    i = pl.program_id(0)
    grp = pl.program_id(1)
    n_grp = pl.num_programs(1)
    per_group = EXPERTS_PER_GROUP // EXPERTS_PER_STEP
    n_blocks = N_EXPERTS // EXPERTS_PER_STEP

    def weight_copies(block_step):
        blk = pl.ds((block_step % n_blocks) * EXPERTS_PER_STEP, EXPERTS_PER_STEP)
        slot = block_step % MOE_WEIGHT_SLOTS
        return [pltpu.make_async_copy(src.at[layer, blk], dst.at[slot], w_sem.at[k, slot])
                for k, (src, dst) in enumerate(((wg_ref, wg_buf), (wu_ref, wu_buf), (wd_ref, wd_buf)))]

    first_block = (i * n_grp + grp) * per_group
    n_block_steps = pl.num_programs(0) * n_grp * per_group

    @pl.when(first_block == 0)
    def _():
        for ahead in range(MOE_WEIGHT_SLOTS - 1):
            for cp in weight_copies(ahead):
                cp.start()

    r = _cond_row(i, MOE_TILE, *geom)
    n_pt = geom[0] // MOE_TILE
    sub = SUB_TILE
    n_sub = MOE_TILE // sub

    @pl.when(grp == 0)
    def _():
        lane = lax.broadcasted_iota(jnp.int32, (sub, ROUTER_LANES), 1)
        ri = lax.broadcasted_iota(jnp.int32, (sub, sub), 0)
        ci = lax.broadcasted_iota(jnp.int32, (sub, sub), 1)
        tril = jnp.where(ci <= ri, 1.0, 0.0).astype(BF16)
        running = jnp.zeros((1, ROUTER_LANES), F32)
        onehots, cums = [], []
        for s in range(n_sub):
            rows = slice(s * sub, (s + 1) * sub)
            h = x_ref[rows, :] * (1.0 + sc_ref[pl.ds(r, 1), :]) + sh_ref[pl.ds(r, 1), :]
            hu_scr[rows, :] = h.astype(BF16)
            logits = _dot_f32ish(h, rw_ref[...]) + rb_ref[...]
            gates, gidx = _route(logits)
            gates_scr[rows, :] = gates
            onehot = jnp.where(lane == gidx, 1.0, 0.0)
            cum = jnp.dot(tril, onehot.astype(BF16), preferred_element_type=F32) + running
            running = cum[sub - 1:sub, :]
            onehots.append(onehot)
            cums.append(cum)
        gi = lax.broadcasted_iota(jnp.int32, (ROUTER_LANES, ROUTER_LANES), 0)
        gj = lax.broadcasted_iota(jnp.int32, (ROUTER_LANES, ROUTER_LANES), 1)
        before = jnp.where((gi < gj) & (gi < N_GROUPS), 1.0, 0.0).astype(BF16)
        starts = _dot_ones_rhs(jnp.broadcast_to(running, (8, ROUTER_LANES)), before)[0:1, :]
        for g in range(N_GROUPS + 1):
            bounds[g] = starts[0, g].astype(jnp.int32)
        dest_cols = [jnp.sum(onehots[s] * (starts + cums[s] - 1.0), axis=-1, keepdims=True) for s in range(n_sub)]
        dest_row = jnp.concatenate(
            [jnp.broadcast_to(dc, (sub, ROUTER_LANES)).T[0:1, :] for dc in dest_cols], axis=1)
        row_id = lax.broadcasted_iota(jnp.int32, (sub, MOE_TILE), 0).astype(F32)
        for s in range(n_sub):
            rows = slice(s * sub, (s + 1) * sub)
            perm_scr[rows, :] = jnp.where(row_id + float(s * sub) == dest_row, 1.0, 0.0).astype(BF16)
        perm = perm_scr[...]
        by_perm = lambda y: jnp.dot(perm, y, preferred_element_type=F32)
        h_scr[...] = by_perm(hu_scr[...]).astype(BF16)
        g_hi, g_lo = _split2(gates_scr[...])
        gates_scr[...] = by_perm(g_hi) + by_perm(g_lo)
        acc_scr[...] = jnp.zeros(acc_scr.shape, F32)

    first, end = bounds[grp], bounds[grp + 1]
    first_al = (first // BF16_ROWS) * BF16_ROWS
    unit = MOE_WINDOWS[0]
    assert all(w == (k + 1) * unit for k, w in enumerate(MOE_WINDOWS))

    def experts_on(e, slot, lo, size):
        start = pl.multiple_of(jnp.minimum(lo, MOE_TILE - size), BF16_ROWS)
        rows = pl.ds(start, size)
        lane = lax.broadcasted_iota(jnp.int32, (size, ROUTER_LANES), 1)
        row = lax.broadcasted_iota(jnp.int32, (size, 1), 0)
        hb = h_scr[rows, :]
        gates = jnp.where(row + start >= lo, gates_scr[rows, :], 0.0)
        out = acc_scr[rows, :]
        for j in range(EXPERTS_PER_STEP):
            lane_j = N_GROUPS + e * EXPERTS_PER_STEP + j
            ge = jnp.sum(jnp.where(lane == lane_j, gates, 0.0), axis=-1, keepdims=True)
            a = jnp.dot(hb, wg_buf[slot, j], preferred_element_type=F32)
            u = jnp.dot(hb, wu_buf[slot, j], preferred_element_type=F32)
            hid = (a * jax.nn.sigmoid(a)) * u * ge
            out = out + jnp.dot(hid.astype(BF16), wd_buf[slot, j], preferred_element_type=F32)
        acc_scr[rows, :] = out

    for part in range(per_group):
        block_step = first_block + part

        @pl.when(block_step + MOE_WEIGHT_SLOTS - 1 < n_block_steps)
        def _():
            for cp in weight_copies(block_step + MOE_WEIGHT_SLOTS - 1):
                cp.start()

        for cp in weight_copies(block_step):
            cp.wait()
        slot = block_step % MOE_WEIGHT_SLOTS
        e = grp * per_group + part

        def window(w, carry, e=e, slot=slot):
            lo = first_al + w * MOE_WINDOWS[-1]
            n_units = jnp.minimum((end - lo + unit - 1) // unit, len(MOE_WINDOWS))
            for k, size in enumerate(MOE_WINDOWS):
                pl.when(n_units == k + 1)(functools.partial(experts_on, e, slot, lo, size))
            return carry

        lax.fori_loop(0, (end - first_al + MOE_WINDOWS[-1] - 1) // MOE_WINDOWS[-1], window, 0)

    @pl.when(grp == n_grp - 1)
    def _():
        acc_scr[...] = lax.dot_general(perm_scr[...], acc_scr[...].astype(BF16), (((0,), (0,)), ((), ())),
                                       preferred_element_type=F32)
        for s in range(n_sub):
            rows = slice(s * sub, (s + 1) * sub)
            z = alpha * x_ref[rows, :] + gate_ref[pl.ds(r, 1), :] * acc_scr[rows, :]
            _store_group(i, n_pt, o_refs, _layer_norm(z, g_ref[...], b_ref[...], LN_EPS), rows)


def _moe_norm(x, mod, layer, rw, rb, w_gate, w_up, w_down, ln_g, ln_b, alpha, geom, split_out):
    n_tok, d = x.shape
    ff = w_gate.shape[-1]
    assert w_gate.dtype == BF16 and w_up.dtype == BF16 and w_down.dtype == BF16
    const = lambda i, e: (0, 0)
    if split_out:
        out_shape = [jax.ShapeDtypeStruct((geom[0], d), F32), jax.ShapeDtypeStruct((n_tok - geom[0], d), F32)]
        out_specs = _group_specs(MOE_TILE, d, geom[0] // MOE_TILE)
    else:
        out_shape = [jax.ShapeDtypeStruct((n_tok, d), F32)]
        out_specs = [pl.BlockSpec((MOE_TILE, d), lambda i, e: (i, 0))]
    return pl.pallas_call(
        functools.partial(_moe_kernel, alpha=alpha, geom=geom, layer=layer),
        out_shape=out_shape,
        grid=(n_tok // MOE_TILE, N_GROUPS),
        in_specs=[pl.BlockSpec((MOE_TILE, d), lambda i, e: (i, 0)),
                  _mod_spec(layer, 3, d), _mod_spec(layer, 4, d), _mod_spec(layer, 5, d),
                  pl.BlockSpec((d, ROUTER_LANES), const),
                  pl.BlockSpec((1, ROUTER_LANES), const),
                  pl.BlockSpec(memory_space=pl.ANY), pl.BlockSpec(memory_space=pl.ANY),
                  pl.BlockSpec(memory_space=pl.ANY),
                  pl.BlockSpec((1, d), const),
                  pl.BlockSpec((1, d), const)],
        out_specs=out_specs,
        scratch_shapes=[pltpu.VMEM((MOE_TILE, d), BF16), pltpu.VMEM((MOE_TILE, d), BF16),
                        pltpu.VMEM((MOE_TILE, ROUTER_LANES), F32),
                        pltpu.VMEM((MOE_TILE, d), F32),
                        pltpu.VMEM((MOE_TILE, MOE_TILE), BF16),
                        pltpu.SMEM((8,), jnp.int32),
                        pltpu.VMEM((MOE_WEIGHT_SLOTS, EXPERTS_PER_STEP, d, ff), BF16),
                        pltpu.VMEM((MOE_WEIGHT_SLOTS, EXPERTS_PER_STEP, d, ff), BF16),
                        pltpu.VMEM((MOE_WEIGHT_SLOTS, EXPERTS_PER_STEP, ff, d), BF16),
                        pltpu.SemaphoreType.DMA((3, MOE_WEIGHT_SLOTS))],
        compiler_params=_cparams(("arbitrary", "arbitrary")),
        name=f"moe_norm_l{layer}",
    )(x, mod, mod, mod, rw, rb, w_gate, w_up, w_down, ln_g.reshape(1, d), ln_b.reshape(1, d))


def _rwkv_kernel(r_ref, k_ref, v_ref, lora_ref, mur_ref, muk_ref, muv_ref, mul_ref,
                 w0_ref, wup_ref, a0_ref, aup_ref, gup_ref, kk_ref, ka_ref, rk_ref, gng_ref, gnb_ref, s0_ref,
                 *rest, seq_len, seqs, unroll, n_ride):
    ride_in, (out_ref, sfin_ref), ride_out = rest[:n_ride], rest[n_ride:n_ride + 2], rest[n_ride + 2:2 * n_ride + 2]
    (lw_s, kz_s, bz_s, rr_s, vv_s, kn_s, abar_s, rt_s, wt_s, vst_s, p_s, tm_s, mb_s, mk_s, nkb_s, txr_s, tx_s, gl_s,
     g_s, h_s, q_s, y0_s, y_s, st_s) = rest[2 * n_ride + 2:]
    for src, dst in zip(ride_in, ride_out):
        dst[...] = src[...].astype(dst.dtype)
    t_len = seqs * seq_len
    n_chunks = t_len // CHUNK
    seq_chunks = seq_len // CHUNK
    c2 = 2 * CHUNK

    row = lax.broadcasted_iota(jnp.int32, (t_len, 1), 0) % seq_len

    def tshift(x, mu):
        prev = jnp.where(row == 0, 0.0, pltpu.roll(x, 1, 0))
        nxt = jnp.where(row == seq_len - 1, 0.0, pltpu.roll(x, t_len - 1, 0))
        return x + mu * (0.5 * (prev + nxt) - x)

    r = tshift(r_ref[...], mur_ref[...])
    k = tshift(k_ref[...], muk_ref[...])
    v = tshift(v_ref[...], muv_ref[...])
    lo = tshift(lora_ref[...], mul_ref[...])
    wd = jnp.tanh(lo[:, 0:PAIR])
    ad = lo[:, PAIR:2 * PAIR]
    gate = _bdot(jax.nn.sigmoid(lo[:, 2 * PAIR:3 * PAIR]), gup_ref[...])

    ri = lax.broadcasted_iota(jnp.int32, (PAIR, PAIR), 0)
    ci = lax.broadcasted_iota(jnp.int32, (PAIR, PAIR), 1)
    same_head = (ri // HEAD) == (ci // HEAD)
    ones_bd = jnp.where(same_head, 1.0, 0.0).astype(BF16)
    eye = ri == ci

    kk = k * kk_ref[...]
    ssq = _dot_ones_rhs(kk * kk, ones_bd)
    kk = kk * lax.rsqrt(jnp.maximum(ssq, 1e-24))
    wup2 = wup_ref[...].reshape(PAIR, PAIR)
    aup2 = aup_ref[...].reshape(PAIR, PAIR)
    bonus = jnp.zeros((t_len, PAIR), F32)
    for z in range(2):
        zrows = (ri // HEAD) == z
        w_pre = w0_ref[z:z + 1, :] + _bdot(wd, jnp.where(zrows, wup2, 0.0))
        lw_s[z] = -EXP_M05 * jax.nn.sigmoid(w_pre)
        a = jax.nn.sigmoid(a0_ref[z:z + 1, :] + _bdot(ad, jnp.where(zrows, aup2, 0.0)))
        kz = k * (1.0 + (a - 1.0) * ka_ref[...])
        kz_s[z] = kz
        bz_s[z] = kk * a
        bonus = bonus + _dot_ones_rhs(r * kz * rk_ref[...], ones_bd) * v
    rr_s[...] = r
    vv_s[...] = v
    kn_s[...] = kk

    rt_i = ri % CHUNK
    cs_i = ci % CHUNK
    strict = (cs_i < rt_i, cs_i > rt_i)
    incl = (cs_i <= rt_i, cs_i >= rt_i)
    t64 = lax.broadcasted_iota(jnp.int32, (CHUNK, CHUNK), 0)
    s64 = lax.broadcasted_iota(jnp.int32, (CHUNK, CHUNK), 1)
    tri = (jnp.where(s64 <= t64, 1.0, 0.0).astype(BF16), jnp.where(s64 >= t64, 1.0, 0.0).astype(BF16))
    lane_head0 = lax.broadcasted_iota(jnp.int32, (CHUNK, PAIR), 1) < HEAD
    eye_f = jnp.where(eye, 1.0, 0.0)
    same16 = (ri // INV_BLOCK) == (ci // INV_BLOCK)

    def stack(x):
        return jnp.concatenate([jnp.where(lane_head0, x, 0.0), jnp.where(lane_head0, 0.0, x)], axis=0)

    def fold(x):
        return x[:CHUNK] + x[CHUNK:]

    def chunk_rows(c):
        return pl.ds(pl.multiple_of(c * CHUNK, CHUNK), CHUNK)

    def each_item(body):
        def both_directions(c, carry):
            body(0, c)
            body(1, c)
            return carry

        lax.fori_loop(0, n_chunks, both_directions, 0, unroll=unroll)

    def stage_operands(z, c):
        rows = chunk_rows(c)
        r_c, kn_c = rr_s[rows, :], kn_s[rows, :]
        lw_c, kz_c, bz_c = lw_s[z, rows, :], kz_s[z, rows, :], bz_s[z, rows, :]
        lg = _dot_ones_lhs(tri[z], lw_c)
        total = lg[CHUNK - 1:CHUNK, :] if z == 0 else lg[0:1, :]
        e_neg = jnp.exp(-lg)
        e_rem = jnp.exp(total - lg)
        abar = stack(kn_c * jnp.exp(lg - lw_c))
        rt = stack(r_c * jnp.exp(lg))
        big = _bdot_nt(jnp.concatenate([abar, rt], axis=0),
                       jnp.concatenate([stack(bz_c * e_neg), stack(kz_c * e_neg)], axis=0))
        mb = jnp.where(strict[z], big[:c2, :c2], 0.0)
        abar_s[z, c] = abar.astype(BF16)
        rt_s[z, c] = rt
        wt_s[z, c] = jnp.concatenate([stack(kz_c * e_rem), -stack(bz_c * e_rem)], axis=0).T.astype(BF16)
        gl_s[z, c] = jnp.exp(total)
        m0 = jnp.where(same16, mb, 0.0)
        tm_s[z, c] = eye_f - m0
        p_s[z, c] = m0.astype(BF16)
        mb_s[z, c] = mb.astype(BF16)
        mk_s[z, c] = jnp.where(strict[z], big[:c2, c2:], 0.0).astype(BF16)
        nkb_s[z, c] = jnp.concatenate([jnp.where(incl[z], big[c2:, c2:], 0.0),
                                       -jnp.where(incl[z], big[c2:, :c2], 0.0)], axis=1).astype(BF16)

    def stage_values(c, carry):
        vst_s[c] = stack(vv_s[chunk_rows(c), :]).astype(BF16)
        return carry

    lax.fori_loop(0, n_chunks, stage_values, 0, unroll=unroll)
    each_item(stage_operands)

    def stage_square(z, c):
        m0 = p_s[z, c]
        p_s[z, c] = jnp.dot(m0, m0, preferred_element_type=F32).astype(BF16)
        mkv = jnp.dot(mk_s[z, c], vst_s[c], preferred_element_type=F32)
        txr_s[z, c] = jnp.concatenate([abar_s[z, c], mkv.astype(BF16)], axis=1)

    each_item(stage_square)

    n_round = INV_BLOCK.bit_length() - 2
    for j in range(n_round):
        def stage_double(z, c, last=(j == n_round - 1)):
            tm, p = tm_s[z, c], p_s[z, c]
            tm_s[z, c] = tm + jnp.dot(tm.astype(BF16), p, preferred_element_type=F32)
            if not last:
                p_s[z, c] = jnp.dot(p, p, preferred_element_type=F32).astype(BF16)

        each_item(stage_double)

    size = INV_BLOCK
    while size < CHUNK:
        off_diag = ((ri // (2 * size)) == (ci // (2 * size))) & ((ri // size) != (ci // size))

        def stage_merge_rhs(z, c, off_diag=off_diag):
            cross = jnp.where(off_diag, mb_s[z, c], jnp.zeros((), BF16))
            p_s[z, c] = jnp.dot(cross, tm_s[z, c].astype(BF16), preferred_element_type=F32).astype(BF16)

        def stage_merge(z, c):
            tm = tm_s[z, c]
            tm_s[z, c] = tm - jnp.dot(tm.astype(BF16), p_s[z, c], preferred_element_type=F32)

        each_item(stage_merge_rhs)
        each_item(stage_merge)
        size *= 2

    def stage_apply(z, c):
        tx_s[z, c] = jnp.dot(tm_s[z, c].astype(BF16), txr_s[z, c], preferred_element_type=F32)

    each_item(stage_apply)

    def stage_maps(z, c):
        tx = tx_s[z, c].astype(BF16)
        v_st = vst_s[c]
        wt = wt_s[z, c]
        from_tx = jnp.dot(wt[:, c2:], tx, preferred_element_type=F32)
        h_s[z, c] = jnp.dot(wt[:, :c2], v_st, preferred_element_type=F32) + from_tx[:, PAIR:]
        g_s[z, c] = jnp.where(eye, gl_s[z, c], 0.0) + from_tx[:, :PAIR]
        nkb = nkb_s[z, c]
        from_tx = jnp.dot(nkb[:, c2:], tx, preferred_element_type=F32)
        q_s[z, c] = fold(rt_s[z, c] + from_tx[:, :PAIR])
        y0_s[z, c] = fold(jnp.dot(nkb[:, :c2], v_st, preferred_element_type=F32) + from_tx[:, PAIR:])

    each_item(stage_maps)

    zero64 = jnp.zeros((HEAD, HEAD), F32)
    chains = [(n, z) for n in range(seqs) for z in range(2)]
    for n, z in chains:
        a = jnp.concatenate([jnp.concatenate([s0_ref[n, z, 0], zero64], axis=1),
                             jnp.concatenate([zero64, s0_ref[n, z, 1]], axis=1)], axis=0)
        st_s[n, z] = a.T

    def propagate(i, carry):
        for n, z in chains:
            c = n * seq_chunks + (i if z == 0 else seq_chunks - 1 - i)
            s = st_s[n, z].astype(BF16)
            y_s[z, chunk_rows(c), :] = _bdot(q_s[z, c], s) + y0_s[z, c]
            st_s[n, z] = _bdot(g_s[z, c], s) + h_s[z, c]
        return carry

    lax.fori_loop(0, seq_chunks, propagate, 0)
    for n, z in chains:
        a = st_s[n, z].T
        sfin_ref[n, z, 0] = a[:HEAD, :HEAD]
        sfin_ref[n, z, 1] = a[HEAD:, HEAD:]

    y = y_s[0] + y_s[1]
    mu = _dot_ones_rhs(y, ones_bd) * (1.0 / HEAD)
    yc = y - mu
    var = _dot_ones_rhs(yc * yc, ones_bd) * (1.0 / HEAD)
    yn = yc * lax.rsqrt(var + GN_EPS) * gng_ref[...] + gnb_ref[...]
    out_ref[...] = (yn + bonus) * gate


def _rwkv_mix(pa, first_tok, n_seq, seq_len, s0, p, ride=()):
    a_width = p["w0"].shape[-1]
    n_pairs = a_width // PAIR
    heads = a_width // HEAD
    seqs = max(1, RWKV_STEP_ROWS // seq_len)
    rows = seqs * seq_len
    assert n_seq % seqs == 0 and first_tok % rows == 0
    t0 = first_tok // rows
    lora_w = 3 * PAIR
    lora_blk = 3 * a_width // lora_w
    n_chunks = rows // CHUNK
    col = lambda off: (lambda s, q: (t0 + s, off + q))
    vec = lambda off: (lambda s, q: (0, off + q))
    in_specs = [
        pl.BlockSpec((rows, PAIR), col(0)),
        pl.BlockSpec((rows, PAIR), col(n_pairs)),
        pl.BlockSpec((rows, PAIR), col(2 * n_pairs)),
        pl.BlockSpec((rows, lora_w), lambda s, q: (t0 + s, lora_blk)),
        pl.BlockSpec((1, PAIR), vec(0)),
        pl.BlockSpec((1, PAIR), vec(n_pairs)),
        pl.BlockSpec((1, PAIR), vec(2 * n_pairs)),
        pl.BlockSpec((1, lora_w), lambda s, q: (0, lora_blk)),
        pl.BlockSpec((2, PAIR), vec(0)),
        pl.BlockSpec((2, HEAD, PAIR), lambda s, q: (0, 0, q)),
        pl.BlockSpec((2, PAIR), vec(0)),
        pl.BlockSpec((2, HEAD, PAIR), lambda s, q: (0, 0, q)),
        pl.BlockSpec((PAIR, PAIR), vec(0)),
        pl.BlockSpec((1, PAIR), vec(0)),
        pl.BlockSpec((1, PAIR), vec(0)),
        pl.BlockSpec((1, PAIR), vec(0)),
        pl.BlockSpec((1, PAIR), vec(0)),
        pl.BlockSpec((1, PAIR), vec(0)),
        pl.BlockSpec((seqs, 2, 2, HEAD, HEAD), lambda s, q: (s, 0, q, 0, 0)),
    ]
    seq2 = lambda: pltpu.VMEM((2, rows, PAIR), F32)
    seq1 = lambda: pltpu.VMEM((rows, PAIR), F32)
    item = lambda rows, cols, dt: pltpu.VMEM((2, n_chunks, rows, cols), dt)
    n_steps = n_seq // seqs * n_pairs
    assert all(a.shape[0] % (n_steps * BF16_ROWS) == 0 for a in ride)
    ride_specs = [pl.BlockSpec((a.shape[0] // n_steps, a.shape[1]), lambda s, q: (s * n_pairs + q, 0)) for a in ride]
    out, s_fin, *cast = pl.pallas_call(
        functools.partial(_rwkv_kernel, seq_len=seq_len, seqs=seqs, unroll=min(n_chunks, 16), n_ride=len(ride)),
        out_shape=[jax.ShapeDtypeStruct((n_seq * seq_len, a_width), F32),
                   jax.ShapeDtypeStruct((n_seq, 2, heads, HEAD, HEAD), F32)]
        + [jax.ShapeDtypeStruct(a.shape, BF16) for a in ride],
        grid=(n_seq // seqs, n_pairs),
        in_specs=in_specs + ride_specs,
        out_specs=[pl.BlockSpec((rows, PAIR), lambda s, q: (s, q)),
                   pl.BlockSpec((seqs, 2, 2, HEAD, HEAD), lambda s, q: (s, 0, q, 0, 0))] + ride_specs,
        scratch_shapes=[seq2(), seq2(), seq2(), seq1(), seq1(), seq1(),
                        item(PAIR, PAIR, BF16),
                        item(PAIR, PAIR, F32),
                        item(PAIR, 2 * PAIR, BF16),
                        pltpu.VMEM((n_chunks, PAIR, PAIR), BF16),
                        item(PAIR, PAIR, BF16),
                        item(PAIR, PAIR, F32),
                        item(PAIR, PAIR, BF16),
                        item(PAIR, PAIR, BF16),
                        item(PAIR, 2 * PAIR, BF16),
                        item(PAIR, 2 * PAIR, BF16),
                        item(PAIR, 2 * PAIR, F32),
                        item(1, PAIR, F32),
                        item(PAIR, PAIR, F32), item(PAIR, PAIR, F32),
                        item(CHUNK, PAIR, F32), item(CHUNK, PAIR, F32),
                        seq2(), pltpu.VMEM((seqs, 2, PAIR, PAIR), F32)],
        compiler_params=_cparams(("arbitrary", "arbitrary")),
        name=f"rwkv_mix_t{seq_len}",
    )(pa, pa, pa, pa, p["ts_mu"], p["ts_mu"], p["ts_mu"], p["ts_mu"],
      p["w0"], p["w_up"], p["a0"], p["a_up"], p["g_up"], p["k_k"], p["k_a"], p["r_k"], p["gn_g"], p["gn_b"], s0, *ride)
    return out, s_fin, cast


def _conv_kernel(pb_ref, prev_ref, next_ref, pwb_ref, dw_ref, dwb_ref, g_ref, b_ref, o_ref, pad_s, shift_s, *, geom, seq):
    width = o_ref.shape[1]
    sub = 64
    n_prompt_tok, dec_seq = geom
    tok = pl.program_id(0) * CONV_TILE
    in_prompt = tok < n_prompt_tok
    seq_len = jnp.where(in_prompt, seq, dec_seq)
    off = jnp.where(in_prompt, tok, tok - n_prompt_tok) % seq_len

    def glu(ref):
        pbv = ref[...] + pwb_ref[...]
        return pbv[:, :width] * jax.nn.sigmoid(pbv[:, width:])

    pad_s[0:CONV_HALO, :] = jnp.where(off > 0, glu(prev_ref), 0.0)
    pad_s[CONV_HALO:CONV_HALO + CONV_TILE, :] = glu(pb_ref)
    pad_s[CONV_HALO + CONV_TILE:, :] = jnp.where(off + CONV_TILE < seq_len, glu(next_ref), 0.0)
    keep = pad_s.shape[0] - SUBLANES
    for phase in range(SUBLANES):
        shift_s[phase, 0:keep, :] = pad_s[phase:phase + keep, :]
    first = CONV_HALO - CONV_W // 2
    for c in range(CONV_TILE // sub):
        acc = jnp.zeros((sub, width), F32)
        for d in range(CONV_W):
            phase = (first + d) % SUBLANES
            start = c * sub + first + d - phase
            acc = acc + shift_s[phase, start:start + sub, :] * dw_ref[d:d + 1, :]
        y = _layer_norm(acc + dwb_ref[...], g_ref[...], b_ref[...], LN_EPS)
        o_ref[c * sub:(c + 1) * sub, :] = y * jax.nn.sigmoid(y)


def _conv_module(pb, geom, seq, pw_b, dw_w, dw_b, cln_g, cln_b):
    n_tok = pb.shape[0]
    width = dw_w.shape[1]
    per_tile = CONV_TILE // CONV_HALO
    last_halo = n_tok // CONV_HALO - 1
    const = lambda i: (0, 0)
    return pl.pallas_call(
        functools.partial(_conv_kernel, geom=geom, seq=seq),
        out_shape=jax.ShapeDtypeStruct((n_tok, width), F32),
        grid=(n_tok // CONV_TILE,),
        in_specs=[pl.BlockSpec((CONV_TILE, 2 * width), lambda i: (i, 0)),
                  pl.BlockSpec((CONV_HALO, 2 * width), lambda i: (jnp.maximum(i * per_tile - 1, 0), 0)),
                  pl.BlockSpec((CONV_HALO, 2 * width), lambda i: (jnp.minimum((i + 1) * per_tile, last_halo), 0)),
                  pl.BlockSpec((1, 2 * width), const),
                  pl.BlockSpec((CONV_W, width), const),
                  pl.BlockSpec((1, width), const),
                  pl.BlockSpec((1, width), const),
                  pl.BlockSpec((1, width), const)],
        out_specs=pl.BlockSpec((CONV_TILE, width), lambda i: (i, 0)),
        scratch_shapes=[pltpu.VMEM((CONV_TILE + 2 * CONV_HALO, width), F32),
                        pltpu.VMEM((SUBLANES, CONV_TILE + 2 * CONV_HALO, width), F32)],
        compiler_params=_cparams(("arbitrary",)),
        name="conv_module",
    )(pb, pb, pb, pw_b.reshape(1, -1), dw_w, dw_b.reshape(1, -1), cln_g.reshape(1, -1), cln_b.reshape(1, -1))


def _diff_lambda(lq1, lk1, lq2, lk2, lam_init):
    dot = lambda a, b: jnp.exp(jnp.sum(a[...] * b[...], axis=-1, keepdims=True))
    return dot(lq1, lk1) - dot(lq2, lk2) + lam_init


def _diff_attend(q, keys, vals, lam, lam_init, subln):
    nq = q.shape[0]
    lane_first = lax.broadcasted_iota(jnp.int32, q.shape, 1) < HEAD
    q = q * (HEAD ** -0.5)
    q_st = jnp.concatenate([jnp.where(lane_first, q, 0.0), jnp.where(lane_first, 0.0, q)], axis=0)
    s = _bdot_nt(q_st, keys)
    s = s - jnp.max(s, axis=-1, keepdims=True)
    pexp = jnp.exp(s)
    ov = _bdot(pexp, vals) / jnp.sum(pexp, axis=-1, keepdims=True)
    o = ov[:nq] - lam * ov[nq:]
    o = o * lax.rsqrt(jnp.mean(o * o, axis=-1, keepdims=True) + RMS_EPS)
    return o * subln * (1.0 - lam_init)


def _head_cols(h):
    return slice(h * PAIR, (h + 1) * PAIR)


def _attn_prompt_kernel(q_ref, k_ref, v_ref, lq1, lk1, lq2, lk2, sub_ref, o_ref, *, lam_init):
    lam = _diff_lambda(lq1, lk1, lq2, lk2, lam_init)
    for h in range(q_ref.shape[1] // PAIR):
        c = _head_cols(h)
        o_ref[:, c] = _diff_attend(q_ref[:, c], k_ref[:, c], v_ref[:, c], lam, lam_init, sub_ref[...])


def _rope(x, cos, sin_signed):
    lane = lax.broadcasted_iota(jnp.int32, x.shape, 1)
    quarter = HEAD // 4
    partner = jnp.where((lane // quarter) % 2 == 0,
                        pltpu.roll(x, LANES - quarter, 1), pltpu.roll(x, quarter, 1))
    return x * cos + partner * sin_signed


def _attn_sample_kernel(q_ref, k_ref, v_ref, ck_ref, cv_ref, cq_ref, sq_ref, cosk_ref, sink_ref,
                        lq1, lk1, lq2, lk2, sub_ref, o_ref, *, lam_init):
    lam = _diff_lambda(lq1, lk1, lq2, lk2, lam_init)
    for h in range(q_ref.shape[1] // PAIR):
        c = _head_cols(h)
        q = _rope(q_ref[:, c], cq_ref[...], sq_ref[...])
        kl = _rope(k_ref[:, c], cosk_ref[...], sink_ref[...])
        keys = jnp.concatenate([ck_ref[:, c], kl], axis=0)
        vals = jnp.concatenate([cv_ref[:, c], v_ref[:, c]], axis=0)
        o_ref[:, c] = _diff_attend(q, keys, vals, lam, lam_init, sub_ref[...])


def _rope_tables(n_lat, grid_w):
    quarter = HEAD // 4
    inv = ROPE_BASE ** (-jnp.arange(quarter, dtype=F32) / quarter)
    t = jnp.arange(n_lat)
    row_ang = (t // grid_w).astype(F32)[:, None] * inv
    col_ang = (t % grid_w).astype(F32)[:, None] * inv
    ang = jnp.concatenate([row_ang, row_ang, col_ang, col_ang] * 2, axis=-1)
    sign = jnp.tile(jnp.concatenate([-jnp.ones(quarter, F32), jnp.ones(quarter, F32)]), 2 * HEAD // (2 * quarter))
    return jnp.cos(ang), jnp.sin(ang) * sign


def _attention(q, k_p, v_p, k_s, v_s, cache_k, cache_v, p, lam_init, n_prompt, seq, n_lat_req, n_lat, grid_w):
    inner = q.shape[1]
    vec = lambda n: pl.BlockSpec((1, n), lambda *_: (0, 0))
    lam_args = [p[n].reshape(1, -1) for n in ("lq1", "lk1", "lq2", "lk2")] + [p["subln_g"].reshape(1, -1)]
    lam_specs = [vec(HEAD)] * 4 + [vec(PAIR)]
    blk = pl.BlockSpec((seq, inner), lambda b: (b, 0))
    o_prompt = pl.pallas_call(
        functools.partial(_attn_prompt_kernel, lam_init=lam_init),
        out_shape=jax.ShapeDtypeStruct((n_prompt * seq, inner), F32),
        grid=(n_prompt,),
        in_specs=[blk, blk, blk] + lam_specs,
        out_specs=blk,
        compiler_params=_cparams(("arbitrary",)),
        name="diff_attn_prompt",
    )(q, k_p, v_p, *lam_args)

    past = cache_k.shape[1]
    qt = SUB_TILE
    n_qt = n_lat // qt
    q0 = n_prompt * seq // qt
    cos, sin = _rope_tables(n_lat, grid_w)
    lat = pl.BlockSpec((n_lat, inner), lambda b, t: (b, 0))
    ctx = pl.BlockSpec((None, past, inner), lambda b, t: (b, 0, 0))
    o_sample = pl.pallas_call(
        functools.partial(_attn_sample_kernel, lam_init=lam_init),
        out_shape=jax.ShapeDtypeStruct((n_lat_req * n_lat, inner), F32),
        grid=(n_lat_req, n_qt),
        in_specs=[pl.BlockSpec((qt, inner), lambda b, t: (q0 + b * n_qt + t, 0)),
                  lat, lat, ctx, ctx,
                  pl.BlockSpec((qt, PAIR), lambda b, t: (t, 0)),
                  pl.BlockSpec((qt, PAIR), lambda b, t: (t, 0)),
                  pl.BlockSpec((n_lat, PAIR), lambda b, t: (0, 0)),
                  pl.BlockSpec((n_lat, PAIR), lambda b, t: (0, 0))] + lam_specs,
        out_specs=pl.BlockSpec((qt, inner), lambda b, t: (b * n_qt + t, 0)),
        compiler_params=_cparams(("arbitrary", "arbitrary")),
        name="diff_attn_sample",
    )(q, k_s, v_s, cache_k, cache_v, cos, sin, cos, sin, *lam_args)
    return o_prompt, o_sample


def kernel(x_prompt, x_sample, state_wkv, cache_k, cache_v, c, c_ctx, mod_w, mod_b, ln1_g, ln1_b, ln2_g, ln2_b, ev_in_w, ev_pw_b, ev_ts_mu, ev_w0, ev_w_up, ev_a0, ev_a_up, ev_g_up, ev_k_k, ev_k_a, ev_r_k, ev_gn_g, ev_gn_b, ev_dw_w, ev_dw_b, ev_cln_g, ev_cln_b, ev_out_w, od_in_w, od_lq1, od_lk1, od_lq2, od_lk2, od_subln_g, od_out_w, rg_w, rg_b, re_w, re_b, ffn_w_gate, ffn_w_up, ffn_w_down):
    nbp, seq, d = x_prompt.shape
    nbs, n_lat, _ = x_sample.shape
    depth = mod_w.shape[0]
    n_prompt_tok = nbp * seq
    geom = (n_prompt_tok, n_lat)
    alpha = (2 * depth) ** 0.25
    a_width = ev_w0.shape[-1]
    a_cols = ev_ts_mu.shape[-1]
    heads = a_width // HEAD
    grid_w = 64
    assert n_prompt_tok % MOE_TILE == 0 and n_lat % MOE_TILE == 0 and n_lat % SUB_TILE == 0
    assert seq % CONV_TILE == 0 and n_lat % CONV_TILE == 0 and CONV_HALO >= CONV_W // 2
    assert 1 + nbs <= 8

    xs = (x_prompt.reshape(n_prompt_tok, d), x_sample.reshape(nbs * n_lat, d))
    cond = jnp.concatenate([c_ctx[None, :], c, jnp.zeros((8 - 1 - nbs, d), F32)], axis=0)
    flat = lambda w: w.reshape(-1, w.shape[-1])
    mod, (ev_in_b,) = _ada_table(cond, mod_w, mod_b, (flat(ev_in_w),))
    ev_in_b = ev_in_b.reshape(ev_in_w.shape)
    late = (ffn_w_gate, ffn_w_up, ffn_w_down, ev_out_w, od_in_w, od_out_w)

    new_wkv, new_k, new_v = [], [], []
    for l in range(depth):
        last = l == depth - 1
        if l % 2 == 0:
            i = l // 2
            pa, pb = _mod_matmul(xs, mod, l, ev_in_b[i], ((a_cols, False), (ev_in_w.shape[-1] - a_cols, False)), geom)
            prm = dict(ts_mu=ev_ts_mu[i][None, :], w0=ev_w0[i], w_up=ev_w_up[i], a0=ev_a0[i], a_up=ev_a_up[i],
                       g_up=ev_g_up[i], k_k=ev_k_k[i][None, :], k_a=ev_k_a[i][None, :],
                       r_k=ev_r_k[i].reshape(1, a_width), gn_g=ev_gn_g[i][None, :], gn_b=ev_gn_b[i][None, :])
            zero_state = jnp.zeros((nbp, 2, heads, HEAD, HEAD), F32)
            oa_p, st, cast = _rwkv_mix(pa, 0, nbp, seq, zero_state, prm, () if l else tuple(flat(w) for w in late))
            if cast:
                w_gate, w_up, w_down, ev_out_b, od_in_b, od_out_b = (c.reshape(w.shape) for c, w in zip(cast, late))
            oa_s, _, _ = _rwkv_mix(pa, n_prompt_tok, nbs, n_lat, state_wkv[:, i], prm)
            conv = (ev_pw_b[i], ev_dw_w[i], ev_dw_b[i], ev_cln_g[i], ev_cln_b[i])
            hs = [(oa_p, oa_s), (_conv_module(pb, geom, seq, *conv),)]
            out_w = ev_out_b[i]
            new_wkv.append(st)
        else:
            j = l // 2
            lam_init = 0.8 - 0.6 * math.exp(-0.3 * l)
            inner = od_in_w.shape[-1] // 3
            q, k_p, k_s, v_p, v_s = _mod_matmul(xs, mod, l, od_in_b[j],
                                                 ((inner, False), (inner, True), (inner, True)), geom)
            prm = dict(lq1=od_lq1[j], lk1=od_lk1[j], lq2=od_lq2[j], lk2=od_lk2[j], subln_g=od_subln_g[j])
            ck = cache_k[:, j].reshape(nbs, -1, inner)
            cv = cache_v[:, j].reshape(nbs, -1, inner)
            hs = [_attention(q, k_p, v_p, k_s, v_s, ck, cv, prm, lam_init, nbp, seq, nbs, n_lat, grid_w)]
            out_w = od_out_b[j]
            new_k.append(k_p.reshape(nbp, seq, inner // PAIR, PAIR))
            new_v.append(v_p.reshape(nbp, seq, inner // PAIR, PAIR))
        x = _out_proj_norm(hs, xs, mod, l, out_w, ln1_g[l], ln1_b[l], alpha, geom)
        rw = jnp.concatenate([rg_w[l], re_w[l],
                              jnp.zeros((d, ROUTER_LANES - N_GROUPS - N_EXPERTS), F32)], axis=1)
        rb = jnp.concatenate([rg_b[l], re_b[l], jnp.zeros((ROUTER_LANES - N_GROUPS - N_EXPERTS,), F32)])[None, :]
        xs = tuple(_moe_norm(x, mod, l, rw, rb, w_gate, w_up, w_down, ln2_g[l], ln2_b[l], alpha, geom,
                             split_out=last))

    y_prompt = xs[0].reshape(nbp, seq, d)
    y_sample = xs[1].reshape(nbs, n_lat, d)
    return (y_prompt, y_sample, jnp.stack(new_wkv, axis=1), jnp.stack(new_k, axis=1), jnp.stack(new_v, axis=1))
```

```python
import functools
import math

import jax
import jax.numpy as jnp
from jax import lax
from jax.experimental import pallas as pl
from jax.experimental.pallas import tpu as pltpu

F32 = jnp.float32
BF16 = jnp.bfloat16

LANES = 128
SUBLANES = 8
HEAD = 64
PAIR = 2 * HEAD
CHUNK = 64
RWKV_STEP_ROWS = 512
INV_BLOCK = 16
PROJ_TILES = (1024, 512, 256)
PROJ_VMEM_BUDGET = 48 * 1024 * 1024
SUB_TILE = 256
MOE_TILE = 1024
ADA_PER_STEP = 2
CONV_W = 31
CONV_TILE = 256
CONV_HALO = 16
N_GROUPS = 4
EXPERTS_PER_GROUP = 8
N_EXPERTS = N_GROUPS * EXPERTS_PER_GROUP
ROUTER_LANES = 128
EXPERTS_PER_STEP = 4
MOE_WEIGHT_SLOTS = 3
MOE_WINDOWS = (128, 256, 384, 512)
BF16_ROWS = 16
LN_EPS = 1e-5
GN_EPS = 64e-5
RMS_EPS = 1e-5
ROPE_BASE = 10000.0
EXP_M05 = math.exp(-0.5)
VMEM_LIMIT = 60 * 1024 * 1024


def _bdot(a, b):
    return jnp.dot(a.astype(BF16), b.astype(BF16), preferred_element_type=F32)


def _bdot_nt(a, b):
    return lax.dot_general(a.astype(BF16), b.astype(BF16), (((1,), (1,)), ((), ())),
                           preferred_element_type=F32)


def _split2(a):
    hi = a.astype(BF16)
    lo = (a - hi.astype(F32)).astype(BF16)
    return hi, lo


def _dot_f32ish(a, b):
    ah, al = _split2(a)
    bh, bl = _split2(b)
    d = lambda x, y: jnp.dot(x, y, preferred_element_type=F32)
    return d(ah, bh) + d(ah, bl) + d(al, bh)


def _dot_ones_lhs(m_bf16, a):
    h, l = _split2(a)
    d = lambda y: jnp.dot(m_bf16, y, preferred_element_type=F32)
    return d(h) + d(l)


def _dot_ones_rhs(a, m_bf16):
    h, l = _split2(a)
    d = lambda x: jnp.dot(x, m_bf16, preferred_element_type=F32)
    return d(h) + d(l)


def _layer_norm(z, g, b, eps):
    mu = jnp.mean(z, axis=-1, keepdims=True)
    zc = z - mu
    var = jnp.mean(zc * zc, axis=-1, keepdims=True)
    return zc * lax.rsqrt(var + eps) * g + b


def _cond_row(tile, tile_rows, n_prompt_tok, dec_seq):
    tok = tile * tile_rows
    return jnp.where(tok < n_prompt_tok, 0, 1 + (tok - n_prompt_tok) // dec_seq)


def _cparams(sem):
    return pltpu.CompilerParams(dimension_semantics=sem, vmem_limit_bytes=VMEM_LIMIT)


def _ada_kernel(cond_ref, w_ref, b_ref, *rest, n_ride):
    ride_in, o_ref, ride_out = rest[:n_ride], rest[n_ride], rest[n_ride + 1:]

    @pl.when(pl.program_id(1) == 0)
    def _():
        for src, dst in zip(ride_in, ride_out):
            dst[...] = src[...].astype(dst.dtype)

    c = cond_ref[...]
    s = c * jax.nn.sigmoid(c)
    d = o_ref.shape[-1]
    y = _dot_f32ish(s, w_ref[0])
    for j in range(ADA_PER_STEP):
        o_ref[0, j] = y[:, j * d:(j + 1) * d] + b_ref[0, j]


def _ada_table(cond8, mod_w, mod_b, ride=()):
    depth, d, _ = mod_w.shape
    assert all(a.shape[0] % (depth * BF16_ROWS) == 0 for a in ride)
    ride_specs = [pl.BlockSpec((a.shape[0] // depth, a.shape[1]), lambda l, j: (l, 0)) for a in ride]
    mod, *cast = pl.pallas_call(
        functools.partial(_ada_kernel, n_ride=len(ride)),
        out_shape=[jax.ShapeDtypeStruct((depth, 6, 8, d), F32)] + [jax.ShapeDtypeStruct(a.shape, BF16) for a in ride],
        grid=(depth, 6 // ADA_PER_STEP),
        in_specs=[pl.BlockSpec((8, d), lambda l, j: (0, 0)),
                  pl.BlockSpec((1, d, ADA_PER_STEP * d), lambda l, j: (l, 0, j)),
                  pl.BlockSpec((1, ADA_PER_STEP, 1, d), lambda l, j: (l, j, 0, 0))] + ride_specs,
        out_specs=[pl.BlockSpec((1, ADA_PER_STEP, 8, d), lambda l, j: (l, j, 0, 0))] + ride_specs,
        compiler_params=_cparams(("arbitrary", "arbitrary")),
        name="ada_table",
    )(cond8, mod_w, mod_b.reshape(depth, 6, 1, d), *ride)
    return mod, cast


def _mod_spec(layer, which, d):
    return pl.BlockSpec((None, None, 8, d), lambda *_: (layer, which, 0, 0))


def _group_specs(tile, width, n_prompt_tiles):
    return [pl.BlockSpec((tile, width), lambda i, *_: (jnp.minimum(i, n_prompt_tiles - 1), 0)),
            pl.BlockSpec((tile, width), lambda i, *_: (jnp.maximum(i - n_prompt_tiles, 0), 0))]


def _load_group(i, n_prompt_tiles, refs, rows=slice(None)):
    if len(refs) == 1:
        return refs[0][rows, :]
    return jnp.where(i < n_prompt_tiles, refs[0][rows, :], refs[1][rows, :])


def _store_group(i, n_prompt_tiles, refs, value, rows=slice(None)):
    if len(refs) == 1:
        refs[0][rows, :] = value
        return

    @pl.when(i < n_prompt_tiles)
    def _():
        refs[0][rows, :] = value

    @pl.when(i >= n_prompt_tiles)
    def _():
        refs[1][rows, :] = value


def _proj_tile(cols_f32, w_elems, geom):
    for tile in PROJ_TILES:
        fits = 2 * 4 * tile * cols_f32 + 2 * w_elems <= PROJ_VMEM_BUDGET
        if fits and geom[0] % tile == 0 and geom[1] % tile == 0:
            return tile
    raise ValueError("no projection tile fits")


def _modmm_kernel(*refs, n_x, outs, geom, tile):
    x_refs = refs[:n_x]
    sh_ref, sc_ref, w_ref = refs[n_x:n_x + 3]
    o_refs = refs[n_x + 3:]
    i = pl.program_id(0)
    n_pt = geom[0] // tile
    r = _cond_row(i, tile, *geom)
    h = _load_group(i, n_pt, x_refs) * (1.0 + sc_ref[pl.ds(r, 1), :]) + sh_ref[pl.ds(r, 1), :]
    hb = h.astype(BF16)
    off = 0
    k = 0
    for n, split in outs:
        y = jnp.dot(hb, w_ref[:, off:off + n], preferred_element_type=F32)
        _store_group(i, n_pt, o_refs[k:k + 1 + split], y)
        off += n
        k += 1 + split


def _mod_matmul(xs, mod, layer, w, outs, geom):
    d, n_out = w.shape
    n_prompt_tok = geom[0]
    n_tok = sum(x.shape[0] for x in xs)
    tile = _proj_tile(len(xs) * d + sum(n * (1 + split) for n, split in outs), w.size, geom)
    n_pt = n_prompt_tok // tile
    whole = lambda n: [pl.BlockSpec((tile, n), lambda i: (i, 0))]
    x_specs = whole(d) if len(xs) == 1 else _group_specs(tile, d, n_pt)
    out_shape, out_specs = [], []
    for n, split in outs:
        if split:
            out_shape += [jax.ShapeDtypeStruct((n_prompt_tok, n), F32),
                          jax.ShapeDtypeStruct((n_tok - n_prompt_tok, n), F32)]
            out_specs += _group_specs(tile, n, n_pt)
        else:
            out_shape += [jax.ShapeDtypeStruct((n_tok, n), F32)]
            out_specs += whole(n)
    return pl.pallas_call(
        functools.partial(_modmm_kernel, n_x=len(xs), outs=outs, geom=geom, tile=tile),
        out_shape=out_shape,
        grid=(n_tok // tile,),
        in_specs=x_specs + [_mod_spec(layer, 0, d), _mod_spec(layer, 1, d),
                            pl.BlockSpec((d, n_out), lambda i: (0, 0), pipeline_mode=pl.Buffered(1))],
        out_specs=out_specs,
        compiler_params=_cparams(("arbitrary",)),
        name=f"mod_matmul_l{layer}",
    )(*xs, mod, mod, w)


def _outproj_kernel(*refs, arity, alpha, geom, tile):
    groups, k = [], 0
    for n in arity:
        groups.append(refs[k:k + n])
        k += n
    gate_ref, w_ref, g_ref, b_ref, o_ref = refs[k:]
    h_groups, x_group = groups[:-1], groups[-1]
    i = pl.program_id(0)
    n_pt = geom[0] // tile
    r = _cond_row(i, tile, *geom)
    y = None
    off = 0
    for h_refs in h_groups:
        n = h_refs[0].shape[1]
        h = _load_group(i, n_pt, h_refs)
        part = jnp.dot(h.astype(BF16), w_ref[off:off + n, :], preferred_element_type=F32)
        y = part if y is None else y + part
        off += n
    z = alpha * _load_group(i, n_pt, x_group) + gate_ref[pl.ds(r, 1), :] * y
    o_ref[...] = _layer_norm(z, g_ref[...], b_ref[...], LN_EPS)


def _out_proj_norm(hs, xs, mod, layer, w, ln_g, ln_b, alpha, geom):
    d = w.shape[1]
    n_tok = sum(x.shape[0] for x in xs)
    tile = _proj_tile(d + sum(len(g) * g[0].shape[1] for g in list(hs) + [xs]), w.size, geom)
    n_pt = geom[0] // tile
    specs = []
    for grp in list(hs) + [xs]:
        n = grp[0].shape[1]
        specs += [pl.BlockSpec((tile, n), lambda i: (i, 0))] if len(grp) == 1 else _group_specs(tile, n, n_pt)
    return pl.pallas_call(
        functools.partial(_outproj_kernel, arity=tuple(len(g) for g in list(hs) + [xs]), alpha=alpha, geom=geom,
                          tile=tile),
        out_shape=jax.ShapeDtypeStruct((n_tok, d), F32),
        grid=(n_tok // tile,),
        in_specs=specs + [
            _mod_spec(layer, 2, d),
            pl.BlockSpec(w.shape, lambda i: (0, 0), pipeline_mode=pl.Buffered(1)),
            pl.BlockSpec((1, d), lambda i: (0, 0)),
            pl.BlockSpec((1, d), lambda i: (0, 0))],
        out_specs=pl.BlockSpec((tile, d), lambda i: (i, 0)),
        compiler_params=_cparams(("arbitrary",)),
        name=f"out_proj_norm_l{layer}",
    )(*[a for g in hs for a in g], *xs, mod, w, ln_g.reshape(1, d), ln_b.reshape(1, d))


def _route(logits):
    neg = jnp.float32(-3e38)
    big = jnp.int32(1 << 20)
    lane = lax.broadcasted_iota(jnp.int32, logits.shape, 1)
    is_g = lane < N_GROUPS
    gl = jnp.where(is_g, logits, neg)
    gmax = jnp.max(gl, axis=-1, keepdims=True)
    gidx = jnp.min(jnp.where(gl == gmax, lane, big), axis=-1, keepdims=True)
    gsum = jnp.sum(jnp.where(is_g, jnp.exp(gl - gmax), 0.0), axis=-1, keepdims=True)
    g_p = 1.0 / gsum
    lo = N_GROUPS + gidx * EXPERTS_PER_GROUP
    in_grp = (lane >= lo) & (lane < lo + EXPERTS_PER_GROUP)
    el = jnp.where(in_grp, logits, neg)
    v1 = jnp.max(el, axis=-1, keepdims=True)
    i1 = jnp.min(jnp.where(el == v1, lane, big), axis=-1, keepdims=True)
    el2 = jnp.where(lane == i1, neg, el)
    v2 = jnp.max(el2, axis=-1, keepdims=True)
    i2 = jnp.min(jnp.where(el2 == v2, lane, big), axis=-1, keepdims=True)
    e21 = jnp.exp(v2 - v1)
    den = 1.0 + e21
    w1 = g_p / den
    w2 = g_p * e21 / den
    return jnp.where(lane == i1, w1, 0.0) + jnp.where(lane == i2, w2, 0.0), gidx


def _moe_kernel(x_ref, sh_ref, sc_ref, gate_ref, rw_ref, rb_ref, wg_ref, wu_ref, wd_ref, g_ref, b_ref,
                *rest, alpha, geom, layer):
    o_refs = rest[:-10]
    hu_scr, h_scr, gates_scr, acc_scr, perm_scr, bounds, wg_buf, wu_buf, wd_buf, w_sem = rest[-10:]
    i = pl.program_id(0)
    grp = pl.program_id(1)
    n_grp = pl.num_programs(1)
    per_group = EXPERTS_PER_GROUP // EXPERTS_PER_STEP
    n_blocks = N_EXPERTS // EXPERTS_PER_STEP

    def weight_copies(block_step):
        blk = pl.ds((block_step % n_blocks) * EXPERTS_PER_STEP, EXPERTS_PER_STEP)
        slot = block_step % MOE_WEIGHT_SLOTS
        return [pltpu.make_async_copy(src.at[layer, blk], dst.at[slot], w_sem.at[k, slot])
                for k, (src, dst) in enumerate(((wg_ref, wg_buf), (wu_ref, wu_buf), (wd_ref, wd_buf)))]

    first_block = (i * n_grp + grp) * per_group
    n_block_steps = pl.num_programs(0) * n_grp * per_group

    @pl.when(first_block == 0)
    def _():
        for ahead in range(MOE_WEIGHT_SLOTS - 1):
            for cp in weight_copies(ahead):
                cp.start()

    def prefetch(block_step):
        @pl.when(block_step + MOE_WEIGHT_SLOTS - 1 < n_block_steps)
        def _():
            for cp in weight_copies(block_step + MOE_WEIGHT_SLOTS - 1):
                cp.start()

    prefetch(first_block)

    r = _cond_row(i, MOE_TILE, *geom)
    n_pt = geom[0] // MOE_TILE
    sub = SUB_TILE
    n_sub = MOE_TILE // sub

    @pl.when(grp == 0)
    def _():
        lane = lax.broadcasted_iota(jnp.int32, (sub, ROUTER_LANES), 1)
        ri = lax.broadcasted_iota(jnp.int32, (sub, sub), 0)
        ci = lax.broadcasted_iota(jnp.int32, (sub, sub), 1)
        tril = jnp.where(ci <= ri, 1.0, 0.0).astype(BF16)
        running = jnp.zeros((1, ROUTER_LANES), F32)
        onehots, cums = [], []
        for s in range(n_sub):
            rows = slice(s * sub, (s + 1) * sub)
            h = x_ref[rows, :] * (1.0 + sc_ref[pl.ds(r, 1), :]) + sh_ref[pl.ds(r, 1), :]
            hu_scr[rows, :] = h.astype(BF16)
            logits = _dot_f32ish(h, rw_ref[...]) + rb_ref[...]
            gates, gidx = _route(logits)
            gates_scr[rows, :] = gates
            onehot = jnp.where(lane == gidx, 1.0, 0.0)
            cum = jnp.dot(tril, onehot.astype(BF16), preferred_element_type=F32) + running
            running = cum[sub - 1:sub, :]
            onehots.append(onehot)
            cums.append(cum)
        gi = lax.broadcasted_iota(jnp.int32, (ROUTER_LANES, ROUTER_LANES), 0)
        gj = lax.broadcasted_iota(jnp.int32, (ROUTER_LANES, ROUTER_LANES), 1)
        before = jnp.where((gi < gj) & (gi < N_GROUPS), 1.0, 0.0).astype(BF16)
        starts = _dot_ones_rhs(jnp.broadcast_to(running, (8, ROUTER_LANES)), before)[0:1, :]
        for g in range(N_GROUPS + 1):
            bounds[g] = starts[0, g].astype(jnp.int32)
        dest_cols = [jnp.sum(onehots[s] * (starts + cums[s] - 1.0), axis=-1, keepdims=True) for s in range(n_sub)]
        dest_row = jnp.concatenate(
            [jnp.broadcast_to(dc, (sub, ROUTER_LANES)).T[0:1, :] for dc in dest_cols], axis=1)
        row_id = lax.broadcasted_iota(jnp.int32, (sub, MOE_TILE), 0).astype(F32)
        for s in range(n_sub):
            rows = slice(s * sub, (s + 1) * sub)
            perm_scr[rows, :] = jnp.where(row_id + float(s * sub) == dest_row, 1.0, 0.0).astype(BF16)
        perm = perm_scr[...]
        by_perm = lambda y: jnp.dot(perm, y, preferred_element_type=F32)
        h_scr[...] = by_perm(hu_scr[...]).astype(BF16)
        g_hi, g_lo = _split2(gates_scr[...])
        gates_scr[...] = by_perm(g_hi) + by_perm(g_lo)
        acc_scr[...] = jnp.zeros(acc_scr.shape, F32)

    first, end = bounds[grp], bounds[grp + 1]
    first_al = (first // BF16_ROWS) * BF16_ROWS
    unit = MOE_WINDOWS[0]
    assert all(w == (k + 1) * unit for k, w in enumerate(MOE_WINDOWS))

    def experts_on(e, slot, lo, size):
        start = pl.multiple_of(jnp.minimum(lo, MOE_TILE - size), BF16_ROWS)
        rows = pl.ds(start, size)
        lane = lax.broadcasted_iota(jnp.int32, (size, ROUTER_LANES), 1)
        row = lax.broadcasted_iota(jnp.int32, (size, 1), 0)
        hb = h_scr[rows, :]
        gates = jnp.where(row + start >= lo, gates_scr[rows, :], 0.0)
        out = acc_scr[rows, :]
        for j in range(EXPERTS_PER_STEP):
            lane_j = N_GROUPS + e * EXPERTS_PER_STEP + j
            ge = jnp.sum(jnp.where(lane == lane_j, gates, 0.0), axis=-1, keepdims=True)
            a = jnp.dot(hb, wg_buf[slot, j], preferred_element_type=F32)
            u = jnp.dot(hb, wu_buf[slot, j], preferred_element_type=F32)
            hid = (a * jax.nn.sigmoid(a)) * u * ge
            out = out + jnp.dot(hid.astype(BF16), wd_buf[slot, j], preferred_element_type=F32)
        acc_scr[rows, :] = out

    for part in range(per_group):
        block_step = first_block + part
        if part:
            prefetch(block_step)

        for cp in weight_copies(block_step):
            cp.wait()
        slot = block_step % MOE_WEIGHT_SLOTS
        e = grp * per_group + part

        def window(w, carry, e=e, slot=slot):
            lo = first_al + w * MOE_WINDOWS[-1]
            n_units = jnp.minimum((end - lo + unit - 1) // unit, len(MOE_WINDOWS))
            for k, size in enumerate(MOE_WINDOWS):
                pl.when(n_units == k + 1)(functools.partial(experts_on, e, slot, lo, size))
            return carry

        lax.fori_loop(0, (end - first_al + MOE_WINDOWS[-1] - 1) // MOE_WINDOWS[-1], window, 0)

    @pl.when(grp == n_grp - 1)
    def _():
        acc_scr[...] = lax.dot_general(perm_scr[...], acc_scr[...].astype(BF16), (((0,), (0,)), ((), ())),
                                       preferred_element_type=F32)
        for s in range(n_sub):
            rows = slice(s * sub, (s + 1) * sub)
            z = alpha * x_ref[rows, :] + gate_ref[pl.ds(r, 1), :] * acc_scr[rows, :]
            _store_group(i, n_pt, o_refs, _layer_norm(z, g_ref[...], b_ref[...], LN_EPS), rows)


def _moe_norm(x, mod, layer, rw, rb, w_gate, w_up, w_down, ln_g, ln_b, alpha, geom, split_out):
    n_tok, d = x.shape
    ff = w_gate.shape[-1]
    assert w_gate.dtype == BF16 and w_up.dtype == BF16 and w_down.dtype == BF16
    const = lambda i, e: (0, 0)
    if split_out:
        out_shape = [jax.ShapeDtypeStruct((geom[0], d), F32), jax.ShapeDtypeStruct((n_tok - geom[0], d), F32)]
        out_specs = _group_specs(MOE_TILE, d, geom[0] // MOE_TILE)
    else:
        out_shape = [jax.ShapeDtypeStruct((n_tok, d), F32)]
        out_specs = [pl.BlockSpec((MOE_TILE, d), lambda i, e: (i, 0))]
    return pl.pallas_call(
        functools.partial(_moe_kernel, alpha=alpha, geom=geom, layer=layer),
        out_shape=out_shape,
        grid=(n_tok // MOE_TILE, N_GROUPS),
        in_specs=[pl.BlockSpec((MOE_TILE, d), lambda i, e: (i, 0)),
                  _mod_spec(layer, 3, d), _mod_spec(layer, 4, d), _mod_spec(layer, 5, d),
                  pl.BlockSpec((d, ROUTER_LANES), const),
                  pl.BlockSpec((1, ROUTER_LANES), const),
                  pl.BlockSpec(memory_space=pl.ANY), pl.BlockSpec(memory_space=pl.ANY),
                  pl.BlockSpec(memory_space=pl.ANY),
                  pl.BlockSpec((1, d), const),
                  pl.BlockSpec((1, d), const)],
        out_specs=out_specs,
        scratch_shapes=[pltpu.VMEM((MOE_TILE, d), BF16), pltpu.VMEM((MOE_TILE, d), BF16),
                        pltpu.VMEM((MOE_TILE, ROUTER_LANES), F32),
                        pltpu.VMEM((MOE_TILE, d), F32),
                        pltpu.VMEM((MOE_TILE, MOE_TILE), BF16),
                        pltpu.SMEM((8,), jnp.int32),
                        pltpu.VMEM((MOE_WEIGHT_SLOTS, EXPERTS_PER_STEP, d, ff), BF16),
                        pltpu.VMEM((MOE_WEIGHT_SLOTS, EXPERTS_PER_STEP, d, ff), BF16),
                        pltpu.VMEM((MOE_WEIGHT_SLOTS, EXPERTS_PER_STEP, ff, d), BF16),
                        pltpu.SemaphoreType.DMA((3, MOE_WEIGHT_SLOTS))],
        compiler_params=_cparams(("arbitrary", "arbitrary")),
        name=f"moe_norm_l{layer}",
    )(x, mod, mod, mod, rw, rb, w_gate, w_up, w_down, ln_g.reshape(1, d), ln_b.reshape(1, d))


def _rwkv_kernel(r_ref, k_ref, v_ref, lora_ref, mur_ref, muk_ref, muv_ref, mul_ref,
                 w0_ref, wup_ref, a0_ref, aup_ref, gup_ref, kk_ref, ka_ref, rk_ref, gng_ref, gnb_ref, s0_ref,
                 *rest, seq_len, seqs, unroll, n_ride):
    ride_in, (out_ref, sfin_ref), ride_out = rest[:n_ride], rest[n_ride:n_ride + 2], rest[n_ride + 2:2 * n_ride + 2]
    (lw_s, kz_s, bz_s, rr_s, vv_s, kn_s, abar_s, rt_s, wt_s, vst_s, p_s, tm_s, mb_s, mk_s, nkb_s, txr_s, tx_s, gl_s,
     g_s, h_s, q_s, y0_s, y_s, st_s) = rest[2 * n_ride + 2:]
    for src, dst in zip(ride_in, ride_out):
        dst[...] = src[...].astype(dst.dtype)
    t_len = seqs * seq_len
    n_chunks = t_len // CHUNK
    seq_chunks = seq_len // CHUNK
    c2 = 2 * CHUNK

    row = lax.broadcasted_iota(jnp.int32, (t_len, 1), 0) % seq_len

    def tshift(x, mu):
        prev = jnp.where(row == 0, 0.0, pltpu.roll(x, 1, 0))
        nxt = jnp.where(row == seq_len - 1, 0.0, pltpu.roll(x, t_len - 1, 0))
        return x + mu * (0.5 * (prev + nxt) - x)

    r = tshift(r_ref[...], mur_ref[...])
    k = tshift(k_ref[...], muk_ref[...])
    v = tshift(v_ref[...], muv_ref[...])
    lo = tshift(lora_ref[...], mul_ref[...])
    wd = jnp.tanh(lo[:, 0:PAIR])
    ad = lo[:, PAIR:2 * PAIR]
    gate = _bdot(jax.nn.sigmoid(lo[:, 2 * PAIR:3 * PAIR]), gup_ref[...])

    ri = lax.broadcasted_iota(jnp.int32, (PAIR, PAIR), 0)
    ci = lax.broadcasted_iota(jnp.int32, (PAIR, PAIR), 1)
    same_head = (ri // HEAD) == (ci // HEAD)
    ones_bd = jnp.where(same_head, 1.0, 0.0).astype(BF16)
    eye = ri == ci

    kk = k * kk_ref[...]
    ssq = _dot_ones_rhs(kk * kk, ones_bd)
    kk = kk * lax.rsqrt(jnp.maximum(ssq, 1e-24))
    wup2 = wup_ref[...].reshape(PAIR, PAIR)
    aup2 = aup_ref[...].reshape(PAIR, PAIR)
    bonus = jnp.zeros((t_len, PAIR), F32)
    for z in range(2):
        zrows = (ri // HEAD) == z
        w_pre = w0_ref[z:z + 1, :] + _bdot(wd, jnp.where(zrows, wup2, 0.0))
        lw_s[z] = -EXP_M05 * jax.nn.sigmoid(w_pre)
        a = jax.nn.sigmoid(a0_ref[z:z + 1, :] + _bdot(ad, jnp.where(zrows, aup2, 0.0)))
        kz = k * (1.0 + (a - 1.0) * ka_ref[...])
        kz_s[z] = kz
        bz_s[z] = kk * a
        bonus = bonus + _dot_ones_rhs(r * kz * rk_ref[...], ones_bd) * v
    rr_s[...] = r
    vv_s[...] = v
    kn_s[...] = kk

    rt_i = ri % CHUNK
    cs_i = ci % CHUNK
    strict = (cs_i < rt_i, cs_i > rt_i)
    incl = (cs_i <= rt_i, cs_i >= rt_i)
    t64 = lax.broadcasted_iota(jnp.int32, (CHUNK, CHUNK), 0)
    s64 = lax.broadcasted_iota(jnp.int32, (CHUNK, CHUNK), 1)
    tri = (jnp.where(s64 <= t64, 1.0, 0.0).astype(BF16), jnp.where(s64 >= t64, 1.0, 0.0).astype(BF16))
    lane_head0 = lax.broadcasted_iota(jnp.int32, (CHUNK, PAIR), 1) < HEAD
    eye_f = jnp.where(eye, 1.0, 0.0)
    same16 = (ri // INV_BLOCK) == (ci // INV_BLOCK)

    def stack(x):
        return jnp.concatenate([jnp.where(lane_head0, x, 0.0), jnp.where(lane_head0, 0.0, x)], axis=0)

    def fold(x):
        return x[:CHUNK] + x[CHUNK:]

    def chunk_rows(c):
        return pl.ds(pl.multiple_of(c * CHUNK, CHUNK), CHUNK)

    def each_item(body):
        def both_directions(c, carry):
            body(0, c)
            body(1, c)
            return carry

        lax.fori_loop(0, n_chunks, both_directions, 0, unroll=unroll)

    def stage_operands(z, c):
        rows = chunk_rows(c)
        r_c, kn_c = rr_s[rows, :], kn_s[rows, :]
        lw_c, kz_c, bz_c = lw_s[z, rows, :], kz_s[z, rows, :], bz_s[z, rows, :]
        lg = _dot_ones_lhs(tri[z], lw_c)
        total = lg[CHUNK - 1:CHUNK, :] if z == 0 else lg[0:1, :]
        e_neg = jnp.exp(-lg)
        e_rem = jnp.exp(total - lg)
        abar = stack(kn_c * jnp.exp(lg - lw_c))
        rt = stack(r_c * jnp.exp(lg))
        big = _bdot_nt(jnp.concatenate([abar, rt], axis=0),
                       jnp.concatenate([stack(bz_c * e_neg), stack(kz_c * e_neg)], axis=0))
        mb = jnp.where(strict[z], big[:c2, :c2], 0.0)
        abar_s[z, c] = abar.astype(BF16)
        rt_s[z, c] = rt
        wt_s[z, c] = jnp.concatenate([stack(kz_c * e_rem), -stack(bz_c * e_rem)], axis=0).T.astype(BF16)
        gl_s[z, c] = jnp.exp(total)
        m0 = jnp.where(same16, mb, 0.0)
        tm_s[z, c] = eye_f - m0
        p_s[z, c] = m0.astype(BF16)
        mb_s[z, c] = mb.astype(BF16)
        mk_s[z, c] = jnp.where(strict[z], big[:c2, c2:], 0.0).astype(BF16)
        nkb_s[z, c] = jnp.concatenate([jnp.where(incl[z], big[c2:, c2:], 0.0),
                                       -jnp.where(incl[z], big[c2:, :c2], 0.0)], axis=1).astype(BF16)

    def stage_values(c, carry):
        vst_s[c] = stack(vv_s[chunk_rows(c), :]).astype(BF16)
        return carry

    lax.fori_loop(0, n_chunks, stage_values, 0, unroll=unroll)
    each_item(stage_operands)

    def stage_square(z, c):
        m0 = p_s[z, c]
        p_s[z, c] = jnp.dot(m0, m0, preferred_element_type=F32).astype(BF16)
        mkv = jnp.dot(mk_s[z, c], vst_s[c], preferred_element_type=F32)
        txr_s[z, c] = jnp.concatenate([abar_s[z, c], mkv.astype(BF16)], axis=1)

    each_item(stage_square)

    n_round = INV_BLOCK.bit_length() - 2
    for j in range(n_round):
        def stage_double(z, c, last=(j == n_round - 1)):
            tm, p = tm_s[z, c], p_s[z, c]
            tm_s[z, c] = tm + jnp.dot(tm.astype(BF16), p, preferred_element_type=F32)
            if not last:
                p_s[z, c] = jnp.dot(p, p, preferred_element_type=F32).astype(BF16)

        each_item(stage_double)

    size = INV_BLOCK
    while size < CHUNK:
        off_diag = ((ri // (2 * size)) == (ci // (2 * size))) & ((ri // size) != (ci // size))

        def stage_merge_rhs(z, c, off_diag=off_diag):
            cross = jnp.where(off_diag, mb_s[z, c], jnp.zeros((), BF16))
            p_s[z, c] = jnp.dot(cross, tm_s[z, c].astype(BF16), preferred_element_type=F32).astype(BF16)

        def stage_merge(z, c):
            tm = tm_s[z, c]
            tm_s[z, c] = tm - jnp.dot(tm.astype(BF16), p_s[z, c], preferred_element_type=F32)

        each_item(stage_merge_rhs)
        each_item(stage_merge)
        size *= 2

    def stage_apply(z, c):
        tx_s[z, c] = jnp.dot(tm_s[z, c].astype(BF16), txr_s[z, c], preferred_element_type=F32)

    each_item(stage_apply)

    def stage_maps(z, c):
        tx = tx_s[z, c].astype(BF16)
        v_st = vst_s[c]
        wt = wt_s[z, c]
        from_tx = jnp.dot(wt[:, c2:], tx, preferred_element_type=F32)
        h_s[z, c] = jnp.dot(wt[:, :c2], v_st, preferred_element_type=F32) + from_tx[:, PAIR:]
        g_s[z, c] = jnp.where(eye, gl_s[z, c], 0.0) + from_tx[:, :PAIR]
        nkb = nkb_s[z, c]
        from_tx = jnp.dot(nkb[:, c2:], tx, preferred_element_type=F32)
        q_s[z, c] = fold(rt_s[z, c] + from_tx[:, :PAIR])
        y0_s[z, c] = fold(jnp.dot(nkb[:, :c2], v_st, preferred_element_type=F32) + from_tx[:, PAIR:])

    each_item(stage_maps)

    zero64 = jnp.zeros((HEAD, HEAD), F32)
    chains = [(n, z) for n in range(seqs) for z in range(2)]
    for n, z in chains:
        a = jnp.concatenate([jnp.concatenate([s0_ref[n, z, 0], zero64], axis=1),
                             jnp.concatenate([zero64, s0_ref[n, z, 1]], axis=1)], axis=0)
        st_s[n, z] = a.T

    def propagate(i, carry):
        for n, z in chains:
            c = n * seq_chunks + (i if z == 0 else seq_chunks - 1 - i)
            s = st_s[n, z].astype(BF16)
            y_s[z, chunk_rows(c), :] = _bdot(q_s[z, c], s) + y0_s[z, c]
            st_s[n, z] = _bdot(g_s[z, c], s) + h_s[z, c]
        return carry

    lax.fori_loop(0, seq_chunks, propagate, 0)
    for n, z in chains:
        a = st_s[n, z].T
        sfin_ref[n, z, 0] = a[:HEAD, :HEAD]
        sfin_ref[n, z, 1] = a[HEAD:, HEAD:]

    y = y_s[0] + y_s[1]
    mu = _dot_ones_rhs(y, ones_bd) * (1.0 / HEAD)
    yc = y - mu
    var = _dot_ones_rhs(yc * yc, ones_bd) * (1.0 / HEAD)
    yn = yc * lax.rsqrt(var + GN_EPS) * gng_ref[...] + gnb_ref[...]
    out_ref[...] = (yn + bonus) * gate


def _rwkv_mix(pa, first_tok, n_seq, seq_len, s0, p, ride=()):
    a_width = p["w0"].shape[-1]
    n_pairs = a_width // PAIR
    heads = a_width // HEAD
    seqs = max(1, RWKV_STEP_ROWS // seq_len)
    rows = seqs * seq_len
    assert n_seq % seqs == 0 and first_tok % rows == 0
    t0 = first_tok // rows
    lora_w = 3 * PAIR
    lora_blk = 3 * a_width // lora_w
    n_chunks = rows // CHUNK
    col = lambda off: (lambda s, q: (t0 + s, off + q))
    vec = lambda off: (lambda s, q: (0, off + q))
    in_specs = [
        pl.BlockSpec((rows, PAIR), col(0)),
        pl.BlockSpec((rows, PAIR), col(n_pairs)),
        pl.BlockSpec((rows, PAIR), col(2 * n_pairs)),
        pl.BlockSpec((rows, lora_w), lambda s, q: (t0 + s, lora_blk)),
        pl.BlockSpec((1, PAIR), vec(0)),
        pl.BlockSpec((1, PAIR), vec(n_pairs)),
        pl.BlockSpec((1, PAIR), vec(2 * n_pairs)),
        pl.BlockSpec((1, lora_w), lambda s, q: (0, lora_blk)),
        pl.BlockSpec((2, PAIR), vec(0)),
        pl.BlockSpec((2, HEAD, PAIR), lambda s, q: (0, 0, q)),
        pl.BlockSpec((2, PAIR), vec(0)),
        pl.BlockSpec((2, HEAD, PAIR), lambda s, q: (0, 0, q)),
        pl.BlockSpec((PAIR, PAIR), vec(0)),
        pl.BlockSpec((1, PAIR), vec(0)),
        pl.BlockSpec((1, PAIR), vec(0)),
        pl.BlockSpec((1, PAIR), vec(0)),
        pl.BlockSpec((1, PAIR), vec(0)),
        pl.BlockSpec((1, PAIR), vec(0)),
        pl.BlockSpec((seqs, 2, 2, HEAD, HEAD), lambda s, q: (s, 0, q, 0, 0)),
    ]
    seq2 = lambda: pltpu.VMEM((2, rows, PAIR), F32)
    seq1 = lambda: pltpu.VMEM((rows, PAIR), F32)
    item = lambda rows, cols, dt: pltpu.VMEM((2, n_chunks, rows, cols), dt)
    n_steps = n_seq // seqs * n_pairs
    assert all(a.shape[0] % (n_steps * BF16_ROWS) == 0 for a in ride)
    ride_specs = [pl.BlockSpec((a.shape[0] // n_steps, a.shape[1]), lambda s, q: (s * n_pairs + q, 0)) for a in ride]
    out, s_fin, *cast = pl.pallas_call(
        functools.partial(_rwkv_kernel, seq_len=seq_len, seqs=seqs, unroll=min(n_chunks, 16), n_ride=len(ride)),
        out_shape=[jax.ShapeDtypeStruct((n_seq * seq_len, a_width), F32),
                   jax.ShapeDtypeStruct((n_seq, 2, heads, HEAD, HEAD), F32)]
        + [jax.ShapeDtypeStruct(a.shape, BF16) for a in ride],
        grid=(n_seq // seqs, n_pairs),
        in_specs=in_specs + ride_specs,
        out_specs=[pl.BlockSpec((rows, PAIR), lambda s, q: (s, q)),
                   pl.BlockSpec((seqs, 2, 2, HEAD, HEAD), lambda s, q: (s, 0, q, 0, 0))] + ride_specs,
        scratch_shapes=[seq2(), seq2(), seq2(), seq1(), seq1(), seq1(),
                        item(PAIR, PAIR, BF16),
                        item(PAIR, PAIR, F32),
                        item(PAIR, 2 * PAIR, BF16),
                        pltpu.VMEM((n_chunks, PAIR, PAIR), BF16),
                        item(PAIR, PAIR, BF16),
                        item(PAIR, PAIR, F32),
                        item(PAIR, PAIR, BF16),
                        item(PAIR, PAIR, BF16),
                        item(PAIR, 2 * PAIR, BF16),
                        item(PAIR, 2 * PAIR, BF16),
                        item(PAIR, 2 * PAIR, F32),
                        item(1, PAIR, F32),
                        item(PAIR, PAIR, F32), item(PAIR, PAIR, F32),
                        item(CHUNK, PAIR, F32), item(CHUNK, PAIR, F32),
                        seq2(), pltpu.VMEM((seqs, 2, PAIR, PAIR), F32)],
        compiler_params=_cparams(("arbitrary", "arbitrary")),
        name=f"rwkv_mix_t{seq_len}",
    )(pa, pa, pa, pa, p["ts_mu"], p["ts_mu"], p["ts_mu"], p["ts_mu"],
      p["w0"], p["w_up"], p["a0"], p["a_up"], p["g_up"], p["k_k"], p["k_a"], p["r_k"], p["gn_g"], p["gn_b"], s0, *ride)
    return out, s_fin, cast


def _conv_kernel(pb_ref, prev_ref, next_ref, pwb_ref, dw_ref, dwb_ref, g_ref, b_ref, o_ref, pad_s, shift_s, *, geom, seq):
    width = o_ref.shape[1]
    sub = 64
    n_prompt_tok, dec_seq = geom
    tok = pl.program_id(0) * CONV_TILE
    in_prompt = tok < n_prompt_tok
    seq_len = jnp.where(in_prompt, seq, dec_seq)
    off = jnp.where(in_prompt, tok, tok - n_prompt_tok) % seq_len

    def glu(ref):
        pbv = ref[...] + pwb_ref[...]
        return pbv[:, :width] * jax.nn.sigmoid(pbv[:, width:])

    pad_s[0:CONV_HALO, :] = jnp.where(off > 0, glu(prev_ref), 0.0)
    pad_s[CONV_HALO:CONV_HALO + CONV_TILE, :] = glu(pb_ref)
    pad_s[CONV_HALO + CONV_TILE:, :] = jnp.where(off + CONV_TILE < seq_len, glu(next_ref), 0.0)
    keep = pad_s.shape[0] - SUBLANES
    for phase in range(SUBLANES):
        shift_s[phase, 0:keep, :] = pad_s[phase:phase + keep, :]
    first = CONV_HALO - CONV_W // 2
    for c in range(CONV_TILE // sub):
        acc = jnp.zeros((sub, width), F32)
        for d in range(CONV_W):
            phase = (first + d) % SUBLANES
            start = c * sub + first + d - phase
            acc = acc + shift_s[phase, start:start + sub, :] * dw_ref[d:d + 1, :]
        y = _layer_norm(acc + dwb_ref[...], g_ref[...], b_ref[...], LN_EPS)
        o_ref[c * sub:(c + 1) * sub, :] = y * jax.nn.sigmoid(y)


def _conv_module(pb, geom, seq, pw_b, dw_w, dw_b, cln_g, cln_b):
    n_tok = pb.shape[0]
    width = dw_w.shape[1]
    per_tile = CONV_TILE // CONV_HALO
    last_halo = n_tok // CONV_HALO - 1
    const = lambda i: (0, 0)
    return pl.pallas_call(
        functools.partial(_conv_kernel, geom=geom, seq=seq),
        out_shape=jax.ShapeDtypeStruct((n_tok, width), F32),
        grid=(n_tok // CONV_TILE,),
        in_specs=[pl.BlockSpec((CONV_TILE, 2 * width), lambda i: (i, 0)),
                  pl.BlockSpec((CONV_HALO, 2 * width), lambda i: (jnp.maximum(i * per_tile - 1, 0), 0)),
                  pl.BlockSpec((CONV_HALO, 2 * width), lambda i: (jnp.minimum((i + 1) * per_tile, last_halo), 0)),
                  pl.BlockSpec((1, 2 * width), const),
                  pl.BlockSpec((CONV_W, width), const),
                  pl.BlockSpec((1, width), const),
                  pl.BlockSpec((1, width), const),
                  pl.BlockSpec((1, width), const)],
        out_specs=pl.BlockSpec((CONV_TILE, width), lambda i: (i, 0)),
        scratch_shapes=[pltpu.VMEM((CONV_TILE + 2 * CONV_HALO, width), F32),
                        pltpu.VMEM((SUBLANES, CONV_TILE + 2 * CONV_HALO, width), F32)],
        compiler_params=_cparams(("arbitrary",)),
        name="conv_module",
    )(pb, pb, pb, pw_b.reshape(1, -1), dw_w, dw_b.reshape(1, -1), cln_g.reshape(1, -1), cln_b.reshape(1, -1))


def _diff_lambda(lq1, lk1, lq2, lk2, lam_init):
    dot = lambda a, b: jnp.exp(jnp.sum(a[...] * b[...], axis=-1, keepdims=True))
    return dot(lq1, lk1) - dot(lq2, lk2) + lam_init


def _diff_attend(q, keys, vals, lam, lam_init, subln):
    nq = q.shape[0]
    lane_first = lax.broadcasted_iota(jnp.int32, q.shape, 1) < HEAD
    q = q * (HEAD ** -0.5)
    q_st = jnp.concatenate([jnp.where(lane_first, q, 0.0), jnp.where(lane_first, 0.0, q)], axis=0)
    s = _bdot_nt(q_st, keys)
    s = s - jnp.max(s, axis=-1, keepdims=True)
    pexp = jnp.exp(s)
    ov = _bdot(pexp, vals) / jnp.sum(pexp, axis=-1, keepdims=True)
    o = ov[:nq] - lam * ov[nq:]
    o = o * lax.rsqrt(jnp.mean(o * o, axis=-1, keepdims=True) + RMS_EPS)
    return o * subln * (1.0 - lam_init)


def _head_cols(h):
    return slice(h * PAIR, (h + 1) * PAIR)


def _attn_prompt_kernel(q_ref, k_ref, v_ref, lq1, lk1, lq2, lk2, sub_ref, o_ref, *, lam_init):
    lam = _diff_lambda(lq1, lk1, lq2, lk2, lam_init)
    for h in range(q_ref.shape[1] // PAIR):
        c = _head_cols(h)
        o_ref[:, c] = _diff_attend(q_ref[:, c], k_ref[:, c], v_ref[:, c], lam, lam_init, sub_ref[...])


def _rope(x, cos, sin_signed):
    lane = lax.broadcasted_iota(jnp.int32, x.shape, 1)
    quarter = HEAD // 4
    partner = jnp.where((lane // quarter) % 2 == 0,
                        pltpu.roll(x, LANES - quarter, 1), pltpu.roll(x, quarter, 1))
    return x * cos + partner * sin_signed


def _attn_sample_kernel(q_ref, k_ref, v_ref, ck_ref, cv_ref, cq_ref, sq_ref, cosk_ref, sink_ref,
                        lq1, lk1, lq2, lk2, sub_ref, o_ref, *, lam_init):
    lam = _diff_lambda(lq1, lk1, lq2, lk2, lam_init)
    for h in range(q_ref.shape[1] // PAIR):
        c = _head_cols(h)
        q = _rope(q_ref[:, c], cq_ref[...], sq_ref[...])
        kl = _rope(k_ref[:, c], cosk_ref[...], sink_ref[...])
        keys = jnp.concatenate([ck_ref[:, c], kl], axis=0)
        vals = jnp.concatenate([cv_ref[:, c], v_ref[:, c]], axis=0)
        o_ref[:, c] = _diff_attend(q, keys, vals, lam, lam_init, sub_ref[...])


def _rope_tables(n_lat, grid_w):
    quarter = HEAD // 4
    inv = ROPE_BASE ** (-jnp.arange(quarter, dtype=F32) / quarter)
    t = jnp.arange(n_lat)
    row_ang = (t // grid_w).astype(F32)[:, None] * inv
    col_ang = (t % grid_w).astype(F32)[:, None] * inv
    ang = jnp.concatenate([row_ang, row_ang, col_ang, col_ang] * 2, axis=-1)
    sign = jnp.tile(jnp.concatenate([-jnp.ones(quarter, F32), jnp.ones(quarter, F32)]), 2 * HEAD // (2 * quarter))
    return jnp.cos(ang), jnp.sin(ang) * sign


def _attention(q, k_p, v_p, k_s, v_s, cache_k, cache_v, p, lam_init, n_prompt, seq, n_lat_req, n_lat, grid_w):
    inner = q.shape[1]
    vec = lambda n: pl.BlockSpec((1, n), lambda *_: (0, 0))
    lam_args = [p[n].reshape(1, -1) for n in ("lq1", "lk1", "lq2", "lk2")] + [p["subln_g"].reshape(1, -1)]
    lam_specs = [vec(HEAD)] * 4 + [vec(PAIR)]
    blk = pl.BlockSpec((seq, inner), lambda b: (b, 0))
    o_prompt = pl.pallas_call(
        functools.partial(_attn_prompt_kernel, lam_init=lam_init),
        out_shape=jax.ShapeDtypeStruct((n_prompt * seq, inner), F32),
        grid=(n_prompt,),
        in_specs=[blk, blk, blk] + lam_specs,
        out_specs=blk,
        compiler_params=_cparams(("arbitrary",)),
        name="diff_attn_prompt",
    )(q, k_p, v_p, *lam_args)

    past = cache_k.shape[1]
    qt = SUB_TILE
    n_qt = n_lat // qt
    q0 = n_prompt * seq // qt
    cos, sin = _rope_tables(n_lat, grid_w)
    lat = pl.BlockSpec((n_lat, inner), lambda b, t: (b, 0))
    ctx = pl.BlockSpec((None, past, inner), lambda b, t: (b, 0, 0))
    o_sample = pl.pallas_call(
        functools.partial(_attn_sample_kernel, lam_init=lam_init),
        out_shape=jax.ShapeDtypeStruct((n_lat_req * n_lat, inner), F32),
        grid=(n_lat_req, n_qt),
        in_specs=[pl.BlockSpec((qt, inner), lambda b, t: (q0 + b * n_qt + t, 0)),
                  lat, lat, ctx, ctx,
                  pl.BlockSpec((qt, PAIR), lambda b, t: (t, 0)),
                  pl.BlockSpec((qt, PAIR), lambda b, t: (t, 0)),
                  pl.BlockSpec((n_lat, PAIR), lambda b, t: (0, 0)),
                  pl.BlockSpec((n_lat, PAIR), lambda b, t: (0, 0))] + lam_specs,
        out_specs=pl.BlockSpec((qt, inner), lambda b, t: (b * n_qt + t, 0)),
        compiler_params=_cparams(("arbitrary", "arbitrary")),
        name="diff_attn_sample",
    )(q, k_s, v_s, cache_k, cache_v, cos, sin, cos, sin, *lam_args)
    return o_prompt, o_sample


def kernel(x_prompt, x_sample, state_wkv, cache_k, cache_v, c, c_ctx, mod_w, mod_b, ln1_g, ln1_b, ln2_g, ln2_b, ev_in_w, ev_pw_b, ev_ts_mu, ev_w0, ev_w_up, ev_a0, ev_a_up, ev_g_up, ev_k_k, ev_k_a, ev_r_k, ev_gn_g, ev_gn_b, ev_dw_w, ev_dw_b, ev_cln_g, ev_cln_b, ev_out_w, od_in_w, od_lq1, od_lk1, od_lq2, od_lk2, od_subln_g, od_out_w, rg_w, rg_b, re_w, re_b, ffn_w_gate, ffn_w_up, ffn_w_down):
    nbp, seq, d = x_prompt.shape
    nbs, n_lat, _ = x_sample.shape
    depth = mod_w.shape[0]
    n_prompt_tok = nbp * seq
    geom = (n_prompt_tok, n_lat)
    alpha = (2 * depth) ** 0.25
    a_width = ev_w0.shape[-1]
    a_cols = ev_ts_mu.shape[-1]
    heads = a_width // HEAD
    grid_w = 64
    assert n_prompt_tok % MOE_TILE == 0 and n_lat % MOE_TILE == 0 and n_lat % SUB_TILE == 0
    assert seq % CONV_TILE == 0 and n_lat % CONV_TILE == 0 and CONV_HALO >= CONV_W // 2
    assert 1 + nbs <= 8

    xs = (x_prompt.reshape(n_prompt_tok, d), x_sample.reshape(nbs * n_lat, d))
    cond = jnp.concatenate([c_ctx[None, :], c, jnp.zeros((8 - 1 - nbs, d), F32)], axis=0)
    flat = lambda w: w.reshape(-1, w.shape[-1])
    mod, (ev_in_b,) = _ada_table(cond, mod_w, mod_b, (flat(ev_in_w),))
    ev_in_b = ev_in_b.reshape(ev_in_w.shape)
    late = (ffn_w_gate, ffn_w_up, ffn_w_down, ev_out_w, od_in_w, od_out_w)

    new_wkv, new_k, new_v = [], [], []
    for l in range(depth):
        last = l == depth - 1
        if l % 2 == 0:
            i = l // 2
            pa, pb = _mod_matmul(xs, mod, l, ev_in_b[i], ((a_cols, False), (ev_in_w.shape[-1] - a_cols, False)), geom)
            prm = dict(ts_mu=ev_ts_mu[i][None, :], w0=ev_w0[i], w_up=ev_w_up[i], a0=ev_a0[i], a_up=ev_a_up[i],
                       g_up=ev_g_up[i], k_k=ev_k_k[i][None, :], k_a=ev_k_a[i][None, :],
                       r_k=ev_r_k[i].reshape(1, a_width), gn_g=ev_gn_g[i][None, :], gn_b=ev_gn_b[i][None, :])
            zero_state = jnp.zeros((nbp, 2, heads, HEAD, HEAD), F32)
            oa_p, st, cast = _rwkv_mix(pa, 0, nbp, seq, zero_state, prm, () if l else tuple(flat(w) for w in late))
            if cast:
                w_gate, w_up, w_down, ev_out_b, od_in_b, od_out_b = (c.reshape(w.shape) for c, w in zip(cast, late))
            oa_s, _, _ = _rwkv_mix(pa, n_prompt_tok, nbs, n_lat, state_wkv[:, i], prm)
            conv = (ev_pw_b[i], ev_dw_w[i], ev_dw_b[i], ev_cln_g[i], ev_cln_b[i])
            hs = [(oa_p, oa_s), (_conv_module(pb, geom, seq, *conv),)]
            out_w = ev_out_b[i]
            new_wkv.append(st)
        else:
            j = l // 2
            lam_init = 0.8 - 0.6 * math.exp(-0.3 * l)
            inner = od_in_w.shape[-1] // 3
            q, k_p, k_s, v_p, v_s = _mod_matmul(xs, mod, l, od_in_b[j],
                                                 ((inner, False), (inner, True), (inner, True)), geom)
            prm = dict(lq1=od_lq1[j], lk1=od_lk1[j], lq2=od_lq2[j], lk2=od_lk2[j], subln_g=od_subln_g[j])
            ck = cache_k[:, j].reshape(nbs, -1, inner)
            cv = cache_v[:, j].reshape(nbs, -1, inner)
            hs = [_attention(q, k_p, v_p, k_s, v_s, ck, cv, prm, lam_init, nbp, seq, nbs, n_lat, grid_w)]
            out_w = od_out_b[j]
            new_k.append(k_p.reshape(nbp, seq, inner // PAIR, PAIR))
            new_v.append(v_p.reshape(nbp, seq, inner // PAIR, PAIR))
        x = _out_proj_norm(hs, xs, mod, l, out_w, ln1_g[l], ln1_b[l], alpha, geom)
        rw = jnp.concatenate([rg_w[l], re_w[l],
                              jnp.zeros((d, ROUTER_LANES - N_GROUPS - N_EXPERTS), F32)], axis=1)
        rb = jnp.concatenate([rg_b[l], re_b[l], jnp.zeros((ROUTER_LANES - N_GROUPS - N_EXPERTS,), F32)])[None, :]
        xs = tuple(_moe_norm(x, mod, l, rw, rb, w_gate, w_up, w_down, ln2_g[l], ln2_b[l], alpha, geom,
                             split_out=last))

    y_prompt = xs[0].reshape(nbp, seq, d)
    y_sample = xs[1].reshape(nbs, n_lat, d)
    return (y_prompt, y_sample, jnp.stack(new_wkv, axis=1), jnp.stack(new_k, axis=1), jnp.stack(new_v, axis=1))
```
